```python
import math
import jax
import jax.numpy as jnp
from jax import lax
import numpy as np

D_MODEL = 2048
BATCH = 4
SEQ = 4096
DEPTH = 2

GRID_W = 64
CTX_LEN = 256
HEAD_DIM = 128
ROPE_BASE = 10000.0
EPS = 1e-6
Q_BLOCK = 128

DIFF_HEADS = 8
DIFF_SUB = HEAD_DIM // 2
SWA_HEADS = 8
SWA_KV_HEADS = 2
SWA_GROUP = SWA_HEADS // SWA_KV_HEADS
WINDOW = 128

DIFF_Q = DIFF_HEADS * 2 * DIFF_SUB
DIFF_V = DIFF_HEADS * HEAD_DIM
SWA_Q = SWA_HEADS * HEAD_DIM
SWA_KV = SWA_KV_HEADS * HEAD_DIM
ATTN_IN = 2 * DIFF_Q + DIFF_V + SWA_Q + 2 * SWA_KV
ATTN_MIX = DIFF_V + SWA_Q

HYENA_ORDER = 2
HYENA_WIDTH = D_MODEL
SHORT_CONV = 3
FILTER_EMB = 33
FILTER_BANDS = (FILTER_EMB - 1) // 2
FILTER_HIDDEN = 64
DECAY_FAST = 0.3
DECAY_SLOW = 1.5
DECAY_TARGET = 1e-2

MOE_GROUPS = 4
MOE_PER_GROUP = 8
MOE_EXPERTS = MOE_GROUPS * MOE_PER_GROUP
MOE_TOPK = 2
MOE_FF = 1024
MOE_BLOCK = 256

kernel_name = 'hybrid_diffattn_swa_hyena_hmoe'


def _rms_norm(x, g):
    xf = x.astype(jnp.float32)
    y = xf * lax.rsqrt(jnp.mean(xf * xf, axis=-1, keepdims=True) + EPS)
    return (y * g.astype(jnp.float32)).astype(x.dtype)


def _heads(t, n, d):
    b, l, _ = t.shape
    return t.reshape(b, l, n, d).transpose(0, 2, 1, 3)


def _merge(t):
    b, n, l, d = t.shape
    return t.transpose(0, 2, 1, 3).reshape(b, l, n * d)


def _axial_rope(n_tokens, dim):
    n_rows = n_tokens // GRID_W
    row = jnp.repeat(jnp.arange(n_rows, dtype=jnp.float32), GRID_W)
    col = jnp.tile(jnp.arange(GRID_W, dtype=jnp.float32), n_rows)
    n_freq = dim // 4
    inv = ROPE_BASE ** (-jnp.arange(n_freq, dtype=jnp.float32) / n_freq)
    ang = jnp.concatenate([row[:, None] * inv, col[:, None] * inv], axis=-1)
    return jnp.cos(ang), jnp.sin(ang)


def _apply_rope(x, cos, sin):
    b, h, l, d = x.shape
    xp = x.reshape(b, h, l, d // 2, 2).astype(jnp.float32)
    x0, x1 = xp[..., 0], xp[..., 1]
    out = jnp.stack([x0 * cos - x1 * sin, x0 * sin + x1 * cos], axis=-1)
    return out.reshape(b, h, l, d).astype(x.dtype)


def _diff_attend(q, k, v, lam):
    s = jnp.einsum('bhqd,bhkd->bhqk', q, k).astype(jnp.float32)
    p = jax.nn.softmax(s, axis=-1)
    b, h2, nq, nk = p.shape
    p = p.reshape(b, h2 // 2, 2, nq, nk)
    a = p[:, :, 0] - lam * p[:, :, 1]
    return jnp.einsum('bhqk,bhkd->bhqd', a.astype(v.dtype), v)


def _sink_softmax(s, sink):
    m = jnp.maximum(jnp.max(s, axis=-1, keepdims=True), sink)
    e = jnp.exp(s - m)
    return e / (jnp.sum(e, axis=-1, keepdims=True) + jnp.exp(sink - m))


def _window_attend(q, k, v, kc, vc, sink):
    b, _, n, d = q.shape
    nb = n // Q_BLOCK
    span = Q_BLOCK + 2 * WINDOW
    qb = q.reshape(b, SWA_KV_HEADS, SWA_GROUP, nb, Q_BLOCK, d)
    idx = jnp.arange(nb)[:, None] * Q_BLOCK + jnp.arange(span)[None, :]
    pad = ((0, 0), (0, 0), (WINDOW, WINDOW), (0, 0))
    kb = jnp.pad(k, pad)[:, :, idx]
    vb = jnp.pad(v, pad)[:, :, idx]
    key_pos = idx - WINDOW
    rel = jnp.arange(span)[None, :] - jnp.arange(Q_BLOCK)[:, None]
    band = (rel >= 0) & (rel <= 2 * WINDOW)
    inside = (key_pos >= 0) & (key_pos < n)
    mask = band[None] & inside[:, None, :]
    s_loc = jnp.einsum('bhgnqd,bhnkd->bhgnqk', qb, kb).astype(jnp.float32)
    s_loc = jnp.where(mask, s_loc, -jnp.inf)
    s_ctx = jnp.einsum('bhgnqd,bhkd->bhgnqk', qb, kc).astype(jnp.float32)
    n_ctx = kc.shape[2]
    sk = sink.astype(jnp.float32).reshape(1, SWA_KV_HEADS, SWA_GROUP, 1, 1, 1)
    p = _sink_softmax(jnp.concatenate([s_ctx, s_loc], axis=-1), sk).astype(v.dtype)
    o = (jnp.einsum('bhgnqk,bhkd->bhgnqd', p[..., :n_ctx], vc)
         + jnp.einsum('bhgnqk,bhnkd->bhgnqd', p[..., n_ctx:], vb))
    return o.reshape(b, SWA_HEADS, n, d)


def _attention_mixer(h, hc, layer, ctx_live, w_in, w_out, dq_g, dk_g, lq1, lk1, lq2, lk2, dsub_g, sq_g, sk_g, sink):
    f32 = jnp.float32
    b, n, _ = h.shape
    cuts = [DIFF_Q, 2 * DIFF_Q, 2 * DIFF_Q + DIFF_V, 2 * DIFF_Q + DIFF_V + SWA_Q, 2 * DIFF_Q + DIFF_V + SWA_Q + SWA_KV]
    dq, dk, dv, sq, sk, sv = jnp.split(h @ w_in, cuts, axis=-1)
    cq, ck, cv, csq, csk, csv = jnp.split(hc @ w_in, cuts, axis=-1)
    lam_init = 0.8 - 0.6 * math.exp(-0.3 * layer)
    lam = (jnp.exp(jnp.sum((lq1 * lk1).astype(f32))) - jnp.exp(jnp.sum((lq2 * lk2).astype(f32))) + lam_init)
    sd = DIFF_SUB ** -0.5
    ss = HEAD_DIM ** -0.5
    cos_d, sin_d = _axial_rope(n, DIFF_SUB)
    cos_s, sin_s = _axial_rope(n, HEAD_DIM)

    q = _apply_rope(_rms_norm(_heads(dq, 2 * DIFF_HEADS, DIFF_SUB), dq_g), cos_d, sin_d) * sd
    k = _apply_rope(_rms_norm(_heads(dk, 2 * DIFF_HEADS, DIFF_SUB), dk_g), cos_d, sin_d)
    kc = _rms_norm(_heads(ck, 2 * DIFF_HEADS, DIFF_SUB), dk_g)
    vc = _heads(cv, DIFF_HEADS, HEAD_DIM)
    k_all = jnp.concatenate([kc, k], axis=2)
    v_all = jnp.concatenate([vc, _heads(dv, DIFF_HEADS, HEAD_DIM)], axis=2)
    nb = n // Q_BLOCK
    q_blocks = q.reshape(b, 2 * DIFF_HEADS, nb, Q_BLOCK, DIFF_SUB).transpose(2, 0, 1, 3, 4)
    o = lax.map(lambda qb: _diff_attend(qb, k_all, v_all, lam), q_blocks)
    o = o.transpose(1, 2, 0, 3, 4).reshape(b, DIFF_HEADS, n, HEAD_DIM)
    o_diff = _merge(_rms_norm(o, dsub_g) * (1.0 - lam_init))

    q2 = _apply_rope(_rms_norm(_heads(sq, SWA_HEADS, HEAD_DIM), sq_g), cos_s, sin_s) * ss
    k2 = _apply_rope(_rms_norm(_heads(sk, SWA_KV_HEADS, HEAD_DIM), sk_g), cos_s, sin_s)
    kc2 = _rms_norm(_heads(csk, SWA_KV_HEADS, HEAD_DIM), sk_g)
    vc2 = _heads(csv, SWA_KV_HEADS, HEAD_DIM)
    o_swa = _merge(_window_attend(q2, k2, _heads(sv, SWA_KV_HEADS, HEAD_DIM), kc2, vc2, sink))
    y = jnp.concatenate([o_diff, o_swa], axis=-1) @ w_out
    if not ctx_live:
        return y, None

    qc = _rms_norm(_heads(cq, 2 * DIFF_HEADS, DIFF_SUB), dq_g) * sd
    oc_diff = _merge(_rms_norm(_diff_attend(qc, kc, vc, lam), dsub_g) * (1.0 - lam_init))
    nc = hc.shape[1]
    qc2 = (_rms_norm(_heads(csq, SWA_HEADS, HEAD_DIM), sq_g) * ss).reshape(b, SWA_KV_HEADS, SWA_GROUP, nc, HEAD_DIM)
    sc = jnp.einsum('bhgqd,bhkd->bhgqk', qc2, kc2).astype(f32)
    pc = _sink_softmax(sc, sink.astype(f32).reshape(1, SWA_KV_HEADS, SWA_GROUP, 1, 1))
    oc_swa = jnp.einsum('bhgqk,bhkd->bhgqd', pc.astype(vc2.dtype), vc2).reshape(b, SWA_HEADS, nc, HEAD_DIM)
    yc = jnp.concatenate([oc_diff, _merge(oc_swa)], axis=-1) @ w_out
    return y, yc


def _short_conv(u, w, b):
    half = SHORT_CONV // 2
    n = u.shape[1]
    up = jnp.pad(u, ((0, 0), (half, half), (0, 0)))
    out = b
    for j in range(SHORT_CONV):
        out = out + up[:, j:j + n] * w[j]
    return out


def _hyena_filters(n, w1, b1, f1, w2, b2, f2, w3):
    t = jnp.linspace(0.0, 1.0, n, dtype=jnp.float32)[:, None]
    w = 2.0 * math.pi * jnp.arange(n, dtype=jnp.float32) / n
    f = jnp.linspace(1e-4, FILTER_BANDS - 1, FILTER_BANDS, dtype=jnp.float32)
    z = jnp.concatenate([t, jnp.cos(w[:, None] * f), -jnp.sin(w[:, None] * f)], axis=-1).astype(w1.dtype)
    a = jnp.sin(f1 * (z @ w1 + b1))
    a = jnp.sin(f2 * (a @ w2 + b2))
    hf = (a @ w3).astype(jnp.float32)
    max_decay = math.log(DECAY_TARGET) / DECAY_FAST
    min_decay = math.log(DECAY_TARGET) / DECAY_SLOW
    deltas = jnp.tile(jnp.linspace(min_decay, max_decay, HYENA_WIDTH, dtype=jnp.float32), 2 * HYENA_ORDER)
    hf = hf * jnp.exp(-t * jnp.abs(deltas))
    return hf.reshape(n, HYENA_ORDER, 2, HYENA_WIDTH)


def _long_conv(u, h_fwd, h_bwd, bias):
    n = u.shape[1]
    kern = jnp.concatenate([h_fwd, jnp.zeros_like(h_fwd[:1]), h_bwd[1:][::-1]], axis=0)
    uf = jnp.fft.rfft(u.astype(jnp.float32), n=2 * n, axis=1)
    kf = jnp.fft.rfft(kern, n=2 * n, axis=0)
    y = jnp.fft.irfft(uf * kf[None], n=2 * n, axis=1)[:, :n]
    return (y + u.astype(jnp.float32) * bias.astype(jnp.float32)).astype(u.dtype)


def _hyena_mixer(u, w_in, b_in, conv_w, conv_b, fw1, fb1, ff1, fw2, fb2, ff2, fw3, fbias, w_out, b_out):
    n = u.shape[1]
    z = _short_conv(u @ w_in + b_in, conv_w, conv_b)
    v, x1, x2 = jnp.split(z, 3, axis=-1)
    filt = _hyena_filters(n, fw1, fb1, ff1, fw2, fb2, ff2, fw3)
    y = x1 * _long_conv(v, filt[:, 0, 0], filt[:, 0, 1], fbias[0])
    y = x2 * _long_conv(y, filt[:, 1, 0], filt[:, 1, 1], fbias[1])
    return y @ w_out + b_out


def _expert_dispatch(t, expert, gates, w_gate, w_up, w_down):
    n, d = t.shape
    n_assign = n * MOE_TOPK
    flat = expert.reshape(-1)
    order = jnp.argsort(flat)
    e_sorted = flat[order]
    counts = jnp.bincount(flat, length=MOE_EXPERTS)
    padded = (counts + MOE_BLOCK - 1) // MOE_BLOCK * MOE_BLOCK
    pad_end = jnp.cumsum(padded)
    pad_start = pad_end - padded
    start = jnp.cumsum(counts) - counts
    slot_sorted = pad_start[e_sorted] + jnp.arange(n_assign) - start[e_sorted]
    slot = jnp.zeros_like(slot_sorted).at[order].set(slot_sorted)
    n_blocks = -(-n_assign // MOE_BLOCK) + MOE_EXPERTS
    src = jnp.full((n_blocks * MOE_BLOCK,), n, jnp.int32).at[slot].set(jnp.arange(n_assign, dtype=jnp.int32) // MOE_TOPK)
    xb = jnp.concatenate([t, jnp.zeros((1, d), t.dtype)], axis=0)[src].reshape(n_blocks, MOE_BLOCK, d)
    block_e = jnp.minimum(jnp.searchsorted(pad_end, jnp.arange(n_blocks) * MOE_BLOCK, side='right'), MOE_EXPERTS - 1)

    def expert_block(args):
        xs, e = args
        return (jax.nn.silu(xs @ w_gate[e]) * (xs @ w_up[e])) @ w_down[e]

    yb = lax.map(expert_block, (xb, block_e)).reshape(-1, d)
    y = yb[slot].reshape(n, MOE_TOPK, d)
    return jnp.einsum('nkd,nk->nd', y, gates.astype(y.dtype))


def _hier_moe(h, wg1, bg1, wg2, bg2, w_gate, w_up, w_down):
    shape = h.shape
    t = h.reshape(-1, shape[-1])
    n = t.shape[0]
    p_grp = jax.nn.softmax((t @ wg1).astype(jnp.float32) + bg1.astype(jnp.float32), axis=-1)
    p_top, grp = lax.top_k(p_grp, 1)
    lg = ((t @ wg2).astype(jnp.float32) + bg2.astype(jnp.float32)).reshape(n, MOE_GROUPS, MOE_PER_GROUP)
    lg = lg[jnp.arange(n), grp[:, 0]]
    top_lg, local = lax.top_k(lg, MOE_TOPK)
    gates = p_top * jax.nn.softmax(top_lg, axis=-1)
    expert = grp * MOE_PER_GROUP + local
    return _expert_dispatch(t, expert, gates, w_gate, w_up, w_down).reshape(shape)


def setup_inputs(seed: int = 0) -> dict:
    key = jax.random.key(seed)
    keys = jax.random.split(key, 48)
    counter = [0]

    def nrm(shape, scale):
        k = keys[counter[0]]
        counter[0] += 1
        return jax.random.normal(k, shape, jnp.float32) * scale

    def gain(shape):
        return 1.0 + nrm(shape, 0.05)

    D = D_MODEL
    W = HYENA_WIDTH
    ne = (DEPTH + 1) // 2
    no = DEPTH // 2
    return {
        'x': nrm((BATCH, SEQ, D), 1.0),
        'c': nrm((BATCH, D), 1.0),
        'ctx': nrm((BATCH, CTX_LEN, D), 1.0),
        'c_ctx': nrm((D,), 1.0),
        'ada_w': nrm((DEPTH, D, 6 * D), 0.5 * D ** -0.5),
        'ada_b': nrm((DEPTH, 6 * D), 0.01),
        'norm1_g': gain((DEPTH, D)),
        'norm2_g': gain((DEPTH, D)),
        'attn_w_in': nrm((ne, D, ATTN_IN), D ** -0.5),
        'attn_w_out': nrm((ne, ATTN_MIX, D), ATTN_MIX ** -0.5),
        'diff_q_g': gain((ne, DIFF_SUB)),
        'diff_k_g': gain((ne, DIFF_SUB)),
        'diff_lq1': nrm((ne, DIFF_SUB), 0.1),
        'diff_lk1': nrm((ne, DIFF_SUB), 0.1),
        'diff_lq2': nrm((ne, DIFF_SUB), 0.1),
        'diff_lk2': nrm((ne, DIFF_SUB), 0.1),
        'diff_sub_g': gain((ne, HEAD_DIM)),
        'swa_q_g': gain((ne, HEAD_DIM)),
        'swa_k_g': gain((ne, HEAD_DIM)),
        'swa_sink': nrm((ne, SWA_HEADS), 0.5),
        'hy_w_in': nrm((no, D, 3 * W), D ** -0.5),
        'hy_b_in': nrm((no, 3 * W), 0.02),
        'hy_conv_w': nrm((no, SHORT_CONV, 3 * W), SHORT_CONV ** -0.5),
        'hy_conv_b': nrm((no, 3 * W), 0.02),
        'flt_w1': nrm((no, FILTER_EMB, FILTER_HIDDEN), FILTER_EMB ** -0.5),
        'flt_b1': nrm((no, FILTER_HIDDEN), 0.1),
        'flt_f1': gain((no, FILTER_HIDDEN)),
        'flt_w2': nrm((no, FILTER_HIDDEN, FILTER_HIDDEN), FILTER_HIDDEN ** -0.5),
        'flt_b2': nrm((no, FILTER_HIDDEN), 0.1),
        'flt_f2': gain((no, FILTER_HIDDEN)),
        'flt_w3': nrm((no, FILTER_HIDDEN, 2 * HYENA_ORDER * W), 0.05 * FILTER_HIDDEN ** -0.5),
        'hy_bias': nrm((no, HYENA_ORDER, W), 0.5),
        'hy_w_out': nrm((no, W, D), W ** -0.5),
        'hy_b_out': nrm((no, D), 0.02),
        'moe_wg1': nrm((DEPTH, D, MOE_GROUPS), D ** -0.5),
        'moe_bg1': nrm((DEPTH, MOE_GROUPS), 0.01),
        'moe_wg2': nrm((DEPTH, D, MOE_EXPERTS), D ** -0.5),
        'moe_bg2': nrm((DEPTH, MOE_EXPERTS), 0.01),
        'moe_w_gate': nrm((DEPTH, MOE_EXPERTS, D, MOE_FF), D ** -0.5),
        'moe_w_up': nrm((DEPTH, MOE_EXPERTS, D, MOE_FF), D ** -0.5),
        'moe_w_down': nrm((DEPTH, MOE_EXPERTS, MOE_FF, D), MOE_FF ** -0.5),
    }


def reference(x, c, ctx, c_ctx, ada_w, ada_b, norm1_g, norm2_g, attn_w_in, attn_w_out, diff_q_g, diff_k_g, diff_lq1, diff_lk1, diff_lq2, diff_lk2, diff_sub_g, swa_q_g, swa_k_g, swa_sink, hy_w_in, hy_b_in, hy_conv_w, hy_conv_b, flt_w1, flt_b1, flt_f1, flt_w2, flt_b2, flt_f2, flt_w3, hy_bias, hy_w_out, hy_b_out, moe_wg1, moe_bg1, moe_wg2, moe_bg2, moe_w_gate, moe_w_up, moe_w_down):
    s_lat = jax.nn.silu(c)
    s_ctx = jax.nn.silu(c_ctx)
    for layer in range(DEPTH):
        even = layer % 2 == 0
        i = layer // 2
        ctx_live = any(j % 2 == 0 for j in range(layer + 1, DEPTH))
        mod = (s_lat @ ada_w[layer] + ada_b[layer])[:, None, :]
        sh1, sc1, g1, sh2, sc2, g2 = jnp.split(mod, 6, axis=-1)
        h = _rms_norm(x, norm1_g[layer]) * (1 + sc1) + sh1
        if even or ctx_live:
            cmod = s_ctx @ ada_w[layer] + ada_b[layer]
            csh1, csc1, cg1, csh2, csc2, cg2 = jnp.split(cmod, 6)
            hc = _rms_norm(ctx, norm1_g[layer]) * (1 + csc1) + csh1
        if even:
            y, yc = _attention_mixer(h, hc, layer, ctx_live, attn_w_in[i], attn_w_out[i], diff_q_g[i], diff_k_g[i],
                                     diff_lq1[i], diff_lk1[i], diff_lq2[i], diff_lk2[i], diff_sub_g[i],
                                     swa_q_g[i], swa_k_g[i], swa_sink[i])
        else:
            hy = (hy_w_in[i], hy_b_in[i], hy_conv_w[i], hy_conv_b[i], flt_w1[i], flt_b1[i], flt_f1[i],
                  flt_w2[i], flt_b2[i], flt_f2[i], flt_w3[i], hy_bias[i], hy_w_out[i], hy_b_out[i])
            y = _hyena_mixer(h, *hy)
            yc = _hyena_mixer(hc, *hy) if ctx_live else None
        moe = (moe_wg1[layer], moe_bg1[layer], moe_wg2[layer], moe_bg2[layer],
               moe_w_gate[layer], moe_w_up[layer], moe_w_down[layer])
        x = x + g1 * y
        x = x + g2 * _hier_moe(_rms_norm(x, norm2_g[layer]) * (1 + sc2) + sh2, *moe)
        if ctx_live:
            ctx = ctx + cg1 * yc
            ctx = ctx + cg2 * _hier_moe(_rms_norm(ctx, norm2_g[layer]) * (1 + csc2) + csh2, *moe)
    return x
```

```python
import functools
import math

import jax
import jax.numpy as jnp
from jax import lax
from jax.experimental import pallas as pl
from jax.experimental.pallas import tpu as pltpu

F32 = jnp.float32
BF16 = jnp.bfloat16
HIGHEST = lax.Precision.HIGHEST

LANES = 128
V7X_VMEM_BYTES = 64 * 1024 * 1024
VMEM_LIMIT = V7X_VMEM_BYTES * 7 // 8

GRID_W = 64
HEAD_DIM = 128
ROPE_BASE = 10000.0
EPS = 1e-6
Q_BLOCK = 128
DIFF_HEADS = 8
DIFF_SUB = HEAD_DIM // 2
SWA_HEADS = 8
SWA_KV_HEADS = 2
SWA_GROUP = SWA_HEADS // SWA_KV_HEADS
WINDOW = 128
DIFF_Q = DIFF_HEADS * 2 * DIFF_SUB
DIFF_V = DIFF_HEADS * HEAD_DIM
SWA_Q = SWA_HEADS * HEAD_DIM
SWA_KV = SWA_KV_HEADS * HEAD_DIM
ATTN_IN = 2 * DIFF_Q + DIFF_V + SWA_Q + 2 * SWA_KV
HYENA_ORDER = 2
SHORT_CONV = 3
FILTER_EMB = 33
FILTER_BANDS = (FILTER_EMB - 1) // 2
FILTER_HIDDEN = 64
DECAY_FAST = 0.3
DECAY_SLOW = 1.5
DECAY_TARGET = 1e-2
MOE_GROUPS = 4
MOE_PER_GROUP = 8
MOE_EXPERTS = MOE_GROUPS * MOE_PER_GROUP
MOE_TOPK = 2
MOE_BLOCK = 256
FF_CHUNK = 256
NEG_INF = float("-inf")


def _tile(n, pref):
    t = pref
    while n % t:
        t //= 2
    return t


def _params(*sem):
    return pltpu.CompilerParams(dimension_semantics=sem, vmem_limit_bytes=VMEM_LIMIT)


def _dot(a, b):
    return jnp.dot(a, b, preferred_element_type=F32)


def _dot_nt(a, b):
    return lax.dot_general(a, b, (((1,), (1,)), ((), ())), preferred_element_type=F32)


def _mod_kernel(c_ref, w_ref, b_ref, o_ref):
    c = c_ref[...]
    s = c * jax.nn.sigmoid(c)
    o_ref[...] = jnp.dot(s, w_ref[...], precision=HIGHEST, preferred_element_type=F32) + b_ref[...]


def _adaln_mod(cond, ada_w, ada_b, layer):
    rows, d = cond.shape
    n = ada_w.shape[2]
    tn = _tile(n, 768)
    return pl.pallas_call(
        _mod_kernel,
        grid=(n // tn,),
        in_specs=[pl.BlockSpec((rows, d), lambda j: (0, 0)),
                  pl.BlockSpec((None, d, tn), lambda j: (layer, 0, j)),
                  pl.BlockSpec((None, 1, tn), lambda j: (layer, 0, j))],
        out_specs=pl.BlockSpec((rows, tn), lambda j: (0, j)),
        out_shape=jax.ShapeDtypeStruct((rows, n), F32),
        compiler_params=_params("parallel"),
        name="adaln_mod",
    )(cond, ada_w, ada_b)


def _norm_mod(x, g, sc, sh):
    inv = lax.rsqrt(jnp.mean(x * x, axis=-1, keepdims=True) + EPS)
    return ((x * inv) * g) * (1.0 + sc) + sh


def _norm_linear_kernel(x_ref, g_ref, sc_ref, sh_ref, w_ref, b_ref, o_ref, h_ref):
    @pl.when(pl.program_id(2) == 0)
    def _():
        h_ref[...] = _norm_mod(x_ref[...], g_ref[...], sc_ref[...], sh_ref[...]).astype(BF16)

    o_ref[...] = (_dot(h_ref[...], w_ref[...]) + b_ref[...]).astype(o_ref.dtype)


def _norm_linear(x, g, sc, sh, w, b, tm, tn, name):
    bsz, l, d = x.shape
    n = w.shape[1]
    tn = _tile(n, tn)
    return pl.pallas_call(
        _norm_linear_kernel,
        grid=(bsz, l // tm, n // tn),
        in_specs=[pl.BlockSpec((None, tm, d), lambda bi, i, j: (bi, i, 0)),
                  pl.BlockSpec((1, d), lambda bi, i, j: (0, 0)),
                  pl.BlockSpec((None, 1, d), lambda bi, i, j: (bi, 0, 0)),
                  pl.BlockSpec((None, 1, d), lambda bi, i, j: (bi, 0, 0)),
                  pl.BlockSpec((d, tn), lambda bi, i, j: (0, j)),
                  pl.BlockSpec((1, tn), lambda bi, i, j: (0, j))],
        out_specs=pl.BlockSpec((None, tm, tn), lambda bi, i, j: (bi, i, j)),
        out_shape=jax.ShapeDtypeStruct((bsz, l, n), F32),
        scratch_shapes=[pltpu.VMEM((tm, d), BF16)],
        compiler_params=_params("parallel", "arbitrary", "arbitrary"),
        name=name,
    )(x, g, sc, sh, w, b)


def _mm_res_kernel(*refs, n_pairs):
    a_refs = refs[:n_pairs]
    w_refs = refs[n_pairs:2 * n_pairs]
    b_ref, x_ref, g_ref, o_ref = refs[2 * n_pairs:]
    acc = _dot(a_refs[0][...], w_refs[0][...])
    for a_ref, w_ref in zip(a_refs[1:], w_refs[1:]):
        acc = acc + _dot(a_ref[...], w_ref[...])
    o_ref[...] = x_ref[...] + g_ref[...] * (acc + b_ref[...])


def _mm_residual(a_list, w_list, b, x, g, tm, name):
    bsz, l, n = x.shape
    n_pairs = len(a_list)
    in_specs = [pl.BlockSpec((None, tm, a.shape[2]), lambda bi, i: (bi, i, 0)) for a in a_list]
    in_specs += [pl.BlockSpec(w.shape, lambda bi, i: (0, 0)) for w in w_list]
    in_specs += [pl.BlockSpec((1, n), lambda bi, i: (0, 0)),
                 pl.BlockSpec((None, tm, n), lambda bi, i: (bi, i, 0)),
                 pl.BlockSpec((None, 1, n), lambda bi, i: (bi, 0, 0))]
    return pl.pallas_call(
        functools.partial(_mm_res_kernel, n_pairs=n_pairs),
        grid=(bsz, l // tm),
        in_specs=in_specs,
        out_specs=pl.BlockSpec((None, tm, n), lambda bi, i: (bi, i, 0)),
        out_shape=jax.ShapeDtypeStruct((bsz, l, n), F32),
        compiler_params=_params("parallel", "parallel"),
        name=name,
    )(*a_list, *w_list, b, x, g)


def _rope_tables(n_tokens, dim, width):
    n_rows = n_tokens // GRID_W
    row = jnp.repeat(jnp.arange(n_rows, dtype=F32), GRID_W)
    col = jnp.tile(jnp.arange(GRID_W, dtype=F32), n_rows)
    n_freq = dim // 4
    inv = ROPE_BASE ** (-jnp.arange(n_freq, dtype=F32) / n_freq)
    ang = jnp.concatenate([row[:, None] * inv, col[:, None] * inv], axis=-1)
    ang = jnp.repeat(ang, 2, axis=-1)
    ang = jnp.tile(ang, (1, width // dim))
    sign = jnp.where(jnp.arange(width) % 2 == 0, -1.0, 1.0).astype(F32)
    return jnp.cos(ang), jnp.sin(ang) * sign


def _rope(x, cos_t, sin_t):
    width = x.shape[-1]
    lane = lax.broadcasted_iota(jnp.int32, (1, width), 1)
    partner = jnp.where(lane % 2 == 0, pltpu.roll(x, width - 1, 1), pltpu.roll(x, 1, 1))
    return x * cos_t + partner * sin_t


def _sub_rms(x, g):
    lane = lax.broadcasted_iota(jnp.int32, (1, 2 * DIFF_SUB), 1)
    lo = lane < DIFF_SUB
    sq = x * x
    ss_lo = jnp.sum(jnp.where(lo, sq, 0.0), axis=-1, keepdims=True)
    ss_hi = jnp.sum(jnp.where(lo, 0.0, sq), axis=-1, keepdims=True)
    inv = jnp.where(lo, lax.rsqrt(ss_lo / DIFF_SUB + EPS), lax.rsqrt(ss_hi / DIFF_SUB + EPS))
    return (x * inv) * g


def _rms(x, g):
    return (x * lax.rsqrt(jnp.mean(x * x, axis=-1, keepdims=True) + EPS)) * g


def _diff_attn_kernel(q_ref, k_ref, v_ref, kc_ref, vc_ref, cq_ref, sq_ref, ck_ref, sk_ref,
                      gq_ref, gk_ref, gs_ref, lamp_ref, o_ref, k_scr, v_scr, lam_scr, *, lam_init, n_ctx):
    lane = lax.broadcasted_iota(jnp.int32, (1, 2 * DIFF_SUB), 1)
    lo = lane < DIFF_SUB

    @pl.when(pl.program_id(2) == 0)
    def _():
        gk = gk_ref[...]
        k_scr[0:n_ctx, :] = _sub_rms(kc_ref[...], gk).astype(BF16)
        k_scr[n_ctx:, :] = _rope(_sub_rms(k_ref[...], gk), ck_ref[...], sk_ref[...]).astype(BF16)
        v_scr[0:n_ctx, :] = vc_ref[...].astype(BF16)
        v_scr[n_ctx:, :] = v_ref[...].astype(BF16)
        lp = lamp_ref[...]
        t1 = jnp.sum(lp[0:1] * lp[1:2], axis=-1, keepdims=True)
        t2 = jnp.sum(lp[2:3] * lp[3:4], axis=-1, keepdims=True)
        lam_scr[...] = jnp.broadcast_to(jnp.exp(t1) - jnp.exp(t2) + lam_init, lam_scr.shape)

    q = _rope(_sub_rms(q_ref[...], gq_ref[...]), cq_ref[...], sq_ref[...]) * (DIFF_SUB ** -0.5)
    q1 = jnp.where(lo, q, 0.0).astype(BF16)
    q2 = jnp.where(lo, 0.0, q).astype(BF16)
    kk = k_scr[...]
    s1 = _dot_nt(q1, kk)
    s2 = _dot_nt(q2, kk)
    e1 = jnp.exp(s1 - jnp.max(s1, axis=-1, keepdims=True))
    e2 = jnp.exp(s2 - jnp.max(s2, axis=-1, keepdims=True))
    r1 = 1.0 / jnp.sum(e1, axis=-1, keepdims=True)
    r2 = lam_scr[:, 0:1] / jnp.sum(e2, axis=-1, keepdims=True)
    a = e1 * r1 - e2 * r2
    o = _dot(a.astype(BF16), v_scr[...])
    o_ref[...] = (_rms(o, gs_ref[...]) * (1.0 - lam_init)).astype(o_ref.dtype)


def _diff_attention(qkv, qkv_c, tabs, gq, gk, gs, lamp, lam_init, tq):
    bsz, l, _ = qkv.shape
    n_ctx = qkv_c.shape[1]
    cos_t, sin_t = tabs
    hb = 2 * DIFF_SUB
    kb = DIFF_Q // hb
    vb = 2 * DIFF_Q // hb
    kern = functools.partial(_diff_attn_kernel, lam_init=lam_init, n_ctx=n_ctx)
    vec = pl.BlockSpec((1, hb), lambda bi, h, i: (0, 0))
    return pl.pallas_call(
        kern,
        grid=(bsz, DIFF_HEADS, l // tq),
        in_specs=[pl.BlockSpec((None, tq, hb), lambda bi, h, i: (bi, i, h)),
                  pl.BlockSpec((None, l, hb), lambda bi, h, i: (bi, 0, kb + h)),
                  pl.BlockSpec((None, l, hb), lambda bi, h, i: (bi, 0, vb + h)),
                  pl.BlockSpec((None, n_ctx, hb), lambda bi, h, i: (bi, 0, kb + h)),
                  pl.BlockSpec((None, n_ctx, hb), lambda bi, h, i: (bi, 0, vb + h)),
                  pl.BlockSpec((tq, hb), lambda bi, h, i: (i, 0)),
                  pl.BlockSpec((tq, hb), lambda bi, h, i: (i, 0)),
                  pl.BlockSpec((l, hb), lambda bi, h, i: (0, 0)),
                  pl.BlockSpec((l, hb), lambda bi, h, i: (0, 0)),
                  vec, vec, vec,
                  pl.BlockSpec((8, hb), lambda bi, h, i: (0, 0))],
        out_specs=pl.BlockSpec((None, tq, hb), lambda bi, h, i: (bi, i, h)),
        out_shape=jax.ShapeDtypeStruct((bsz, l, DIFF_V), BF16),
        scratch_shapes=[pltpu.VMEM((n_ctx + l, hb), BF16), pltpu.VMEM((n_ctx + l, hb), BF16),
                        pltpu.VMEM((1, hb), F32)],
        compiler_params=_params("parallel", "parallel", "arbitrary"),
        name="diff_attention",
    )(qkv, qkv, qkv, qkv_c, qkv_c, cos_t, sin_t, cos_t, sin_t, gq, gk, gs, lamp)


def _swa_kernel(q_ref, k_ref, v_ref, kc_ref, vc_ref, cq_ref, sq_ref, ck_ref, sk_ref, gq_ref, gk_ref, sink_ref,
                o_ref, k_scr, v_scr, kc_scr, vc_scr, *, seq):
    kvh = pl.program_id(1)
    n = pl.program_id(2)
    span = Q_BLOCK + 2 * WINDOW

    @pl.when(n == 0)
    def _():
        gk = gk_ref[...]
        zeros = jnp.zeros((WINDOW, HEAD_DIM), BF16)
        k_scr[0:WINDOW, :] = zeros
        v_scr[0:WINDOW, :] = zeros
        k_scr[WINDOW + seq:, :] = zeros
        v_scr[WINDOW + seq:, :] = zeros
        k_scr[WINDOW:WINDOW + seq, :] = _rope(_rms(k_ref[...], gk), ck_ref[...], sk_ref[...]).astype(BF16)
        v_scr[WINDOW:WINDOW + seq, :] = v_ref[...].astype(BF16)
        kc_scr[...] = _rms(kc_ref[...], gk).astype(BF16)
        vc_scr[...] = vc_ref[...].astype(BF16)

    start = pl.multiple_of(n * Q_BLOCK, Q_BLOCK)
    kw = k_scr[pl.ds(start, span), :]
    vw = v_scr[pl.ds(start, span), :]
    kc = kc_scr[...]
    vc = vc_scr[...]
    qidx = lax.broadcasted_iota(jnp.int32, (Q_BLOCK, span), 0)
    kidx = lax.broadcasted_iota(jnp.int32, (Q_BLOCK, span), 1)
    rel = kidx - qidx
    key_pos = n * Q_BLOCK + kidx - WINDOW
    mask = (rel >= 0) & (rel <= 2 * WINDOW) & (key_pos >= 0) & (key_pos < seq)
    gq = gq_ref[...]
    cq = cq_ref[...]
    sq = sq_ref[...]
    outs = []
    for j in range(SWA_GROUP):
        q = _rope(_rms(q_ref[:, j * HEAD_DIM:(j + 1) * HEAD_DIM], gq), cq, sq) * (HEAD_DIM ** -0.5)
        q = q.astype(BF16)
        s_loc = jnp.where(mask, _dot_nt(q, kw), NEG_INF)
        s_ctx = _dot_nt(q, kc)
        sink = sink_ref[pl.ds(kvh * SWA_GROUP + j, 1), 0:1]
        m = jnp.maximum(jnp.maximum(jnp.max(s_loc, axis=-1, keepdims=True),
                                    jnp.max(s_ctx, axis=-1, keepdims=True)), sink)
        e_loc = jnp.exp(s_loc - m)
        e_ctx = jnp.exp(s_ctx - m)
        denom = (jnp.sum(e_loc, axis=-1, keepdims=True) + jnp.sum(e_ctx, axis=-1, keepdims=True)
                 + jnp.exp(sink - m))
        r = 1.0 / denom
        o = _dot((e_ctx * r).astype(BF16), vc) + _dot((e_loc * r).astype(BF16), vw)
        outs.append(o)
    o_ref[...] = jnp.concatenate(outs, axis=-1).astype(o_ref.dtype)


def _swa_attention(qkv, qkv_c, tabs, gq, gk, sink_rows):
    bsz, l, _ = qkv.shape
    n_ctx = qkv_c.shape[1]
    cos_t, sin_t = tabs
    gw = SWA_GROUP * HEAD_DIM
    q0 = (2 * DIFF_Q + DIFF_V) // gw
    k0 = (2 * DIFF_Q + DIFF_V + SWA_Q) // HEAD_DIM
    v0 = k0 + SWA_KV_HEADS
    vec = pl.BlockSpec((1, HEAD_DIM), lambda bi, h, i: (0, 0))
    return pl.pallas_call(
        functools.partial(_swa_kernel, seq=l),
        grid=(bsz, SWA_KV_HEADS, l // Q_BLOCK),
        in_specs=[pl.BlockSpec((None, Q_BLOCK, gw), lambda bi, h, i: (bi, i, q0 + h)),
                  pl.BlockSpec((None, l, HEAD_DIM), lambda bi, h, i: (bi, 0, k0 + h)),
                  pl.BlockSpec((None, l, HEAD_DIM), lambda bi, h, i: (bi, 0, v0 + h)),
                  pl.BlockSpec((None, n_ctx, HEAD_DIM), lambda bi, h, i: (bi, 0, k0 + h)),
                  pl.BlockSpec((None, n_ctx, HEAD_DIM), lambda bi, h, i: (bi, 0, v0 + h)),
                  pl.BlockSpec((Q_BLOCK, HEAD_DIM), lambda bi, h, i: (i, 0)),
                  pl.BlockSpec((Q_BLOCK, HEAD_DIM), lambda bi, h, i: (i, 0)),
                  pl.BlockSpec((l, HEAD_DIM), lambda bi, h, i: (0, 0)),
                  pl.BlockSpec((l, HEAD_DIM), lambda bi, h, i: (0, 0)),
                  vec, vec,
                  pl.BlockSpec((SWA_HEADS, HEAD_DIM), lambda bi, h, i: (0, 0))],
        out_specs=pl.BlockSpec((None, Q_BLOCK, gw), lambda bi, h, i: (bi, i, h)),
        out_shape=jax.ShapeDtypeStruct((bsz, l, SWA_Q), BF16),
        scratch_shapes=[pltpu.VMEM((l + 2 * WINDOW, HEAD_DIM), BF16), pltpu.VMEM((l + 2 * WINDOW, HEAD_DIM), BF16),
                        pltpu.VMEM((n_ctx, HEAD_DIM), BF16), pltpu.VMEM((n_ctx, HEAD_DIM), BF16)],
        compiler_params=_params("parallel", "parallel", "arbitrary"),
        name="swa_attention",
    )(qkv, qkv, qkv, qkv_c, qkv_c, cos_t, sin_t, cos_t, sin_t, gq, gk, sink_rows)


def _router_kernel(x_ref, g_ref, sc_ref, sh_ref, w_ref, b_ref, h_ref, e_ref, p_ref):
    h = _norm_mod(x_ref[...], g_ref[...], sc_ref[...], sh_ref[...])
    h_ref[...] = h.astype(BF16)
    logits = jnp.dot(h, w_ref[...], precision=HIGHEST, preferred_element_type=F32) + b_ref[...]
    lane_i = lax.broadcasted_iota(jnp.int32, (1, LANES), 1)
    lane = lane_i.astype(F32)
    big = float(LANES)
    lg = jnp.where(lane_i < MOE_GROUPS, logits, NEG_INF)
    mg = jnp.max(lg, axis=-1, keepdims=True)
    p_top = 1.0 / jnp.sum(jnp.exp(lg - mg), axis=-1, keepdims=True)
    grp = jnp.min(jnp.where(lg == mg, lane, big), axis=-1, keepdims=True)
    e_lane = lane - MOE_GROUPS
    in_grp = (e_lane >= grp * MOE_PER_GROUP) & (e_lane < (grp + 1.0) * MOE_PER_GROUP)
    le = jnp.where(in_grp, logits, NEG_INF)
    t1 = jnp.max(le, axis=-1, keepdims=True)
    i1 = jnp.min(jnp.where(le == t1, lane, big), axis=-1, keepdims=True)
    le2 = jnp.where(lane == i1, NEG_INF, le)
    t2 = jnp.max(le2, axis=-1, keepdims=True)
    i2 = jnp.min(jnp.where(le2 == t2, lane, big), axis=-1, keepdims=True)
    w2 = jnp.exp(t2 - t1)
    inv = p_top / (1.0 + w2)
    e_ref[...] = jnp.where(lane_i == 0, i1 - MOE_GROUPS, jnp.where(lane_i == 1, i2 - MOE_GROUPS, 0.0)).astype(jnp.int32)
    p_ref[...] = jnp.where(lane_i == 0, inv, jnp.where(lane_i == 1, inv * w2, 0.0))


def _router(x, g, sc, sh, w, b, tm):
    bsz, l, d = x.shape
    row = lambda bi, i: (bi, i, 0)
    return pl.pallas_call(
        _router_kernel,
        grid=(bsz, l // tm),
        in_specs=[pl.BlockSpec((None, tm, d), row),
                  pl.BlockSpec((1, d), lambda bi, i: (0, 0)),
                  pl.BlockSpec((None, 1, d), lambda bi, i: (bi, 0, 0)),
                  pl.BlockSpec((None, 1, d), lambda bi, i: (bi, 0, 0)),
                  pl.BlockSpec((d, LANES), lambda bi, i: (0, 0)),
                  pl.BlockSpec((1, LANES), lambda bi, i: (0, 0))],
        out_specs=[pl.BlockSpec((None, tm, d), row),
                   pl.BlockSpec((None, tm, LANES), row),
                   pl.BlockSpec((None, tm, LANES), row)],
        out_shape=[jax.ShapeDtypeStruct((bsz, l, d), BF16),
                   jax.ShapeDtypeStruct((bsz, l, LANES), jnp.int32),
                   jax.ShapeDtypeStruct((bsz, l, LANES), F32)],
        compiler_params=_params("parallel", "parallel"),
        name="moe_router",
    )(x, g, sc, sh, w, b)


def _expert_kernel(be_ref, nu_ref, x_ref, wg_ref, wu_ref, wd_ref, o_ref):
    i = pl.program_id(0)

    @pl.when(i < nu_ref[0])
    def _():
        x = x_ref[...]
        ff = wg_ref.shape[1]
        acc = jnp.zeros(o_ref.shape, F32)
        for c in range(ff // FF_CHUNK):
            cols = slice(c * FF_CHUNK, (c + 1) * FF_CHUNK)
            gate = _dot(x, wg_ref[:, cols])
            up = _dot(x, wu_ref[:, cols])
            hidden = (gate * jax.nn.sigmoid(gate) * up).astype(BF16)
            acc = acc + _dot(hidden, wd_ref[cols, :])
        o_ref[...] = acc

    @pl.when(i >= nu_ref[0])
    def _():
        o_ref[...] = jnp.zeros(o_ref.shape, F32)


def _expert_ffn(xb, block_e, n_used, w_gate, w_up, w_down, layer):
    n_blocks, _, d = xb.shape
    ff = w_gate.shape[2]
    base = layer * MOE_EXPERTS
    grid_spec = pltpu.PrefetchScalarGridSpec(
        num_scalar_prefetch=2,
        grid=(n_blocks,),
        in_specs=[pl.BlockSpec((None, MOE_BLOCK, d), lambda i, be, nu: (i, 0, 0)),
                  pl.BlockSpec((None, d, ff), lambda i, be, nu: (base + be[i], 0, 0)),
                  pl.BlockSpec((None, d, ff), lambda i, be, nu: (base + be[i], 0, 0)),
                  pl.BlockSpec((None, ff, d), lambda i, be, nu: (base + be[i], 0, 0))],
        out_specs=pl.BlockSpec((None, MOE_BLOCK, d), lambda i, be, nu: (i, 0, 0)),
    )
    return pl.pallas_call(
        _expert_kernel,
        grid_spec=grid_spec,
        out_shape=jax.ShapeDtypeStruct((n_blocks, MOE_BLOCK, d), F32),
        compiler_params=_params("arbitrary"),
        name="expert_ffn",
    )(block_e, n_used, xb, w_gate, w_up, w_down)


def _hier_moe(x, g, sc, sh, gate2, w_route, b_route, w_gate, w_up, w_down, layer):
    bsz, l, d = x.shape
    n = bsz * l
    h, e_out, p_out = _router(x, g, sc, sh, w_route, b_route, min(512, l))
    t = h.reshape(n, d)
    expert = e_out.reshape(n, LANES)[:, :MOE_TOPK]
    gates = p_out.reshape(n, LANES)[:, :MOE_TOPK]
    n_assign = n * MOE_TOPK
    flat = expert.reshape(-1)
    order = jnp.argsort(flat)
    e_sorted = flat[order]
    counts = jnp.bincount(flat, length=MOE_EXPERTS)
    padded = (counts + MOE_BLOCK - 1) // MOE_BLOCK * MOE_BLOCK
    pad_end = jnp.cumsum(padded)
    pad_start = pad_end - padded
    seg_start = jnp.cumsum(counts) - counts
    slot_sorted = pad_start[e_sorted] + jnp.arange(n_assign) - seg_start[e_sorted]
    slot = jnp.zeros_like(slot_sorted).at[order].set(slot_sorted)
    n_blocks = -(-n_assign // MOE_BLOCK) + MOE_EXPERTS
    src = jnp.full((n_blocks * MOE_BLOCK,), n, jnp.int32).at[slot].set(
        jnp.arange(n_assign, dtype=jnp.int32) // MOE_TOPK)
    xb = jnp.concatenate([t, jnp.zeros((1, d), t.dtype)], axis=0)[src].reshape(n_blocks, MOE_BLOCK, d)
    block_e = jnp.minimum(jnp.searchsorted(pad_end, jnp.arange(n_blocks) * MOE_BLOCK, side='right'),
                          MOE_EXPERTS - 1).astype(jnp.int32)
    n_used = (pad_end[-1:] // MOE_BLOCK).astype(jnp.int32)
    yb = _expert_ffn(xb, block_e, n_used, w_gate, w_up, w_down, layer).reshape(-1, d)
    y = yb[slot].reshape(n, MOE_TOPK, d)
    moe = jnp.einsum('nkd,nk->nd', y, gates).reshape(bsz, l, d)
    return x + gate2 * moe


def _short_conv_kernel(zv_ref, z1_ref, z2_ref, wv_ref, w1_ref, w2_ref, bv_ref, b1_ref, b2_ref,
                       v_ref, x1_ref, x2_ref):
    n = zv_ref.shape[0]
    row = lax.broadcasted_iota(jnp.int32, (n, 1), 0)

    def conv(z_ref, w_ref, b_ref):
        z = z_ref[...]
        w = w_ref[...]
        prev = jnp.where(row == 0, 0.0, pltpu.roll(z, 1, 0))
        nxt = jnp.where(row == n - 1, 0.0, pltpu.roll(z, n - 1, 0))
        return b_ref[...] + prev * w[0:1] + z * w[1:2] + nxt * w[2:3]

    v_ref[...] = conv(zv_ref, wv_ref, bv_ref).astype(v_ref.dtype)
    x1_ref[...] = conv(z1_ref, w1_ref, b1_ref).astype(x1_ref.dtype)
    x2_ref[...] = conv(z2_ref, w2_ref, b2_ref).astype(x2_ref.dtype)


def _short_conv(z, conv_w, conv_b, tc):
    bsz, l, w3 = z.shape
    w = w3 // 3
    nc = w // tc
    zs = lambda k: pl.BlockSpec((None, l, tc), lambda bi, j: (bi, 0, k * nc + j))
    ws = lambda k: pl.BlockSpec((8, tc), lambda bi, j: (0, k * nc + j))
    bs = lambda k: pl.BlockSpec((1, tc), lambda bi, j: (0, k * nc + j))
    out = pl.BlockSpec((None, l, tc), lambda bi, j: (bi, 0, j))
    return pl.pallas_call(
        _short_conv_kernel,
        grid=(bsz, nc),
        in_specs=[zs(0), zs(1), zs(2), ws(0), ws(1), ws(2), bs(0), bs(1), bs(2)],
        out_specs=[out, out, out],
        out_shape=[jax.ShapeDtypeStruct((bsz, l, w), BF16), jax.ShapeDtypeStruct((bsz, l, w), F32),
                   jax.ShapeDtypeStruct((bsz, l, w), F32)],
        compiler_params=_params("parallel", "parallel"),
        name="short_conv",
    )(z, z, z, conv_w, conv_w, conv_w, conv_b, conv_b, conv_b)


def _filter_kernel(z_ref, w1_ref, b1_ref, f1_ref, w2_ref, b2_ref, f2_ref, w3_ref, dl_ref, bw_ref, o_ref, a_scr):
    n = z_ref.shape[0]

    @pl.when(pl.program_id(0) == 0)
    def _():
        a = jnp.sin(f1_ref[...] * (jnp.dot(z_ref[...], w1_ref[...], precision=HIGHEST,
                                           preferred_element_type=F32) + b1_ref[...]))
        a_scr[...] = jnp.sin(f2_ref[...] * (jnp.dot(a, w2_ref[...], precision=HIGHEST,
                                                    preferred_element_type=F32) + b2_ref[...]))

    hf = jnp.dot(a_scr[...], w3_ref[...], precision=HIGHEST, preferred_element_type=F32)
    row = lax.broadcasted_iota(jnp.int32, (n, 1), 0)
    t = row.astype(F32) * (1.0 / (n - 1))
    hf = hf * jnp.exp(-t * dl_ref[...])
    o_ref[...] = jnp.where((row == 0) & (bw_ref[...] > 0.5), 0.0, hf)


def _hyena_filters(z, w1, b1, f1, w2, b2, f2, w3, deltas, is_bwd, tn):
    n = z.shape[0]
    cols = w3.shape[1]
    full = lambda a: pl.BlockSpec(a.shape, lambda j: (0, 0))
    tile = lambda r: pl.BlockSpec((r, tn), lambda j: (0, j))
    return pl.pallas_call(
        _filter_kernel,
        grid=(cols // tn,),
        in_specs=[full(z), full(w1), full(b1), full(f1), full(w2), full(b2), full(f2),
                  tile(w3.shape[0]), tile(1), tile(1)],
        out_specs=tile(n),
        out_shape=jax.ShapeDtypeStruct((n, cols), F32),
        scratch_shapes=[pltpu.VMEM((n, LANES), F32)],
        compiler_params=_params("arbitrary"),
        name="hyena_filters",
    )(z, w1, b1, f1, w2, b2, f2, w3, deltas, is_bwd)


def _dft_tables(n):
    f = jnp.arange(n, dtype=jnp.int32)[:, None]
    s = jnp.arange(n, dtype=jnp.int32)[None, :]
    idx = ((2 * f + 1) * s) % (4 * n)
    ang = idx.astype(F32) * (math.pi / (2 * n))
    c = jnp.cos(ang).astype(BF16)
    sn = jnp.sin(ang).astype(BF16)
    return c, sn, c.T, sn.T


def _dft_filter_kernel(c_ref, s_ref, hf_ref, hb_ref, kr_ref, ki_ref, sum_scr, dif_scr, *, scale):
    @pl.when(pl.program_id(2) == 0)
    def _():
        hf = hf_ref[...]
        hb = hb_ref[...]
        sum_scr[...] = (hf + hb).astype(BF16)
        dif_scr[...] = (hb - hf).astype(BF16)

    kr_ref[...] = _dot(c_ref[...], sum_scr[...]) * scale
    ki_ref[...] = _dot(s_ref[...], dif_scr[...]) * scale


def _dft_filter(c, s, filt, width, tf, tn):
    n = c.shape[0]
    nc = width // tn
    out = pl.BlockSpec((None, tf, tn), lambda o, j, i: (o, i, j))
    return pl.pallas_call(
        functools.partial(_dft_filter_kernel, scale=1.0 / n),
        grid=(HYENA_ORDER, nc, n // tf),
        in_specs=[pl.BlockSpec((tf, n), lambda o, j, i: (i, 0)),
                  pl.BlockSpec((tf, n), lambda o, j, i: (i, 0)),
                  pl.BlockSpec((n, tn), lambda o, j, i: (0, (2 * o) * nc + j)),
                  pl.BlockSpec((n, tn), lambda o, j, i: (0, (2 * o + 1) * nc + j))],
        out_specs=[out, out],
        out_shape=[jax.ShapeDtypeStruct((HYENA_ORDER, n, width), F32)] * 2,
        scratch_shapes=[pltpu.VMEM((n, tn), BF16), pltpu.VMEM((n, tn), BF16)],
        compiler_params=_params("parallel", "parallel", "arbitrary"),
        name="dft_filter",
    )(c, s, filt, filt)


def _dft_fwd_kernel(c_ref, s_ref, u_ref, kr_ref, ki_ref, pr_ref, pi_ref):
    u = u_ref[...]
    ur = _dot(c_ref[...], u)
    us = _dot(s_ref[...], u)
    kr = kr_ref[...]
    ki = ki_ref[...]
    pr_ref[...] = (ur * kr + us * ki).astype(pr_ref.dtype)
    pi_ref[...] = (ur * ki - us * kr).astype(pi_ref.dtype)


def _dft_fwd(c, s, u, kr, ki, order, tf, tn):
    bsz, n, width = u.shape
    out = pl.BlockSpec((None, tf, tn), lambda bi, j, i: (bi, i, j))
    ks = pl.BlockSpec((None, tf, tn), lambda bi, j, i: (order, i, j))
    return pl.pallas_call(
        _dft_fwd_kernel,
        grid=(bsz, width // tn, n // tf),
        in_specs=[pl.BlockSpec((tf, n), lambda bi, j, i: (i, 0)),
                  pl.BlockSpec((tf, n), lambda bi, j, i: (i, 0)),
                  pl.BlockSpec((None, n, tn), lambda bi, j, i: (bi, 0, j)),
                  ks, ks],
        out_specs=[out, out],
        out_shape=[jax.ShapeDtypeStruct((bsz, n, width), BF16)] * 2,
        compiler_params=_params("parallel", "parallel", "parallel"),
        name="dft_fwd",
    )(c, s, u, kr, ki)


def _dft_inv_kernel(ct_ref, st_ref, pr_ref, pi_ref, u_ref, b_ref, x_ref, o_ref):
    y = _dot(ct_ref[...], pr_ref[...]) - _dot(st_ref[...], pi_ref[...])
    u = u_ref[...].astype(F32)
    o_ref[...] = (x_ref[...] * (y + u * b_ref[...])).astype(o_ref.dtype)


def _dft_inv(ct, st, pr, pi, u, bias, x, order, tt, tn):
    bsz, n, width = u.shape
    tile = pl.BlockSpec((None, tt, tn), lambda bi, j, i: (bi, i, j))
    spec = pl.BlockSpec((None, n, tn), lambda bi, j, i: (bi, 0, j))
    return pl.pallas_call(
        _dft_inv_kernel,
        grid=(bsz, width // tn, n // tt),
        in_specs=[pl.BlockSpec((tt, n), lambda bi, j, i: (i, 0)),
                  pl.BlockSpec((tt, n), lambda bi, j, i: (i, 0)),
                  spec, spec, tile,
                  pl.BlockSpec((None, 1, tn), lambda bi, j, i: (order, 0, j)),
                  tile],
        out_specs=tile,
        out_shape=jax.ShapeDtypeStruct((bsz, n, width), BF16),
        compiler_params=_params("parallel", "parallel", "parallel"),
        name="dft_inv",
    )(ct, st, pr, pi, u, bias, x)


def _filter_features(n):
    t = jnp.linspace(0.0, 1.0, n, dtype=F32)[:, None]
    w = 2.0 * math.pi * jnp.arange(n, dtype=F32) / n
    f = jnp.linspace(1e-4, FILTER_BANDS - 1, FILTER_BANDS, dtype=F32)
    z = jnp.concatenate([t, jnp.cos(w[:, None] * f), -jnp.sin(w[:, None] * f)], axis=-1)
    return jnp.pad(z, ((0, 0), (0, LANES - FILTER_EMB)))


def _pad_to(a, shape):
    return jnp.pad(a, [(0, t - s) for s, t in zip(a.shape, shape)])


def kernel(x, c, ctx, c_ctx, ada_w, ada_b, norm1_g, norm2_g, attn_w_in, attn_w_out, diff_q_g, diff_k_g, diff_lq1, diff_lk1, diff_lq2, diff_lk2, diff_sub_g, swa_q_g, swa_k_g, swa_sink, hy_w_in, hy_b_in, hy_conv_w, hy_conv_b, flt_w1, flt_b1, flt_f1, flt_w2, flt_b2, flt_f2, flt_w3, hy_bias, hy_w_out, hy_b_out, moe_wg1, moe_bg1, moe_wg2, moe_bg2, moe_w_gate, moe_w_up, moe_w_down):
    bsz, seq, d = x.shape
    depth = ada_w.shape[0]
    n_ctx = ctx.shape[1]
    cond = jnp.concatenate([c, c_ctx[None, :], jnp.zeros((8 - bsz - 1, d), F32)], axis=0)
    ada_b3 = ada_b[:, None, :]
    zero_bias = lambda n: jnp.zeros((1, n), F32)
    wg_all = moe_w_gate.astype(BF16).reshape(depth * MOE_EXPERTS, d, -1)
    wu_all = moe_w_up.astype(BF16).reshape(depth * MOE_EXPERTS, d, -1)
    wd_all = moe_w_down.astype(BF16).reshape(depth * MOE_EXPERTS, -1, d)
    tm = min(512, seq)

    for layer in range(depth):
        even = layer % 2 == 0
        i = layer // 2
        ctx_live = any(j % 2 == 0 for j in range(layer + 1, depth))
        assert not ctx_live, "context-stream update is only needed for deeper stacks"
        mod = _adaln_mod(cond, ada_w, ada_b3, layer)
        sh1, sc1, g1, sh2, sc2, g2 = [mod[:, k * d:(k + 1) * d] for k in range(6)]
        lat = lambda a: a[:bsz, None, :]
        cmod = lambda a: jnp.broadcast_to(a[bsz:bsz + 1, None, :], (bsz, 1, d))
        n1 = norm1_g[layer][None, :]
        if even:
            w_in = attn_w_in[i].astype(BF16)
            qkv = _norm_linear(x, n1, lat(sc1), lat(sh1), w_in, zero_bias(ATTN_IN), tm, 1536, "attn_in")
            qkv_c = _norm_linear(ctx, n1, cmod(sc1), cmod(sh1), w_in, zero_bias(ATTN_IN), min(tm, n_ctx), 1536,
                                 "attn_in_ctx")
            lam_init = 0.8 - 0.6 * math.exp(-0.3 * layer)
            lamp = _pad_to(jnp.stack([diff_lq1[i], diff_lk1[i], diff_lq2[i], diff_lk2[i]]), (8, 2 * DIFF_SUB))
            two = lambda a: jnp.tile(a, 2)[None, :]
            o_diff = _diff_attention(qkv, qkv_c, _rope_tables(seq, DIFF_SUB, 2 * DIFF_SUB), two(diff_q_g[i]),
                                     two(diff_k_g[i]), diff_sub_g[i][None, :], lamp, lam_init, 128)
            sink_rows = jnp.broadcast_to(swa_sink[i][:, None], (SWA_HEADS, HEAD_DIM))
            o_swa = _swa_attention(qkv, qkv_c, _rope_tables(seq, HEAD_DIM, HEAD_DIM), swa_q_g[i][None, :],
                                   swa_k_g[i][None, :], sink_rows)
            w_out = attn_w_out[i].astype(BF16)
            x = _mm_residual([o_diff, o_swa], [w_out[:DIFF_V], w_out[DIFF_V:]], zero_bias(d), x, lat(g1), tm,
                             "attn_out")
        else:
            width = hy_w_out.shape[1]
            z = _norm_linear(x, n1, lat(sc1), lat(sh1), hy_w_in[i].astype(BF16), hy_b_in[i][None, :], tm, 1536,
                             "hyena_in")
            v, x1, x2 = _short_conv(z, _pad_to(hy_conv_w[i], (8, 3 * width)), hy_conv_b[i][None, :], 128)
            max_decay = math.log(DECAY_TARGET) / DECAY_FAST
            min_decay = math.log(DECAY_TARGET) / DECAY_SLOW
            deltas = jnp.abs(jnp.tile(jnp.linspace(min_decay, max_decay, width, dtype=F32), 2 * HYENA_ORDER))[None, :]
            is_bwd = jnp.tile(jnp.concatenate([jnp.zeros((width,), F32), jnp.ones((width,), F32)]),
                              HYENA_ORDER)[None, :]
            sq = (LANES, LANES)
            filt = _hyena_filters(
                _filter_features(seq), _pad_to(flt_w1[i], sq), _pad_to(flt_b1[i][None, :], (1, LANES)),
                _pad_to(flt_f1[i][None, :], (1, LANES)), _pad_to(flt_w2[i], sq),
                _pad_to(flt_b2[i][None, :], (1, LANES)), _pad_to(flt_f2[i][None, :], (1, LANES)),
                _pad_to(flt_w3[i], (LANES, 2 * HYENA_ORDER * width)), deltas, is_bwd, 256)
            cm, sm, ct, st = _dft_tables(seq)
            tf = min(512, seq)
            tn = min(512, width)
            kr, ki = _dft_filter(cm, sm, filt, width, tf, min(256, width))
            bias = hy_bias[i][:, None, :]
            pr, pi = _dft_fwd(cm, sm, v, kr, ki, 0, tf, tn)
            y1 = _dft_inv(ct, st, pr, pi, v, bias, x1, 0, tf, tn)
            pr, pi = _dft_fwd(cm, sm, y1, kr, ki, 1, tf, tn)
            y2 = _dft_inv(ct, st, pr, pi, y1, bias, x2, 1, tf, tn)
            x = _mm_residual([y2], [hy_w_out[i].astype(BF16)], hy_b_out[i][None, :], x, lat(g1), tm, "hyena_out")
        w_route = _pad_to(jnp.concatenate([moe_wg1[layer], moe_wg2[layer]], axis=1), (d, LANES))
        b_route = _pad_to(jnp.concatenate([moe_bg1[layer], moe_bg2[layer]])[None, :], (1, LANES))
        x = _hier_moe(x, norm2_g[layer][None, :], lat(sc2), lat(sh2), lat(g2), w_route, b_route,
                      wg_all, wu_all, wd_all, layer)
    return x
```

```python
import functools
import math

import jax
import jax.numpy as jnp
from jax import lax
from jax.experimental import pallas as pl
from jax.experimental.pallas import tpu as pltpu

F32 = jnp.float32
BF16 = jnp.bfloat16
HIGHEST = lax.Precision.HIGHEST

LANES = 128
V7X_VMEM_BYTES = 64 * 1024 * 1024
VMEM_LIMIT = V7X_VMEM_BYTES * 7 // 8

GRID_W = 64
HEAD_DIM = 128
ROPE_BASE = 10000.0
EPS = 1e-6
Q_BLOCK = 128
DIFF_HEADS = 8
DIFF_SUB = HEAD_DIM // 2
SWA_HEADS = 8
SWA_KV_HEADS = 2
SWA_GROUP = SWA_HEADS // SWA_KV_HEADS
WINDOW = 128
DIFF_Q = DIFF_HEADS * 2 * DIFF_SUB
DIFF_V = DIFF_HEADS * HEAD_DIM
SWA_Q = SWA_HEADS * HEAD_DIM
SWA_KV = SWA_KV_HEADS * HEAD_DIM
ATTN_IN = 2 * DIFF_Q + DIFF_V + SWA_Q + 2 * SWA_KV
HYENA_ORDER = 2
SHORT_CONV = 3
FILTER_EMB = 33
FILTER_BANDS = (FILTER_EMB - 1) // 2
FILTER_HIDDEN = 64
DECAY_FAST = 0.3
DECAY_SLOW = 1.5
DECAY_TARGET = 1e-2
MOE_GROUPS = 4
MOE_PER_GROUP = 8
MOE_EXPERTS = MOE_GROUPS * MOE_PER_GROUP
MOE_TOPK = 2
MOE_BLOCK = 256
FF_CHUNK = 256
NEG_INF = float("-inf")


def _tile(n, pref):
    t = pref
    while n % t:
        t //= 2
    return t


def _params(*sem):
    return pltpu.CompilerParams(dimension_semantics=sem, vmem_limit_bytes=VMEM_LIMIT)


def _dot(a, b):
    return jnp.dot(a, b, preferred_element_type=F32)


def _dot_nt(a, b):
    return lax.dot_general(a, b, (((1,), (1,)), ((), ())), preferred_element_type=F32)


def _mod_kernel(c_ref, w_ref, b_ref, o_ref):
    c = c_ref[...]
    s = c * jax.nn.sigmoid(c)
    o_ref[...] = jnp.dot(s, w_ref[...], precision=HIGHEST, preferred_element_type=F32) + b_ref[...]


def _adaln_mod(cond, ada_w, ada_b, layer):
    rows, d = cond.shape
    n = ada_w.shape[2]
    tn = _tile(n, 768)
    return pl.pallas_call(
        _mod_kernel,
        grid=(n // tn,),
        in_specs=[pl.BlockSpec((rows, d), lambda j: (0, 0)),
                  pl.BlockSpec((None, d, tn), lambda j: (layer, 0, j)),
                  pl.BlockSpec((None, 1, tn), lambda j: (layer, 0, j))],
        out_specs=pl.BlockSpec((rows, tn), lambda j: (0, j)),
        out_shape=jax.ShapeDtypeStruct((rows, n), F32),
        compiler_params=_params("parallel"),
        name="adaln_mod",
    )(cond, ada_w, ada_b)


def _norm_mod(x, g, sc, sh):
    inv = lax.rsqrt(jnp.mean(x * x, axis=-1, keepdims=True) + EPS)
    return ((x * inv) * g) * (1.0 + sc) + sh


def _norm_linear_kernel(x_ref, g_ref, sc_ref, sh_ref, w_ref, b_ref, o_ref, h_ref):
    @pl.when(pl.program_id(2) == 0)
    def _():
        h_ref[...] = _norm_mod(x_ref[...], g_ref[...], sc_ref[...], sh_ref[...]).astype(BF16)

    o_ref[...] = (_dot(h_ref[...], w_ref[...]) + b_ref[...]).astype(o_ref.dtype)


def _norm_linear(x, g, sc, sh, w, b, tm, tn, name):
    bsz, l, d = x.shape
    n = w.shape[1]
    tn = _tile(n, tn)
    return pl.pallas_call(
        _norm_linear_kernel,
        grid=(bsz, l // tm, n // tn),
        in_specs=[pl.BlockSpec((None, tm, d), lambda bi, i, j: (bi, i, 0)),
                  pl.BlockSpec((1, d), lambda bi, i, j: (0, 0)),
                  pl.BlockSpec((None, 1, d), lambda bi, i, j: (bi, 0, 0)),
                  pl.BlockSpec((None, 1, d), lambda bi, i, j: (bi, 0, 0)),
                  pl.BlockSpec((d, tn), lambda bi, i, j: (0, j)),
                  pl.BlockSpec((1, tn), lambda bi, i, j: (0, j))],
        out_specs=pl.BlockSpec((None, tm, tn), lambda bi, i, j: (bi, i, j)),
        out_shape=jax.ShapeDtypeStruct((bsz, l, n), F32),
        scratch_shapes=[pltpu.VMEM((tm, d), BF16)],
        compiler_params=_params("parallel", "arbitrary", "arbitrary"),
        name=name,
    )(x, g, sc, sh, w, b)


def _mm_res_kernel(*refs, n_pairs):
    a_refs = refs[:n_pairs]
    w_refs = refs[n_pairs:2 * n_pairs]
    b_ref, x_ref, g_ref, o_ref = refs[2 * n_pairs:]
    acc = _dot(a_refs[0][...], w_refs[0][...])
    for a_ref, w_ref in zip(a_refs[1:], w_refs[1:]):
        acc = acc + _dot(a_ref[...], w_ref[...])
    o_ref[...] = x_ref[...] + g_ref[...] * (acc + b_ref[...])


def _mm_residual(a_list, w_list, b, x, g, tm, name):
    bsz, l, n = x.shape
    n_pairs = len(a_list)
    in_specs = [pl.BlockSpec((None, tm, a.shape[2]), lambda bi, i: (bi, i, 0)) for a in a_list]
    in_specs += [pl.BlockSpec(w.shape, lambda bi, i: (0, 0)) for w in w_list]
    in_specs += [pl.BlockSpec((1, n), lambda bi, i: (0, 0)),
                 pl.BlockSpec((None, tm, n), lambda bi, i: (bi, i, 0)),
                 pl.BlockSpec((None, 1, n), lambda bi, i: (bi, 0, 0))]
    return pl.pallas_call(
        functools.partial(_mm_res_kernel, n_pairs=n_pairs),
        grid=(bsz, l // tm),
        in_specs=in_specs,
        out_specs=pl.BlockSpec((None, tm, n), lambda bi, i: (bi, i, 0)),
        out_shape=jax.ShapeDtypeStruct((bsz, l, n), F32),
        compiler_params=_params("parallel", "parallel"),
        name=name,
    )(*a_list, *w_list, b, x, g)


def _rope_tables(n_tokens, dim, width):
    n_rows = n_tokens // GRID_W
    row = jnp.repeat(jnp.arange(n_rows, dtype=F32), GRID_W)
    col = jnp.tile(jnp.arange(GRID_W, dtype=F32), n_rows)
    n_freq = dim // 4
    inv = ROPE_BASE ** (-jnp.arange(n_freq, dtype=F32) / n_freq)
    ang = jnp.concatenate([row[:, None] * inv, col[:, None] * inv], axis=-1)
    ang = jnp.repeat(ang, 2, axis=-1)
    ang = jnp.tile(ang, (1, width // dim))
    sign = jnp.where(jnp.arange(width) % 2 == 0, -1.0, 1.0).astype(F32)
    return jnp.cos(ang), jnp.sin(ang) * sign


def _rope(x, cos_t, sin_t):
    width = x.shape[-1]
    lane = lax.broadcasted_iota(jnp.int32, (1, width), 1)
    partner = jnp.where(lane % 2 == 0, pltpu.roll(x, width - 1, 1), pltpu.roll(x, 1, 1))
    return x * cos_t + partner * sin_t


def _sub_rms(x, g):
    lane = lax.broadcasted_iota(jnp.int32, (1, 2 * DIFF_SUB), 1)
    lo = lane < DIFF_SUB
    sq = x * x
    ss_lo = jnp.sum(jnp.where(lo, sq, 0.0), axis=-1, keepdims=True)
    ss_hi = jnp.sum(jnp.where(lo, 0.0, sq), axis=-1, keepdims=True)
    inv = jnp.where(lo, lax.rsqrt(ss_lo / DIFF_SUB + EPS), lax.rsqrt(ss_hi / DIFF_SUB + EPS))
    return (x * inv) * g


def _rms(x, g):
    return (x * lax.rsqrt(jnp.mean(x * x, axis=-1, keepdims=True) + EPS)) * g


def _diff_attn_kernel(q_ref, k_ref, v_ref, kc_ref, vc_ref, cq_ref, sq_ref, ck_ref, sk_ref,
                      gq_ref, gk_ref, gs_ref, lamp_ref, o_ref, k_scr, v_scr, lam_scr, *, lam_init, n_ctx):
    lane = lax.broadcasted_iota(jnp.int32, (1, 2 * DIFF_SUB), 1)
    lo = lane < DIFF_SUB

    @pl.when(pl.program_id(2) == 0)
    def _():
        gk = gk_ref[...]
        k_scr[0:n_ctx, :] = _sub_rms(kc_ref[...], gk).astype(BF16)
        k_scr[n_ctx:, :] = _rope(_sub_rms(k_ref[...], gk), ck_ref[...], sk_ref[...]).astype(BF16)
        v_scr[0:n_ctx, :] = vc_ref[...].astype(BF16)
        v_scr[n_ctx:, :] = v_ref[...].astype(BF16)
        lp = lamp_ref[...]
        t1 = jnp.sum(lp[0:1] * lp[1:2], axis=-1, keepdims=True)
        t2 = jnp.sum(lp[2:3] * lp[3:4], axis=-1, keepdims=True)
        lam_scr[...] = jnp.broadcast_to(jnp.exp(t1) - jnp.exp(t2) + lam_init, lam_scr.shape)

    q = _rope(_sub_rms(q_ref[...], gq_ref[...]), cq_ref[...], sq_ref[...]) * (DIFF_SUB ** -0.5)
    q1 = jnp.where(lo, q, 0.0).astype(BF16)
    q2 = jnp.where(lo, 0.0, q).astype(BF16)
    kk = k_scr[...]
    s1 = _dot_nt(q1, kk)
    s2 = _dot_nt(q2, kk)
    e1 = jnp.exp(s1 - jnp.max(s1, axis=-1, keepdims=True))
    e2 = jnp.exp(s2 - jnp.max(s2, axis=-1, keepdims=True))
    r1 = 1.0 / jnp.sum(e1, axis=-1, keepdims=True)
    r2 = lam_scr[:, 0:1] / jnp.sum(e2, axis=-1, keepdims=True)
    a = e1 * r1 - e2 * r2
    o = _dot(a.astype(BF16), v_scr[...])
    o_ref[...] = (_rms(o, gs_ref[...]) * (1.0 - lam_init)).astype(o_ref.dtype)


def _diff_attention(qkv, qkv_c, tabs, gq, gk, gs, lamp, lam_init, tq):
    bsz, l, _ = qkv.shape
    n_ctx = qkv_c.shape[1]
    cos_t, sin_t = tabs
    hb = 2 * DIFF_SUB
    kb = DIFF_Q // hb
    vb = 2 * DIFF_Q // hb
    kern = functools.partial(_diff_attn_kernel, lam_init=lam_init, n_ctx=n_ctx)
    vec = pl.BlockSpec((1, hb), lambda bi, h, i: (0, 0))
    return pl.pallas_call(
        kern,
        grid=(bsz, DIFF_HEADS, l // tq),
        in_specs=[pl.BlockSpec((None, tq, hb), lambda bi, h, i: (bi, i, h)),
                  pl.BlockSpec((None, l, hb), lambda bi, h, i: (bi, 0, kb + h)),
                  pl.BlockSpec((None, l, hb), lambda bi, h, i: (bi, 0, vb + h)),
                  pl.BlockSpec((None, n_ctx, hb), lambda bi, h, i: (bi, 0, kb + h)),
                  pl.BlockSpec((None, n_ctx, hb), lambda bi, h, i: (bi, 0, vb + h)),
                  pl.BlockSpec((tq, hb), lambda bi, h, i: (i, 0)),
                  pl.BlockSpec((tq, hb), lambda bi, h, i: (i, 0)),
                  pl.BlockSpec((l, hb), lambda bi, h, i: (0, 0)),
                  pl.BlockSpec((l, hb), lambda bi, h, i: (0, 0)),
                  vec, vec, vec,
                  pl.BlockSpec((8, hb), lambda bi, h, i: (0, 0))],
        out_specs=pl.BlockSpec((None, tq, hb), lambda bi, h, i: (bi, i, h)),
        out_shape=jax.ShapeDtypeStruct((bsz, l, DIFF_V), BF16),
        scratch_shapes=[pltpu.VMEM((n_ctx + l, hb), BF16), pltpu.VMEM((n_ctx + l, hb), BF16),
                        pltpu.VMEM((1, hb), F32)],
        compiler_params=_params("parallel", "parallel", "arbitrary"),
        name="diff_attention",
    )(qkv, qkv, qkv, qkv_c, qkv_c, cos_t, sin_t, cos_t, sin_t, gq, gk, gs, lamp)


def _swa_kernel(q_ref, k_ref, v_ref, kc_ref, vc_ref, cq_ref, sq_ref, ck_ref, sk_ref, gq_ref, gk_ref, sink_ref,
                o_ref, k_scr, v_scr, kc_scr, vc_scr, *, seq):
    kvh = pl.program_id(1)
    n = pl.program_id(2)
    span = Q_BLOCK + 2 * WINDOW

    @pl.when(n == 0)
    def _():
        gk = gk_ref[...]
        zeros = jnp.zeros((WINDOW, HEAD_DIM), BF16)
        k_scr[0:WINDOW, :] = zeros
        v_scr[0:WINDOW, :] = zeros
        k_scr[WINDOW + seq:, :] = zeros
        v_scr[WINDOW + seq:, :] = zeros
        k_scr[WINDOW:WINDOW + seq, :] = _rope(_rms(k_ref[...], gk), ck_ref[...], sk_ref[...]).astype(BF16)
        v_scr[WINDOW:WINDOW + seq, :] = v_ref[...].astype(BF16)
        kc_scr[...] = _rms(kc_ref[...], gk).astype(BF16)
        vc_scr[...] = vc_ref[...].astype(BF16)

    start = pl.multiple_of(n * Q_BLOCK, Q_BLOCK)
    kw = k_scr[pl.ds(start, span), :]
    vw = v_scr[pl.ds(start, span), :]
    kc = kc_scr[...]
    vc = vc_scr[...]
    qidx = lax.broadcasted_iota(jnp.int32, (Q_BLOCK, span), 0)
    kidx = lax.broadcasted_iota(jnp.int32, (Q_BLOCK, span), 1)
    rel = kidx - qidx
    key_pos = n * Q_BLOCK + kidx - WINDOW
    mask = (rel >= 0) & (rel <= 2 * WINDOW) & (key_pos >= 0) & (key_pos < seq)
    gq = gq_ref[...]
    cq = cq_ref[...]
    sq = sq_ref[...]
    outs = []
    for j in range(SWA_GROUP):
        q = _rope(_rms(q_ref[:, j * HEAD_DIM:(j + 1) * HEAD_DIM], gq), cq, sq) * (HEAD_DIM ** -0.5)
        q = q.astype(BF16)
        s_loc = jnp.where(mask, _dot_nt(q, kw), NEG_INF)
        s_ctx = _dot_nt(q, kc)
        sink = sink_ref[pl.ds(kvh * SWA_GROUP + j, 1), 0:1]
        m = jnp.maximum(jnp.maximum(jnp.max(s_loc, axis=-1, keepdims=True),
                                    jnp.max(s_ctx, axis=-1, keepdims=True)), sink)
        e_loc = jnp.exp(s_loc - m)
        e_ctx = jnp.exp(s_ctx - m)
        denom = (jnp.sum(e_loc, axis=-1, keepdims=True) + jnp.sum(e_ctx, axis=-1, keepdims=True)
                 + jnp.exp(sink - m))
        r = 1.0 / denom
        o = _dot((e_ctx * r).astype(BF16), vc) + _dot((e_loc * r).astype(BF16), vw)
        outs.append(o)
    o_ref[...] = jnp.concatenate(outs, axis=-1).astype(o_ref.dtype)


def _swa_attention(qkv, qkv_c, tabs, gq, gk, sink_rows):
    bsz, l, _ = qkv.shape
    n_ctx = qkv_c.shape[1]
    cos_t, sin_t = tabs
    gw = SWA_GROUP * HEAD_DIM
    q0 = (2 * DIFF_Q + DIFF_V) // gw
    k0 = (2 * DIFF_Q + DIFF_V + SWA_Q) // HEAD_DIM
    v0 = k0 + SWA_KV_HEADS
    vec = pl.BlockSpec((1, HEAD_DIM), lambda bi, h, i: (0, 0))
    return pl.pallas_call(
        functools.partial(_swa_kernel, seq=l),
        grid=(bsz, SWA_KV_HEADS, l // Q_BLOCK),
        in_specs=[pl.BlockSpec((None, Q_BLOCK, gw), lambda bi, h, i: (bi, i, q0 + h)),
                  pl.BlockSpec((None, l, HEAD_DIM), lambda bi, h, i: (bi, 0, k0 + h)),
                  pl.BlockSpec((None, l, HEAD_DIM), lambda bi, h, i: (bi, 0, v0 + h)),
                  pl.BlockSpec((None, n_ctx, HEAD_DIM), lambda bi, h, i: (bi, 0, k0 + h)),
                  pl.BlockSpec((None, n_ctx, HEAD_DIM), lambda bi, h, i: (bi, 0, v0 + h)),
                  pl.BlockSpec((Q_BLOCK, HEAD_DIM), lambda bi, h, i: (i, 0)),
                  pl.BlockSpec((Q_BLOCK, HEAD_DIM), lambda bi, h, i: (i, 0)),
                  pl.BlockSpec((l, HEAD_DIM), lambda bi, h, i: (0, 0)),
                  pl.BlockSpec((l, HEAD_DIM), lambda bi, h, i: (0, 0)),
                  vec, vec,
                  pl.BlockSpec((SWA_HEADS, HEAD_DIM), lambda bi, h, i: (0, 0))],
        out_specs=pl.BlockSpec((None, Q_BLOCK, gw), lambda bi, h, i: (bi, i, h)),
        out_shape=jax.ShapeDtypeStruct((bsz, l, SWA_Q), BF16),
        scratch_shapes=[pltpu.VMEM((l + 2 * WINDOW, HEAD_DIM), BF16), pltpu.VMEM((l + 2 * WINDOW, HEAD_DIM), BF16),
                        pltpu.VMEM((n_ctx, HEAD_DIM), BF16), pltpu.VMEM((n_ctx, HEAD_DIM), BF16)],
        compiler_params=_params("parallel", "parallel", "arbitrary"),
        name="swa_attention",
    )(qkv, qkv, qkv, qkv_c, qkv_c, cos_t, sin_t, cos_t, sin_t, gq, gk, sink_rows)


def _router_kernel(x_ref, g_ref, sc_ref, sh_ref, w_ref, b_ref, tri_ref, h_ref, e_ref, p_ref, cnt_ref, run_scr):
    @pl.when((pl.program_id(0) == 0) & (pl.program_id(1) == 0))
    def _():
        run_scr[...] = jnp.zeros(run_scr.shape, F32)

    h = _norm_mod(x_ref[...], g_ref[...], sc_ref[...], sh_ref[...])
    h_ref[...] = h
    logits = jnp.dot(h, w_ref[...], precision=HIGHEST, preferred_element_type=F32) + b_ref[...]
    lane_i = lax.broadcasted_iota(jnp.int32, (1, LANES), 1)
    lane = lane_i.astype(F32)
    big = float(LANES)
    lg = jnp.where(lane_i < MOE_GROUPS, logits, NEG_INF)
    mg = jnp.max(lg, axis=-1, keepdims=True)
    p_top = 1.0 / jnp.sum(jnp.exp(lg - mg), axis=-1, keepdims=True)
    grp = jnp.min(jnp.where(lg == mg, lane, big), axis=-1, keepdims=True)
    e_lane = lane - MOE_GROUPS
    in_grp = (e_lane >= grp * MOE_PER_GROUP) & (e_lane < (grp + 1.0) * MOE_PER_GROUP)
    le = jnp.where(in_grp, logits, NEG_INF)
    t1 = jnp.max(le, axis=-1, keepdims=True)
    i1 = jnp.min(jnp.where(le == t1, lane, big), axis=-1, keepdims=True)
    le2 = jnp.where(lane == i1, NEG_INF, le)
    t2 = jnp.max(le2, axis=-1, keepdims=True)
    i2 = jnp.min(jnp.where(le2 == t2, lane, big), axis=-1, keepdims=True)
    w2 = jnp.exp(t2 - t1)
    inv = p_top / (1.0 + w2)
    hot1 = lane == i1
    hot2 = lane == i2
    onehot = jnp.where(hot1 | hot2, 1.0, 0.0)
    before = run_scr[...] + _dot(tri_ref[...], onehot.astype(BF16))
    r1 = jnp.sum(jnp.where(hot1, before, 0.0), axis=-1, keepdims=True)
    r2 = jnp.sum(jnp.where(hot2, before, 0.0), axis=-1, keepdims=True)
    run_scr[...] = run_scr[...] + jnp.sum(onehot, axis=0, keepdims=True)
    cnt_ref[...] = run_scr[...]
    e_ref[...] = jnp.where(lane_i == 0, i1 - MOE_GROUPS,
                           jnp.where(lane_i == 1, i2 - MOE_GROUPS,
                                     jnp.where(lane_i == 2, r1, jnp.where(lane_i == 3, r2, 0.0)))).astype(jnp.int32)
    p_ref[...] = jnp.where(lane_i == 0, inv, jnp.where(lane_i == 1, inv * w2, 0.0))


def _router(x, g, sc, sh, w, b, tm):
    bsz, l, d = x.shape
    row = lambda bi, i: (bi, i, 0)
    tri = (jnp.arange(tm)[:, None] > jnp.arange(tm)[None, :]).astype(BF16)
    return pl.pallas_call(
        _router_kernel,
        grid=(bsz, l // tm),
        in_specs=[pl.BlockSpec((None, tm, d), row),
                  pl.BlockSpec((1, d), lambda bi, i: (0, 0)),
                  pl.BlockSpec((None, 1, d), lambda bi, i: (bi, 0, 0)),
                  pl.BlockSpec((None, 1, d), lambda bi, i: (bi, 0, 0)),
                  pl.BlockSpec((d, LANES), lambda bi, i: (0, 0)),
                  pl.BlockSpec((1, LANES), lambda bi, i: (0, 0)),
                  pl.BlockSpec((tm, tm), lambda bi, i: (0, 0))],
        out_specs=[pl.BlockSpec((None, tm, d), row),
                   pl.BlockSpec((None, tm, LANES), row),
                   pl.BlockSpec((None, tm, LANES), row),
                   pl.BlockSpec((1, LANES), lambda bi, i: (0, 0))],
        out_shape=[jax.ShapeDtypeStruct((bsz, l, d), F32),
                   jax.ShapeDtypeStruct((bsz, l, LANES), jnp.int32),
                   jax.ShapeDtypeStruct((bsz, l, LANES), F32),
                   jax.ShapeDtypeStruct((1, LANES), F32)],
        scratch_shapes=[pltpu.VMEM((1, LANES), F32)],
        compiler_params=_params("arbitrary", "arbitrary"),
        name="moe_router",
    )(x, g, sc, sh, w, b, tri)


def _dispatch_kernel(slot_ref, h_ref, init_ref, xb_ref, sem):
    del init_ref
    tm = h_ref.shape[0]
    base = pl.program_id(0) * tm

    def copy(r, k):
        dst = slot_ref[(base + r) * MOE_TOPK + k]
        return pltpu.make_async_copy(h_ref.at[pl.ds(r, 1)], xb_ref.at[pl.ds(dst, 1)], sem)

    def start(r, carry):
        for k in range(MOE_TOPK):
            copy(r, k).start()
        return carry

    def wait(r, carry):
        for k in range(MOE_TOPK):
            copy(r, k).wait()
        return carry

    lax.fori_loop(0, tm, start, 0)
    lax.fori_loop(0, tm, wait, 0)


def _dispatch(slot, h, n_rows, tm):
    n, d = h.shape
    grid_spec = pltpu.PrefetchScalarGridSpec(
        num_scalar_prefetch=1,
        grid=(n // tm,),
        in_specs=[pl.BlockSpec((tm, d), lambda i, s: (i, 0)),
                  pl.BlockSpec(memory_space=pl.ANY)],
        out_specs=pl.BlockSpec(memory_space=pl.ANY),
        scratch_shapes=[pltpu.SemaphoreType.DMA(())],
    )
    return pl.pallas_call(
        _dispatch_kernel,
        grid_spec=grid_spec,
        out_shape=jax.ShapeDtypeStruct((n_rows, d), F32),
        input_output_aliases={2: 0},
        compiler_params=_params("arbitrary"),
        name="moe_dispatch",
    )(slot, h, jnp.zeros((n_rows, d), F32))


def _combine_kernel(slot_ref, x_ref, g_ref, p_ref, yb_ref, o_ref, y_scr, sem):
    tm = x_ref.shape[0]
    base = pl.program_id(0) * tm

    def copy(r, k):
        src = slot_ref[(base + r) * MOE_TOPK + k]
        return pltpu.make_async_copy(yb_ref.at[pl.ds(src, 1)], y_scr.at[k, pl.ds(r, 1)], sem)

    def start(r, carry):
        for k in range(MOE_TOPK):
            copy(r, k).start()
        return carry

    def wait(r, carry):
        for k in range(MOE_TOPK):
            copy(r, k).wait()
        return carry

    lax.fori_loop(0, tm, start, 0)
    lax.fori_loop(0, tm, wait, 0)
    p = p_ref[...]
    moe = y_scr[0] * p[:, 0:1]
    for k in range(1, MOE_TOPK):
        moe = moe + y_scr[k] * p[:, k:k + 1]
    o_ref[...] = x_ref[...] + g_ref[...] * moe


def _combine(slot, x, g, gates, yb, tm):
    bsz, l, d = x.shape
    nt = l // tm
    grid_spec = pltpu.PrefetchScalarGridSpec(
        num_scalar_prefetch=1,
        grid=(bsz * nt,),
        in_specs=[pl.BlockSpec((None, tm, d), lambda i, s: (i // nt, i % nt, 0)),
                  pl.BlockSpec((None, 1, d), lambda i, s: (i // nt, 0, 0)),
                  pl.BlockSpec((None, tm, LANES), lambda i, s: (i // nt, i % nt, 0)),
                  pl.BlockSpec(memory_space=pl.ANY)],
        out_specs=pl.BlockSpec((None, tm, d), lambda i, s: (i // nt, i % nt, 0)),
        scratch_shapes=[pltpu.VMEM((MOE_TOPK, tm, d), F32), pltpu.SemaphoreType.DMA(())],
    )
    return pl.pallas_call(
        _combine_kernel,
        grid_spec=grid_spec,
        out_shape=jax.ShapeDtypeStruct((bsz, l, d), F32),
        compiler_params=_params("arbitrary"),
        name="moe_combine",
    )(slot, x, g, gates, yb)


def _expert_kernel(be_ref, nu_ref, x_ref, wg_ref, wu_ref, wd_ref, o_ref):
    i = pl.program_id(0)

    @pl.when(i < nu_ref[0])
    def _():
        x = x_ref[...].astype(BF16)
        ff = wg_ref.shape[1]
        acc = jnp.zeros(o_ref.shape, F32)
        for c in range(ff // FF_CHUNK):
            cols = slice(c * FF_CHUNK, (c + 1) * FF_CHUNK)
            gate = _dot(x, wg_ref[:, cols])
            up = _dot(x, wu_ref[:, cols])
            hidden = (gate * jax.nn.sigmoid(gate) * up).astype(BF16)
            acc = acc + _dot(hidden, wd_ref[cols, :])
        o_ref[...] = acc

    @pl.when(i >= nu_ref[0])
    def _():
        o_ref[...] = jnp.zeros(o_ref.shape, F32)


def _expert_ffn(xb, block_e, n_used, w_gate, w_up, w_down, layer):
    n_blocks, _, d = xb.shape
    ff = w_gate.shape[2]
    base = layer * MOE_EXPERTS
    grid_spec = pltpu.PrefetchScalarGridSpec(
        num_scalar_prefetch=2,
        grid=(n_blocks,),
        in_specs=[pl.BlockSpec((None, MOE_BLOCK, d), lambda i, be, nu: (i, 0, 0)),
                  pl.BlockSpec((None, d, ff), lambda i, be, nu: (base + be[i], 0, 0)),
                  pl.BlockSpec((None, d, ff), lambda i, be, nu: (base + be[i], 0, 0)),
                  pl.BlockSpec((None, ff, d), lambda i, be, nu: (base + be[i], 0, 0))],
        out_specs=pl.BlockSpec((None, MOE_BLOCK, d), lambda i, be, nu: (i, 0, 0)),
    )
    return pl.pallas_call(
        _expert_kernel,
        grid_spec=grid_spec,
        out_shape=jax.ShapeDtypeStruct((n_blocks, MOE_BLOCK, d), F32),
        compiler_params=_params("arbitrary"),
        name="expert_ffn",
    )(block_e, n_used, xb, w_gate, w_up, w_down)


def _hier_moe(x, g, sc, sh, gate2, w_route, b_route, w_gate, w_up, w_down, layer):
    bsz, l, d = x.shape
    n = bsz * l
    tm = min(512, l)
    h, route, gates, cnt = _router(x, g, sc, sh, w_route, b_route, tm)
    route = route.reshape(n, LANES)
    expert = route[:, :MOE_TOPK]
    rank = route[:, MOE_TOPK:2 * MOE_TOPK]
    counts = cnt[0, MOE_GROUPS:MOE_GROUPS + MOE_EXPERTS].astype(jnp.int32)
    padded = (counts + MOE_BLOCK - 1) // MOE_BLOCK * MOE_BLOCK
    pad_end = jnp.cumsum(padded)
    pad_start = pad_end - padded
    hit = expert[:, :, None] == jnp.arange(MOE_EXPERTS, dtype=jnp.int32)
    slot = (jnp.sum(jnp.where(hit, pad_start, 0), axis=-1) + rank).reshape(-1)
    n_blocks = -(-(n * MOE_TOPK) // MOE_BLOCK) + MOE_EXPERTS
    block_row = jnp.arange(n_blocks, dtype=jnp.int32)[:, None] * MOE_BLOCK
    block_e = jnp.minimum(jnp.sum((pad_end[None, :] <= block_row).astype(jnp.int32), axis=-1), MOE_EXPERTS - 1)
    n_used = (pad_end[-1:] // MOE_BLOCK).astype(jnp.int32)
    xb = _dispatch(slot, h.reshape(n, d), n_blocks * MOE_BLOCK, tm).reshape(n_blocks, MOE_BLOCK, d)
    yb = _expert_ffn(xb, block_e, n_used, w_gate, w_up, w_down, layer).reshape(-1, d)
    return _combine(slot, x, gate2, gates, yb, tm)


def _short_conv_kernel(zv_ref, z1_ref, z2_ref, wv_ref, w1_ref, w2_ref, bv_ref, b1_ref, b2_ref,
                       v_ref, x1_ref, x2_ref):
    n = zv_ref.shape[0]
    row = lax.broadcasted_iota(jnp.int32, (n, 1), 0)

    def conv(z_ref, w_ref, b_ref):
        z = z_ref[...]
        w = w_ref[...]
        prev = jnp.where(row == 0, 0.0, pltpu.roll(z, 1, 0))
        nxt = jnp.where(row == n - 1, 0.0, pltpu.roll(z, n - 1, 0))
        return b_ref[...] + prev * w[0:1] + z * w[1:2] + nxt * w[2:3]

    v_ref[...] = conv(zv_ref, wv_ref, bv_ref).astype(v_ref.dtype)
    x1_ref[...] = conv(z1_ref, w1_ref, b1_ref).astype(x1_ref.dtype)
    x2_ref[...] = conv(z2_ref, w2_ref, b2_ref).astype(x2_ref.dtype)


def _short_conv(z, conv_w, conv_b, tc):
    bsz, l, w3 = z.shape
    w = w3 // 3
    nc = w // tc
    zs = lambda k: pl.BlockSpec((None, l, tc), lambda bi, j: (bi, 0, k * nc + j))
    ws = lambda k: pl.BlockSpec((8, tc), lambda bi, j: (0, k * nc + j))
    bs = lambda k: pl.BlockSpec((1, tc), lambda bi, j: (0, k * nc + j))
    out = pl.BlockSpec((None, l, tc), lambda bi, j: (bi, 0, j))
    return pl.pallas_call(
        _short_conv_kernel,
        grid=(bsz, nc),
        in_specs=[zs(0), zs(1), zs(2), ws(0), ws(1), ws(2), bs(0), bs(1), bs(2)],
        out_specs=[out, out, out],
        out_shape=[jax.ShapeDtypeStruct((bsz, l, w), BF16), jax.ShapeDtypeStruct((bsz, l, w), F32),
                   jax.ShapeDtypeStruct((bsz, l, w), F32)],
        compiler_params=_params("parallel", "parallel"),
        name="short_conv",
    )(z, z, z, conv_w, conv_w, conv_w, conv_b, conv_b, conv_b)


def _filter_kernel(z_ref, w1_ref, b1_ref, f1_ref, w2_ref, b2_ref, f2_ref, w3_ref, dl_ref, bw_ref, o_ref, a_scr):
    n = z_ref.shape[0]

    @pl.when(pl.program_id(0) == 0)
    def _():
        a = jnp.sin(f1_ref[...] * (jnp.dot(z_ref[...], w1_ref[...], precision=HIGHEST,
                                           preferred_element_type=F32) + b1_ref[...]))
        a_scr[...] = jnp.sin(f2_ref[...] * (jnp.dot(a, w2_ref[...], precision=HIGHEST,
                                                    preferred_element_type=F32) + b2_ref[...]))

    hf = jnp.dot(a_scr[...], w3_ref[...], precision=HIGHEST, preferred_element_type=F32)
    row = lax.broadcasted_iota(jnp.int32, (n, 1), 0)
    t = row.astype(F32) * (1.0 / (n - 1))
    hf = hf * jnp.exp(-t * dl_ref[...])
    o_ref[...] = jnp.where((row == 0) & (bw_ref[...] > 0.5), 0.0, hf)


def _hyena_filters(z, w1, b1, f1, w2, b2, f2, w3, deltas, is_bwd, tn):
    n = z.shape[0]
    cols = w3.shape[1]
    full = lambda a: pl.BlockSpec(a.shape, lambda j: (0, 0))
    tile = lambda r: pl.BlockSpec((r, tn), lambda j: (0, j))
    return pl.pallas_call(
        _filter_kernel,
        grid=(cols // tn,),
        in_specs=[full(z), full(w1), full(b1), full(f1), full(w2), full(b2), full(f2),
                  tile(w3.shape[0]), tile(1), tile(1)],
        out_specs=tile(n),
        out_shape=jax.ShapeDtypeStruct((n, cols), F32),
        scratch_shapes=[pltpu.VMEM((n, LANES), F32)],
        compiler_params=_params("arbitrary"),
        name="hyena_filters",
    )(z, w1, b1, f1, w2, b2, f2, w3, deltas, is_bwd)


def _dft_tables(n):
    f = jnp.arange(n, dtype=jnp.int32)[:, None]
    s = jnp.arange(n, dtype=jnp.int32)[None, :]
    idx = ((2 * f + 1) * s) % (4 * n)
    ang = idx.astype(F32) * (math.pi / (2 * n))
    c = jnp.cos(ang).astype(BF16)
    sn = jnp.sin(ang).astype(BF16)
    return c, sn, c.T, sn.T


def _dft_filter_kernel(c_ref, s_ref, hf_ref, hb_ref, kr_ref, ki_ref, sum_scr, dif_scr, *, scale):
    @pl.when(pl.program_id(2) == 0)
    def _():
        hf = hf_ref[...]
        hb = hb_ref[...]
        sum_scr[...] = (hf + hb).astype(BF16)
        dif_scr[...] = (hb - hf).astype(BF16)

    kr_ref[...] = _dot(c_ref[...], sum_scr[...]) * scale
    ki_ref[...] = _dot(s_ref[...], dif_scr[...]) * scale


def _dft_filter(c, s, filt, width, tf, tn):
    n = c.shape[0]
    nc = width // tn
    out = pl.BlockSpec((None, tf, tn), lambda o, j, i: (o, i, j))
    return pl.pallas_call(
        functools.partial(_dft_filter_kernel, scale=1.0 / n),
        grid=(HYENA_ORDER, nc, n // tf),
        in_specs=[pl.BlockSpec((tf, n), lambda o, j, i: (i, 0)),
                  pl.BlockSpec((tf, n), lambda o, j, i: (i, 0)),
                  pl.BlockSpec((n, tn), lambda o, j, i: (0, (2 * o) * nc + j)),
                  pl.BlockSpec((n, tn), lambda o, j, i: (0, (2 * o + 1) * nc + j))],
        out_specs=[out, out],
        out_shape=[jax.ShapeDtypeStruct((HYENA_ORDER, n, width), F32)] * 2,
        scratch_shapes=[pltpu.VMEM((n, tn), BF16), pltpu.VMEM((n, tn), BF16)],
        compiler_params=_params("parallel", "parallel", "arbitrary"),
        name="dft_filter",
    )(c, s, filt, filt)


def _dft_fwd_kernel(c_ref, s_ref, u_ref, kr_ref, ki_ref, pr_ref, pi_ref):
    u = u_ref[...]
    ur = _dot(c_ref[...], u)
    us = _dot(s_ref[...], u)
    kr = kr_ref[...]
    ki = ki_ref[...]
    pr_ref[...] = (ur * kr + us * ki).astype(pr_ref.dtype)
    pi_ref[...] = (ur * ki - us * kr).astype(pi_ref.dtype)


def _dft_fwd(c, s, u, kr, ki, order, tf, tn):
    bsz, n, width = u.shape
    out = pl.BlockSpec((None, tf, tn), lambda bi, j, i: (bi, i, j))
    ks = pl.BlockSpec((None, tf, tn), lambda bi, j, i: (order, i, j))
    return pl.pallas_call(
        _dft_fwd_kernel,
        grid=(bsz, width // tn, n // tf),
        in_specs=[pl.BlockSpec((tf, n), lambda bi, j, i: (i, 0)),
                  pl.BlockSpec((tf, n), lambda bi, j, i: (i, 0)),
                  pl.BlockSpec((None, n, tn), lambda bi, j, i: (bi, 0, j)),
                  ks, ks],
        out_specs=[out, out],
        out_shape=[jax.ShapeDtypeStruct((bsz, n, width), BF16)] * 2,
        compiler_params=_params("parallel", "parallel", "parallel"),
        name="dft_fwd",
    )(c, s, u, kr, ki)


def _dft_inv_kernel(ct_ref, st_ref, pr_ref, pi_ref, u_ref, b_ref, x_ref, o_ref):
    y = _dot(ct_ref[...], pr_ref[...]) - _dot(st_ref[...], pi_ref[...])
    u = u_ref[...].astype(F32)
    o_ref[...] = (x_ref[...] * (y + u * b_ref[...])).astype(o_ref.dtype)


def _dft_inv(ct, st, pr, pi, u, bias, x, order, tt, tn):
    bsz, n, width = u.shape
    tile = pl.BlockSpec((None, tt, tn), lambda bi, j, i: (bi, i, j))
    spec = pl.BlockSpec((None, n, tn), lambda bi, j, i: (bi, 0, j))
    return pl.pallas_call(
        _dft_inv_kernel,
        grid=(bsz, width // tn, n // tt),
        in_specs=[pl.BlockSpec((tt, n), lambda bi, j, i: (i, 0)),
                  pl.BlockSpec((tt, n), lambda bi, j, i: (i, 0)),
                  spec, spec, tile,
                  pl.BlockSpec((None, 1, tn), lambda bi, j, i: (order, 0, j)),
                  tile],
        out_specs=tile,
        out_shape=jax.ShapeDtypeStruct((bsz, n, width), BF16),
        compiler_params=_params("parallel", "parallel", "parallel"),
        name="dft_inv",
    )(ct, st, pr, pi, u, bias, x)


def _filter_features(n):
    t = jnp.linspace(0.0, 1.0, n, dtype=F32)[:, None]
    w = 2.0 * math.pi * jnp.arange(n, dtype=F32) / n
    f = jnp.linspace(1e-4, FILTER_BANDS - 1, FILTER_BANDS, dtype=F32)
    z = jnp.concatenate([t, jnp.cos(w[:, None] * f), -jnp.sin(w[:, None] * f)], axis=-1)
    return jnp.pad(z, ((0, 0), (0, LANES - FILTER_EMB)))


def _pad_to(a, shape):
    return jnp.pad(a, [(0, t - s) for s, t in zip(a.shape, shape)])


def kernel(x, c, ctx, c_ctx, ada_w, ada_b, norm1_g, norm2_g, attn_w_in, attn_w_out, diff_q_g, diff_k_g, diff_lq1, diff_lk1, diff_lq2, diff_lk2, diff_sub_g, swa_q_g, swa_k_g, swa_sink, hy_w_in, hy_b_in, hy_conv_w, hy_conv_b, flt_w1, flt_b1, flt_f1, flt_w2, flt_b2, flt_f2, flt_w3, hy_bias, hy_w_out, hy_b_out, moe_wg1, moe_bg1, moe_wg2, moe_bg2, moe_w_gate, moe_w_up, moe_w_down):
    bsz, seq, d = x.shape
    depth = ada_w.shape[0]
    n_ctx = ctx.shape[1]
    cond = jnp.concatenate([c, c_ctx[None, :], jnp.zeros((8 - bsz - 1, d), F32)], axis=0)
    ada_b3 = ada_b[:, None, :]
    zero_bias = lambda n: jnp.zeros((1, n), F32)
    wg_all = moe_w_gate.astype(BF16).reshape(depth * MOE_EXPERTS, d, -1)
    wu_all = moe_w_up.astype(BF16).reshape(depth * MOE_EXPERTS, d, -1)
    wd_all = moe_w_down.astype(BF16).reshape(depth * MOE_EXPERTS, -1, d)
    tm = min(512, seq)

    for layer in range(depth):
        even = layer % 2 == 0
        i = layer // 2
        ctx_live = any(j % 2 == 0 for j in range(layer + 1, depth))
        assert not ctx_live, "context-stream update is only needed for deeper stacks"
        mod = _adaln_mod(cond, ada_w, ada_b3, layer)
        sh1, sc1, g1, sh2, sc2, g2 = [mod[:, k * d:(k + 1) * d] for k in range(6)]
        lat = lambda a: a[:bsz, None, :]
        cmod = lambda a: jnp.broadcast_to(a[bsz:bsz + 1, None, :], (bsz, 1, d))
        n1 = norm1_g[layer][None, :]
        if even:
            w_in = attn_w_in[i].astype(BF16)
            qkv = _norm_linear(x, n1, lat(sc1), lat(sh1), w_in, zero_bias(ATTN_IN), tm, 1536, "attn_in")
            qkv_c = _norm_linear(ctx, n1, cmod(sc1), cmod(sh1), w_in, zero_bias(ATTN_IN), min(tm, n_ctx), 1536,
                                 "attn_in_ctx")
            lam_init = 0.8 - 0.6 * math.exp(-0.3 * layer)
            lamp = _pad_to(jnp.stack([diff_lq1[i], diff_lk1[i], diff_lq2[i], diff_lk2[i]]), (8, 2 * DIFF_SUB))
            two = lambda a: jnp.tile(a, 2)[None, :]
            o_diff = _diff_attention(qkv, qkv_c, _rope_tables(seq, DIFF_SUB, 2 * DIFF_SUB), two(diff_q_g[i]),
                                     two(diff_k_g[i]), diff_sub_g[i][None, :], lamp, lam_init, 128)
            sink_rows = jnp.broadcast_to(swa_sink[i][:, None], (SWA_HEADS, HEAD_DIM))
            o_swa = _swa_attention(qkv, qkv_c, _rope_tables(seq, HEAD_DIM, HEAD_DIM), swa_q_g[i][None, :],
                                   swa_k_g[i][None, :], sink_rows)
            w_out = attn_w_out[i].astype(BF16)
            x = _mm_residual([o_diff, o_swa], [w_out[:DIFF_V], w_out[DIFF_V:]], zero_bias(d), x, lat(g1), tm,
                             "attn_out")
        else:
            width = hy_w_out.shape[1]
            z = _norm_linear(x, n1, lat(sc1), lat(sh1), hy_w_in[i].astype(BF16), hy_b_in[i][None, :], tm, 1536,
                             "hyena_in")
            v, x1, x2 = _short_conv(z, _pad_to(hy_conv_w[i], (8, 3 * width)), hy_conv_b[i][None, :], 128)
            max_decay = math.log(DECAY_TARGET) / DECAY_FAST
            min_decay = math.log(DECAY_TARGET) / DECAY_SLOW
            deltas = jnp.abs(jnp.tile(jnp.linspace(min_decay, max_decay, width, dtype=F32), 2 * HYENA_ORDER))[None, :]
            is_bwd = jnp.tile(jnp.concatenate([jnp.zeros((width,), F32), jnp.ones((width,), F32)]),
                              HYENA_ORDER)[None, :]
            sq = (LANES, LANES)
            filt = _hyena_filters(
                _filter_features(seq), _pad_to(flt_w1[i], sq), _pad_to(flt_b1[i][None, :], (1, LANES)),
                _pad_to(flt_f1[i][None, :], (1, LANES)), _pad_to(flt_w2[i], sq),
                _pad_to(flt_b2[i][None, :], (1, LANES)), _pad_to(flt_f2[i][None, :], (1, LANES)),
                _pad_to(flt_w3[i], (LANES, 2 * HYENA_ORDER * width)), deltas, is_bwd, 256)
            cm, sm, ct, st = _dft_tables(seq)
            tf = min(512, seq)
            tn = min(512, width)
            kr, ki = _dft_filter(cm, sm, filt, width, tf, min(256, width))
            bias = hy_bias[i][:, None, :]
            pr, pi = _dft_fwd(cm, sm, v, kr, ki, 0, tf, tn)
            y1 = _dft_inv(ct, st, pr, pi, v, bias, x1, 0, tf, tn)
            pr, pi = _dft_fwd(cm, sm, y1, kr, ki, 1, tf, tn)
            y2 = _dft_inv(ct, st, pr, pi, y1, bias, x2, 1, tf, tn)
            x = _mm_residual([y2], [hy_w_out[i].astype(BF16)], hy_b_out[i][None, :], x, lat(g1), tm, "hyena_out")
        w_route = _pad_to(jnp.concatenate([moe_wg1[layer], moe_wg2[layer]], axis=1), (d, LANES))
        b_route = _pad_to(jnp.concatenate([moe_bg1[layer], moe_bg2[layer]])[None, :], (1, LANES))
        x = _hier_moe(x, norm2_g[layer][None, :], lat(sc2), lat(sh2), lat(g2), w_route, b_route,
                      wg_all, wu_all, wd_all, layer)
    return x
```

```python
import functools
import math

import jax
import jax.numpy as jnp
from jax import lax
from jax.experimental import pallas as pl
from jax.experimental.pallas import tpu as pltpu

F32 = jnp.float32
BF16 = jnp.bfloat16
HIGHEST = lax.Precision.HIGHEST

LANES = 128
V7X_VMEM_BYTES = 64 * 1024 * 1024
VMEM_LIMIT = V7X_VMEM_BYTES * 7 // 8

GRID_W = 64
HEAD_DIM = 128
ROPE_BASE = 10000.0
EPS = 1e-6
Q_BLOCK = 128
DIFF_HEADS = 8
DIFF_SUB = HEAD_DIM // 2
SWA_HEADS = 8
SWA_KV_HEADS = 2
SWA_GROUP = SWA_HEADS // SWA_KV_HEADS
WINDOW = 128
DIFF_Q = DIFF_HEADS * 2 * DIFF_SUB
DIFF_V = DIFF_HEADS * HEAD_DIM
SWA_Q = SWA_HEADS * HEAD_DIM
SWA_KV = SWA_KV_HEADS * HEAD_DIM
ATTN_IN = 2 * DIFF_Q + DIFF_V + SWA_Q + 2 * SWA_KV
HYENA_ORDER = 2
SHORT_CONV = 3
FILTER_EMB = 33
FILTER_BANDS = (FILTER_EMB - 1) // 2
FILTER_HIDDEN = 64
DECAY_FAST = 0.3
DECAY_SLOW = 1.5
DECAY_TARGET = 1e-2
MOE_GROUPS = 4
MOE_PER_GROUP = 8
MOE_EXPERTS = MOE_GROUPS * MOE_PER_GROUP
MOE_TOPK = 2
MOE_BLOCK = 256
FF_CHUNK = 256
DIFF_KEY_CHUNK = 512
LOG2_E = 1.4426950408889634
NEG_INF = float("-inf")


def _tile(n, pref):
    t = pref
    while n % t:
        t //= 2
    return t


def _params(*sem):
    return pltpu.CompilerParams(dimension_semantics=sem, vmem_limit_bytes=VMEM_LIMIT)


def _dot(a, b):
    return jnp.dot(a, b, preferred_element_type=F32)


def _dot_nt(a, b):
    return lax.dot_general(a, b, (((1,), (1,)), ((), ())), preferred_element_type=F32)


def _mod_kernel(c_ref, w_ref, b_ref, o_ref):
    c = c_ref[...]
    s = c * jax.nn.sigmoid(c)
    o_ref[...] = jnp.dot(s, w_ref[...], precision=HIGHEST, preferred_element_type=F32) + b_ref[...]


def _adaln_mod(cond, ada_w, ada_b, layer):
    rows, d = cond.shape
    n = ada_w.shape[2]
    tn = _tile(n, 768)
    return pl.pallas_call(
        _mod_kernel,
        grid=(n // tn,),
        in_specs=[pl.BlockSpec((rows, d), lambda j: (0, 0)),
                  pl.BlockSpec((None, d, tn), lambda j: (layer, 0, j)),
                  pl.BlockSpec((None, 1, tn), lambda j: (layer, 0, j))],
        out_specs=pl.BlockSpec((rows, tn), lambda j: (0, j)),
        out_shape=jax.ShapeDtypeStruct((rows, n), F32),
        compiler_params=_params("parallel"),
        name="adaln_mod",
    )(cond, ada_w, ada_b)


def _norm_mod(x, g, sc, sh):
    inv = lax.rsqrt(jnp.mean(x * x, axis=-1, keepdims=True) + EPS)
    return ((x * inv) * g) * (1.0 + sc) + sh


def _norm_linear_kernel(x_ref, g_ref, sc_ref, sh_ref, w_ref, b_ref, o_ref, h_ref):
    @pl.when(pl.program_id(2) == 0)
    def _():
        h_ref[...] = _norm_mod(x_ref[...], g_ref[...], sc_ref[...], sh_ref[...]).astype(BF16)

    o_ref[...] = (_dot(h_ref[...], w_ref[...]) + b_ref[...]).astype(o_ref.dtype)


def _norm_linear(x, g, sc, sh, w, b, tm, tn, name):
    bsz, l, d = x.shape
    n = w.shape[1]
    tn = _tile(n, tn)
    return pl.pallas_call(
        _norm_linear_kernel,
        grid=(bsz, l // tm, n // tn),
        in_specs=[pl.BlockSpec((None, tm, d), lambda bi, i, j: (bi, i, 0)),
                  pl.BlockSpec((1, d), lambda bi, i, j: (0, 0)),
                  pl.BlockSpec((None, 1, d), lambda bi, i, j: (bi, 0, 0)),
                  pl.BlockSpec((None, 1, d), lambda bi, i, j: (bi, 0, 0)),
                  pl.BlockSpec((d, tn), lambda bi, i, j: (0, j)),
                  pl.BlockSpec((1, tn), lambda bi, i, j: (0, j))],
        out_specs=pl.BlockSpec((None, tm, tn), lambda bi, i, j: (bi, i, j)),
        out_shape=jax.ShapeDtypeStruct((bsz, l, n), F32),
        scratch_shapes=[pltpu.VMEM((tm, d), BF16)],
        compiler_params=_params("parallel", "arbitrary", "arbitrary"),
        name=name,
    )(x, g, sc, sh, w, b)


def _mm_res_kernel(*refs, n_pairs):
    a_refs = refs[:n_pairs]
    w_refs = refs[n_pairs:2 * n_pairs]
    b_ref, x_ref, g_ref, o_ref = refs[2 * n_pairs:]
    acc = _dot(a_refs[0][...], w_refs[0][...])
    for a_ref, w_ref in zip(a_refs[1:], w_refs[1:]):
        acc = acc + _dot(a_ref[...], w_ref[...])
    o_ref[...] = x_ref[...] + g_ref[...] * (acc + b_ref[...])


def _mm_residual(a_list, w_list, b, x, g, tm, name):
    bsz, l, n = x.shape
    n_pairs = len(a_list)
    in_specs = [pl.BlockSpec((None, tm, a.shape[2]), lambda bi, i: (bi, i, 0)) for a in a_list]
    in_specs += [pl.BlockSpec(w.shape, lambda bi, i: (0, 0)) for w in w_list]
    in_specs += [pl.BlockSpec((1, n), lambda bi, i: (0, 0)),
                 pl.BlockSpec((None, tm, n), lambda bi, i: (bi, i, 0)),
                 pl.BlockSpec((None, 1, n), lambda bi, i: (bi, 0, 0))]
    return pl.pallas_call(
        functools.partial(_mm_res_kernel, n_pairs=n_pairs),
        grid=(bsz, l // tm),
        in_specs=in_specs,
        out_specs=pl.BlockSpec((None, tm, n), lambda bi, i: (bi, i, 0)),
        out_shape=jax.ShapeDtypeStruct((bsz, l, n), F32),
        compiler_params=_params("parallel", "parallel"),
        name=name,
    )(*a_list, *w_list, b, x, g)


def _rope_tables(n_tokens, dim, width):
    n_rows = n_tokens // GRID_W
    row = jnp.repeat(jnp.arange(n_rows, dtype=F32), GRID_W)
    col = jnp.tile(jnp.arange(GRID_W, dtype=F32), n_rows)
    n_freq = dim // 4
    inv = ROPE_BASE ** (-jnp.arange(n_freq, dtype=F32) / n_freq)
    ang = jnp.concatenate([row[:, None] * inv, col[:, None] * inv], axis=-1)
    ang = jnp.repeat(ang, 2, axis=-1)
    ang = jnp.tile(ang, (1, width // dim))
    sign = jnp.where(jnp.arange(width) % 2 == 0, -1.0, 1.0).astype(F32)
    return jnp.cos(ang), jnp.sin(ang) * sign


def _rope(x, cos_t, sin_t):
    width = x.shape[-1]
    lane = lax.broadcasted_iota(jnp.int32, (1, width), 1)
    partner = jnp.where(lane % 2 == 0, pltpu.roll(x, width - 1, 1), pltpu.roll(x, 1, 1))
    return x * cos_t + partner * sin_t


def _sub_rms(x, g):
    lane = lax.broadcasted_iota(jnp.int32, (1, 2 * DIFF_SUB), 1)
    lo = lane < DIFF_SUB
    sq = x * x
    ss_lo = jnp.sum(jnp.where(lo, sq, 0.0), axis=-1, keepdims=True)
    ss_hi = jnp.sum(jnp.where(lo, 0.0, sq), axis=-1, keepdims=True)
    inv = jnp.where(lo, lax.rsqrt(ss_lo / DIFF_SUB + EPS), lax.rsqrt(ss_hi / DIFF_SUB + EPS))
    return (x * inv) * g


def _rms(x, g):
    return (x * lax.rsqrt(jnp.mean(x * x, axis=-1, keepdims=True) + EPS)) * g


def _diff_attn_kernel(q_ref, k_ref, v_ref, kc_ref, vc_ref, cos_ref, sin_ref,
                      gq_ref, gk_ref, gs_ref, lamp_ref, o_ref, q_scr, k_scr, v_scr, s_scr, lam_scr,
                      *, lam_init, n_ctx, tq):
    n_keys = k_scr.shape[0]

    @pl.when(pl.program_id(2) == 0)
    def _():
        lane = lax.broadcasted_iota(jnp.int32, (1, 2 * DIFF_SUB), 1)
        lo = lane < DIFF_SUB
        gk = gk_ref[...]
        cos_t = cos_ref[...]
        sin_t = sin_ref[...]
        k_scr[0:n_ctx, :] = _sub_rms(kc_ref[...], gk).astype(BF16)
        k_scr[n_ctx:, :] = _rope(_sub_rms(k_ref[...], gk), cos_t, sin_t).astype(BF16)
        v_scr[0:n_ctx, 0:HEAD_DIM] = vc_ref[...].astype(BF16)
        v_scr[n_ctx:, 0:HEAD_DIM] = v_ref[...].astype(BF16)
        v_scr[:, HEAD_DIM:] = jnp.broadcast_to(jnp.where(lane == 0, 1.0, 0.0).astype(BF16), (n_keys, HEAD_DIM))
        q = _rope(_sub_rms(q_ref[...], gq_ref[...]), cos_t, sin_t) * (DIFF_SUB ** -0.5 * LOG2_E)
        q_scr[0] = jnp.where(lo, q, 0.0).astype(BF16)
        q_scr[1] = jnp.where(lo, 0.0, q).astype(BF16)
        lp = lamp_ref[...]
        t1 = jnp.sum(lp[0:1] * lp[1:2], axis=-1, keepdims=True)
        t2 = jnp.sum(lp[2:3] * lp[3:4], axis=-1, keepdims=True)
        lam_scr[...] = jnp.broadcast_to(jnp.exp(t1) - jnp.exp(t2) + lam_init, lam_scr.shape)

    chunks = [(c0, min(c0 + DIFF_KEY_CHUNK, n_keys)) for c0 in range(0, n_keys, DIFF_KEY_CHUNK)]
    lam = lam_scr[:, 0:1]
    for t in range(tq // Q_BLOCK):
        rows = pl.ds(pl.multiple_of(pl.program_id(2) * tq + t * Q_BLOCK, Q_BLOCK), Q_BLOCK)
        qs = [q_scr[0, rows, :], q_scr[1, rows, :]]

        mx = [jnp.full((Q_BLOCK, LANES), NEG_INF, F32) for _ in range(2)]
        for c0, c1 in chunks:
            kc = k_scr[c0:c1, :]
            for h in range(2):
                s = _dot_nt(qs[h], kc)
                s_scr[2 * t + h, :, c0:c1] = s
                for j in range((c1 - c0) // LANES):
                    mx[h] = jnp.maximum(mx[h], s[:, j * LANES:(j + 1) * LANES])
        mrow = [jnp.max(m, axis=-1, keepdims=True) for m in mx]

        out = [jnp.zeros((Q_BLOCK, 2 * HEAD_DIM), F32) for _ in range(2)]
        for c0, c1 in chunks:
            vc = v_scr[c0:c1, :]
            for h in range(2):
                e = jnp.exp2(s_scr[2 * t + h, :, c0:c1] - mrow[h])
                out[h] = out[h] + _dot(e.astype(BF16), vc)
        r1 = 1.0 / out[0][:, HEAD_DIM:HEAD_DIM + 1]
        r2 = lam / out[1][:, HEAD_DIM:HEAD_DIM + 1]
        o = out[0][:, 0:HEAD_DIM] * r1 - out[1][:, 0:HEAD_DIM] * r2
        o_ref[t * Q_BLOCK:(t + 1) * Q_BLOCK, :] = (_rms(o, gs_ref[...]) * (1.0 - lam_init)).astype(o_ref.dtype)


def _diff_attention(qkv, qkv_c, tabs, gq, gk, gs, lamp, lam_init, tq):
    bsz, l, _ = qkv.shape
    n_ctx = qkv_c.shape[1]
    cos_t, sin_t = tabs
    hb = 2 * DIFF_SUB
    kb = DIFF_Q // hb
    vb = 2 * DIFF_Q // hb
    kern = functools.partial(_diff_attn_kernel, lam_init=lam_init, n_ctx=n_ctx, tq=tq)
    vec = pl.BlockSpec((1, hb), lambda bi, h, i: (0, 0))
    return pl.pallas_call(
        kern,
        grid=(bsz, DIFF_HEADS, l // tq),
        in_specs=[pl.BlockSpec((None, l, hb), lambda bi, h, i: (bi, 0, h)),
                  pl.BlockSpec((None, l, hb), lambda bi, h, i: (bi, 0, kb + h)),
                  pl.BlockSpec((None, l, hb), lambda bi, h, i: (bi, 0, vb + h)),
                  pl.BlockSpec((None, n_ctx, hb), lambda bi, h, i: (bi, 0, kb + h)),
                  pl.BlockSpec((None, n_ctx, hb), lambda bi, h, i: (bi, 0, vb + h)),
                  pl.BlockSpec((l, hb), lambda bi, h, i: (0, 0)),
                  pl.BlockSpec((l, hb), lambda bi, h, i: (0, 0)),
                  vec, vec, vec,
                  pl.BlockSpec((8, hb), lambda bi, h, i: (0, 0))],
        out_specs=pl.BlockSpec((None, tq, hb), lambda bi, h, i: (bi, i, h)),
        out_shape=jax.ShapeDtypeStruct((bsz, l, DIFF_V), BF16),
        scratch_shapes=[pltpu.VMEM((2, l, hb), BF16), pltpu.VMEM((n_ctx + l, hb), BF16),
                        pltpu.VMEM((n_ctx + l, 2 * HEAD_DIM), BF16),
                        pltpu.VMEM((2 * (tq // Q_BLOCK), Q_BLOCK, n_ctx + l), F32),
                        pltpu.VMEM((1, hb), F32)],
        compiler_params=_params("parallel", "parallel", "arbitrary"),
        name="diff_attention",
    )(qkv, qkv, qkv, qkv_c, qkv_c, cos_t, sin_t, gq, gk, gs, lamp)


def _swa_kernel(q_ref, k_ref, v_ref, kc_ref, vc_ref, cos_ref, sin_ref, gq_ref, gk_ref, sink_ref,
                o_ref, q_scr, k_scr, v_scr, kc_scr, vc_scr, *, seq):
    kvh = pl.program_id(1)
    n = pl.program_id(2)
    span = Q_BLOCK + 2 * WINDOW

    @pl.when(n == 0)
    def _():
        gk = gk_ref[...]
        gq = gq_ref[...]
        cos_t = cos_ref[...]
        sin_t = sin_ref[...]
        zeros = jnp.zeros((WINDOW, HEAD_DIM), BF16)
        k_scr[0:WINDOW, :] = zeros
        v_scr[0:WINDOW, :] = zeros
        k_scr[WINDOW + seq:, :] = zeros
        v_scr[WINDOW + seq:, :] = zeros
        k_scr[WINDOW:WINDOW + seq, :] = _rope(_rms(k_ref[...], gk), cos_t, sin_t).astype(BF16)
        v_scr[WINDOW:WINDOW + seq, :] = v_ref[...].astype(BF16)
        kc_scr[...] = _rms(kc_ref[...], gk).astype(BF16)
        vc_scr[...] = vc_ref[...].astype(BF16)
        for j in range(SWA_GROUP):
            qj = _rope(_rms(q_ref[:, j * HEAD_DIM:(j + 1) * HEAD_DIM], gq), cos_t, sin_t) * (HEAD_DIM ** -0.5)
            q_scr[j] = qj.astype(BF16)

    start = pl.multiple_of(n * Q_BLOCK, Q_BLOCK)
    kw = k_scr[pl.ds(start, span), :]
    vw = v_scr[pl.ds(start, span), :]
    kc = kc_scr[...]
    vc = vc_scr[...]
    rows = SWA_GROUP * Q_BLOCK
    ridx = lax.broadcasted_iota(jnp.int32, (rows, span), 0)
    kidx = lax.broadcasted_iota(jnp.int32, (rows, span), 1)
    rel = kidx - (ridx & (Q_BLOCK - 1))
    key_pos = n * Q_BLOCK + kidx - WINDOW
    mask = (rel >= 0) & (rel <= 2 * WINDOW) & (key_pos >= 0) & (key_pos < seq)
    head = lax.broadcasted_iota(jnp.int32, (rows, 1), 0) // Q_BLOCK
    sink = jnp.zeros((rows, 1), F32)
    for j in range(SWA_GROUP):
        sink = jnp.where(head == j, sink_ref[pl.ds(kvh * SWA_GROUP + j, 1), 0:1], sink)
    q = jnp.concatenate([q_scr[j, pl.ds(start, Q_BLOCK), :] for j in range(SWA_GROUP)], axis=0)
    s_loc = jnp.where(mask, _dot_nt(q, kw), NEG_INF)
    s_ctx = _dot_nt(q, kc)
    m = jnp.maximum(jnp.maximum(jnp.max(s_loc, axis=-1, keepdims=True),
                                jnp.max(s_ctx, axis=-1, keepdims=True)), sink)
    e_loc = jnp.exp(s_loc - m)
    e_ctx = jnp.exp(s_ctx - m)
    denom = (jnp.sum(e_loc, axis=-1, keepdims=True) + jnp.sum(e_ctx, axis=-1, keepdims=True)
             + jnp.exp(sink - m))
    r = 1.0 / denom
    o = _dot((e_ctx * r).astype(BF16), vc) + _dot((e_loc * r).astype(BF16), vw)
    o_ref[...] = jnp.concatenate([o[j * Q_BLOCK:(j + 1) * Q_BLOCK] for j in range(SWA_GROUP)],
                                 axis=-1).astype(o_ref.dtype)


def _swa_attention(qkv, qkv_c, tabs, gq, gk, sink_rows):
    bsz, l, _ = qkv.shape
    n_ctx = qkv_c.shape[1]
    cos_t, sin_t = tabs
    gw = SWA_GROUP * HEAD_DIM
    q0 = (2 * DIFF_Q + DIFF_V) // gw
    k0 = (2 * DIFF_Q + DIFF_V + SWA_Q) // HEAD_DIM
    v0 = k0 + SWA_KV_HEADS
    vec = pl.BlockSpec((1, HEAD_DIM), lambda bi, h, i: (0, 0))
    return pl.pallas_call(
        functools.partial(_swa_kernel, seq=l),
        grid=(bsz, SWA_KV_HEADS, l // Q_BLOCK),
        in_specs=[pl.BlockSpec((None, l, gw), lambda bi, h, i: (bi, 0, q0 + h)),
                  pl.BlockSpec((None, l, HEAD_DIM), lambda bi, h, i: (bi, 0, k0 + h)),
                  pl.BlockSpec((None, l, HEAD_DIM), lambda bi, h, i: (bi, 0, v0 + h)),
                  pl.BlockSpec((None, n_ctx, HEAD_DIM), lambda bi, h, i: (bi, 0, k0 + h)),
                  pl.BlockSpec((None, n_ctx, HEAD_DIM), lambda bi, h, i: (bi, 0, v0 + h)),
                  pl.BlockSpec((l, HEAD_DIM), lambda bi, h, i: (0, 0)),
                  pl.BlockSpec((l, HEAD_DIM), lambda bi, h, i: (0, 0)),
                  vec, vec,
                  pl.BlockSpec((SWA_HEADS, HEAD_DIM), lambda bi, h, i: (0, 0))],
        out_specs=pl.BlockSpec((None, Q_BLOCK, gw), lambda bi, h, i: (bi, i, h)),
        out_shape=jax.ShapeDtypeStruct((bsz, l, SWA_Q), BF16),
        scratch_shapes=[pltpu.VMEM((SWA_GROUP, l, HEAD_DIM), BF16),
                        pltpu.VMEM((l + 2 * WINDOW, HEAD_DIM), BF16), pltpu.VMEM((l + 2 * WINDOW, HEAD_DIM), BF16),
                        pltpu.VMEM((n_ctx, HEAD_DIM), BF16), pltpu.VMEM((n_ctx, HEAD_DIM), BF16)],
        compiler_params=_params("parallel", "parallel", "arbitrary"),
        name="swa_attention",
    )(qkv, qkv, qkv, qkv_c, qkv_c, cos_t, sin_t, gq, gk, sink_rows)


def _router_kernel(x_ref, g_ref, sc_ref, sh_ref, w_ref, b_ref, tri_ref, h_ref, e_ref, p_ref, cnt_ref, run_scr):
    @pl.when((pl.program_id(0) == 0) & (pl.program_id(1) == 0))
    def _():
        run_scr[...] = jnp.zeros(run_scr.shape, F32)

    h = _norm_mod(x_ref[...], g_ref[...], sc_ref[...], sh_ref[...])
    h_ref[...] = h
    logits = jnp.dot(h, w_ref[...], precision=HIGHEST, preferred_element_type=F32) + b_ref[...]
    lane_i = lax.broadcasted_iota(jnp.int32, (1, LANES), 1)
    lane = lane_i.astype(F32)
    big = float(LANES)
    lg = jnp.where(lane_i < MOE_GROUPS, logits, NEG_INF)
    mg = jnp.max(lg, axis=-1, keepdims=True)
    p_top = 1.0 / jnp.sum(jnp.exp(lg - mg), axis=-1, keepdims=True)
    grp = jnp.min(jnp.where(lg == mg, lane, big), axis=-1, keepdims=True)
    e_lane = lane - MOE_GROUPS
    in_grp = (e_lane >= grp * MOE_PER_GROUP) & (e_lane < (grp + 1.0) * MOE_PER_GROUP)
    le = jnp.where(in_grp, logits, NEG_INF)
    t1 = jnp.max(le, axis=-1, keepdims=True)
    i1 = jnp.min(jnp.where(le == t1, lane, big), axis=-1, keepdims=True)
    le2 = jnp.where(lane == i1, NEG_INF, le)
    t2 = jnp.max(le2, axis=-1, keepdims=True)
    i2 = jnp.min(jnp.where(le2 == t2, lane, big), axis=-1, keepdims=True)
    w2 = jnp.exp(t2 - t1)
    inv = p_top / (1.0 + w2)
    hot1 = lane == i1
    hot2 = lane == i2
    onehot = jnp.where(hot1 | hot2, 1.0, 0.0)
    before = run_scr[...] + _dot(tri_ref[...], onehot.astype(BF16))
    r1 = jnp.sum(jnp.where(hot1, before, 0.0), axis=-1, keepdims=True)
    r2 = jnp.sum(jnp.where(hot2, before, 0.0), axis=-1, keepdims=True)
    run_scr[...] = run_scr[...] + jnp.sum(onehot, axis=0, keepdims=True)
    cnt_ref[...] = run_scr[...]
    e_ref[...] = jnp.where(lane_i == 0, i1 - MOE_GROUPS,
                           jnp.where(lane_i == 1, i2 - MOE_GROUPS,
                                     jnp.where(lane_i == 2, r1, jnp.where(lane_i == 3, r2, 0.0)))).astype(jnp.int32)
    p_ref[...] = jnp.where(lane_i == 0, inv, jnp.where(lane_i == 1, inv * w2, 0.0))


def _router(x, g, sc, sh, w, b, tm):
    bsz, l, d = x.shape
    row = lambda bi, i: (bi, i, 0)
    tri = (jnp.arange(tm)[:, None] > jnp.arange(tm)[None, :]).astype(BF16)
    return pl.pallas_call(
        _router_kernel,
        grid=(bsz, l // tm),
        in_specs=[pl.BlockSpec((None, tm, d), row),
                  pl.BlockSpec((1, d), lambda bi, i: (0, 0)),
                  pl.BlockSpec((None, 1, d), lambda bi, i: (bi, 0, 0)),
                  pl.BlockSpec((None, 1, d), lambda bi, i: (bi, 0, 0)),
                  pl.BlockSpec((d, LANES), lambda bi, i: (0, 0)),
                  pl.BlockSpec((1, LANES), lambda bi, i: (0, 0)),
                  pl.BlockSpec((tm, tm), lambda bi, i: (0, 0))],
        out_specs=[pl.BlockSpec((None, tm, d), row),
                   pl.BlockSpec((None, tm, LANES), row),
                   pl.BlockSpec((None, tm, LANES), row),
                   pl.BlockSpec((1, LANES), lambda bi, i: (0, 0))],
        out_shape=[jax.ShapeDtypeStruct((bsz, l, d), F32),
                   jax.ShapeDtypeStruct((bsz, l, LANES), jnp.int32),
                   jax.ShapeDtypeStruct((bsz, l, LANES), F32),
                   jax.ShapeDtypeStruct((1, LANES), F32)],
        scratch_shapes=[pltpu.VMEM((1, LANES), F32)],
        compiler_params=_params("arbitrary", "arbitrary"),
        name="moe_router",
    )(x, g, sc, sh, w, b, tri)


def _dispatch_kernel(slot_ref, h_ref, init_ref, xb_ref, sem):
    del init_ref
    tm = h_ref.shape[0]
    base = pl.program_id(0) * tm

    def copy(r, k):
        dst = slot_ref[(base + r) * MOE_TOPK + k]
        return pltpu.make_async_copy(h_ref.at[pl.ds(r, 1)], xb_ref.at[pl.ds(dst, 1)], sem)

    def start(r, carry):
        for k in range(MOE_TOPK):
            copy(r, k).start()
        return carry

    def wait(r, carry):
        for k in range(MOE_TOPK):
            copy(r, k).wait()
        return carry

    lax.fori_loop(0, tm, start, 0)
    lax.fori_loop(0, tm, wait, 0)


def _dispatch(slot, h, n_rows, tm):
    n, d = h.shape
    grid_spec = pltpu.PrefetchScalarGridSpec(
        num_scalar_prefetch=1,
        grid=(n // tm,),
        in_specs=[pl.BlockSpec((tm, d), lambda i, s: (i, 0)),
                  pl.BlockSpec(memory_space=pl.ANY)],
        out_specs=pl.BlockSpec(memory_space=pl.ANY),
        scratch_shapes=[pltpu.SemaphoreType.DMA(())],
    )
    return pl.pallas_call(
        _dispatch_kernel,
        grid_spec=grid_spec,
        out_shape=jax.ShapeDtypeStruct((n_rows, d), F32),
        input_output_aliases={2: 0},
        compiler_params=_params("arbitrary"),
        name="moe_dispatch",
    )(slot, h, jnp.zeros((n_rows, d), F32))


def _combine_kernel(slot_ref, x_ref, g_ref, p_ref, yb_ref, o_ref, y_scr, sem):
    tm = x_ref.shape[0]
    base = pl.program_id(0) * tm

    def copy(r, k):
        src = slot_ref[(base + r) * MOE_TOPK + k]
        return pltpu.make_async_copy(yb_ref.at[pl.ds(src, 1)], y_scr.at[k, pl.ds(r, 1)], sem)

    def start(r, carry):
        for k in range(MOE_TOPK):
            copy(r, k).start()
        return carry

    def wait(r, carry):
        for k in range(MOE_TOPK):
            copy(r, k).wait()
        return carry

    lax.fori_loop(0, tm, start, 0)
    lax.fori_loop(0, tm, wait, 0)
    p = p_ref[...]
    moe = y_scr[0] * p[:, 0:1]
    for k in range(1, MOE_TOPK):
        moe = moe + y_scr[k] * p[:, k:k + 1]
    o_ref[...] = x_ref[...] + g_ref[...] * moe


def _combine(slot, x, g, gates, yb, tm):
    bsz, l, d = x.shape
    nt = l // tm
    grid_spec = pltpu.PrefetchScalarGridSpec(
        num_scalar_prefetch=1,
        grid=(bsz * nt,),
        in_specs=[pl.BlockSpec((None, tm, d), lambda i, s: (i // nt, i % nt, 0)),
                  pl.BlockSpec((None, 1, d), lambda i, s: (i // nt, 0, 0)),
                  pl.BlockSpec((None, tm, LANES), lambda i, s: (i // nt, i % nt, 0)),
                  pl.BlockSpec(memory_space=pl.ANY)],
        out_specs=pl.BlockSpec((None, tm, d), lambda i, s: (i // nt, i % nt, 0)),
        scratch_shapes=[pltpu.VMEM((MOE_TOPK, tm, d), F32), pltpu.SemaphoreType.DMA(())],
    )
    return pl.pallas_call(
        _combine_kernel,
        grid_spec=grid_spec,
        out_shape=jax.ShapeDtypeStruct((bsz, l, d), F32),
        compiler_params=_params("arbitrary"),
        name="moe_combine",
    )(slot, x, g, gates, yb)


def _expert_kernel(be_ref, nu_ref, x_ref, wg_ref, wu_ref, wd_ref, o_ref):
    i = pl.program_id(0)

    @pl.when(i < nu_ref[0])
    def _():
        x = x_ref[...].astype(BF16)
        ff = wg_ref.shape[1]
        acc = jnp.zeros(o_ref.shape, F32)
        for c in range(ff // FF_CHUNK):
            cols = slice(c * FF_CHUNK, (c + 1) * FF_CHUNK)
            gate = _dot(x, wg_ref[:, cols])
            up = _dot(x, wu_ref[:, cols])
            hidden = (gate * jax.nn.sigmoid(gate) * up).astype(BF16)
            acc = acc + _dot(hidden, wd_ref[cols, :])
        o_ref[...] = acc

    @pl.when(i >= nu_ref[0])
    def _():
        o_ref[...] = jnp.zeros(o_ref.shape, F32)


def _expert_ffn(xb, block_e, n_used, w_gate, w_up, w_down, layer):
    n_blocks, _, d = xb.shape
    ff = w_gate.shape[2]
    base = layer * MOE_EXPERTS
    grid_spec = pltpu.PrefetchScalarGridSpec(
        num_scalar_prefetch=2,
        grid=(n_blocks,),
        in_specs=[pl.BlockSpec((None, MOE_BLOCK, d), lambda i, be, nu: (i, 0, 0)),
                  pl.BlockSpec((None, d, ff), lambda i, be, nu: (base + be[i], 0, 0)),
                  pl.BlockSpec((None, d, ff), lambda i, be, nu: (base + be[i], 0, 0)),
                  pl.BlockSpec((None, ff, d), lambda i, be, nu: (base + be[i], 0, 0))],
        out_specs=pl.BlockSpec((None, MOE_BLOCK, d), lambda i, be, nu: (i, 0, 0)),
    )
    return pl.pallas_call(
        _expert_kernel,
        grid_spec=grid_spec,
        out_shape=jax.ShapeDtypeStruct((n_blocks, MOE_BLOCK, d), F32),
        compiler_params=_params("arbitrary"),
        name="expert_ffn",
    )(block_e, n_used, xb, w_gate, w_up, w_down)


def _hier_moe(x, g, sc, sh, gate2, w_route, b_route, w_gate, w_up, w_down, layer):
    bsz, l, d = x.shape
    n = bsz * l
    tm = min(512, l)
    h, route, gates, cnt = _router(x, g, sc, sh, w_route, b_route, tm)
    route = route.reshape(n, LANES)
    expert = route[:, :MOE_TOPK]
    rank = route[:, MOE_TOPK:2 * MOE_TOPK]
    counts = cnt[0, MOE_GROUPS:MOE_GROUPS + MOE_EXPERTS].astype(jnp.int32)
    padded = (counts + MOE_BLOCK - 1) // MOE_BLOCK * MOE_BLOCK
    pad_end = jnp.cumsum(padded)
    pad_start = pad_end - padded
    hit = expert[:, :, None] == jnp.arange(MOE_EXPERTS, dtype=jnp.int32)
    slot = (jnp.sum(jnp.where(hit, pad_start, 0), axis=-1) + rank).reshape(-1)
    n_blocks = -(-(n * MOE_TOPK) // MOE_BLOCK) + MOE_EXPERTS
    block_row = jnp.arange(n_blocks, dtype=jnp.int32)[:, None] * MOE_BLOCK
    block_e = jnp.minimum(jnp.sum((pad_end[None, :] <= block_row).astype(jnp.int32), axis=-1), MOE_EXPERTS - 1)
    n_used = (pad_end[-1:] // MOE_BLOCK).astype(jnp.int32)
    xb = _dispatch(slot, h.reshape(n, d), n_blocks * MOE_BLOCK, tm).reshape(n_blocks, MOE_BLOCK, d)
    yb = _expert_ffn(xb, block_e, n_used, w_gate, w_up, w_down, layer).reshape(-1, d)
    return _combine(slot, x, gate2, gates, yb, tm)


def _short_conv_kernel(zv_ref, z1_ref, z2_ref, wv_ref, w1_ref, w2_ref, bv_ref, b1_ref, b2_ref,
                       v_ref, x1_ref, x2_ref):
    n = zv_ref.shape[0]
    row = lax.broadcasted_iota(jnp.int32, (n, 1), 0)

    def conv(z_ref, w_ref, b_ref):
        z = z_ref[...]
        w = w_ref[...]
        prev = jnp.where(row == 0, 0.0, pltpu.roll(z, 1, 0))
        nxt = jnp.where(row == n - 1, 0.0, pltpu.roll(z, n - 1, 0))
        return b_ref[...] + prev * w[0:1] + z * w[1:2] + nxt * w[2:3]

    v_ref[...] = conv(zv_ref, wv_ref, bv_ref).astype(v_ref.dtype)
    x1_ref[...] = conv(z1_ref, w1_ref, b1_ref).astype(x1_ref.dtype)
    x2_ref[...] = conv(z2_ref, w2_ref, b2_ref).astype(x2_ref.dtype)


def _short_conv(z, conv_w, conv_b, tc):
    bsz, l, w3 = z.shape
    w = w3 // 3
    nc = w // tc
    zs = lambda k: pl.BlockSpec((None, l, tc), lambda bi, j: (bi, 0, k * nc + j))
    ws = lambda k: pl.BlockSpec((8, tc), lambda bi, j: (0, k * nc + j))
    bs = lambda k: pl.BlockSpec((1, tc), lambda bi, j: (0, k * nc + j))
    out = pl.BlockSpec((None, l, tc), lambda bi, j: (bi, 0, j))
    return pl.pallas_call(
        _short_conv_kernel,
        grid=(bsz, nc),
        in_specs=[zs(0), zs(1), zs(2), ws(0), ws(1), ws(2), bs(0), bs(1), bs(2)],
        out_specs=[out, out, out],
        out_shape=[jax.ShapeDtypeStruct((bsz, l, w), BF16), jax.ShapeDtypeStruct((bsz, l, w), F32),
                   jax.ShapeDtypeStruct((bsz, l, w), F32)],
        compiler_params=_params("parallel", "parallel"),
        name="short_conv",
    )(z, z, z, conv_w, conv_w, conv_w, conv_b, conv_b, conv_b)


def _filter_kernel(z_ref, w1_ref, b1_ref, f1_ref, w2_ref, b2_ref, f2_ref, w3_ref, dl_ref, bw_ref, o_ref, a_scr):
    n = z_ref.shape[0]

    @pl.when(pl.program_id(0) == 0)
    def _():
        a = jnp.sin(f1_ref[...] * (jnp.dot(z_ref[...], w1_ref[...], precision=HIGHEST,
                                           preferred_element_type=F32) + b1_ref[...]))
        a_scr[...] = jnp.sin(f2_ref[...] * (jnp.dot(a, w2_ref[...], precision=HIGHEST,
                                                    preferred_element_type=F32) + b2_ref[...]))

    hf = jnp.dot(a_scr[...], w3_ref[...], precision=HIGHEST, preferred_element_type=F32)
    row = lax.broadcasted_iota(jnp.int32, (n, 1), 0)
    t = row.astype(F32) * (1.0 / (n - 1))
    hf = hf * jnp.exp(-t * dl_ref[...])
    o_ref[...] = jnp.where((row == 0) & (bw_ref[...] > 0.5), 0.0, hf)


def _hyena_filters(z, w1, b1, f1, w2, b2, f2, w3, deltas, is_bwd, tn):
    n = z.shape[0]
    cols = w3.shape[1]
    full = lambda a: pl.BlockSpec(a.shape, lambda j: (0, 0))
    tile = lambda r: pl.BlockSpec((r, tn), lambda j: (0, j))
    return pl.pallas_call(
        _filter_kernel,
        grid=(cols // tn,),
        in_specs=[full(z), full(w1), full(b1), full(f1), full(w2), full(b2), full(f2),
                  tile(w3.shape[0]), tile(1), tile(1)],
        out_specs=tile(n),
        out_shape=jax.ShapeDtypeStruct((n, cols), F32),
        scratch_shapes=[pltpu.VMEM((n, LANES), F32)],
        compiler_params=_params("arbitrary"),
        name="hyena_filters",
    )(z, w1, b1, f1, w2, b2, f2, w3, deltas, is_bwd)


def _dft_tables(n):
    f = jnp.arange(n, dtype=jnp.int32)[:, None]
    s = jnp.arange(n, dtype=jnp.int32)[None, :]
    idx = ((2 * f + 1) * s) % (4 * n)
    ang = idx.astype(F32) * (math.pi / (2 * n))
    c = jnp.cos(ang).astype(BF16)
    sn = jnp.sin(ang).astype(BF16)
    return c, sn, c.T, sn.T


def _dft_filter_kernel(c_ref, s_ref, hf_ref, hb_ref, kr_ref, ki_ref, sum_scr, dif_scr, *, scale):
    @pl.when(pl.program_id(2) == 0)
    def _():
        hf = hf_ref[...]
        hb = hb_ref[...]
        sum_scr[...] = (hf + hb).astype(BF16)
        dif_scr[...] = (hb - hf).astype(BF16)

    kr_ref[...] = _dot(c_ref[...], sum_scr[...]) * scale
    ki_ref[...] = _dot(s_ref[...], dif_scr[...]) * scale


def _dft_filter(c, s, filt, width, tf, tn):
    n = c.shape[0]
    nc = width // tn
    out = pl.BlockSpec((None, tf, tn), lambda o, j, i: (o, i, j))
    return pl.pallas_call(
        functools.partial(_dft_filter_kernel, scale=1.0 / n),
        grid=(HYENA_ORDER, nc, n // tf),
        in_specs=[pl.BlockSpec((tf, n), lambda o, j, i: (i, 0)),
                  pl.BlockSpec((tf, n), lambda o, j, i: (i, 0)),
                  pl.BlockSpec((n, tn), lambda o, j, i: (0, (2 * o) * nc + j)),
                  pl.BlockSpec((n, tn), lambda o, j, i: (0, (2 * o + 1) * nc + j))],
        out_specs=[out, out],
        out_shape=[jax.ShapeDtypeStruct((HYENA_ORDER, n, width), F32)] * 2,
        scratch_shapes=[pltpu.VMEM((n, tn), BF16), pltpu.VMEM((n, tn), BF16)],
        compiler_params=_params("parallel", "parallel", "arbitrary"),
        name="dft_filter",
    )(c, s, filt, filt)


def _dft_fwd_kernel(c_ref, s_ref, u_ref, kr_ref, ki_ref, pr_ref, pi_ref):
    u = u_ref[...]
    ur = _dot(c_ref[...], u)
    us = _dot(s_ref[...], u)
    kr = kr_ref[...]
    ki = ki_ref[...]
    pr_ref[...] = (ur * kr + us * ki).astype(pr_ref.dtype)
    pi_ref[...] = (ur * ki - us * kr).astype(pi_ref.dtype)


def _dft_fwd(c, s, u, kr, ki, order, tf, tn):
    bsz, n, width = u.shape
    out = pl.BlockSpec((None, tf, tn), lambda bi, j, i: (bi, i, j))
    ks = pl.BlockSpec((None, tf, tn), lambda bi, j, i: (order, i, j))
    return pl.pallas_call(
        _dft_fwd_kernel,
        grid=(bsz, width // tn, n // tf),
        in_specs=[pl.BlockSpec((tf, n), lambda bi, j, i: (i, 0)),
                  pl.BlockSpec((tf, n), lambda bi, j, i: (i, 0)),
                  pl.BlockSpec((None, n, tn), lambda bi, j, i: (bi, 0, j)),
                  ks, ks],
        out_specs=[out, out],
        out_shape=[jax.ShapeDtypeStruct((bsz, n, width), BF16)] * 2,
        compiler_params=_params("parallel", "parallel", "parallel"),
        name="dft_fwd",
    )(c, s, u, kr, ki)


def _dft_inv_kernel(ct_ref, st_ref, pr_ref, pi_ref, u_ref, b_ref, x_ref, o_ref):
    y = _dot(ct_ref[...], pr_ref[...]) - _dot(st_ref[...], pi_ref[...])
    u = u_ref[...].astype(F32)
    o_ref[...] = (x_ref[...] * (y + u * b_ref[...])).astype(o_ref.dtype)


def _dft_inv(ct, st, pr, pi, u, bias, x, order, tt, tn):
    bsz, n, width = u.shape
    tile = pl.BlockSpec((None, tt, tn), lambda bi, j, i: (bi, i, j))
    spec = pl.BlockSpec((None, n, tn), lambda bi, j, i: (bi, 0, j))
    return pl.pallas_call(
        _dft_inv_kernel,
        grid=(bsz, width // tn, n // tt),
        in_specs=[pl.BlockSpec((tt, n), lambda bi, j, i: (i, 0)),
                  pl.BlockSpec((tt, n), lambda bi, j, i: (i, 0)),
                  spec, spec, tile,
                  pl.BlockSpec((None, 1, tn), lambda bi, j, i: (order, 0, j)),
                  tile],
        out_specs=tile,
        out_shape=jax.ShapeDtypeStruct((bsz, n, width), BF16),
        compiler_params=_params("parallel", "parallel", "parallel"),
        name="dft_inv",
    )(ct, st, pr, pi, u, bias, x)


def _filter_features(n):
    t = jnp.linspace(0.0, 1.0, n, dtype=F32)[:, None]
    w = 2.0 * math.pi * jnp.arange(n, dtype=F32) / n
    f = jnp.linspace(1e-4, FILTER_BANDS - 1, FILTER_BANDS, dtype=F32)
    z = jnp.concatenate([t, jnp.cos(w[:, None] * f), -jnp.sin(w[:, None] * f)], axis=-1)
    return jnp.pad(z, ((0, 0), (0, LANES - FILTER_EMB)))


def _pad_to(a, shape):
    return jnp.pad(a, [(0, t - s) for s, t in zip(a.shape, shape)])


def kernel(x, c, ctx, c_ctx, ada_w, ada_b, norm1_g, norm2_g, attn_w_in, attn_w_out, diff_q_g, diff_k_g, diff_lq1, diff_lk1, diff_lq2, diff_lk2, diff_sub_g, swa_q_g, swa_k_g, swa_sink, hy_w_in, hy_b_in, hy_conv_w, hy_conv_b, flt_w1, flt_b1, flt_f1, flt_w2, flt_b2, flt_f2, flt_w3, hy_bias, hy_w_out, hy_b_out, moe_wg1, moe_bg1, moe_wg2, moe_bg2, moe_w_gate, moe_w_up, moe_w_down):
    bsz, seq, d = x.shape
    depth = ada_w.shape[0]
    n_ctx = ctx.shape[1]
    cond = jnp.concatenate([c, c_ctx[None, :], jnp.zeros((8 - bsz - 1, d), F32)], axis=0)
    ada_b3 = ada_b[:, None, :]
    zero_bias = lambda n: jnp.zeros((1, n), F32)
    wg_all = moe_w_gate.astype(BF16).reshape(depth * MOE_EXPERTS, d, -1)
    wu_all = moe_w_up.astype(BF16).reshape(depth * MOE_EXPERTS, d, -1)
    wd_all = moe_w_down.astype(BF16).reshape(depth * MOE_EXPERTS, -1, d)
    tm = min(512, seq)

    for layer in range(depth):
        even = layer % 2 == 0
        i = layer // 2
        ctx_live = any(j % 2 == 0 for j in range(layer + 1, depth))
        assert not ctx_live, "context-stream update is only needed for deeper stacks"
        mod = _adaln_mod(cond, ada_w, ada_b3, layer)
        sh1, sc1, g1, sh2, sc2, g2 = [mod[:, k * d:(k + 1) * d] for k in range(6)]
        lat = lambda a: a[:bsz, None, :]
        cmod = lambda a: jnp.broadcast_to(a[bsz:bsz + 1, None, :], (bsz, 1, d))
        n1 = norm1_g[layer][None, :]
        if even:
            w_in = attn_w_in[i].astype(BF16)
            qkv = _norm_linear(x, n1, lat(sc1), lat(sh1), w_in, zero_bias(ATTN_IN), tm, 1536, "attn_in")
            qkv_c = _norm_linear(ctx, n1, cmod(sc1), cmod(sh1), w_in, zero_bias(ATTN_IN), min(tm, n_ctx), 1536,
                                 "attn_in_ctx")
            lam_init = 0.8 - 0.6 * math.exp(-0.3 * layer)
            lamp = _pad_to(jnp.stack([diff_lq1[i], diff_lk1[i], diff_lq2[i], diff_lk2[i]]), (8, 2 * DIFF_SUB))
            two = lambda a: jnp.tile(a, 2)[None, :]
            o_diff = _diff_attention(qkv, qkv_c, _rope_tables(seq, DIFF_SUB, 2 * DIFF_SUB), two(diff_q_g[i]),
                                     two(diff_k_g[i]), diff_sub_g[i][None, :], lamp, lam_init, min(2 * Q_BLOCK, seq))
            sink_rows = jnp.broadcast_to(swa_sink[i][:, None], (SWA_HEADS, HEAD_DIM))
            o_swa = _swa_attention(qkv, qkv_c, _rope_tables(seq, HEAD_DIM, HEAD_DIM), swa_q_g[i][None, :],
                                   swa_k_g[i][None, :], sink_rows)
            w_out = attn_w_out[i].astype(BF16)
            x = _mm_residual([o_diff, o_swa], [w_out[:DIFF_V], w_out[DIFF_V:]], zero_bias(d), x, lat(g1), tm,
                             "attn_out")
        else:
            width = hy_w_out.shape[1]
            z = _norm_linear(x, n1, lat(sc1), lat(sh1), hy_w_in[i].astype(BF16), hy_b_in[i][None, :], tm, 1536,
                             "hyena_in")
            v, x1, x2 = _short_conv(z, _pad_to(hy_conv_w[i], (8, 3 * width)), hy_conv_b[i][None, :], 128)
            max_decay = math.log(DECAY_TARGET) / DECAY_FAST
            min_decay = math.log(DECAY_TARGET) / DECAY_SLOW
            deltas = jnp.abs(jnp.tile(jnp.linspace(min_decay, max_decay, width, dtype=F32), 2 * HYENA_ORDER))[None, :]
            is_bwd = jnp.tile(jnp.concatenate([jnp.zeros((width,), F32), jnp.ones((width,), F32)]),
                              HYENA_ORDER)[None, :]
            sq = (LANES, LANES)
            filt = _hyena_filters(
                _filter_features(seq), _pad_to(flt_w1[i], sq), _pad_to(flt_b1[i][None, :], (1, LANES)),
                _pad_to(flt_f1[i][None, :], (1, LANES)), _pad_to(flt_w2[i], sq),
                _pad_to(flt_b2[i][None, :], (1, LANES)), _pad_to(flt_f2[i][None, :], (1, LANES)),
                _pad_to(flt_w3[i], (LANES, 2 * HYENA_ORDER * width)), deltas, is_bwd, 256)
            cm, sm, ct, st = _dft_tables(seq)
            tf = min(512, seq)
            tn = min(512, width)
            kr, ki = _dft_filter(cm, sm, filt, width, tf, min(256, width))
            bias = hy_bias[i][:, None, :]
            pr, pi = _dft_fwd(cm, sm, v, kr, ki, 0, tf, tn)
            y1 = _dft_inv(ct, st, pr, pi, v, bias, x1, 0, tf, tn)
            pr, pi = _dft_fwd(cm, sm, y1, kr, ki, 1, tf, tn)
            y2 = _dft_inv(ct, st, pr, pi, y1, bias, x2, 1, tf, tn)
            x = _mm_residual([y2], [hy_w_out[i].astype(BF16)], hy_b_out[i][None, :], x, lat(g1), tm, "hyena_out")
        w_route = _pad_to(jnp.concatenate([moe_wg1[layer], moe_wg2[layer]], axis=1), (d, LANES))
        b_route = _pad_to(jnp.concatenate([moe_bg1[layer], moe_bg2[layer]])[None, :], (1, LANES))
        x = _hier_moe(x, norm2_g[layer][None, :], lat(sc2), lat(sh2), lat(g2), w_route, b_route,
                      wg_all, wu_all, wd_all, layer)
    return x
```

```python
import functools
import math

import jax
import jax.numpy as jnp
from jax import lax
from jax.experimental import pallas as pl
from jax.experimental.pallas import tpu as pltpu

F32 = jnp.float32
BF16 = jnp.bfloat16
HIGHEST = lax.Precision.HIGHEST

LANES = 128
V7X_VMEM_BYTES = 64 * 1024 * 1024
VMEM_LIMIT = V7X_VMEM_BYTES * 7 // 8

GRID_W = 64
HEAD_DIM = 128
ROPE_BASE = 10000.0
EPS = 1e-6
Q_BLOCK = 128
DIFF_HEADS = 8
DIFF_SUB = HEAD_DIM // 2
SWA_HEADS = 8
SWA_KV_HEADS = 2
SWA_GROUP = SWA_HEADS // SWA_KV_HEADS
WINDOW = 128
DIFF_Q = DIFF_HEADS * 2 * DIFF_SUB
DIFF_V = DIFF_HEADS * HEAD_DIM
SWA_Q = SWA_HEADS * HEAD_DIM
SWA_KV = SWA_KV_HEADS * HEAD_DIM
ATTN_IN = 2 * DIFF_Q + DIFF_V + SWA_Q + 2 * SWA_KV
HYENA_ORDER = 2
SHORT_CONV = 3
FILTER_EMB = 33
FILTER_BANDS = (FILTER_EMB - 1) // 2
FILTER_HIDDEN = 64
DECAY_FAST = 0.3
DECAY_SLOW = 1.5
DECAY_TARGET = 1e-2
MOE_GROUPS = 4
MOE_PER_GROUP = 8
MOE_EXPERTS = MOE_GROUPS * MOE_PER_GROUP
MOE_TOPK = 2
MOE_BLOCK = 256
FF_CHUNK = 256
DIFF_KEY_CHUNK = 512
LOG2_E = 1.4426950408889634
NEG_INF = float("-inf")


def _tile(n, pref):
    t = pref
    while n % t:
        t //= 2
    return t


def _params(*sem):
    return pltpu.CompilerParams(dimension_semantics=sem, vmem_limit_bytes=VMEM_LIMIT)


def _dot(a, b):
    return jnp.dot(a, b, preferred_element_type=F32)


def _dot_nt(a, b):
    return lax.dot_general(a, b, (((1,), (1,)), ((), ())), preferred_element_type=F32)


def _mod_kernel(c_ref, w_ref, b_ref, o_ref):
    c = c_ref[...]
    s = c * jax.nn.sigmoid(c)
    o_ref[...] = jnp.dot(s, w_ref[...], precision=HIGHEST, preferred_element_type=F32) + b_ref[...]


def _adaln_mod(cond, ada_w, ada_b, layer):
    rows, d = cond.shape
    n = ada_w.shape[2]
    tn = _tile(n, 768)
    return pl.pallas_call(
        _mod_kernel,
        grid=(n // tn,),
        in_specs=[pl.BlockSpec((rows, d), lambda j: (0, 0)),
                  pl.BlockSpec((None, d, tn), lambda j: (layer, 0, j)),
                  pl.BlockSpec((None, 1, tn), lambda j: (layer, 0, j))],
        out_specs=pl.BlockSpec((rows, tn), lambda j: (0, j)),
        out_shape=jax.ShapeDtypeStruct((rows, n), F32),
        compiler_params=_params("parallel"),
        name="adaln_mod",
    )(cond, ada_w, ada_b)


def _norm_mod(x, g, sc, sh):
    inv = lax.rsqrt(jnp.mean(x * x, axis=-1, keepdims=True) + EPS)
    return ((x * inv) * g) * (1.0 + sc) + sh


def _norm_linear_kernel(x_ref, g_ref, sc_ref, sh_ref, w_ref, b_ref, o_ref, h_ref):
    @pl.when(pl.program_id(2) == 0)
    def _():
        h_ref[...] = _norm_mod(x_ref[...], g_ref[...], sc_ref[...], sh_ref[...]).astype(BF16)

    o_ref[...] = (_dot(h_ref[...], w_ref[...]) + b_ref[...]).astype(o_ref.dtype)


def _norm_linear(x, g, sc, sh, w, b, tm, tn, name):
    bsz, l, d = x.shape
    n = w.shape[1]
    tn = _tile(n, tn)
    return pl.pallas_call(
        _norm_linear_kernel,
        grid=(bsz, l // tm, n // tn),
        in_specs=[pl.BlockSpec((None, tm, d), lambda bi, i, j: (bi, i, 0)),
                  pl.BlockSpec((1, d), lambda bi, i, j: (0, 0)),
                  pl.BlockSpec((None, 1, d), lambda bi, i, j: (bi, 0, 0)),
                  pl.BlockSpec((None, 1, d), lambda bi, i, j: (bi, 0, 0)),
                  pl.BlockSpec((d, tn), lambda bi, i, j: (0, j)),
                  pl.BlockSpec((1, tn), lambda bi, i, j: (0, j))],
        out_specs=pl.BlockSpec((None, tm, tn), lambda bi, i, j: (bi, i, j)),
        out_shape=jax.ShapeDtypeStruct((bsz, l, n), BF16),
        scratch_shapes=[pltpu.VMEM((tm, d), BF16)],
        compiler_params=_params("parallel", "arbitrary", "arbitrary"),
        name=name,
    )(x, g, sc, sh, w, b)


def _mm_res_kernel(*refs, n_pairs):
    a_refs = refs[:n_pairs]
    w_refs = refs[n_pairs:2 * n_pairs]
    b_ref, x_ref, g_ref, o_ref = refs[2 * n_pairs:]
    acc = _dot(a_refs[0][...], w_refs[0][...])
    for a_ref, w_ref in zip(a_refs[1:], w_refs[1:]):
        acc = acc + _dot(a_ref[...], w_ref[...])
    o_ref[...] = x_ref[...] + g_ref[...] * (acc + b_ref[...])


def _mm_residual(a_list, w_list, b, x, g, tm, name):
    bsz, l, n = x.shape
    n_pairs = len(a_list)
    in_specs = [pl.BlockSpec((None, tm, a.shape[2]), lambda bi, i: (bi, i, 0)) for a in a_list]
    in_specs += [pl.BlockSpec(w.shape, lambda bi, i: (0, 0)) for w in w_list]
    in_specs += [pl.BlockSpec((1, n), lambda bi, i: (0, 0)),
                 pl.BlockSpec((None, tm, n), lambda bi, i: (bi, i, 0)),
                 pl.BlockSpec((None, 1, n), lambda bi, i: (bi, 0, 0))]
    return pl.pallas_call(
        functools.partial(_mm_res_kernel, n_pairs=n_pairs),
        grid=(bsz, l // tm),
        in_specs=in_specs,
        out_specs=pl.BlockSpec((None, tm, n), lambda bi, i: (bi, i, 0)),
        out_shape=jax.ShapeDtypeStruct((bsz, l, n), F32),
        compiler_params=_params("parallel", "parallel"),
        name=name,
    )(*a_list, *w_list, b, x, g)


def _rope_tables(n_tokens, dim, width):
    n_rows = n_tokens // GRID_W
    row = jnp.repeat(jnp.arange(n_rows, dtype=F32), GRID_W)
    col = jnp.tile(jnp.arange(GRID_W, dtype=F32), n_rows)
    n_freq = dim // 4
    inv = ROPE_BASE ** (-jnp.arange(n_freq, dtype=F32) / n_freq)
    ang = jnp.concatenate([row[:, None] * inv, col[:, None] * inv], axis=-1)
    ang = jnp.repeat(ang, 2, axis=-1)
    ang = jnp.tile(ang, (1, width // dim))
    sign = jnp.where(jnp.arange(width) % 2 == 0, -1.0, 1.0).astype(F32)
    return jnp.cos(ang), jnp.sin(ang) * sign


def _rope(x, cos_t, sin_t):
    width = x.shape[-1]
    lane = lax.broadcasted_iota(jnp.int32, (1, width), 1)
    partner = jnp.where(lane % 2 == 0, pltpu.roll(x, width - 1, 1), pltpu.roll(x, 1, 1))
    return x * cos_t + partner * sin_t


def _sub_rms(x, g):
    lane = lax.broadcasted_iota(jnp.int32, (1, 2 * DIFF_SUB), 1)
    lo = lane < DIFF_SUB
    sq = x * x
    ss_lo = jnp.sum(jnp.where(lo, sq, 0.0), axis=-1, keepdims=True)
    ss_hi = jnp.sum(jnp.where(lo, 0.0, sq), axis=-1, keepdims=True)
    inv = jnp.where(lo, lax.rsqrt(ss_lo / DIFF_SUB + EPS), lax.rsqrt(ss_hi / DIFF_SUB + EPS))
    return (x * inv) * g


def _rms(x, g):
    return (x * lax.rsqrt(jnp.mean(x * x, axis=-1, keepdims=True) + EPS)) * g


def _diff_attn_kernel(q_ref, k_ref, v_ref, kc_ref, vc_ref, cos_ref, sin_ref,
                      gq_ref, gk_ref, gs_ref, lamp_ref, o_ref, q_scr, k_scr, v_scr, s_scr, lam_scr,
                      *, lam_init, n_ctx, tq):
    n_keys = k_scr.shape[0]

    @pl.when(pl.program_id(2) == 0)
    def _():
        lane = lax.broadcasted_iota(jnp.int32, (1, 2 * DIFF_SUB), 1)
        lo = lane < DIFF_SUB
        gk = gk_ref[...]
        cos_t = cos_ref[...]
        sin_t = sin_ref[...]
        k_scr[0:n_ctx, :] = _sub_rms(kc_ref[...].astype(F32), gk).astype(BF16)
        k_scr[n_ctx:, :] = _rope(_sub_rms(k_ref[...].astype(F32), gk), cos_t, sin_t).astype(BF16)
        v_scr[0:n_ctx, 0:HEAD_DIM] = vc_ref[...].astype(BF16)
        v_scr[n_ctx:, 0:HEAD_DIM] = v_ref[...].astype(BF16)
        v_scr[:, HEAD_DIM:] = jnp.broadcast_to(jnp.where(lane == 0, 1.0, 0.0).astype(BF16), (n_keys, HEAD_DIM))
        q = _rope(_sub_rms(q_ref[...].astype(F32), gq_ref[...]), cos_t, sin_t) * (DIFF_SUB ** -0.5 * LOG2_E)
        q_scr[0] = jnp.where(lo, q, 0.0).astype(BF16)
        q_scr[1] = jnp.where(lo, 0.0, q).astype(BF16)
        lp = lamp_ref[...]
        t1 = jnp.sum(lp[0:1] * lp[1:2], axis=-1, keepdims=True)
        t2 = jnp.sum(lp[2:3] * lp[3:4], axis=-1, keepdims=True)
        lam_scr[...] = jnp.broadcast_to(jnp.exp(t1) - jnp.exp(t2) + lam_init, lam_scr.shape)

    chunks = [(c0, min(c0 + DIFF_KEY_CHUNK, n_keys)) for c0 in range(0, n_keys, DIFF_KEY_CHUNK)]
    lam = lam_scr[:, 0:1]
    for t in range(tq // Q_BLOCK):
        rows = pl.ds(pl.multiple_of(pl.program_id(2) * tq + t * Q_BLOCK, Q_BLOCK), Q_BLOCK)
        qs = [q_scr[0, rows, :], q_scr[1, rows, :]]

        mx = [jnp.full((Q_BLOCK, LANES), NEG_INF, F32) for _ in range(2)]
        for c0, c1 in chunks:
            kc = k_scr[c0:c1, :]
            for h in range(2):
                s = _dot_nt(qs[h], kc)
                s_scr[2 * t + h, :, c0:c1] = s
                for j in range((c1 - c0) // LANES):
                    mx[h] = jnp.maximum(mx[h], s[:, j * LANES:(j + 1) * LANES])
        mrow = [jnp.max(m, axis=-1, keepdims=True) for m in mx]

        out = [jnp.zeros((Q_BLOCK, 2 * HEAD_DIM), F32) for _ in range(2)]
        for c0, c1 in chunks:
            vc = v_scr[c0:c1, :]
            for h in range(2):
                e = jnp.exp2(s_scr[2 * t + h, :, c0:c1] - mrow[h])
                out[h] = out[h] + _dot(e.astype(BF16), vc)
        r1 = 1.0 / out[0][:, HEAD_DIM:HEAD_DIM + 1]
        r2 = lam / out[1][:, HEAD_DIM:HEAD_DIM + 1]
        o = out[0][:, 0:HEAD_DIM] * r1 - out[1][:, 0:HEAD_DIM] * r2
        o_ref[t * Q_BLOCK:(t + 1) * Q_BLOCK, :] = (_rms(o, gs_ref[...]) * (1.0 - lam_init)).astype(o_ref.dtype)


def _diff_attention(qkv, qkv_c, tabs, gq, gk, gs, lamp, lam_init, tq):
    bsz, l, _ = qkv.shape
    n_ctx = qkv_c.shape[1]
    cos_t, sin_t = tabs
    hb = 2 * DIFF_SUB
    kb = DIFF_Q // hb
    vb = 2 * DIFF_Q // hb
    kern = functools.partial(_diff_attn_kernel, lam_init=lam_init, n_ctx=n_ctx, tq=tq)
    vec = pl.BlockSpec((1, hb), lambda bi, h, i: (0, 0))
    return pl.pallas_call(
        kern,
        grid=(bsz, DIFF_HEADS, l // tq),
        in_specs=[pl.BlockSpec((None, l, hb), lambda bi, h, i: (bi, 0, h)),
                  pl.BlockSpec((None, l, hb), lambda bi, h, i: (bi, 0, kb + h)),
                  pl.BlockSpec((None, l, hb), lambda bi, h, i: (bi, 0, vb + h)),
                  pl.BlockSpec((None, n_ctx, hb), lambda bi, h, i: (bi, 0, kb + h)),
                  pl.BlockSpec((None, n_ctx, hb), lambda bi, h, i: (bi, 0, vb + h)),
                  pl.BlockSpec((l, hb), lambda bi, h, i: (0, 0)),
                  pl.BlockSpec((l, hb), lambda bi, h, i: (0, 0)),
                  vec, vec, vec,
                  pl.BlockSpec((8, hb), lambda bi, h, i: (0, 0))],
        out_specs=pl.BlockSpec((None, tq, hb), lambda bi, h, i: (bi, i, h)),
        out_shape=jax.ShapeDtypeStruct((bsz, l, DIFF_V), BF16),
        scratch_shapes=[pltpu.VMEM((2, l, hb), BF16), pltpu.VMEM((n_ctx + l, hb), BF16),
                        pltpu.VMEM((n_ctx + l, 2 * HEAD_DIM), BF16),
                        pltpu.VMEM((2 * (tq // Q_BLOCK), Q_BLOCK, n_ctx + l), F32),
                        pltpu.VMEM((1, hb), F32)],
        compiler_params=_params("parallel", "parallel", "arbitrary"),
        name="diff_attention",
    )(qkv, qkv, qkv, qkv_c, qkv_c, cos_t, sin_t, gq, gk, gs, lamp)


def _swa_kernel(q_ref, k_ref, v_ref, kc_ref, vc_ref, cos_ref, sin_ref, gq_ref, gk_ref, sink_ref,
                o_ref, q_scr, k_scr, v_scr, kc_scr, vc_scr, *, seq):
    kvh = pl.program_id(1)
    n = pl.program_id(2)
    span = Q_BLOCK + 2 * WINDOW

    @pl.when(n == 0)
    def _():
        gk = gk_ref[...]
        gq = gq_ref[...]
        cos_t = cos_ref[...]
        sin_t = sin_ref[...]
        zeros = jnp.zeros((WINDOW, HEAD_DIM), BF16)
        k_scr[0:WINDOW, :] = zeros
        v_scr[0:WINDOW, :] = zeros
        k_scr[WINDOW + seq:, :] = zeros
        v_scr[WINDOW + seq:, :] = zeros
        k_scr[WINDOW:WINDOW + seq, :] = _rope(_rms(k_ref[...].astype(F32), gk), cos_t, sin_t).astype(BF16)
        v_scr[WINDOW:WINDOW + seq, :] = v_ref[...].astype(BF16)
        kc_scr[...] = _rms(kc_ref[...].astype(F32), gk).astype(BF16)
        vc_scr[...] = vc_ref[...].astype(BF16)
        for j in range(SWA_GROUP):
            qj = q_ref[:, j * HEAD_DIM:(j + 1) * HEAD_DIM].astype(F32)
            qj = _rope(_rms(qj, gq), cos_t, sin_t) * (HEAD_DIM ** -0.5)
            q_scr[j] = qj.astype(BF16)

    start = pl.multiple_of(n * Q_BLOCK, Q_BLOCK)
    kw = k_scr[pl.ds(start, span), :]
    vw = v_scr[pl.ds(start, span), :]
    kc = kc_scr[...]
    vc = vc_scr[...]
    rows = SWA_GROUP * Q_BLOCK
    ridx = lax.broadcasted_iota(jnp.int32, (rows, span), 0)
    kidx = lax.broadcasted_iota(jnp.int32, (rows, span), 1)
    rel = kidx - (ridx & (Q_BLOCK - 1))
    key_pos = n * Q_BLOCK + kidx - WINDOW
    mask = (rel >= 0) & (rel <= 2 * WINDOW) & (key_pos >= 0) & (key_pos < seq)
    head = lax.broadcasted_iota(jnp.int32, (rows, 1), 0) // Q_BLOCK
    sink = jnp.zeros((rows, 1), F32)
    for j in range(SWA_GROUP):
        sink = jnp.where(head == j, sink_ref[pl.ds(kvh * SWA_GROUP + j, 1), 0:1], sink)
    q = jnp.concatenate([q_scr[j, pl.ds(start, Q_BLOCK), :] for j in range(SWA_GROUP)], axis=0)
    s_loc = jnp.where(mask, _dot_nt(q, kw), NEG_INF)
    s_ctx = _dot_nt(q, kc)
    m = jnp.maximum(jnp.maximum(jnp.max(s_loc, axis=-1, keepdims=True),
                                jnp.max(s_ctx, axis=-1, keepdims=True)), sink)
    e_loc = jnp.exp(s_loc - m)
    e_ctx = jnp.exp(s_ctx - m)
    denom = (jnp.sum(e_loc, axis=-1, keepdims=True) + jnp.sum(e_ctx, axis=-1, keepdims=True)
             + jnp.exp(sink - m))
    r = 1.0 / denom
    o = _dot((e_ctx * r).astype(BF16), vc) + _dot((e_loc * r).astype(BF16), vw)
    o_ref[...] = jnp.concatenate([o[j * Q_BLOCK:(j + 1) * Q_BLOCK] for j in range(SWA_GROUP)],
                                 axis=-1).astype(o_ref.dtype)


def _swa_attention(qkv, qkv_c, tabs, gq, gk, sink_rows):
    bsz, l, _ = qkv.shape
    n_ctx = qkv_c.shape[1]
    cos_t, sin_t = tabs
    gw = SWA_GROUP * HEAD_DIM
    q0 = (2 * DIFF_Q + DIFF_V) // gw
    k0 = (2 * DIFF_Q + DIFF_V + SWA_Q) // HEAD_DIM
    v0 = k0 + SWA_KV_HEADS
    vec = pl.BlockSpec((1, HEAD_DIM), lambda bi, h, i: (0, 0))
    return pl.pallas_call(
        functools.partial(_swa_kernel, seq=l),
        grid=(bsz, SWA_KV_HEADS, l // Q_BLOCK),
        in_specs=[pl.BlockSpec((None, l, gw), lambda bi, h, i: (bi, 0, q0 + h)),
                  pl.BlockSpec((None, l, HEAD_DIM), lambda bi, h, i: (bi, 0, k0 + h)),
                  pl.BlockSpec((None, l, HEAD_DIM), lambda bi, h, i: (bi, 0, v0 + h)),
                  pl.BlockSpec((None, n_ctx, HEAD_DIM), lambda bi, h, i: (bi, 0, k0 + h)),
                  pl.BlockSpec((None, n_ctx, HEAD_DIM), lambda bi, h, i: (bi, 0, v0 + h)),
                  pl.BlockSpec((l, HEAD_DIM), lambda bi, h, i: (0, 0)),
                  pl.BlockSpec((l, HEAD_DIM), lambda bi, h, i: (0, 0)),
                  vec, vec,
                  pl.BlockSpec((SWA_HEADS, HEAD_DIM), lambda bi, h, i: (0, 0))],
        out_specs=pl.BlockSpec((None, Q_BLOCK, gw), lambda bi, h, i: (bi, i, h)),
        out_shape=jax.ShapeDtypeStruct((bsz, l, SWA_Q), BF16),
        scratch_shapes=[pltpu.VMEM((SWA_GROUP, l, HEAD_DIM), BF16),
                        pltpu.VMEM((l + 2 * WINDOW, HEAD_DIM), BF16), pltpu.VMEM((l + 2 * WINDOW, HEAD_DIM), BF16),
                        pltpu.VMEM((n_ctx, HEAD_DIM), BF16), pltpu.VMEM((n_ctx, HEAD_DIM), BF16)],
        compiler_params=_params("parallel", "parallel", "arbitrary"),
        name="swa_attention",
    )(qkv, qkv, qkv, qkv_c, qkv_c, cos_t, sin_t, gq, gk, sink_rows)


def _router_kernel(x_ref, g_ref, sc_ref, sh_ref, w_ref, b_ref, tri_ref, h_ref, e_ref, p_ref, cnt_ref, run_scr):
    @pl.when((pl.program_id(0) == 0) & (pl.program_id(1) == 0))
    def _():
        run_scr[...] = jnp.zeros(run_scr.shape, F32)

    h = _norm_mod(x_ref[...], g_ref[...], sc_ref[...], sh_ref[...])
    h_ref[...] = h
    logits = jnp.dot(h, w_ref[...], precision=HIGHEST, preferred_element_type=F32) + b_ref[...]
    lane_i = lax.broadcasted_iota(jnp.int32, (1, LANES), 1)
    lane = lane_i.astype(F32)
    big = float(LANES)
    lg = jnp.where(lane_i < MOE_GROUPS, logits, NEG_INF)
    mg = jnp.max(lg, axis=-1, keepdims=True)
    p_top = 1.0 / jnp.sum(jnp.exp(lg - mg), axis=-1, keepdims=True)
    grp = jnp.min(jnp.where(lg == mg, lane, big), axis=-1, keepdims=True)
    e_lane = lane - MOE_GROUPS
    in_grp = (e_lane >= grp * MOE_PER_GROUP) & (e_lane < (grp + 1.0) * MOE_PER_GROUP)
    le = jnp.where(in_grp, logits, NEG_INF)
    t1 = jnp.max(le, axis=-1, keepdims=True)
    i1 = jnp.min(jnp.where(le == t1, lane, big), axis=-1, keepdims=True)
    le2 = jnp.where(lane == i1, NEG_INF, le)
    t2 = jnp.max(le2, axis=-1, keepdims=True)
    i2 = jnp.min(jnp.where(le2 == t2, lane, big), axis=-1, keepdims=True)
    w2 = jnp.exp(t2 - t1)
    inv = p_top / (1.0 + w2)
    hot1 = lane == i1
    hot2 = lane == i2
    onehot = jnp.where(hot1 | hot2, 1.0, 0.0)
    before = run_scr[...] + _dot(tri_ref[...], onehot.astype(BF16))
    r1 = jnp.sum(jnp.where(hot1, before, 0.0), axis=-1, keepdims=True)
    r2 = jnp.sum(jnp.where(hot2, before, 0.0), axis=-1, keepdims=True)
    run_scr[...] = run_scr[...] + jnp.sum(onehot, axis=0, keepdims=True)
    cnt_ref[...] = run_scr[...]
    e_ref[...] = jnp.where(lane_i == 0, i1 - MOE_GROUPS,
                           jnp.where(lane_i == 1, i2 - MOE_GROUPS,
                                     jnp.where(lane_i == 2, r1, jnp.where(lane_i == 3, r2, 0.0)))).astype(jnp.int32)
    p_ref[...] = jnp.where(lane_i == 0, inv, jnp.where(lane_i == 1, inv * w2, 0.0))


def _router(x, g, sc, sh, w, b, tm):
    bsz, l, d = x.shape
    row = lambda bi, i: (bi, i, 0)
    tri = (jnp.arange(tm)[:, None] > jnp.arange(tm)[None, :]).astype(BF16)
    return pl.pallas_call(
        _router_kernel,
        grid=(bsz, l // tm),
        in_specs=[pl.BlockSpec((None, tm, d), row),
                  pl.BlockSpec((1, d), lambda bi, i: (0, 0)),
                  pl.BlockSpec((None, 1, d), lambda bi, i: (bi, 0, 0)),
                  pl.BlockSpec((None, 1, d), lambda bi, i: (bi, 0, 0)),
                  pl.BlockSpec((d, LANES), lambda bi, i: (0, 0)),
                  pl.BlockSpec((1, LANES), lambda bi, i: (0, 0)),
                  pl.BlockSpec((tm, tm), lambda bi, i: (0, 0))],
        out_specs=[pl.BlockSpec((None, tm, d), row),
                   pl.BlockSpec((None, tm, LANES), row),
                   pl.BlockSpec((None, tm, LANES), row),
                   pl.BlockSpec((1, LANES), lambda bi, i: (0, 0))],
        out_shape=[jax.ShapeDtypeStruct((bsz, l, d), F32),
                   jax.ShapeDtypeStruct((bsz, l, LANES), jnp.int32),
                   jax.ShapeDtypeStruct((bsz, l, LANES), F32),
                   jax.ShapeDtypeStruct((1, LANES), F32)],
        scratch_shapes=[pltpu.VMEM((1, LANES), F32)],
        compiler_params=_params("arbitrary", "arbitrary"),
        name="moe_router",
    )(x, g, sc, sh, w, b, tri)


def _dispatch_kernel(slot_ref, h_ref, init_ref, xb_ref, sem):
    del init_ref
    tm = h_ref.shape[0]
    base = pl.program_id(0) * tm

    def copy(r, k):
        dst = slot_ref[(base + r) * MOE_TOPK + k]
        return pltpu.make_async_copy(h_ref.at[pl.ds(r, 1)], xb_ref.at[pl.ds(dst, 1)], sem)

    def start(r, carry):
        for k in range(MOE_TOPK):
            copy(r, k).start(priority=k)
        return carry

    def wait(r, carry):
        for k in range(MOE_TOPK):
            copy(r, k).wait()
        return carry

    lax.fori_loop(0, tm, start, 0, unroll=8)
    lax.fori_loop(0, tm, wait, 0, unroll=8)


def _dispatch(slot, h, n_rows, tm):
    n, d = h.shape
    grid_spec = pltpu.PrefetchScalarGridSpec(
        num_scalar_prefetch=1,
        grid=(n // tm,),
        in_specs=[pl.BlockSpec((tm, d), lambda i, s: (i, 0)),
                  pl.BlockSpec(memory_space=pl.ANY)],
        out_specs=pl.BlockSpec(memory_space=pl.ANY),
        scratch_shapes=[pltpu.SemaphoreType.DMA(())],
    )
    return pl.pallas_call(
        _dispatch_kernel,
        grid_spec=grid_spec,
        out_shape=jax.ShapeDtypeStruct((n_rows, d), F32),
        input_output_aliases={2: 0},
        compiler_params=_params("arbitrary"),
        name="moe_dispatch",
    )(slot, h, jnp.zeros((n_rows, d), F32))


def _combine_kernel(slot_ref, x_ref, g_ref, p_ref, yb_ref, o_ref, y_scr, sem):
    tm = x_ref.shape[0]
    base = pl.program_id(0) * tm

    def copy(r, k):
        src = slot_ref[(base + r) * MOE_TOPK + k]
        return pltpu.make_async_copy(yb_ref.at[pl.ds(src, 1)], y_scr.at[k, pl.ds(r, 1)], sem)

    def start(r, carry):
        for k in range(MOE_TOPK):
            copy(r, k).start(priority=k)
        return carry

    def wait(r, carry):
        for k in range(MOE_TOPK):
            copy(r, k).wait()
        return carry

    lax.fori_loop(0, tm, start, 0, unroll=8)
    lax.fori_loop(0, tm, wait, 0, unroll=8)
    p = p_ref[...]
    moe = y_scr[0] * p[:, 0:1]
    for k in range(1, MOE_TOPK):
        moe = moe + y_scr[k] * p[:, k:k + 1]
    o_ref[...] = x_ref[...] + g_ref[...] * moe


def _combine(slot, x, g, gates, yb, tm):
    bsz, l, d = x.shape
    nt = l // tm
    grid_spec = pltpu.PrefetchScalarGridSpec(
        num_scalar_prefetch=1,
        grid=(bsz * nt,),
        in_specs=[pl.BlockSpec((None, tm, d), lambda i, s: (i // nt, i % nt, 0)),
                  pl.BlockSpec((None, 1, d), lambda i, s: (i // nt, 0, 0)),
                  pl.BlockSpec((None, tm, LANES), lambda i, s: (i // nt, i % nt, 0)),
                  pl.BlockSpec(memory_space=pl.ANY)],
        out_specs=pl.BlockSpec((None, tm, d), lambda i, s: (i // nt, i % nt, 0)),
        scratch_shapes=[pltpu.VMEM((MOE_TOPK, tm, d), F32), pltpu.SemaphoreType.DMA(())],
    )
    return pl.pallas_call(
        _combine_kernel,
        grid_spec=grid_spec,
        out_shape=jax.ShapeDtypeStruct((bsz, l, d), F32),
        compiler_params=_params("arbitrary"),
        name="moe_combine",
    )(slot, x, g, gates, yb)


def _expert_kernel(be_ref, nu_ref, x_ref, wg_ref, wu_ref, wd_ref, o_ref):
    i = pl.program_id(0)

    @pl.when(i < nu_ref[0])
    def _():
        x = x_ref[...].astype(BF16)
        ff = wg_ref.shape[1]
        acc = jnp.zeros(o_ref.shape, F32)
        for c in range(ff // FF_CHUNK):
            cols = slice(c * FF_CHUNK, (c + 1) * FF_CHUNK)
            gate = _dot(x, wg_ref[:, cols])
            up = _dot(x, wu_ref[:, cols])
            hidden = (gate * jax.nn.sigmoid(gate) * up).astype(BF16)
            acc = acc + _dot(hidden, wd_ref[cols, :])
        o_ref[...] = acc

    @pl.when(i >= nu_ref[0])
    def _():
        o_ref[...] = jnp.zeros(o_ref.shape, F32)


def _expert_ffn(xb, block_e, n_used, w_gate, w_up, w_down, layer):
    n_blocks, _, d = xb.shape
    ff = w_gate.shape[2]
    base = layer * MOE_EXPERTS
    grid_spec = pltpu.PrefetchScalarGridSpec(
        num_scalar_prefetch=2,
        grid=(n_blocks,),
        in_specs=[pl.BlockSpec((None, MOE_BLOCK, d), lambda i, be, nu: (i, 0, 0)),
                  pl.BlockSpec((None, d, ff), lambda i, be, nu: (base + be[i], 0, 0)),
                  pl.BlockSpec((None, d, ff), lambda i, be, nu: (base + be[i], 0, 0)),
                  pl.BlockSpec((None, ff, d), lambda i, be, nu: (base + be[i], 0, 0))],
        out_specs=pl.BlockSpec((None, MOE_BLOCK, d), lambda i, be, nu: (i, 0, 0)),
    )
    return pl.pallas_call(
        _expert_kernel,
        grid_spec=grid_spec,
        out_shape=jax.ShapeDtypeStruct((n_blocks, MOE_BLOCK, d), F32),
        compiler_params=_params("arbitrary"),
        name="expert_ffn",
    )(block_e, n_used, xb, w_gate, w_up, w_down)


def _hier_moe(x, g, sc, sh, gate2, w_route, b_route, w_gate, w_up, w_down, layer):
    bsz, l, d = x.shape
    n = bsz * l
    tm = min(512, l)
    h, route, gates, cnt = _router(x, g, sc, sh, w_route, b_route, tm)
    route = route.reshape(n, LANES)
    expert = route[:, :MOE_TOPK]
    rank = route[:, MOE_TOPK:2 * MOE_TOPK]
    counts = cnt[0, MOE_GROUPS:MOE_GROUPS + MOE_EXPERTS].astype(jnp.int32)
    padded = (counts + MOE_BLOCK - 1) // MOE_BLOCK * MOE_BLOCK
    pad_end = jnp.cumsum(padded)
    pad_start = pad_end - padded
    hit = expert[:, :, None] == jnp.arange(MOE_EXPERTS, dtype=jnp.int32)
    slot = (jnp.sum(jnp.where(hit, pad_start, 0), axis=-1) + rank).reshape(-1)
    n_blocks = -(-(n * MOE_TOPK) // MOE_BLOCK) + MOE_EXPERTS
    block_row = jnp.arange(n_blocks, dtype=jnp.int32)[:, None] * MOE_BLOCK
    block_e = jnp.minimum(jnp.sum((pad_end[None, :] <= block_row).astype(jnp.int32), axis=-1), MOE_EXPERTS - 1)
    n_used = (pad_end[-1:] // MOE_BLOCK).astype(jnp.int32)
    xb = _dispatch(slot, h.reshape(n, d), n_blocks * MOE_BLOCK, tm).reshape(n_blocks, MOE_BLOCK, d)
    yb = _expert_ffn(xb, block_e, n_used, w_gate, w_up, w_down, layer).reshape(-1, d)
    return _combine(slot, x, gate2, gates, yb, tm)


def _short_conv_kernel(zv_ref, z1_ref, z2_ref, wv_ref, w1_ref, w2_ref, bv_ref, b1_ref, b2_ref,
                       v_ref, x1_ref, x2_ref):
    n = zv_ref.shape[0]
    row = lax.broadcasted_iota(jnp.int32, (n, 1), 0)

    def conv(z_ref, w_ref, b_ref):
        z = z_ref[...].astype(F32)
        w = w_ref[...]
        prev = jnp.where(row == 0, 0.0, pltpu.roll(z, 1, 0))
        nxt = jnp.where(row == n - 1, 0.0, pltpu.roll(z, n - 1, 0))
        return b_ref[...] + prev * w[0:1] + z * w[1:2] + nxt * w[2:3]

    v_ref[...] = conv(zv_ref, wv_ref, bv_ref).astype(v_ref.dtype)
    x1_ref[...] = conv(z1_ref, w1_ref, b1_ref).astype(x1_ref.dtype)
    x2_ref[...] = conv(z2_ref, w2_ref, b2_ref).astype(x2_ref.dtype)


def _short_conv(z, conv_w, conv_b, tc):
    bsz, l, w3 = z.shape
    w = w3 // 3
    nc = w // tc
    zs = lambda k: pl.BlockSpec((None, l, tc), lambda bi, j: (bi, 0, k * nc + j))
    ws = lambda k: pl.BlockSpec((8, tc), lambda bi, j: (0, k * nc + j))
    bs = lambda k: pl.BlockSpec((1, tc), lambda bi, j: (0, k * nc + j))
    out = pl.BlockSpec((None, l, tc), lambda bi, j: (bi, 0, j))
    return pl.pallas_call(
        _short_conv_kernel,
        grid=(bsz, nc),
        in_specs=[zs(0), zs(1), zs(2), ws(0), ws(1), ws(2), bs(0), bs(1), bs(2)],
        out_specs=[out, out, out],
        out_shape=[jax.ShapeDtypeStruct((bsz, l, w), BF16)] * 3,
        compiler_params=_params("parallel", "parallel"),
        name="short_conv",
    )(z, z, z, conv_w, conv_w, conv_w, conv_b, conv_b, conv_b)


def _filter_kernel(z_ref, w1_ref, b1_ref, f1_ref, w2_ref, b2_ref, f2_ref, w3_ref, dl_ref, bw_ref, o_ref, a_scr):
    n = z_ref.shape[0]

    @pl.when(pl.program_id(0) == 0)
    def _():
        a = jnp.sin(f1_ref[...] * (jnp.dot(z_ref[...], w1_ref[...], precision=HIGHEST,
                                           preferred_element_type=F32) + b1_ref[...]))
        a_scr[...] = jnp.sin(f2_ref[...] * (jnp.dot(a, w2_ref[...], precision=HIGHEST,
                                                    preferred_element_type=F32) + b2_ref[...]))

    hf = jnp.dot(a_scr[...], w3_ref[...], precision=HIGHEST, preferred_element_type=F32)
    row = lax.broadcasted_iota(jnp.int32, (n, 1), 0)
    t = row.astype(F32) * (1.0 / (n - 1))
    hf = hf * jnp.exp(-t * dl_ref[...])
    o_ref[...] = jnp.where((row == 0) & (bw_ref[...] > 0.5), 0.0, hf)


def _hyena_filters(z, w1, b1, f1, w2, b2, f2, w3, deltas, is_bwd, tn):
    n = z.shape[0]
    cols = w3.shape[1]
    full = lambda a: pl.BlockSpec(a.shape, lambda j: (0, 0))
    tile = lambda r: pl.BlockSpec((r, tn), lambda j: (0, j))
    return pl.pallas_call(
        _filter_kernel,
        grid=(cols // tn,),
        in_specs=[full(z), full(w1), full(b1), full(f1), full(w2), full(b2), full(f2),
                  tile(w3.shape[0]), tile(1), tile(1)],
        out_specs=tile(n),
        out_shape=jax.ShapeDtypeStruct((n, cols), F32),
        scratch_shapes=[pltpu.VMEM((n, LANES), F32)],
        compiler_params=_params("arbitrary"),
        name="hyena_filters",
    )(z, w1, b1, f1, w2, b2, f2, w3, deltas, is_bwd)


def _dft_tables(n):
    r = 1 << (n.bit_length() // 2)
    f = jnp.arange(n, dtype=jnp.int32)
    ang = lambda step: (((2 * f[:, None] + 1) * (step * jnp.arange(r, dtype=jnp.int32)[None, :])) % (4 * n)
                        ).astype(F32) * (math.pi / (2 * n))
    hi, lo = ang(r)[:, :n // r], ang(1)
    ch, sh, cl, sl = jnp.cos(hi), jnp.sin(hi), jnp.cos(lo), jnp.sin(lo)
    c = (ch[:, :, None] * cl[:, None, :] - sh[:, :, None] * sl[:, None, :]).reshape(n, n)
    sn = (sh[:, :, None] * cl[:, None, :] + ch[:, :, None] * sl[:, None, :]).reshape(n, n)
    ct = (ch.T[:, None, :] * cl.T[None, :, :] - sh.T[:, None, :] * sl.T[None, :, :]).reshape(n, n)
    st = (sh.T[:, None, :] * cl.T[None, :, :] + ch.T[:, None, :] * sl.T[None, :, :]).reshape(n, n)
    return c.astype(BF16), sn.astype(BF16), ct.astype(BF16), st.astype(BF16)


def _dft_filter_kernel(c_ref, s_ref, hf_ref, hb_ref, kr_ref, ki_ref, sum_scr, dif_scr, *, scale):
    @pl.when(pl.program_id(2) == 0)
    def _():
        hf = hf_ref[...]
        hb = hb_ref[...]
        sum_scr[...] = (hf + hb).astype(BF16)
        dif_scr[...] = (hb - hf).astype(BF16)

    kr_ref[...] = _dot(c_ref[...], sum_scr[...]) * scale
    ki_ref[...] = _dot(s_ref[...], dif_scr[...]) * scale


def _dft_filter(c, s, filt, width, tf, tn):
    n = c.shape[0]
    nc = width // tn
    out = pl.BlockSpec((None, tf, tn), lambda o, j, i: (o, i, j))
    return pl.pallas_call(
        functools.partial(_dft_filter_kernel, scale=1.0 / n),
        grid=(HYENA_ORDER, nc, n // tf),
        in_specs=[pl.BlockSpec((tf, n), lambda o, j, i: (i, 0)),
                  pl.BlockSpec((tf, n), lambda o, j, i: (i, 0)),
                  pl.BlockSpec((n, tn), lambda o, j, i: (0, (2 * o) * nc + j)),
                  pl.BlockSpec((n, tn), lambda o, j, i: (0, (2 * o + 1) * nc + j))],
        out_specs=[out, out],
        out_shape=[jax.ShapeDtypeStruct((HYENA_ORDER, n, width), F32)] * 2,
        scratch_shapes=[pltpu.VMEM((n, tn), BF16), pltpu.VMEM((n, tn), BF16)],
        compiler_params=_params("parallel", "parallel", "arbitrary"),
        name="dft_filter",
    )(c, s, filt, filt)


def _dft_fwd_kernel(c_ref, s_ref, u_ref, kr_ref, ki_ref, pr_ref, pi_ref):
    u = u_ref[...]
    ur = _dot(c_ref[...], u)
    us = _dot(s_ref[...], u)
    kr = kr_ref[...]
    ki = ki_ref[...]
    pr_ref[...] = (ur * kr + us * ki).astype(pr_ref.dtype)
    pi_ref[...] = (ur * ki - us * kr).astype(pi_ref.dtype)


def _dft_fwd(c, s, u, kr, ki, order, tf, tn):
    bsz, n, width = u.shape
    out = pl.BlockSpec((None, tf, tn), lambda bi, j, i: (bi, i, j))
    ks = pl.BlockSpec((None, tf, tn), lambda bi, j, i: (order, i, j))
    return pl.pallas_call(
        _dft_fwd_kernel,
        grid=(bsz, width // tn, n // tf),
        in_specs=[pl.BlockSpec((tf, n), lambda bi, j, i: (i, 0)),
                  pl.BlockSpec((tf, n), lambda bi, j, i: (i, 0)),
                  pl.BlockSpec((None, n, tn), lambda bi, j, i: (bi, 0, j)),
                  ks, ks],
        out_specs=[out, out],
        out_shape=[jax.ShapeDtypeStruct((bsz, n, width), BF16)] * 2,
        compiler_params=_params("parallel", "parallel", "parallel"),
        name="dft_fwd",
    )(c, s, u, kr, ki)


def _dft_inv_kernel(ct_ref, st_ref, pr_ref, pi_ref, u_ref, b_ref, x_ref, o_ref):
    y = _dot(ct_ref[...], pr_ref[...]) - _dot(st_ref[...], pi_ref[...])
    u = u_ref[...].astype(F32)
    o_ref[...] = (x_ref[...] * (y + u * b_ref[...])).astype(o_ref.dtype)


def _dft_inv(ct, st, pr, pi, u, bias, x, order, tt, tn):
    bsz, n, width = u.shape
    tile = pl.BlockSpec((None, tt, tn), lambda bi, j, i: (bi, i, j))
    spec = pl.BlockSpec((None, n, tn), lambda bi, j, i: (bi, 0, j))
    return pl.pallas_call(
        _dft_inv_kernel,
        grid=(bsz, width // tn, n // tt),
        in_specs=[pl.BlockSpec((tt, n), lambda bi, j, i: (i, 0)),
                  pl.BlockSpec((tt, n), lambda bi, j, i: (i, 0)),
                  spec, spec, tile,
                  pl.BlockSpec((None, 1, tn), lambda bi, j, i: (order, 0, j)),
                  tile],
        out_specs=tile,
        out_shape=jax.ShapeDtypeStruct((bsz, n, width), BF16),
        compiler_params=_params("parallel", "parallel", "parallel"),
        name="dft_inv",
    )(ct, st, pr, pi, u, bias, x)


def _filter_features(n):
    t = jnp.linspace(0.0, 1.0, n, dtype=F32)[:, None]
    w = 2.0 * math.pi * jnp.arange(n, dtype=F32) / n
    f = jnp.linspace(1e-4, FILTER_BANDS - 1, FILTER_BANDS, dtype=F32)
    z = jnp.concatenate([t, jnp.cos(w[:, None] * f), -jnp.sin(w[:, None] * f)], axis=-1)
    return jnp.pad(z, ((0, 0), (0, LANES - FILTER_EMB)))


def _pad_to(a, shape):
    return jnp.pad(a, [(0, t - s) for s, t in zip(a.shape, shape)])


def kernel(x, c, ctx, c_ctx, ada_w, ada_b, norm1_g, norm2_g, attn_w_in, attn_w_out, diff_q_g, diff_k_g, diff_lq1, diff_lk1, diff_lq2, diff_lk2, diff_sub_g, swa_q_g, swa_k_g, swa_sink, hy_w_in, hy_b_in, hy_conv_w, hy_conv_b, flt_w1, flt_b1, flt_f1, flt_w2, flt_b2, flt_f2, flt_w3, hy_bias, hy_w_out, hy_b_out, moe_wg1, moe_bg1, moe_wg2, moe_bg2, moe_w_gate, moe_w_up, moe_w_down):
    bsz, seq, d = x.shape
    depth = ada_w.shape[0]
    n_ctx = ctx.shape[1]
    cond = jnp.concatenate([c, c_ctx[None, :], jnp.zeros((8 - bsz - 1, d), F32)], axis=0)
    ada_b3 = ada_b[:, None, :]
    zero_bias = lambda n: jnp.zeros((1, n), F32)
    wg_all = moe_w_gate.astype(BF16).reshape(depth * MOE_EXPERTS, d, -1)
    wu_all = moe_w_up.astype(BF16).reshape(depth * MOE_EXPERTS, d, -1)
    wd_all = moe_w_down.astype(BF16).reshape(depth * MOE_EXPERTS, -1, d)
    tm = min(512, seq)

    for layer in range(depth):
        even = layer % 2 == 0
        i = layer // 2
        ctx_live = any(j % 2 == 0 for j in range(layer + 1, depth))
        assert not ctx_live, "context-stream update is only needed for deeper stacks"
        mod = _adaln_mod(cond, ada_w, ada_b3, layer)
        sh1, sc1, g1, sh2, sc2, g2 = [mod[:, k * d:(k + 1) * d] for k in range(6)]
        lat = lambda a: a[:bsz, None, :]
        cmod = lambda a: jnp.broadcast_to(a[bsz:bsz + 1, None, :], (bsz, 1, d))
        n1 = norm1_g[layer][None, :]
        if even:
            w_in = attn_w_in[i].astype(BF16)
            qkv = _norm_linear(x, n1, lat(sc1), lat(sh1), w_in, zero_bias(ATTN_IN), tm, 1536, "attn_in")
            qkv_c = _norm_linear(ctx, n1, cmod(sc1), cmod(sh1), w_in, zero_bias(ATTN_IN), min(tm, n_ctx), 1536,
                                 "attn_in_ctx")
            lam_init = 0.8 - 0.6 * math.exp(-0.3 * layer)
            lamp = _pad_to(jnp.stack([diff_lq1[i], diff_lk1[i], diff_lq2[i], diff_lk2[i]]), (8, 2 * DIFF_SUB))
            two = lambda a: jnp.tile(a, 2)[None, :]
            o_diff = _diff_attention(qkv, qkv_c, _rope_tables(seq, DIFF_SUB, 2 * DIFF_SUB), two(diff_q_g[i]),
                                     two(diff_k_g[i]), diff_sub_g[i][None, :], lamp, lam_init, min(2 * Q_BLOCK, seq))
            sink_rows = jnp.broadcast_to(swa_sink[i][:, None], (SWA_HEADS, HEAD_DIM))
            o_swa = _swa_attention(qkv, qkv_c, _rope_tables(seq, HEAD_DIM, HEAD_DIM), swa_q_g[i][None, :],
                                   swa_k_g[i][None, :], sink_rows)
            w_out = attn_w_out[i].astype(BF16)
            x = _mm_residual([o_diff, o_swa], [w_out[:DIFF_V], w_out[DIFF_V:]], zero_bias(d), x, lat(g1), tm,
                             "attn_out")
        else:
            width = hy_w_out.shape[1]
            z = _norm_linear(x, n1, lat(sc1), lat(sh1), hy_w_in[i].astype(BF16), hy_b_in[i][None, :], tm, 1536,
                             "hyena_in")
            v, x1, x2 = _short_conv(z, _pad_to(hy_conv_w[i], (8, 3 * width)), hy_conv_b[i][None, :], 128)
            max_decay = math.log(DECAY_TARGET) / DECAY_FAST
            min_decay = math.log(DECAY_TARGET) / DECAY_SLOW
            deltas = jnp.abs(jnp.tile(jnp.linspace(min_decay, max_decay, width, dtype=F32), 2 * HYENA_ORDER))[None, :]
            is_bwd = jnp.tile(jnp.concatenate([jnp.zeros((width,), F32), jnp.ones((width,), F32)]),
                              HYENA_ORDER)[None, :]
            sq = (LANES, LANES)
            filt = _hyena_filters(
                _filter_features(seq), _pad_to(flt_w1[i], sq), _pad_to(flt_b1[i][None, :], (1, LANES)),
                _pad_to(flt_f1[i][None, :], (1, LANES)), _pad_to(flt_w2[i], sq),
                _pad_to(flt_b2[i][None, :], (1, LANES)), _pad_to(flt_f2[i][None, :], (1, LANES)),
                _pad_to(flt_w3[i], (LANES, 2 * HYENA_ORDER * width)), deltas, is_bwd, 256)
            cm, sm, ct, st = _dft_tables(seq)
            tf = min(512, seq)
            tn = min(512, width)
            kr, ki = _dft_filter(cm, sm, filt, width, tf, min(256, width))
            bias = hy_bias[i][:, None, :]
            pr, pi = _dft_fwd(cm, sm, v, kr, ki, 0, tf, tn)
            y1 = _dft_inv(ct, st, pr, pi, v, bias, x1, 0, tf, tn)
            pr, pi = _dft_fwd(cm, sm, y1, kr, ki, 1, tf, tn)
            y2 = _dft_inv(ct, st, pr, pi, y1, bias, x2, 1, tf, tn)
            x = _mm_residual([y2], [hy_w_out[i].astype(BF16)], hy_b_out[i][None, :], x, lat(g1), tm, "hyena_out")
        w_route = _pad_to(jnp.concatenate([moe_wg1[layer], moe_wg2[layer]], axis=1), (d, LANES))
        b_route = _pad_to(jnp.concatenate([moe_bg1[layer], moe_bg2[layer]])[None, :], (1, LANES))
        x = _hier_moe(x, norm2_g[layer][None, :], lat(sc2), lat(sh2), lat(g2), w_route, b_route,
                      wg_all, wu_all, wd_all, layer)
    return x
```

```python
import functools
import math

import jax
import jax.numpy as jnp
from jax import lax
from jax.experimental import pallas as pl
from jax.experimental.pallas import tpu as pltpu

F32 = jnp.float32
BF16 = jnp.bfloat16
HIGHEST = lax.Precision.HIGHEST

LANES = 128
V7X_VMEM_BYTES = 64 * 1024 * 1024
VMEM_LIMIT = V7X_VMEM_BYTES * 7 // 8

GRID_W = 64
HEAD_DIM = 128
ROPE_BASE = 10000.0
EPS = 1e-6
Q_BLOCK = 128
DIFF_HEADS = 8
DIFF_SUB = HEAD_DIM // 2
SWA_HEADS = 8
SWA_KV_HEADS = 2
SWA_GROUP = SWA_HEADS // SWA_KV_HEADS
WINDOW = 128
DIFF_Q = DIFF_HEADS * 2 * DIFF_SUB
DIFF_V = DIFF_HEADS * HEAD_DIM
SWA_Q = SWA_HEADS * HEAD_DIM
SWA_KV = SWA_KV_HEADS * HEAD_DIM
ATTN_IN = 2 * DIFF_Q + DIFF_V + SWA_Q + 2 * SWA_KV
HYENA_ORDER = 2
SHORT_CONV = 3
FILTER_EMB = 33
FILTER_BANDS = (FILTER_EMB - 1) // 2
FILTER_HIDDEN = 64
DECAY_FAST = 0.3
DECAY_SLOW = 1.5
DECAY_TARGET = 1e-2
MOE_GROUPS = 4
MOE_PER_GROUP = 8
MOE_EXPERTS = MOE_GROUPS * MOE_PER_GROUP
MOE_TOPK = 2
MOE_BLOCK = 256
FF_CHUNK = 256
DIFF_KEY_CHUNK = 512
LOG2_E = 1.4426950408889634
NEG_INF = float("-inf")


def _tile(n, pref):
    t = pref
    while n % t:
        t //= 2
    return t


def _params(*sem):
    return pltpu.CompilerParams(dimension_semantics=sem, vmem_limit_bytes=VMEM_LIMIT)


def _dot(a, b):
    return jnp.dot(a, b, preferred_element_type=F32)


def _dot_nt(a, b):
    return lax.dot_general(a, b, (((1,), (1,)), ((), ())), preferred_element_type=F32)


def _mod_kernel(c_ref, w_ref, b_ref, o_ref):
    c = c_ref[...]
    s = c * jax.nn.sigmoid(c)
    o_ref[...] = jnp.dot(s, w_ref[...], precision=HIGHEST, preferred_element_type=F32) + b_ref[...]


def _adaln_mod(cond, ada_w, ada_b, layer):
    rows, d = cond.shape
    n = ada_w.shape[2]
    tn = _tile(n, 768)
    return pl.pallas_call(
        _mod_kernel,
        grid=(n // tn,),
        in_specs=[pl.BlockSpec((rows, d), lambda j: (0, 0)),
                  pl.BlockSpec((None, d, tn), lambda j: (layer, 0, j)),
                  pl.BlockSpec((None, 1, tn), lambda j: (layer, 0, j))],
        out_specs=pl.BlockSpec((rows, tn), lambda j: (0, j)),
        out_shape=jax.ShapeDtypeStruct((rows, n), F32),
        compiler_params=_params("parallel"),
        name="adaln_mod",
    )(cond, ada_w, ada_b)


def _norm_mod(x, g, sc, sh):
    inv = lax.rsqrt(jnp.mean(x * x, axis=-1, keepdims=True) + EPS)
    return ((x * inv) * g) * (1.0 + sc) + sh


def _norm_linear_kernel(x_ref, g_ref, sc_ref, sh_ref, w_ref, b_ref, o_ref, h_ref):
    @pl.when(pl.program_id(2) == 0)
    def _():
        h_ref[...] = _norm_mod(x_ref[...], g_ref[...], sc_ref[...], sh_ref[...]).astype(BF16)

    o_ref[...] = (_dot(h_ref[...], w_ref[...]) + b_ref[...]).astype(o_ref.dtype)


def _norm_linear(x, g, sc, sh, w, b, tm, tn, name):
    bsz, l, d = x.shape
    n = w.shape[1]
    tn = _tile(n, tn)
    return pl.pallas_call(
        _norm_linear_kernel,
        grid=(bsz, l // tm, n // tn),
        in_specs=[pl.BlockSpec((None, tm, d), lambda bi, i, j: (bi, i, 0)),
                  pl.BlockSpec((1, d), lambda bi, i, j: (0, 0)),
                  pl.BlockSpec((None, 1, d), lambda bi, i, j: (bi, 0, 0)),
                  pl.BlockSpec((None, 1, d), lambda bi, i, j: (bi, 0, 0)),
                  pl.BlockSpec((d, tn), lambda bi, i, j: (0, j)),
                  pl.BlockSpec((1, tn), lambda bi, i, j: (0, j))],
        out_specs=pl.BlockSpec((None, tm, tn), lambda bi, i, j: (bi, i, j)),
        out_shape=jax.ShapeDtypeStruct((bsz, l, n), BF16),
        scratch_shapes=[pltpu.VMEM((tm, d), BF16)],
        compiler_params=_params("parallel", "arbitrary", "arbitrary"),
        name=name,
    )(x, g, sc, sh, w, b)


def _mm_res_kernel(*refs, n_pairs):
    a_refs = refs[:n_pairs]
    w_refs = refs[n_pairs:2 * n_pairs]
    b_ref, x_ref, g_ref, o_ref = refs[2 * n_pairs:]
    acc = _dot(a_refs[0][...], w_refs[0][...])
    for a_ref, w_ref in zip(a_refs[1:], w_refs[1:]):
        acc = acc + _dot(a_ref[...], w_ref[...])
    o_ref[...] = x_ref[...] + g_ref[...] * (acc + b_ref[...])


def _mm_residual(a_list, w_list, b, x, g, tm, name):
    bsz, l, n = x.shape
    n_pairs = len(a_list)
    in_specs = [pl.BlockSpec((None, tm, a.shape[2]), lambda bi, i: (bi, i, 0)) for a in a_list]
    in_specs += [pl.BlockSpec(w.shape, lambda bi, i: (0, 0)) for w in w_list]
    in_specs += [pl.BlockSpec((1, n), lambda bi, i: (0, 0)),
                 pl.BlockSpec((None, tm, n), lambda bi, i: (bi, i, 0)),
                 pl.BlockSpec((None, 1, n), lambda bi, i: (bi, 0, 0))]
    return pl.pallas_call(
        functools.partial(_mm_res_kernel, n_pairs=n_pairs),
        grid=(bsz, l // tm),
        in_specs=in_specs,
        out_specs=pl.BlockSpec((None, tm, n), lambda bi, i: (bi, i, 0)),
        out_shape=jax.ShapeDtypeStruct((bsz, l, n), F32),
        compiler_params=_params("parallel", "parallel"),
        name=name,
    )(*a_list, *w_list, b, x, g)


def _rope_tables(n_tokens, dim, width):
    n_rows = n_tokens // GRID_W
    row = jnp.repeat(jnp.arange(n_rows, dtype=F32), GRID_W)
    col = jnp.tile(jnp.arange(GRID_W, dtype=F32), n_rows)
    n_freq = dim // 4
    inv = ROPE_BASE ** (-jnp.arange(n_freq, dtype=F32) / n_freq)
    ang = jnp.concatenate([row[:, None] * inv, col[:, None] * inv], axis=-1)
    ang = jnp.repeat(ang, 2, axis=-1)
    ang = jnp.tile(ang, (1, width // dim))
    sign = jnp.where(jnp.arange(width) % 2 == 0, -1.0, 1.0).astype(F32)
    return jnp.cos(ang), jnp.sin(ang) * sign


def _rope(x, cos_t, sin_t):
    width = x.shape[-1]
    lane = lax.broadcasted_iota(jnp.int32, (1, width), 1)
    partner = jnp.where(lane % 2 == 0, pltpu.roll(x, width - 1, 1), pltpu.roll(x, 1, 1))
    return x * cos_t + partner * sin_t


def _sub_rms(x, g):
    lane = lax.broadcasted_iota(jnp.int32, (1, 2 * DIFF_SUB), 1)
    lo = lane < DIFF_SUB
    sq = x * x
    ss_lo = jnp.sum(jnp.where(lo, sq, 0.0), axis=-1, keepdims=True)
    ss_hi = jnp.sum(jnp.where(lo, 0.0, sq), axis=-1, keepdims=True)
    inv = jnp.where(lo, lax.rsqrt(ss_lo / DIFF_SUB + EPS), lax.rsqrt(ss_hi / DIFF_SUB + EPS))
    return (x * inv) * g


def _rms(x, g):
    return (x * lax.rsqrt(jnp.mean(x * x, axis=-1, keepdims=True) + EPS)) * g


def _diff_attn_kernel(q_ref, k_ref, v_ref, kc_ref, vc_ref, cos_ref, sin_ref,
                      gq_ref, gk_ref, gs_ref, lamp_ref, o_ref, q_scr, k_scr, v_scr, s_scr, lam_scr,
                      *, lam_init, n_ctx, tq):
    n_keys = k_scr.shape[0]

    @pl.when(pl.program_id(2) == 0)
    def _():
        lane = lax.broadcasted_iota(jnp.int32, (1, 2 * DIFF_SUB), 1)
        lo = lane < DIFF_SUB
        gk = gk_ref[...]
        cos_t = cos_ref[...]
        sin_t = sin_ref[...]
        k_scr[0:n_ctx, :] = _sub_rms(kc_ref[...].astype(F32), gk).astype(BF16)
        k_scr[n_ctx:, :] = _rope(_sub_rms(k_ref[...].astype(F32), gk), cos_t, sin_t).astype(BF16)
        v_scr[0:n_ctx, 0:HEAD_DIM] = vc_ref[...].astype(BF16)
        v_scr[n_ctx:, 0:HEAD_DIM] = v_ref[...].astype(BF16)
        v_scr[:, HEAD_DIM:] = jnp.broadcast_to(jnp.where(lane == 0, 1.0, 0.0).astype(BF16), (n_keys, HEAD_DIM))
        q = _rope(_sub_rms(q_ref[...].astype(F32), gq_ref[...]), cos_t, sin_t) * (DIFF_SUB ** -0.5 * LOG2_E)
        q_scr[0] = jnp.where(lo, q, 0.0).astype(BF16)
        q_scr[1] = jnp.where(lo, 0.0, q).astype(BF16)
        lp = lamp_ref[...]
        t1 = jnp.sum(lp[0:1] * lp[1:2], axis=-1, keepdims=True)
        t2 = jnp.sum(lp[2:3] * lp[3:4], axis=-1, keepdims=True)
        lam_scr[...] = jnp.broadcast_to(jnp.exp(t1) - jnp.exp(t2) + lam_init, lam_scr.shape)

    chunks = [(c0, min(c0 + DIFF_KEY_CHUNK, n_keys)) for c0 in range(0, n_keys, DIFF_KEY_CHUNK)]
    lam = lam_scr[:, 0:1]
    for t in range(tq // Q_BLOCK):
        rows = pl.ds(pl.multiple_of(pl.program_id(2) * tq + t * Q_BLOCK, Q_BLOCK), Q_BLOCK)
        qs = [q_scr[0, rows, :], q_scr[1, rows, :]]

        mx = [jnp.full((Q_BLOCK, LANES), NEG_INF, F32) for _ in range(2)]
        for c0, c1 in chunks:
            kc = k_scr[c0:c1, :]
            for h in range(2):
                s = _dot_nt(qs[h], kc)
                s_scr[2 * t + h, :, c0:c1] = s
                for j in range((c1 - c0) // LANES):
                    mx[h] = jnp.maximum(mx[h], s[:, j * LANES:(j + 1) * LANES])
        mrow = [jnp.max(m, axis=-1, keepdims=True) for m in mx]

        out = [jnp.zeros((Q_BLOCK, 2 * HEAD_DIM), F32) for _ in range(2)]
        for c0, c1 in chunks:
            vc = v_scr[c0:c1, :]
            for h in range(2):
                e = jnp.exp2(s_scr[2 * t + h, :, c0:c1] - mrow[h])
                out[h] = out[h] + _dot(e.astype(BF16), vc)
        r1 = 1.0 / out[0][:, HEAD_DIM:HEAD_DIM + 1]
        r2 = lam / out[1][:, HEAD_DIM:HEAD_DIM + 1]
        o = out[0][:, 0:HEAD_DIM] * r1 - out[1][:, 0:HEAD_DIM] * r2
        o_ref[t * Q_BLOCK:(t + 1) * Q_BLOCK, :] = (_rms(o, gs_ref[...]) * (1.0 - lam_init)).astype(o_ref.dtype)


def _diff_attention(qkv, qkv_c, tabs, gq, gk, gs, lamp, lam_init, tq):
    bsz, l, _ = qkv.shape
    n_ctx = qkv_c.shape[1]
    cos_t, sin_t = tabs
    hb = 2 * DIFF_SUB
    kb = DIFF_Q // hb
    vb = 2 * DIFF_Q // hb
    kern = functools.partial(_diff_attn_kernel, lam_init=lam_init, n_ctx=n_ctx, tq=tq)
    vec = pl.BlockSpec((1, hb), lambda bi, h, i: (0, 0))
    return pl.pallas_call(
        kern,
        grid=(bsz, DIFF_HEADS, l // tq),
        in_specs=[pl.BlockSpec((None, l, hb), lambda bi, h, i: (bi, 0, h)),
                  pl.BlockSpec((None, l, hb), lambda bi, h, i: (bi, 0, kb + h)),
                  pl.BlockSpec((None, l, hb), lambda bi, h, i: (bi, 0, vb + h)),
                  pl.BlockSpec((None, n_ctx, hb), lambda bi, h, i: (bi, 0, kb + h)),
                  pl.BlockSpec((None, n_ctx, hb), lambda bi, h, i: (bi, 0, vb + h)),
                  pl.BlockSpec((l, hb), lambda bi, h, i: (0, 0)),
                  pl.BlockSpec((l, hb), lambda bi, h, i: (0, 0)),
                  vec, vec, vec,
                  pl.BlockSpec((8, hb), lambda bi, h, i: (0, 0))],
        out_specs=pl.BlockSpec((None, tq, hb), lambda bi, h, i: (bi, i, h)),
        out_shape=jax.ShapeDtypeStruct((bsz, l, DIFF_V), BF16),
        scratch_shapes=[pltpu.VMEM((2, l, hb), BF16), pltpu.VMEM((n_ctx + l, hb), BF16),
                        pltpu.VMEM((n_ctx + l, 2 * HEAD_DIM), BF16),
                        pltpu.VMEM((2 * (tq // Q_BLOCK), Q_BLOCK, n_ctx + l), F32),
                        pltpu.VMEM((1, hb), F32)],
        compiler_params=_params("parallel", "parallel", "arbitrary"),
        name="diff_attention",
    )(qkv, qkv, qkv, qkv_c, qkv_c, cos_t, sin_t, gq, gk, gs, lamp)


def _swa_kernel(q_ref, k_ref, v_ref, kc_ref, vc_ref, cos_ref, sin_ref, gq_ref, gk_ref, sink_ref,
                o_ref, q_scr, k_scr, v_scr, kc_scr, vc_scr, *, seq):
    kvh = pl.program_id(1)
    n = pl.program_id(2)
    span = Q_BLOCK + 2 * WINDOW

    @pl.when(n == 0)
    def _():
        gk = gk_ref[...]
        gq = gq_ref[...]
        cos_t = cos_ref[...]
        sin_t = sin_ref[...]
        zeros = jnp.zeros((WINDOW, HEAD_DIM), BF16)
        k_scr[0:WINDOW, :] = zeros
        v_scr[0:WINDOW, :] = zeros
        k_scr[WINDOW + seq:, :] = zeros
        v_scr[WINDOW + seq:, :] = zeros
        k_scr[WINDOW:WINDOW + seq, :] = _rope(_rms(k_ref[...].astype(F32), gk), cos_t, sin_t).astype(BF16)
        v_scr[WINDOW:WINDOW + seq, :] = v_ref[...].astype(BF16)
        kc_scr[...] = _rms(kc_ref[...].astype(F32), gk).astype(BF16)
        vc_scr[...] = vc_ref[...].astype(BF16)
        for j in range(SWA_GROUP):
            qj = q_ref[:, j * HEAD_DIM:(j + 1) * HEAD_DIM].astype(F32)
            qj = _rope(_rms(qj, gq), cos_t, sin_t) * (HEAD_DIM ** -0.5)
            q_scr[j] = qj.astype(BF16)

    start = pl.multiple_of(n * Q_BLOCK, Q_BLOCK)
    kw = k_scr[pl.ds(start, span), :]
    vw = v_scr[pl.ds(start, span), :]
    kc = kc_scr[...]
    vc = vc_scr[...]
    rows = SWA_GROUP * Q_BLOCK
    ridx = lax.broadcasted_iota(jnp.int32, (rows, span), 0)
    kidx = lax.broadcasted_iota(jnp.int32, (rows, span), 1)
    rel = kidx - (ridx & (Q_BLOCK - 1))
    key_pos = n * Q_BLOCK + kidx - WINDOW
    mask = (rel >= 0) & (rel <= 2 * WINDOW) & (key_pos >= 0) & (key_pos < seq)
    head = lax.broadcasted_iota(jnp.int32, (rows, 1), 0) // Q_BLOCK
    sink = jnp.zeros((rows, 1), F32)
    for j in range(SWA_GROUP):
        sink = jnp.where(head == j, sink_ref[pl.ds(kvh * SWA_GROUP + j, 1), 0:1], sink)
    q = jnp.concatenate([q_scr[j, pl.ds(start, Q_BLOCK), :] for j in range(SWA_GROUP)], axis=0)
    s_loc = jnp.where(mask, _dot_nt(q, kw), NEG_INF)
    s_ctx = _dot_nt(q, kc)
    m = jnp.maximum(jnp.maximum(jnp.max(s_loc, axis=-1, keepdims=True),
                                jnp.max(s_ctx, axis=-1, keepdims=True)), sink)
    e_loc = jnp.exp(s_loc - m)
    e_ctx = jnp.exp(s_ctx - m)
    denom = (jnp.sum(e_loc, axis=-1, keepdims=True) + jnp.sum(e_ctx, axis=-1, keepdims=True)
             + jnp.exp(sink - m))
    r = 1.0 / denom
    o = _dot((e_ctx * r).astype(BF16), vc) + _dot((e_loc * r).astype(BF16), vw)
    o_ref[...] = jnp.concatenate([o[j * Q_BLOCK:(j + 1) * Q_BLOCK] for j in range(SWA_GROUP)],
                                 axis=-1).astype(o_ref.dtype)


def _swa_attention(qkv, qkv_c, tabs, gq, gk, sink_rows):
    bsz, l, _ = qkv.shape
    n_ctx = qkv_c.shape[1]
    cos_t, sin_t = tabs
    gw = SWA_GROUP * HEAD_DIM
    q0 = (2 * DIFF_Q + DIFF_V) // gw
    k0 = (2 * DIFF_Q + DIFF_V + SWA_Q) // HEAD_DIM
    v0 = k0 + SWA_KV_HEADS
    vec = pl.BlockSpec((1, HEAD_DIM), lambda bi, h, i: (0, 0))
    return pl.pallas_call(
        functools.partial(_swa_kernel, seq=l),
        grid=(bsz, SWA_KV_HEADS, l // Q_BLOCK),
        in_specs=[pl.BlockSpec((None, l, gw), lambda bi, h, i: (bi, 0, q0 + h)),
                  pl.BlockSpec((None, l, HEAD_DIM), lambda bi, h, i: (bi, 0, k0 + h)),
                  pl.BlockSpec((None, l, HEAD_DIM), lambda bi, h, i: (bi, 0, v0 + h)),
                  pl.BlockSpec((None, n_ctx, HEAD_DIM), lambda bi, h, i: (bi, 0, k0 + h)),
                  pl.BlockSpec((None, n_ctx, HEAD_DIM), lambda bi, h, i: (bi, 0, v0 + h)),
                  pl.BlockSpec((l, HEAD_DIM), lambda bi, h, i: (0, 0)),
                  pl.BlockSpec((l, HEAD_DIM), lambda bi, h, i: (0, 0)),
                  vec, vec,
                  pl.BlockSpec((SWA_HEADS, HEAD_DIM), lambda bi, h, i: (0, 0))],
        out_specs=pl.BlockSpec((None, Q_BLOCK, gw), lambda bi, h, i: (bi, i, h)),
        out_shape=jax.ShapeDtypeStruct((bsz, l, SWA_Q), BF16),
        scratch_shapes=[pltpu.VMEM((SWA_GROUP, l, HEAD_DIM), BF16),
                        pltpu.VMEM((l + 2 * WINDOW, HEAD_DIM), BF16), pltpu.VMEM((l + 2 * WINDOW, HEAD_DIM), BF16),
                        pltpu.VMEM((n_ctx, HEAD_DIM), BF16), pltpu.VMEM((n_ctx, HEAD_DIM), BF16)],
        compiler_params=_params("parallel", "parallel", "arbitrary"),
        name="swa_attention",
    )(qkv, qkv, qkv, qkv_c, qkv_c, cos_t, sin_t, gq, gk, sink_rows)


def _router_kernel(x_ref, g_ref, sc_ref, sh_ref, w_ref, b_ref, tri_ref, h_ref, e_ref, p_ref, cnt_ref, run_scr):
    @pl.when((pl.program_id(0) == 0) & (pl.program_id(1) == 0))
    def _():
        run_scr[...] = jnp.zeros(run_scr.shape, F32)

    h = _norm_mod(x_ref[...], g_ref[...], sc_ref[...], sh_ref[...])
    h_ref[...] = h
    logits = jnp.dot(h, w_ref[...], precision=HIGHEST, preferred_element_type=F32) + b_ref[...]
    lane_i = lax.broadcasted_iota(jnp.int32, (1, LANES), 1)
    lane = lane_i.astype(F32)
    big = float(LANES)
    lg = jnp.where(lane_i < MOE_GROUPS, logits, NEG_INF)
    mg = jnp.max(lg, axis=-1, keepdims=True)
    p_top = 1.0 / jnp.sum(jnp.exp(lg - mg), axis=-1, keepdims=True)
    grp = jnp.min(jnp.where(lg == mg, lane, big), axis=-1, keepdims=True)
    e_lane = lane - MOE_GROUPS
    in_grp = (e_lane >= grp * MOE_PER_GROUP) & (e_lane < (grp + 1.0) * MOE_PER_GROUP)
    le = jnp.where(in_grp, logits, NEG_INF)
    t1 = jnp.max(le, axis=-1, keepdims=True)
    i1 = jnp.min(jnp.where(le == t1, lane, big), axis=-1, keepdims=True)
    le2 = jnp.where(lane == i1, NEG_INF, le)
    t2 = jnp.max(le2, axis=-1, keepdims=True)
    i2 = jnp.min(jnp.where(le2 == t2, lane, big), axis=-1, keepdims=True)
    w2 = jnp.exp(t2 - t1)
    inv = p_top / (1.0 + w2)
    hot1 = lane == i1
    hot2 = lane == i2
    onehot = jnp.where(hot1 | hot2, 1.0, 0.0)
    before = run_scr[...] + _dot(tri_ref[...], onehot.astype(BF16))
    r1 = jnp.sum(jnp.where(hot1, before, 0.0), axis=-1, keepdims=True)
    r2 = jnp.sum(jnp.where(hot2, before, 0.0), axis=-1, keepdims=True)
    run_scr[...] = run_scr[...] + jnp.sum(onehot, axis=0, keepdims=True)
    cnt_ref[...] = run_scr[...]
    e_ref[...] = jnp.where(lane_i == 0, i1 - MOE_GROUPS,
                           jnp.where(lane_i == 1, i2 - MOE_GROUPS,
                                     jnp.where(lane_i == 2, r1, jnp.where(lane_i == 3, r2, 0.0)))).astype(jnp.int32)
    p_ref[...] = jnp.where(lane_i == 0, inv, jnp.where(lane_i == 1, inv * w2, 0.0))


def _router(x, g, sc, sh, w, b, tm):
    bsz, l, d = x.shape
    row = lambda bi, i: (bi, i, 0)
    tri = (jnp.arange(tm)[:, None] > jnp.arange(tm)[None, :]).astype(BF16)
    return pl.pallas_call(
        _router_kernel,
        grid=(bsz, l // tm),
        in_specs=[pl.BlockSpec((None, tm, d), row),
                  pl.BlockSpec((1, d), lambda bi, i: (0, 0)),
                  pl.BlockSpec((None, 1, d), lambda bi, i: (bi, 0, 0)),
                  pl.BlockSpec((None, 1, d), lambda bi, i: (bi, 0, 0)),
                  pl.BlockSpec((d, LANES), lambda bi, i: (0, 0)),
                  pl.BlockSpec((1, LANES), lambda bi, i: (0, 0)),
                  pl.BlockSpec((tm, tm), lambda bi, i: (0, 0))],
        out_specs=[pl.BlockSpec((None, tm, d), row),
                   pl.BlockSpec((None, tm, LANES), row),
                   pl.BlockSpec((None, tm, LANES), row),
                   pl.BlockSpec((1, LANES), lambda bi, i: (0, 0))],
        out_shape=[jax.ShapeDtypeStruct((bsz, l, d), F32),
                   jax.ShapeDtypeStruct((bsz, l, LANES), jnp.int32),
                   jax.ShapeDtypeStruct((bsz, l, LANES), F32),
                   jax.ShapeDtypeStruct((1, LANES), F32)],
        scratch_shapes=[pltpu.VMEM((1, LANES), F32)],
        compiler_params=_params("arbitrary", "arbitrary"),
        name="moe_router",
    )(x, g, sc, sh, w, b, tri)


def _dispatch_kernel(slot_ref, h_ref, init_ref, xb_ref, sem):
    del init_ref
    tm = h_ref.shape[0]
    base = pl.program_id(0) * tm

    def copy(r, k):
        dst = slot_ref[(base + r) * MOE_TOPK + k]
        return pltpu.make_async_copy(h_ref.at[pl.ds(r, 1)], xb_ref.at[pl.ds(dst, 1)], sem)

    def start(r, carry):
        for k in range(MOE_TOPK):
            copy(r, k).start(priority=k)
        return carry

    def wait(r, carry):
        for k in range(MOE_TOPK):
            copy(r, k).wait()
        return carry

    lax.fori_loop(0, tm, start, 0, unroll=8)
    lax.fori_loop(0, tm, wait, 0, unroll=8)


def _dispatch(slot, h, n_rows, tm):
    n, d = h.shape
    grid_spec = pltpu.PrefetchScalarGridSpec(
        num_scalar_prefetch=1,
        grid=(n // tm,),
        in_specs=[pl.BlockSpec((tm, d), lambda i, s: (i, 0)),
                  pl.BlockSpec(memory_space=pl.ANY)],
        out_specs=pl.BlockSpec(memory_space=pl.ANY),
        scratch_shapes=[pltpu.SemaphoreType.DMA(())],
    )
    return pl.pallas_call(
        _dispatch_kernel,
        grid_spec=grid_spec,
        out_shape=jax.ShapeDtypeStruct((n_rows, d), F32),
        input_output_aliases={2: 0},
        compiler_params=_params("arbitrary"),
        name="moe_dispatch",
    )(slot, h, jnp.zeros((n_rows, d), F32))


def _combine_kernel(slot_ref, x_ref, g_ref, p_ref, yb_ref, o_ref, y_scr, sem):
    tm = x_ref.shape[0]
    base = pl.program_id(0) * tm

    def copy(r, k):
        src = slot_ref[(base + r) * MOE_TOPK + k]
        return pltpu.make_async_copy(yb_ref.at[pl.ds(src, 1)], y_scr.at[k, pl.ds(r, 1)], sem)

    def start(r, carry):
        for k in range(MOE_TOPK):
            copy(r, k).start(priority=k)
        return carry

    def wait(r, carry):
        for k in range(MOE_TOPK):
            copy(r, k).wait()
        return carry

    lax.fori_loop(0, tm, start, 0, unroll=8)
    lax.fori_loop(0, tm, wait, 0, unroll=8)
    p = p_ref[...]
    moe = y_scr[0] * p[:, 0:1]
    for k in range(1, MOE_TOPK):
        moe = moe + y_scr[k] * p[:, k:k + 1]
    o_ref[...] = x_ref[...] + g_ref[...] * moe


def _combine(slot, x, g, gates, yb, tm):
    bsz, l, d = x.shape
    nt = l // tm
    grid_spec = pltpu.PrefetchScalarGridSpec(
        num_scalar_prefetch=1,
        grid=(bsz * nt,),
        in_specs=[pl.BlockSpec((None, tm, d), lambda i, s: (i // nt, i % nt, 0)),
                  pl.BlockSpec((None, 1, d), lambda i, s: (i // nt, 0, 0)),
                  pl.BlockSpec((None, tm, LANES), lambda i, s: (i // nt, i % nt, 0)),
                  pl.BlockSpec(memory_space=pl.ANY)],
        out_specs=pl.BlockSpec((None, tm, d), lambda i, s: (i // nt, i % nt, 0)),
        scratch_shapes=[pltpu.VMEM((MOE_TOPK, tm, d), F32), pltpu.SemaphoreType.DMA(())],
    )
    return pl.pallas_call(
        _combine_kernel,
        grid_spec=grid_spec,
        out_shape=jax.ShapeDtypeStruct((bsz, l, d), F32),
        compiler_params=_params("arbitrary"),
        name="moe_combine",
    )(slot, x, g, gates, yb)


def _expert_kernel(be_ref, nu_ref, x_ref, wg_ref, wu_ref, wd_ref, o_ref):
    i = pl.program_id(0)

    @pl.when(i < nu_ref[0])
    def _():
        x = x_ref[...].astype(BF16)
        ff = wg_ref.shape[1]
        acc = jnp.zeros(o_ref.shape, F32)
        for c in range(ff // FF_CHUNK):
            cols = slice(c * FF_CHUNK, (c + 1) * FF_CHUNK)
            gate = _dot(x, wg_ref[:, cols])
            up = _dot(x, wu_ref[:, cols])
            hidden = (gate * jax.nn.sigmoid(gate) * up).astype(BF16)
            acc = acc + _dot(hidden, wd_ref[cols, :])
        o_ref[...] = acc

    @pl.when(i >= nu_ref[0])
    def _():
        o_ref[...] = jnp.zeros(o_ref.shape, F32)


def _expert_ffn(xb, block_e, n_used, w_gate, w_up, w_down, layer):
    n_blocks, _, d = xb.shape
    ff = w_gate.shape[2]
    base = layer * MOE_EXPERTS
    grid_spec = pltpu.PrefetchScalarGridSpec(
        num_scalar_prefetch=2,
        grid=(n_blocks,),
        in_specs=[pl.BlockSpec((None, MOE_BLOCK, d), lambda i, be, nu: (i, 0, 0)),
                  pl.BlockSpec((None, d, ff), lambda i, be, nu: (base + be[i], 0, 0)),
                  pl.BlockSpec((None, d, ff), lambda i, be, nu: (base + be[i], 0, 0)),
                  pl.BlockSpec((None, ff, d), lambda i, be, nu: (base + be[i], 0, 0))],
        out_specs=pl.BlockSpec((None, MOE_BLOCK, d), lambda i, be, nu: (i, 0, 0)),
    )
    return pl.pallas_call(
        _expert_kernel,
        grid_spec=grid_spec,
        out_shape=jax.ShapeDtypeStruct((n_blocks, MOE_BLOCK, d), F32),
        compiler_params=_params("arbitrary"),
        name="expert_ffn",
    )(block_e, n_used, xb, w_gate, w_up, w_down)


def _hier_moe(x, g, sc, sh, gate2, w_route, b_route, w_gate, w_up, w_down, layer):
    bsz, l, d = x.shape
    n = bsz * l
    tm = min(512, l)
    h, route, gates, cnt = _router(x, g, sc, sh, w_route, b_route, tm)
    route = route.reshape(n, LANES)
    expert = route[:, :MOE_TOPK]
    rank = route[:, MOE_TOPK:2 * MOE_TOPK]
    counts = cnt[0, MOE_GROUPS:MOE_GROUPS + MOE_EXPERTS].astype(jnp.int32)
    padded = (counts + MOE_BLOCK - 1) // MOE_BLOCK * MOE_BLOCK
    pad_end = jnp.cumsum(padded)
    pad_start = pad_end - padded
    hit = expert[:, :, None] == jnp.arange(MOE_EXPERTS, dtype=jnp.int32)
    slot = (jnp.sum(jnp.where(hit, pad_start, 0), axis=-1) + rank).reshape(-1)
    n_blocks = -(-(n * MOE_TOPK) // MOE_BLOCK) + MOE_EXPERTS
    block_row = jnp.arange(n_blocks, dtype=jnp.int32)[:, None] * MOE_BLOCK
    block_e = jnp.minimum(jnp.sum((pad_end[None, :] <= block_row).astype(jnp.int32), axis=-1), MOE_EXPERTS - 1)
    n_used = (pad_end[-1:] // MOE_BLOCK).astype(jnp.int32)
    xb = _dispatch(slot, h.reshape(n, d), n_blocks * MOE_BLOCK, tm).reshape(n_blocks, MOE_BLOCK, d)
    yb = _expert_ffn(xb, block_e, n_used, w_gate, w_up, w_down, layer).reshape(-1, d)
    return _combine(slot, x, gate2, gates, yb, tm)


def _short_conv_kernel(zv_ref, z1_ref, z2_ref, wv_ref, w1_ref, w2_ref, bv_ref, b1_ref, b2_ref,
                       v_ref, x1_ref, x2_ref):
    half = zv_ref.shape[0] // 2
    row = lax.broadcasted_iota(jnp.int32, (half, 1), 0)

    def conv(z_ref, w_ref, b_ref):
        ze = z_ref[0:half, :].astype(F32)
        zo = z_ref[half:, :].astype(F32)
        w = w_ref[...]
        b = b_ref[...]
        prev_e = jnp.where(row == 0, 0.0, pltpu.roll(zo, 1, 0))
        next_o = jnp.where(row == half - 1, 0.0, pltpu.roll(ze, half - 1, 0))
        even = b + prev_e * w[0:1] + ze * w[1:2] + zo * w[2:3]
        odd = b + ze * w[0:1] + zo * w[1:2] + next_o * w[2:3]
        return jnp.concatenate([even, odd], axis=0)

    v_ref[...] = conv(zv_ref, wv_ref, bv_ref).astype(v_ref.dtype)
    x1_ref[...] = conv(z1_ref, w1_ref, b1_ref).astype(x1_ref.dtype)
    x2_ref[...] = conv(z2_ref, w2_ref, b2_ref).astype(x2_ref.dtype)


def _short_conv(z, conv_w, conv_b, tc):
    bsz, l, w3 = z.shape
    w = w3 // 3
    nc = w // tc
    zs = lambda k: pl.BlockSpec((None, l, tc), lambda bi, j: (bi, 0, k * nc + j))
    ws = lambda k: pl.BlockSpec((8, tc), lambda bi, j: (0, k * nc + j))
    bs = lambda k: pl.BlockSpec((1, tc), lambda bi, j: (0, k * nc + j))
    out = pl.BlockSpec((None, l, tc), lambda bi, j: (bi, 0, j))
    return pl.pallas_call(
        _short_conv_kernel,
        grid=(bsz, nc),
        in_specs=[zs(0), zs(1), zs(2), ws(0), ws(1), ws(2), bs(0), bs(1), bs(2)],
        out_specs=[out, out, out],
        out_shape=[jax.ShapeDtypeStruct((bsz, l, w), BF16)] * 3,
        compiler_params=_params("parallel", "parallel"),
        name="short_conv",
    )(z, z, z, conv_w, conv_w, conv_w, conv_b, conv_b, conv_b)


def _filter_kernel(z_ref, w1_ref, b1_ref, f1_ref, w2_ref, b2_ref, f2_ref, w3_ref, dl_ref, bw_ref, o_ref, a_scr):
    n = z_ref.shape[0]

    @pl.when(pl.program_id(0) == 0)
    def _():
        a = jnp.sin(f1_ref[...] * (jnp.dot(z_ref[...], w1_ref[...], precision=HIGHEST,
                                           preferred_element_type=F32) + b1_ref[...]))
        a_scr[...] = jnp.sin(f2_ref[...] * (jnp.dot(a, w2_ref[...], precision=HIGHEST,
                                                    preferred_element_type=F32) + b2_ref[...]))

    hf = jnp.dot(a_scr[...], w3_ref[...], precision=HIGHEST, preferred_element_type=F32)
    row = lax.broadcasted_iota(jnp.int32, (n, 1), 0)
    tap = jnp.where(row < n // 2, 2 * row, 2 * (row - n // 2) + 1)
    t = tap.astype(F32) * (1.0 / (n - 1))
    hf = hf * jnp.exp(-t * dl_ref[...])
    o_ref[...] = jnp.where((tap == 0) & (bw_ref[...] > 0.5), 0.0, hf)


def _hyena_filters(z, w1, b1, f1, w2, b2, f2, w3, deltas, is_bwd, tn):
    n = z.shape[0]
    cols = w3.shape[1]
    full = lambda a: pl.BlockSpec(a.shape, lambda j: (0, 0))
    tile = lambda r: pl.BlockSpec((r, tn), lambda j: (0, j))
    return pl.pallas_call(
        _filter_kernel,
        grid=(cols // tn,),
        in_specs=[full(z), full(w1), full(b1), full(f1), full(w2), full(b2), full(f2),
                  tile(w3.shape[0]), tile(1), tile(1)],
        out_specs=tile(n),
        out_shape=jax.ShapeDtypeStruct((n, cols), F32),
        scratch_shapes=[pltpu.VMEM((n, LANES), F32)],
        compiler_params=_params("arbitrary"),
        name="hyena_filters",
    )(z, w1, b1, f1, w2, b2, f2, w3, deltas, is_bwd)


def _dft_tables(n):
    h = n // 2
    r = 1 << (h.bit_length() // 2)
    f = jnp.arange(h, dtype=jnp.int32)[:, None]
    k = jnp.arange(r, dtype=jnp.int32)[None, :]
    ang = lambda s: (((2 * f + 1) * s) % (4 * n)).astype(F32) * (math.pi / (2 * n))
    hi = ang(2 * r * k)[:, :h // r]
    ch, sh = jnp.cos(hi), jnp.sin(hi)
    fwd, inv = [], []
    for trig in ("cos", "sin"):
        for parity in range(2):
            lo = ang(2 * k + parity)
            cl, sl = jnp.cos(lo), jnp.sin(lo)
            a, b, sign = (ch, sh, -1.0) if trig == "cos" else (sh, ch, 1.0)
            fwd.append((a[:, :, None] * cl[:, None, :] + sign * b[:, :, None] * sl[:, None, :]).reshape(h, h))
            inv.append((a.T[:, None, :] * cl.T[None, :, :] + sign * b.T[:, None, :] * sl.T[None, :, :]).reshape(h, h))
    return jnp.stack(fwd).astype(BF16), jnp.stack(inv).astype(BF16)


def _dft_filter_kernel(m_ref, hf_ref, hb_ref, kr_ref, ki_ref, sum_scr, dif_scr, *, scale):
    @pl.when(pl.program_id(2) == 0)
    def _():
        hf = hf_ref[...]
        hb = hb_ref[...]
        sum_scr[...] = (hf + hb).astype(BF16)
        dif_scr[...] = (hb - hf).astype(BF16)

    ce = _dot(m_ref[0], sum_scr[0])
    co = _dot(m_ref[1], sum_scr[1])
    se = _dot(m_ref[2], dif_scr[0])
    so = _dot(m_ref[3], dif_scr[1])
    kr_ref[0] = (ce + co) * scale
    kr_ref[1] = (ce - co) * scale
    ki_ref[0] = (se + so) * scale
    ki_ref[1] = (so - se) * scale


def _dft_filter(fwd, filt, width, tf, tn):
    h = fwd.shape[1]
    nc = width // tn
    out = pl.BlockSpec((None, 2, tf, tn), lambda o, j, i: (o, 0, i, j))
    return pl.pallas_call(
        functools.partial(_dft_filter_kernel, scale=1.0 / (2 * h)),
        grid=(HYENA_ORDER, nc, h // tf),
        in_specs=[pl.BlockSpec((4, tf, h), lambda o, j, i: (0, i, 0)),
                  pl.BlockSpec((2, h, tn), lambda o, j, i: (0, 0, (2 * o) * nc + j)),
                  pl.BlockSpec((2, h, tn), lambda o, j, i: (0, 0, (2 * o + 1) * nc + j))],
        out_specs=[out, out],
        out_shape=[jax.ShapeDtypeStruct((HYENA_ORDER, 2, h, width), F32)] * 2,
        scratch_shapes=[pltpu.VMEM((2, h, tn), BF16), pltpu.VMEM((2, h, tn), BF16)],
        compiler_params=_params("parallel", "parallel", "arbitrary"),
        name="dft_filter",
    )(fwd, filt, filt)


def _dft_fwd_kernel(m_ref, u_ref, kr_ref, ki_ref, pr_ref, pi_ref):
    ue = u_ref[0]
    uo = u_ref[1]
    ce = _dot(m_ref[0], ue)
    co = _dot(m_ref[1], uo)
    se = _dot(m_ref[2], ue)
    so = _dot(m_ref[3], uo)
    for half, (ur, us) in enumerate(((ce + co, se + so), (ce - co, so - se))):
        kr = kr_ref[half]
        ki = ki_ref[half]
        pr_ref[half] = (ur * kr + us * ki).astype(pr_ref.dtype)
        pi_ref[half] = (ur * ki - us * kr).astype(pi_ref.dtype)


def _dft_fwd(fwd, u, kr, ki, order, tf, tn):
    bsz, _, h, width = u.shape
    out = pl.BlockSpec((None, 2, tf, tn), lambda bi, j, i: (bi, 0, i, j))
    ks = pl.BlockSpec((None, 2, tf, tn), lambda bi, j, i: (order, 0, i, j))
    return pl.pallas_call(
        _dft_fwd_kernel,
        grid=(bsz, width // tn, h // tf),
        in_specs=[pl.BlockSpec((4, tf, h), lambda bi, j, i: (0, i, 0)),
                  pl.BlockSpec((None, 2, h, tn), lambda bi, j, i: (bi, 0, 0, j)),
                  ks, ks],
        out_specs=[out, out],
        out_shape=[jax.ShapeDtypeStruct((bsz, 2, h, width), BF16)] * 2,
        compiler_params=_params("parallel", "parallel", "parallel"),
        name="dft_fwd",
    )(fwd, u, kr, ki)


def _dft_inv_kernel(m_ref, pr_ref, pi_ref, u_ref, b_ref, x_ref, o_ref, ra_scr, rb_scr):
    @pl.when(pl.program_id(3) == 0)
    def _():
        sign = jnp.where(pl.program_id(2) == 0, 1.0, -1.0)
        ra_scr[...] = (pr_ref[0].astype(F32) + sign * pr_ref[1].astype(F32)).astype(BF16)
        rb_scr[...] = (pi_ref[0].astype(F32) - sign * pi_ref[1].astype(F32)).astype(BF16)

    y = _dot(m_ref[0], ra_scr[...]) - _dot(m_ref[1], rb_scr[...])
    u = u_ref[...].astype(F32)
    o_ref[...] = (x_ref[...] * (y + u * b_ref[...])).astype(o_ref.dtype)


def _dft_inv(inv, pr, pi, u, bias, x, order, tt, tn):
    bsz, _, h, width = u.shape
    tile = pl.BlockSpec((None, None, tt, tn), lambda bi, j, p, i: (bi, p, i, j))
    spec = pl.BlockSpec((None, 2, h, tn), lambda bi, j, p, i: (bi, 0, 0, j))
    return pl.pallas_call(
        _dft_inv_kernel,
        grid=(bsz, width // tn, 2, h // tt),
        in_specs=[pl.BlockSpec((2, None, tt, h), lambda bi, j, p, i: (0, p, i, 0)),
                  spec, spec, tile,
                  pl.BlockSpec((None, 1, tn), lambda bi, j, p, i: (order, 0, j)),
                  tile],
        out_specs=tile,
        out_shape=jax.ShapeDtypeStruct((bsz, 2, h, width), BF16),
        scratch_shapes=[pltpu.VMEM((h, tn), BF16), pltpu.VMEM((h, tn), BF16)],
        compiler_params=_params("parallel", "parallel", "arbitrary", "arbitrary"),
        name="dft_inv",
    )(inv.reshape(2, 2, h, h), pr, pi, u, bias, x)


def _filter_features(n):
    t = jnp.linspace(0.0, 1.0, n, dtype=F32)[:, None]
    w = 2.0 * math.pi * jnp.arange(n, dtype=F32) / n
    f = jnp.linspace(1e-4, FILTER_BANDS - 1, FILTER_BANDS, dtype=F32)
    z = jnp.concatenate([t, jnp.cos(w[:, None] * f), -jnp.sin(w[:, None] * f)], axis=-1)
    return jnp.pad(z, ((0, 0), (0, LANES - FILTER_EMB)))


def _pad_to(a, shape):
    return jnp.pad(a, [(0, t - s) for s, t in zip(a.shape, shape)])


def kernel(x, c, ctx, c_ctx, ada_w, ada_b, norm1_g, norm2_g, attn_w_in, attn_w_out, diff_q_g, diff_k_g, diff_lq1, diff_lk1, diff_lq2, diff_lk2, diff_sub_g, swa_q_g, swa_k_g, swa_sink, hy_w_in, hy_b_in, hy_conv_w, hy_conv_b, flt_w1, flt_b1, flt_f1, flt_w2, flt_b2, flt_f2, flt_w3, hy_bias, hy_w_out, hy_b_out, moe_wg1, moe_bg1, moe_wg2, moe_bg2, moe_w_gate, moe_w_up, moe_w_down):
    bsz, seq, d = x.shape
    depth = ada_w.shape[0]
    n_ctx = ctx.shape[1]
    cond = jnp.concatenate([c, c_ctx[None, :], jnp.zeros((8 - bsz - 1, d), F32)], axis=0)
    ada_b3 = ada_b[:, None, :]
    zero_bias = lambda n: jnp.zeros((1, n), F32)
    wg_all = moe_w_gate.astype(BF16).reshape(depth * MOE_EXPERTS, d, -1)
    wu_all = moe_w_up.astype(BF16).reshape(depth * MOE_EXPERTS, d, -1)
    wd_all = moe_w_down.astype(BF16).reshape(depth * MOE_EXPERTS, -1, d)
    tm = min(512, seq)

    for layer in range(depth):
        even = layer % 2 == 0
        i = layer // 2
        ctx_live = any(j % 2 == 0 for j in range(layer + 1, depth))
        assert not ctx_live, "context-stream update is only needed for deeper stacks"
        mod = _adaln_mod(cond, ada_w, ada_b3, layer)
        sh1, sc1, g1, sh2, sc2, g2 = [mod[:, k * d:(k + 1) * d] for k in range(6)]
        lat = lambda a: a[:bsz, None, :]
        cmod = lambda a: jnp.broadcast_to(a[bsz:bsz + 1, None, :], (bsz, 1, d))
        n1 = norm1_g[layer][None, :]
        if even:
            w_in = attn_w_in[i].astype(BF16)
            qkv = _norm_linear(x, n1, lat(sc1), lat(sh1), w_in, zero_bias(ATTN_IN), tm, 1536, "attn_in")
            qkv_c = _norm_linear(ctx, n1, cmod(sc1), cmod(sh1), w_in, zero_bias(ATTN_IN), min(tm, n_ctx), 1536,
                                 "attn_in_ctx")
            lam_init = 0.8 - 0.6 * math.exp(-0.3 * layer)
            lamp = _pad_to(jnp.stack([diff_lq1[i], diff_lk1[i], diff_lq2[i], diff_lk2[i]]), (8, 2 * DIFF_SUB))
            two = lambda a: jnp.tile(a, 2)[None, :]
            o_diff = _diff_attention(qkv, qkv_c, _rope_tables(seq, DIFF_SUB, 2 * DIFF_SUB), two(diff_q_g[i]),
                                     two(diff_k_g[i]), diff_sub_g[i][None, :], lamp, lam_init, min(2 * Q_BLOCK, seq))
            sink_rows = jnp.broadcast_to(swa_sink[i][:, None], (SWA_HEADS, HEAD_DIM))
            o_swa = _swa_attention(qkv, qkv_c, _rope_tables(seq, HEAD_DIM, HEAD_DIM), swa_q_g[i][None, :],
                                   swa_k_g[i][None, :], sink_rows)
            w_out = attn_w_out[i].astype(BF16)
            x = _mm_residual([o_diff, o_swa], [w_out[:DIFF_V], w_out[DIFF_V:]], zero_bias(d), x, lat(g1), tm,
                             "attn_out")
        else:
            width = hy_w_out.shape[1]
            half = seq // 2
            parity_order = lambda a: a.reshape(a.shape[0], half, 2, -1).swapaxes(1, 2).reshape(a.shape)
            token_order = lambda a: a.reshape(a.shape[0], 2, half, -1).swapaxes(1, 2).reshape(a.shape)
            split = lambda a: a.reshape(a.shape[0], 2, half, a.shape[2])
            x = parity_order(x)
            z = _norm_linear(x, n1, lat(sc1), lat(sh1), hy_w_in[i].astype(BF16), hy_b_in[i][None, :], tm, 1536,
                             "hyena_in")
            v, x1, x2 = _short_conv(z, _pad_to(hy_conv_w[i], (8, 3 * width)), hy_conv_b[i][None, :], 128)
            max_decay = math.log(DECAY_TARGET) / DECAY_FAST
            min_decay = math.log(DECAY_TARGET) / DECAY_SLOW
            deltas = jnp.abs(jnp.tile(jnp.linspace(min_decay, max_decay, width, dtype=F32), 2 * HYENA_ORDER))[None, :]
            is_bwd = jnp.tile(jnp.concatenate([jnp.zeros((width,), F32), jnp.ones((width,), F32)]),
                              HYENA_ORDER)[None, :]
            sq = (LANES, LANES)
            feats = _filter_features(seq)
            feats = jnp.concatenate([feats[0::2], feats[1::2]], axis=0)
            filt = _hyena_filters(
                feats, _pad_to(flt_w1[i], sq), _pad_to(flt_b1[i][None, :], (1, LANES)),
                _pad_to(flt_f1[i][None, :], (1, LANES)), _pad_to(flt_w2[i], sq),
                _pad_to(flt_b2[i][None, :], (1, LANES)), _pad_to(flt_f2[i][None, :], (1, LANES)),
                _pad_to(flt_w3[i], (LANES, 2 * HYENA_ORDER * width)), deltas, is_bwd, 256)
            fwd, inv = _dft_tables(seq)
            tf = min(512, half)
            tn = min(512, width)
            kr, ki = _dft_filter(fwd, filt.reshape(2, half, -1), width, tf, min(256, width))
            bias = hy_bias[i][:, None, :]
            v, x1, x2 = split(v), split(x1), split(x2)
            pr, pi = _dft_fwd(fwd, v, kr, ki, 0, tf, tn)
            y1 = _dft_inv(inv, pr, pi, v, bias, x1, 0, tf, tn)
            pr, pi = _dft_fwd(fwd, y1, kr, ki, 1, tf, tn)
            y2 = _dft_inv(inv, pr, pi, y1, bias, x2, 1, tf, tn).reshape(bsz, seq, width)
            x = _mm_residual([y2], [hy_w_out[i].astype(BF16)], hy_b_out[i][None, :], x, lat(g1), tm, "hyena_out")
            x = token_order(x)
        w_route = _pad_to(jnp.concatenate([moe_wg1[layer], moe_wg2[layer]], axis=1), (d, LANES))
        b_route = _pad_to(jnp.concatenate([moe_bg1[layer], moe_bg2[layer]])[None, :], (1, LANES))
        x = _hier_moe(x, norm2_g[layer][None, :], lat(sc2), lat(sh2), lat(g2), w_route, b_route,
                      wg_all, wu_all, wd_all, layer)
    return x
```

```python
import functools
import math

import jax
import jax.numpy as jnp
from jax import lax
from jax.experimental import pallas as pl
from jax.experimental.pallas import tpu as pltpu

F32 = jnp.float32
BF16 = jnp.bfloat16
HIGHEST = lax.Precision.HIGHEST

LANES = 128
V7X_VMEM_BYTES = 64 * 1024 * 1024
VMEM_LIMIT = V7X_VMEM_BYTES * 7 // 8

GRID_W = 64
HEAD_DIM = 128
ROPE_BASE = 10000.0
EPS = 1e-6
Q_BLOCK = 128
DIFF_HEADS = 8
DIFF_SUB = HEAD_DIM // 2
SWA_HEADS = 8
SWA_KV_HEADS = 2
SWA_GROUP = SWA_HEADS // SWA_KV_HEADS
WINDOW = 128
DIFF_Q = DIFF_HEADS * 2 * DIFF_SUB
DIFF_V = DIFF_HEADS * HEAD_DIM
SWA_Q = SWA_HEADS * HEAD_DIM
SWA_KV = SWA_KV_HEADS * HEAD_DIM
ATTN_IN = 2 * DIFF_Q + DIFF_V + SWA_Q + 2 * SWA_KV
HYENA_ORDER = 2
SHORT_CONV = 3
FILTER_EMB = 33
FILTER_BANDS = (FILTER_EMB - 1) // 2
FILTER_HIDDEN = 64
DECAY_FAST = 0.3
DECAY_SLOW = 1.5
DECAY_TARGET = 1e-2
MOE_GROUPS = 4
MOE_PER_GROUP = 8
MOE_EXPERTS = MOE_GROUPS * MOE_PER_GROUP
MOE_TOPK = 2
MOE_BLOCK = 256
FF_CHUNK = 256
DIFF_KEY_CHUNK = 512
DFT_RADIX = 4
LOG2_E = 1.4426950408889634
NEG_INF = float("-inf")


def _tile(n, pref):
    t = pref
    while n % t:
        t //= 2
    return t


def _params(*sem):
    return pltpu.CompilerParams(dimension_semantics=sem, vmem_limit_bytes=VMEM_LIMIT)


def _dot(a, b):
    return jnp.dot(a, b, preferred_element_type=F32)


def _dot_nt(a, b):
    return lax.dot_general(a, b, (((1,), (1,)), ((), ())), preferred_element_type=F32)


def _mod_kernel(c_ref, w_ref, b_ref, o_ref):
    c = c_ref[...]
    s = c * jax.nn.sigmoid(c)
    o_ref[...] = jnp.dot(s, w_ref[...], precision=HIGHEST, preferred_element_type=F32) + b_ref[...]


def _adaln_mod(cond, ada_w, ada_b, layer):
    rows, d = cond.shape
    n = ada_w.shape[2]
    tn = _tile(n, 768)
    return pl.pallas_call(
        _mod_kernel,
        grid=(n // tn,),
        in_specs=[pl.BlockSpec((rows, d), lambda j: (0, 0)),
                  pl.BlockSpec((None, d, tn), lambda j: (layer, 0, j)),
                  pl.BlockSpec((None, 1, tn), lambda j: (layer, 0, j))],
        out_specs=pl.BlockSpec((rows, tn), lambda j: (0, j)),
        out_shape=jax.ShapeDtypeStruct((rows, n), F32),
        compiler_params=_params("parallel"),
        name="adaln_mod",
    )(cond, ada_w, ada_b)


def _norm_mod(x, g, sc, sh):
    inv = lax.rsqrt(jnp.mean(x * x, axis=-1, keepdims=True) + EPS)
    return ((x * inv) * g) * (1.0 + sc) + sh


def _norm_linear_kernel(x_ref, g_ref, sc_ref, sh_ref, w_ref, b_ref, o_ref, h_ref):
    @pl.when(pl.program_id(2) == 0)
    def _():
        h_ref[...] = _norm_mod(x_ref[...], g_ref[...], sc_ref[...], sh_ref[...]).astype(BF16)

    o_ref[...] = (_dot(h_ref[...], w_ref[...]) + b_ref[...]).astype(o_ref.dtype)


def _norm_linear(x, g, sc, sh, w, b, tm, tn, name):
    bsz, l, d = x.shape
    n = w.shape[1]
    tn = _tile(n, tn)
    return pl.pallas_call(
        _norm_linear_kernel,
        grid=(bsz, l // tm, n // tn),
        in_specs=[pl.BlockSpec((None, tm, d), lambda bi, i, j: (bi, i, 0)),
                  pl.BlockSpec((1, d), lambda bi, i, j: (0, 0)),
                  pl.BlockSpec((None, 1, d), lambda bi, i, j: (bi, 0, 0)),
                  pl.BlockSpec((None, 1, d), lambda bi, i, j: (bi, 0, 0)),
                  pl.BlockSpec((d, tn), lambda bi, i, j: (0, j)),
                  pl.BlockSpec((1, tn), lambda bi, i, j: (0, j))],
        out_specs=pl.BlockSpec((None, tm, tn), lambda bi, i, j: (bi, i, j)),
        out_shape=jax.ShapeDtypeStruct((bsz, l, n), BF16),
        scratch_shapes=[pltpu.VMEM((tm, d), BF16)],
        compiler_params=_params("parallel", "arbitrary", "arbitrary"),
        name=name,
    )(x, g, sc, sh, w, b)


def _mm_res_kernel(*refs, n_pairs):
    a_refs = refs[:n_pairs]
    w_refs = refs[n_pairs:2 * n_pairs]
    b_ref, x_ref, g_ref, o_ref = refs[2 * n_pairs:]
    acc = _dot(a_refs[0][...], w_refs[0][...])
    for a_ref, w_ref in zip(a_refs[1:], w_refs[1:]):
        acc = acc + _dot(a_ref[...], w_ref[...])
    o_ref[...] = x_ref[...] + g_ref[...] * (acc + b_ref[...])


def _mm_residual(a_list, w_list, b, x, g, tm, name):
    bsz, l, n = x.shape
    n_pairs = len(a_list)

    def a_spec(a):
        if a.ndim == 4:
            per_half = a.shape[2] // tm
            return pl.BlockSpec((None, None, tm, a.shape[3]), lambda bi, i: (bi, i // per_half, i % per_half, 0))
        return pl.BlockSpec((None, tm, a.shape[2]), lambda bi, i: (bi, i, 0))

    in_specs = [a_spec(a) for a in a_list]
    in_specs += [pl.BlockSpec(w.shape, lambda bi, i: (0, 0)) for w in w_list]
    in_specs += [pl.BlockSpec((1, n), lambda bi, i: (0, 0)),
                 pl.BlockSpec((None, tm, n), lambda bi, i: (bi, i, 0)),
                 pl.BlockSpec((None, 1, n), lambda bi, i: (bi, 0, 0))]
    return pl.pallas_call(
        functools.partial(_mm_res_kernel, n_pairs=n_pairs),
        grid=(bsz, l // tm),
        in_specs=in_specs,
        out_specs=pl.BlockSpec((None, tm, n), lambda bi, i: (bi, i, 0)),
        out_shape=jax.ShapeDtypeStruct((bsz, l, n), F32),
        compiler_params=_params("parallel", "parallel"),
        name=name,
    )(*a_list, *w_list, b, x, g)


def _rope_tables(n_tokens, dim, width):
    n_rows = n_tokens // GRID_W
    row = jnp.repeat(jnp.arange(n_rows, dtype=F32), GRID_W)
    col = jnp.tile(jnp.arange(GRID_W, dtype=F32), n_rows)
    n_freq = dim // 4
    inv = ROPE_BASE ** (-jnp.arange(n_freq, dtype=F32) / n_freq)
    ang = jnp.concatenate([row[:, None] * inv, col[:, None] * inv], axis=-1)
    ang = jnp.repeat(ang, 2, axis=-1)
    ang = jnp.tile(ang, (1, width // dim))
    sign = jnp.where(jnp.arange(width) % 2 == 0, -1.0, 1.0).astype(F32)
    return jnp.cos(ang), jnp.sin(ang) * sign


def _rope(x, cos_t, sin_t):
    width = x.shape[-1]
    lane = lax.broadcasted_iota(jnp.int32, (1, width), 1)
    partner = jnp.where(lane % 2 == 0, pltpu.roll(x, width - 1, 1), pltpu.roll(x, 1, 1))
    return x * cos_t + partner * sin_t


def _sub_rms(x, g):
    lane = lax.broadcasted_iota(jnp.int32, (1, 2 * DIFF_SUB), 1)
    lo = lane < DIFF_SUB
    sq = x * x
    ss_lo = jnp.sum(jnp.where(lo, sq, 0.0), axis=-1, keepdims=True)
    ss_hi = jnp.sum(jnp.where(lo, 0.0, sq), axis=-1, keepdims=True)
    inv = jnp.where(lo, lax.rsqrt(ss_lo / DIFF_SUB + EPS), lax.rsqrt(ss_hi / DIFF_SUB + EPS))
    return (x * inv) * g


def _rms(x, g):
    return (x * lax.rsqrt(jnp.mean(x * x, axis=-1, keepdims=True) + EPS)) * g


def _diff_attn_kernel(q_ref, k_ref, v_ref, kc_ref, vc_ref, cos_ref, sin_ref,
                      gq_ref, gk_ref, gs_ref, lamp_ref, o_ref, q_scr, k_scr, v_scr, s_scr, lam_scr,
                      *, lam_init, n_ctx, tq):
    n_keys = k_scr.shape[0]

    @pl.when(pl.program_id(2) == 0)
    def _():
        lane = lax.broadcasted_iota(jnp.int32, (1, 2 * DIFF_SUB), 1)
        lo = lane < DIFF_SUB
        gk = gk_ref[...]
        cos_t = cos_ref[...]
        sin_t = sin_ref[...]
        k_scr[0:n_ctx, :] = _sub_rms(kc_ref[...].astype(F32), gk).astype(BF16)
        k_scr[n_ctx:, :] = _rope(_sub_rms(k_ref[...].astype(F32), gk), cos_t, sin_t).astype(BF16)
        v_scr[0:n_ctx, 0:HEAD_DIM] = vc_ref[...].astype(BF16)
        v_scr[n_ctx:, 0:HEAD_DIM] = v_ref[...].astype(BF16)
        v_scr[:, HEAD_DIM:] = jnp.broadcast_to(jnp.where(lane == 0, 1.0, 0.0).astype(BF16), (n_keys, HEAD_DIM))
        q = _rope(_sub_rms(q_ref[...].astype(F32), gq_ref[...]), cos_t, sin_t) * (DIFF_SUB ** -0.5 * LOG2_E)
        q_scr[0] = jnp.where(lo, q, 0.0).astype(BF16)
        q_scr[1] = jnp.where(lo, 0.0, q).astype(BF16)
        lp = lamp_ref[...]
        t1 = jnp.sum(lp[0:1] * lp[1:2], axis=-1, keepdims=True)
        t2 = jnp.sum(lp[2:3] * lp[3:4], axis=-1, keepdims=True)
        lam_scr[...] = jnp.broadcast_to(jnp.exp(t1) - jnp.exp(t2) + lam_init, lam_scr.shape)

    chunks = [(c0, min(c0 + DIFF_KEY_CHUNK, n_keys)) for c0 in range(0, n_keys, DIFF_KEY_CHUNK)]
    lam = lam_scr[:, 0:1]
    for t in range(tq // Q_BLOCK):
        rows = pl.ds(pl.multiple_of(pl.program_id(2) * tq + t * Q_BLOCK, Q_BLOCK), Q_BLOCK)
        qs = [q_scr[0, rows, :], q_scr[1, rows, :]]

        mx = [jnp.full((Q_BLOCK, LANES), NEG_INF, F32) for _ in range(2)]
        for c0, c1 in chunks:
            kc = k_scr[c0:c1, :]
            for h in range(2):
                s = _dot_nt(qs[h], kc)
                s_scr[2 * t + h, :, c0:c1] = s
                for j in range((c1 - c0) // LANES):
                    mx[h] = jnp.maximum(mx[h], s[:, j * LANES:(j + 1) * LANES])
        mrow = [jnp.max(m, axis=-1, keepdims=True) for m in mx]

        out = [jnp.zeros((Q_BLOCK, 2 * HEAD_DIM), F32) for _ in range(2)]
        for c0, c1 in chunks:
            vc = v_scr[c0:c1, :]
            for h in range(2):
                e = jnp.exp2(s_scr[2 * t + h, :, c0:c1] - mrow[h])
                out[h] = out[h] + _dot(e.astype(BF16), vc)
        r1 = 1.0 / out[0][:, HEAD_DIM:HEAD_DIM + 1]
        r2 = lam / out[1][:, HEAD_DIM:HEAD_DIM + 1]
        o = out[0][:, 0:HEAD_DIM] * r1 - out[1][:, 0:HEAD_DIM] * r2
        o_ref[t * Q_BLOCK:(t + 1) * Q_BLOCK, :] = (_rms(o, gs_ref[...]) * (1.0 - lam_init)).astype(o_ref.dtype)


def _diff_attention(qkv, qkv_c, tabs, gq, gk, gs, lamp, lam_init, tq):
    bsz, l, _ = qkv.shape
    n_ctx = qkv_c.shape[1]
    cos_t, sin_t = tabs
    hb = 2 * DIFF_SUB
    kb = DIFF_Q // hb
    vb = 2 * DIFF_Q // hb
    kern = functools.partial(_diff_attn_kernel, lam_init=lam_init, n_ctx=n_ctx, tq=tq)
    vec = pl.BlockSpec((1, hb), lambda bi, h, i: (0, 0))
    return pl.pallas_call(
        kern,
        grid=(bsz, DIFF_HEADS, l // tq),
        in_specs=[pl.BlockSpec((None, l, hb), lambda bi, h, i: (bi, 0, h)),
                  pl.BlockSpec((None, l, hb), lambda bi, h, i: (bi, 0, kb + h)),
                  pl.BlockSpec((None, l, hb), lambda bi, h, i: (bi, 0, vb + h)),
                  pl.BlockSpec((None, n_ctx, hb), lambda bi, h, i: (bi, 0, kb + h)),
                  pl.BlockSpec((None, n_ctx, hb), lambda bi, h, i: (bi, 0, vb + h)),
                  pl.BlockSpec((l, hb), lambda bi, h, i: (0, 0)),
                  pl.BlockSpec((l, hb), lambda bi, h, i: (0, 0)),
                  vec, vec, vec,
                  pl.BlockSpec((8, hb), lambda bi, h, i: (0, 0))],
        out_specs=pl.BlockSpec((None, tq, hb), lambda bi, h, i: (bi, i, h)),
        out_shape=jax.ShapeDtypeStruct((bsz, l, DIFF_V), BF16),
        scratch_shapes=[pltpu.VMEM((2, l, hb), BF16), pltpu.VMEM((n_ctx + l, hb), BF16),
                        pltpu.VMEM((n_ctx + l, 2 * HEAD_DIM), BF16),
                        pltpu.VMEM((2 * (tq // Q_BLOCK), Q_BLOCK, n_ctx + l), F32),
                        pltpu.VMEM((1, hb), F32)],
        compiler_params=_params("parallel", "parallel", "arbitrary"),
        name="diff_attention",
    )(qkv, qkv, qkv, qkv_c, qkv_c, cos_t, sin_t, gq, gk, gs, lamp)


def _swa_kernel(q_ref, k_ref, v_ref, kc_ref, vc_ref, cos_ref, sin_ref, gq_ref, gk_ref, sink_ref,
                o_ref, q_scr, k_scr, v_scr, kc_scr, vc_scr, *, seq):
    kvh = pl.program_id(1)
    n = pl.program_id(2)
    span = Q_BLOCK + 2 * WINDOW

    @pl.when(n == 0)
    def _():
        gk = gk_ref[...]
        gq = gq_ref[...]
        cos_t = cos_ref[...]
        sin_t = sin_ref[...]
        zeros = jnp.zeros((WINDOW, HEAD_DIM), BF16)
        k_scr[0:WINDOW, :] = zeros
        v_scr[0:WINDOW, :] = zeros
        k_scr[WINDOW + seq:, :] = zeros
        v_scr[WINDOW + seq:, :] = zeros
        k_scr[WINDOW:WINDOW + seq, :] = _rope(_rms(k_ref[...].astype(F32), gk), cos_t, sin_t).astype(BF16)
        v_scr[WINDOW:WINDOW + seq, :] = v_ref[...].astype(BF16)
        kc_scr[...] = _rms(kc_ref[...].astype(F32), gk).astype(BF16)
        vc_scr[...] = vc_ref[...].astype(BF16)
        for j in range(SWA_GROUP):
            qj = q_ref[:, j * HEAD_DIM:(j + 1) * HEAD_DIM].astype(F32)
            qj = _rope(_rms(qj, gq), cos_t, sin_t) * (HEAD_DIM ** -0.5)
            q_scr[j] = qj.astype(BF16)

    start = pl.multiple_of(n * Q_BLOCK, Q_BLOCK)
    kw = k_scr[pl.ds(start, span), :]
    vw = v_scr[pl.ds(start, span), :]
    kc = kc_scr[...]
    vc = vc_scr[...]
    rows = SWA_GROUP * Q_BLOCK
    ridx = lax.broadcasted_iota(jnp.int32, (rows, span), 0)
    kidx = lax.broadcasted_iota(jnp.int32, (rows, span), 1)
    rel = kidx - (ridx & (Q_BLOCK - 1))
    key_pos = n * Q_BLOCK + kidx - WINDOW
    mask = (rel >= 0) & (rel <= 2 * WINDOW) & (key_pos >= 0) & (key_pos < seq)
    head = lax.broadcasted_iota(jnp.int32, (rows, 1), 0) // Q_BLOCK
    sink = jnp.zeros((rows, 1), F32)
    for j in range(SWA_GROUP):
        sink = jnp.where(head == j, sink_ref[pl.ds(kvh * SWA_GROUP + j, 1), 0:1], sink)
    q = jnp.concatenate([q_scr[j, pl.ds(start, Q_BLOCK), :] for j in range(SWA_GROUP)], axis=0)
    s_loc = jnp.where(mask, _dot_nt(q, kw), NEG_INF)
    s_ctx = _dot_nt(q, kc)
    m = jnp.maximum(jnp.maximum(jnp.max(s_loc, axis=-1, keepdims=True),
                                jnp.max(s_ctx, axis=-1, keepdims=True)), sink)
    e_loc = jnp.exp(s_loc - m)
    e_ctx = jnp.exp(s_ctx - m)
    denom = (jnp.sum(e_loc, axis=-1, keepdims=True) + jnp.sum(e_ctx, axis=-1, keepdims=True)
             + jnp.exp(sink - m))
    r = 1.0 / denom
    o = _dot((e_ctx * r).astype(BF16), vc) + _dot((e_loc * r).astype(BF16), vw)
    o_ref[...] = jnp.concatenate([o[j * Q_BLOCK:(j + 1) * Q_BLOCK] for j in range(SWA_GROUP)],
                                 axis=-1).astype(o_ref.dtype)


def _swa_attention(qkv, qkv_c, tabs, gq, gk, sink_rows):
    bsz, l, _ = qkv.shape
    n_ctx = qkv_c.shape[1]
    cos_t, sin_t = tabs
    gw = SWA_GROUP * HEAD_DIM
    q0 = (2 * DIFF_Q + DIFF_V) // gw
    k0 = (2 * DIFF_Q + DIFF_V + SWA_Q) // HEAD_DIM
    v0 = k0 + SWA_KV_HEADS
    vec = pl.BlockSpec((1, HEAD_DIM), lambda bi, h, i: (0, 0))
    return pl.pallas_call(
        functools.partial(_swa_kernel, seq=l),
        grid=(bsz, SWA_KV_HEADS, l // Q_BLOCK),
        in_specs=[pl.BlockSpec((None, l, gw), lambda bi, h, i: (bi, 0, q0 + h)),
                  pl.BlockSpec((None, l, HEAD_DIM), lambda bi, h, i: (bi, 0, k0 + h)),
                  pl.BlockSpec((None, l, HEAD_DIM), lambda bi, h, i: (bi, 0, v0 + h)),
                  pl.BlockSpec((None, n_ctx, HEAD_DIM), lambda bi, h, i: (bi, 0, k0 + h)),
                  pl.BlockSpec((None, n_ctx, HEAD_DIM), lambda bi, h, i: (bi, 0, v0 + h)),
                  pl.BlockSpec((l, HEAD_DIM), lambda bi, h, i: (0, 0)),
                  pl.BlockSpec((l, HEAD_DIM), lambda bi, h, i: (0, 0)),
                  vec, vec,
                  pl.BlockSpec((SWA_HEADS, HEAD_DIM), lambda bi, h, i: (0, 0))],
        out_specs=pl.BlockSpec((None, Q_BLOCK, gw), lambda bi, h, i: (bi, i, h)),
        out_shape=jax.ShapeDtypeStruct((bsz, l, SWA_Q), BF16),
        scratch_shapes=[pltpu.VMEM((SWA_GROUP, l, HEAD_DIM), BF16),
                        pltpu.VMEM((l + 2 * WINDOW, HEAD_DIM), BF16), pltpu.VMEM((l + 2 * WINDOW, HEAD_DIM), BF16),
                        pltpu.VMEM((n_ctx, HEAD_DIM), BF16), pltpu.VMEM((n_ctx, HEAD_DIM), BF16)],
        compiler_params=_params("parallel", "parallel", "arbitrary"),
        name="swa_attention",
    )(qkv, qkv, qkv, qkv_c, qkv_c, cos_t, sin_t, gq, gk, sink_rows)


def _router_kernel(x_ref, g_ref, sc_ref, sh_ref, w_ref, b_ref, tri_ref, h_ref, e_ref, p_ref, cnt_ref, run_scr):
    @pl.when((pl.program_id(0) == 0) & (pl.program_id(1) == 0))
    def _():
        run_scr[...] = jnp.zeros(run_scr.shape, F32)

    h = _norm_mod(x_ref[...], g_ref[...], sc_ref[...], sh_ref[...])
    h_ref[...] = h
    logits = jnp.dot(h, w_ref[...], precision=HIGHEST, preferred_element_type=F32) + b_ref[...]
    lane_i = lax.broadcasted_iota(jnp.int32, (1, LANES), 1)
    lane = lane_i.astype(F32)
    big = float(LANES)
    lg = jnp.where(lane_i < MOE_GROUPS, logits, NEG_INF)
    mg = jnp.max(lg, axis=-1, keepdims=True)
    p_top = 1.0 / jnp.sum(jnp.exp(lg - mg), axis=-1, keepdims=True)
    grp = jnp.min(jnp.where(lg == mg, lane, big), axis=-1, keepdims=True)
    e_lane = lane - MOE_GROUPS
    in_grp = (e_lane >= grp * MOE_PER_GROUP) & (e_lane < (grp + 1.0) * MOE_PER_GROUP)
    le = jnp.where(in_grp, logits, NEG_INF)
    t1 = jnp.max(le, axis=-1, keepdims=True)
    i1 = jnp.min(jnp.where(le == t1, lane, big), axis=-1, keepdims=True)
    le2 = jnp.where(lane == i1, NEG_INF, le)
    t2 = jnp.max(le2, axis=-1, keepdims=True)
    i2 = jnp.min(jnp.where(le2 == t2, lane, big), axis=-1, keepdims=True)
    w2 = jnp.exp(t2 - t1)
    inv = p_top / (1.0 + w2)
    hot1 = lane == i1
    hot2 = lane == i2
    onehot = jnp.where(hot1 | hot2, 1.0, 0.0)
    before = run_scr[...] + _dot(tri_ref[...], onehot.astype(BF16))
    r1 = jnp.sum(jnp.where(hot1, before, 0.0), axis=-1, keepdims=True)
    r2 = jnp.sum(jnp.where(hot2, before, 0.0), axis=-1, keepdims=True)
    run_scr[...] = run_scr[...] + jnp.sum(onehot, axis=0, keepdims=True)
    cnt_ref[...] = run_scr[...]
    e_ref[...] = jnp.where(lane_i == 0, i1 - MOE_GROUPS,
                           jnp.where(lane_i == 1, i2 - MOE_GROUPS,
                                     jnp.where(lane_i == 2, r1, jnp.where(lane_i == 3, r2, 0.0)))).astype(jnp.int32)
    p_ref[...] = jnp.where(lane_i == 0, inv, jnp.where(lane_i == 1, inv * w2, 0.0))


def _router(x, g, sc, sh, w, b, tm):
    bsz, l, d = x.shape
    row = lambda bi, i: (bi, i, 0)
    tri = (jnp.arange(tm)[:, None] > jnp.arange(tm)[None, :]).astype(BF16)
    return pl.pallas_call(
        _router_kernel,
        grid=(bsz, l // tm),
        in_specs=[pl.BlockSpec((None, tm, d), row),
                  pl.BlockSpec((1, d), lambda bi, i: (0, 0)),
                  pl.BlockSpec((None, 1, d), lambda bi, i: (bi, 0, 0)),
                  pl.BlockSpec((None, 1, d), lambda bi, i: (bi, 0, 0)),
                  pl.BlockSpec((d, LANES), lambda bi, i: (0, 0)),
                  pl.BlockSpec((1, LANES), lambda bi, i: (0, 0)),
                  pl.BlockSpec((tm, tm), lambda bi, i: (0, 0))],
        out_specs=[pl.BlockSpec((None, tm, d), row),
                   pl.BlockSpec((None, tm, LANES), row),
                   pl.BlockSpec((None, tm, LANES), row),
                   pl.BlockSpec((1, LANES), lambda bi, i: (0, 0))],
        out_shape=[jax.ShapeDtypeStruct((bsz, l, d), F32),
                   jax.ShapeDtypeStruct((bsz, l, LANES), jnp.int32),
                   jax.ShapeDtypeStruct((bsz, l, LANES), F32),
                   jax.ShapeDtypeStruct((1, LANES), F32)],
        scratch_shapes=[pltpu.VMEM((1, LANES), F32)],
        compiler_params=_params("arbitrary", "arbitrary"),
        name="moe_router",
    )(x, g, sc, sh, w, b, tri)


def _dispatch_kernel(slot_ref, h_ref, init_ref, xb_ref, sem):
    del init_ref
    tm = h_ref.shape[0]
    base = pl.program_id(0) * tm

    def copy(r, k):
        dst = slot_ref[(base + r) * MOE_TOPK + k]
        return pltpu.make_async_copy(h_ref.at[pl.ds(r, 1)], xb_ref.at[pl.ds(dst, 1)], sem)

    def start(r, carry):
        for k in range(MOE_TOPK):
            copy(r, k).start(priority=k)
        return carry

    def wait(r, carry):
        for k in range(MOE_TOPK):
            copy(r, k).wait()
        return carry

    lax.fori_loop(0, tm, start, 0, unroll=8)
    lax.fori_loop(0, tm, wait, 0, unroll=8)


def _dispatch(slot, h, n_rows, tm):
    n, d = h.shape
    grid_spec = pltpu.PrefetchScalarGridSpec(
        num_scalar_prefetch=1,
        grid=(n // tm,),
        in_specs=[pl.BlockSpec((tm, d), lambda i, s: (i, 0)),
                  pl.BlockSpec(memory_space=pl.ANY)],
        out_specs=pl.BlockSpec(memory_space=pl.ANY),
        scratch_shapes=[pltpu.SemaphoreType.DMA(())],
    )
    return pl.pallas_call(
        _dispatch_kernel,
        grid_spec=grid_spec,
        out_shape=jax.ShapeDtypeStruct((n_rows, d), F32),
        input_output_aliases={2: 0},
        compiler_params=_params("arbitrary"),
        name="moe_dispatch",
    )(slot, h, jnp.zeros((n_rows, d), F32))


def _combine_kernel(slot_ref, x_ref, g_ref, p_ref, yb_ref, o_ref, y_scr, sem):
    tm = x_ref.shape[0]
    base = pl.program_id(0) * tm

    def copy(r, k):
        src = slot_ref[(base + r) * MOE_TOPK + k]
        return pltpu.make_async_copy(yb_ref.at[pl.ds(src, 1)], y_scr.at[k, pl.ds(r, 1)], sem)

    def start(r, carry):
        for k in range(MOE_TOPK):
            copy(r, k).start(priority=k)
        return carry

    def wait(r, carry):
        for k in range(MOE_TOPK):
            copy(r, k).wait()
        return carry

    lax.fori_loop(0, tm, start, 0, unroll=8)
    lax.fori_loop(0, tm, wait, 0, unroll=8)
    p = p_ref[...]
    moe = y_scr[0] * p[:, 0:1]
    for k in range(1, MOE_TOPK):
        moe = moe + y_scr[k] * p[:, k:k + 1]
    o_ref[...] = x_ref[...] + g_ref[...] * moe


def _combine(slot, x, g, gates, yb, tm):
    bsz, l, d = x.shape
    nt = l // tm
    grid_spec = pltpu.PrefetchScalarGridSpec(
        num_scalar_prefetch=1,
        grid=(bsz * nt,),
        in_specs=[pl.BlockSpec((None, tm, d), lambda i, s: (i // nt, i % nt, 0)),
                  pl.BlockSpec((None, 1, d), lambda i, s: (i // nt, 0, 0)),
                  pl.BlockSpec((None, tm, LANES), lambda i, s: (i // nt, i % nt, 0)),
                  pl.BlockSpec(memory_space=pl.ANY)],
        out_specs=pl.BlockSpec((None, tm, d), lambda i, s: (i // nt, i % nt, 0)),
        scratch_shapes=[pltpu.VMEM((MOE_TOPK, tm, d), F32), pltpu.SemaphoreType.DMA(())],
    )
    return pl.pallas_call(
        _combine_kernel,
        grid_spec=grid_spec,
        out_shape=jax.ShapeDtypeStruct((bsz, l, d), F32),
        compiler_params=_params("arbitrary"),
        name="moe_combine",
    )(slot, x, g, gates, yb)


def _expert_kernel(be_ref, nu_ref, x_ref, wg_ref, wu_ref, wd_ref, o_ref):
    i = pl.program_id(0)

    @pl.when(i < nu_ref[0])
    def _():
        x = x_ref[...].astype(BF16)
        ff = wg_ref.shape[1]
        acc = jnp.zeros(o_ref.shape, F32)
        for c in range(ff // FF_CHUNK):
            cols = slice(c * FF_CHUNK, (c + 1) * FF_CHUNK)
            gate = _dot(x, wg_ref[:, cols])
            up = _dot(x, wu_ref[:, cols])
            hidden = (gate * jax.nn.sigmoid(gate) * up).astype(BF16)
            acc = acc + _dot(hidden, wd_ref[cols, :])
        o_ref[...] = acc

    @pl.when(i >= nu_ref[0])
    def _():
        o_ref[...] = jnp.zeros(o_ref.shape, F32)


def _expert_ffn(xb, block_e, n_used, w_gate, w_up, w_down, layer):
    n_blocks, _, d = xb.shape
    ff = w_gate.shape[2]
    base = layer * MOE_EXPERTS
    grid_spec = pltpu.PrefetchScalarGridSpec(
        num_scalar_prefetch=2,
        grid=(n_blocks,),
        in_specs=[pl.BlockSpec((None, MOE_BLOCK, d), lambda i, be, nu: (i, 0, 0)),
                  pl.BlockSpec((None, d, ff), lambda i, be, nu: (base + be[i], 0, 0)),
                  pl.BlockSpec((None, d, ff), lambda i, be, nu: (base + be[i], 0, 0)),
                  pl.BlockSpec((None, ff, d), lambda i, be, nu: (base + be[i], 0, 0))],
        out_specs=pl.BlockSpec((None, MOE_BLOCK, d), lambda i, be, nu: (i, 0, 0)),
    )
    return pl.pallas_call(
        _expert_kernel,
        grid_spec=grid_spec,
        out_shape=jax.ShapeDtypeStruct((n_blocks, MOE_BLOCK, d), F32),
        compiler_params=_params("arbitrary"),
        name="expert_ffn",
    )(block_e, n_used, xb, w_gate, w_up, w_down)


def _hier_moe(x, g, sc, sh, gate2, w_route, b_route, w_gate, w_up, w_down, layer):
    bsz, l, d = x.shape
    n = bsz * l
    tm = min(512, l)
    h, route, gates, cnt = _router(x, g, sc, sh, w_route, b_route, tm)
    route = route.reshape(n, LANES)
    expert = route[:, :MOE_TOPK]
    rank = route[:, MOE_TOPK:2 * MOE_TOPK]
    counts = cnt[0, MOE_GROUPS:MOE_GROUPS + MOE_EXPERTS].astype(jnp.int32)
    padded = (counts + MOE_BLOCK - 1) // MOE_BLOCK * MOE_BLOCK
    pad_end = jnp.cumsum(padded)
    pad_start = pad_end - padded
    hit = expert[:, :, None] == jnp.arange(MOE_EXPERTS, dtype=jnp.int32)
    slot = (jnp.sum(jnp.where(hit, pad_start, 0), axis=-1) + rank).reshape(-1)
    n_blocks = -(-(n * MOE_TOPK) // MOE_BLOCK) + MOE_EXPERTS
    block_row = jnp.arange(n_blocks, dtype=jnp.int32)[:, None] * MOE_BLOCK
    block_e = jnp.minimum(jnp.sum((pad_end[None, :] <= block_row).astype(jnp.int32), axis=-1), MOE_EXPERTS - 1)
    n_used = (pad_end[-1:] // MOE_BLOCK).astype(jnp.int32)
    xb = _dispatch(slot, h.reshape(n, d), n_blocks * MOE_BLOCK, tm).reshape(n_blocks, MOE_BLOCK, d)
    yb = _expert_ffn(xb, block_e, n_used, w_gate, w_up, w_down, layer).reshape(-1, d)
    return _combine(slot, x, gate2, gates, yb, tm)


def _short_conv_kernel(zv_ref, z1_ref, z2_ref, wv_ref, w1_ref, w2_ref, bv_ref, b1_ref, b2_ref,
                       v_ref, x1_ref, x2_ref):
    rdx = DFT_RADIX
    q = zv_ref.shape[0] // rdx
    row = lax.broadcasted_iota(jnp.int32, (q, 1), 0)

    for z_ref, w_ref, b_ref, o_ref in ((zv_ref, wv_ref, bv_ref, v_ref), (z1_ref, w1_ref, b1_ref, x1_ref),
                                       (z2_ref, w2_ref, b2_ref, x2_ref)):
        z = [z_ref[k * q:(k + 1) * q, :].astype(F32) for k in range(rdx)]
        w = w_ref[...]
        b = b_ref[...]
        for k in range(rdx):
            prev = z[k - 1] if k > 0 else jnp.where(row == 0, 0.0, pltpu.roll(z[rdx - 1], 1, 0))
            nxt = z[k + 1] if k < rdx - 1 else jnp.where(row == q - 1, 0.0, pltpu.roll(z[0], q - 1, 0))
            o_ref[k] = (b + prev * w[0:1] + z[k] * w[1:2] + nxt * w[2:3]).astype(o_ref.dtype)


def _short_conv(z, conv_w, conv_b, tc):
    bsz, l, w3 = z.shape
    w = w3 // 3
    nc = w // tc
    zs = lambda k: pl.BlockSpec((None, l, tc), lambda bi, j: (bi, 0, k * nc + j))
    ws = lambda k: pl.BlockSpec((8, tc), lambda bi, j: (0, k * nc + j))
    bs = lambda k: pl.BlockSpec((1, tc), lambda bi, j: (0, k * nc + j))
    out = pl.BlockSpec((None, DFT_RADIX, l // DFT_RADIX, tc), lambda bi, j: (bi, 0, 0, j))
    return pl.pallas_call(
        _short_conv_kernel,
        grid=(bsz, nc),
        in_specs=[zs(0), zs(1), zs(2), ws(0), ws(1), ws(2), bs(0), bs(1), bs(2)],
        out_specs=[out, out, out],
        out_shape=[jax.ShapeDtypeStruct((bsz, DFT_RADIX, l // DFT_RADIX, w), BF16)] * 3,
        compiler_params=_params("parallel", "parallel"),
        name="short_conv",
    )(z, z, z, conv_w, conv_w, conv_w, conv_b, conv_b, conv_b)


def _filter_kernel(z_ref, w1_ref, b1_ref, f1_ref, w2_ref, b2_ref, f2_ref, w3_ref, dl_ref, bw_ref, o_ref, a_scr):
    n = z_ref.shape[0]

    @pl.when(pl.program_id(0) == 0)
    def _():
        a = jnp.sin(f1_ref[...] * (jnp.dot(z_ref[...], w1_ref[...], precision=HIGHEST,
                                           preferred_element_type=F32) + b1_ref[...]))
        a_scr[...] = jnp.sin(f2_ref[...] * (jnp.dot(a, w2_ref[...], precision=HIGHEST,
                                                    preferred_element_type=F32) + b2_ref[...]))

    hf = jnp.dot(a_scr[...], w3_ref[...], precision=HIGHEST, preferred_element_type=F32)
    q = n // DFT_RADIX
    row = lax.broadcasted_iota(jnp.int32, (n, 1), 0)
    assert q & (q - 1) == 0
    tap = DFT_RADIX * (row & (q - 1)) + lax.shift_right_logical(row, q.bit_length() - 1)
    t = tap.astype(F32) * (1.0 / (n - 1))
    hf = hf * jnp.exp(-t * dl_ref[...])
    hf = jnp.where((tap == 0) & (bw_ref[...] > 0.5), 0.0, hf)
    for k in range(DFT_RADIX):
        o_ref[k] = hf[k * q:(k + 1) * q]


def _hyena_filters(z, w1, b1, f1, w2, b2, f2, w3, deltas, is_bwd, tn):
    n = z.shape[0]
    cols = w3.shape[1]
    full = lambda a: pl.BlockSpec(a.shape, lambda j: (0, 0))
    tile = lambda r: pl.BlockSpec((r, tn), lambda j: (0, j))
    return pl.pallas_call(
        _filter_kernel,
        grid=(cols // tn,),
        in_specs=[full(z), full(w1), full(b1), full(f1), full(w2), full(b2), full(f2),
                  tile(w3.shape[0]), tile(1), tile(1)],
        out_specs=pl.BlockSpec((DFT_RADIX, n // DFT_RADIX, tn), lambda j: (0, 0, j)),
        out_shape=jax.ShapeDtypeStruct((DFT_RADIX, n // DFT_RADIX, cols), F32),
        scratch_shapes=[pltpu.VMEM((n, LANES), F32)],
        compiler_params=_params("arbitrary"),
        name="hyena_filters",
    )(z, w1, b1, f1, w2, b2, f2, w3, deltas, is_bwd)


def _dft_groups():
    groups = [(0, 1), (DFT_RADIX // 2, -1)]
    for m in range(1, DFT_RADIX // 2):
        groups += [(m, -1), (m, 1)]
    return groups


def _dft_coef(g, k):
    m, b = _dft_groups()[g]
    assert DFT_RADIX in (2, 4)
    quarter = (4 * m * k // DFT_RADIX) % 4
    return (1, 0, -1, 0)[quarter], (0, 1, 0, -1)[quarter], b


def _signed_sum(terms):
    acc = None
    for coef, thunk in terms:
        if coef == 0:
            continue
        val = thunk()
        if acc is None:
            acc = val if coef > 0 else -val
        else:
            acc = acc + val if coef > 0 else acc - val
    return acc


def _spectrum(cprod, sprod, want_c=True, want_s=True):
    cprod = functools.lru_cache(maxsize=None)(cprod)
    sprod = functools.lru_cache(maxsize=None)(sprod)
    out = []
    for g in range(DFT_RADIX):
        cu, su = [], []
        for k in range(DFT_RADIX):
            ca, sa, b = _dft_coef(g, k)
            cu += [(ca, functools.partial(cprod, k)), (-b * sa, functools.partial(sprod, k))]
            su += [(sa, functools.partial(cprod, k)), (b * ca, functools.partial(sprod, k))]
        out.append((_signed_sum(cu) if want_c else None, _signed_sum(su) if want_s else None))
    return out


def _dft_tables(n):
    q = n // DFT_RADIX
    r = 1 << (q.bit_length() // 2)
    f = jnp.arange(q, dtype=jnp.int32)[:, None]
    idx = jnp.arange(r, dtype=jnp.int32)[None, :]
    ang = lambda s: (((2 * f + 1) * s) % (4 * n)).astype(F32) * (math.pi / (2 * n))
    hi = ang(DFT_RADIX * r * idx)[:, :q // r]
    ch, sh = jnp.cos(hi), jnp.sin(hi)
    fwd, inv = [], []
    for trig in ("cos", "sin"):
        for k in range(DFT_RADIX):
            lo = ang(DFT_RADIX * idx + k)
            cl, sl = jnp.cos(lo), jnp.sin(lo)
            a, b, sign = (ch, sh, -1.0) if trig == "cos" else (sh, ch, 1.0)
            fwd.append((a[:, :, None] * cl[:, None, :] + sign * b[:, :, None] * sl[:, None, :]).reshape(q, q))
            inv.append((a.T[:, None, :] * cl.T[None, :, :] + sign * b.T[:, None, :] * sl.T[None, :, :]).reshape(q, q))
    shape = (2, DFT_RADIX, q, q)
    return jnp.stack(fwd).astype(BF16).reshape(shape), jnp.stack(inv).astype(BF16).reshape(shape)


def _dft_filter_kernel(m_ref, hf_ref, hb_ref, kr_ref, ki_ref, sum_scr, dif_scr, *, scale):
    @pl.when(pl.program_id(2) == 0)
    def _():
        hf = hf_ref[...]
        hb = hb_ref[...]
        sum_scr[...] = (hf + hb).astype(BF16)
        dif_scr[...] = (hb - hf).astype(BF16)

    of_sum = _spectrum(lambda k: _dot(m_ref[0, k], sum_scr[k]), lambda k: _dot(m_ref[1, k], sum_scr[k]),
                       want_s=False)
    of_dif = _spectrum(lambda k: _dot(m_ref[0, k], dif_scr[k]), lambda k: _dot(m_ref[1, k], dif_scr[k]),
                       want_c=False)
    for g in range(DFT_RADIX):
        kr_ref[g] = of_sum[g][0] * scale
        ki_ref[g] = of_dif[g][1] * scale


def _dft_filter(fwd, filt, width, tf, tn):
    rdx, q = fwd.shape[1], fwd.shape[2]
    nc = width // tn
    out = pl.BlockSpec((None, rdx, tf, tn), lambda o, j, i: (o, 0, i, j))
    return pl.pallas_call(
        functools.partial(_dft_filter_kernel, scale=1.0 / (rdx * q)),
        grid=(HYENA_ORDER, nc, q // tf),
        in_specs=[pl.BlockSpec((2, rdx, tf, q), lambda o, j, i: (0, 0, i, 0)),
                  pl.BlockSpec((rdx, q, tn), lambda o, j, i: (0, 0, (2 * o) * nc + j)),
                  pl.BlockSpec((rdx, q, tn), lambda o, j, i: (0, 0, (2 * o + 1) * nc + j))],
        out_specs=[out, out],
        out_shape=[jax.ShapeDtypeStruct((HYENA_ORDER, rdx, q, width), F32)] * 2,
        scratch_shapes=[pltpu.VMEM((rdx, q, tn), BF16), pltpu.VMEM((rdx, q, tn), BF16)],
        compiler_params=_params("parallel", "parallel", "arbitrary"),
        name="dft_filter",
    )(fwd, filt, filt)


def _dft_fwd_kernel(m_ref, u_ref, kr_ref, ki_ref, pr_ref, pi_ref):
    spec = _spectrum(lambda k: _dot(m_ref[0, k], u_ref[k]), lambda k: _dot(m_ref[1, k], u_ref[k]))
    for g, (ur, us) in enumerate(spec):
        kr = kr_ref[g]
        ki = ki_ref[g]
        pr_ref[g] = (ur * kr + us * ki).astype(pr_ref.dtype)
        pi_ref[g] = (ur * ki - us * kr).astype(pi_ref.dtype)


def _dft_fwd(fwd, u, kr, ki, order, tf, tn):
    bsz, rdx, q, width = u.shape
    out = pl.BlockSpec((None, rdx, tf, tn), lambda bi, j, i: (bi, 0, i, j))
    ks = pl.BlockSpec((None, rdx, tf, tn), lambda bi, j, i: (order, 0, i, j))
    return pl.pallas_call(
        _dft_fwd_kernel,
        grid=(bsz, width // tn, q // tf),
        in_specs=[pl.BlockSpec((2, rdx, tf, q), lambda bi, j, i: (0, 0, i, 0)),
                  pl.BlockSpec((None, rdx, q, tn), lambda bi, j, i: (bi, 0, 0, j)),
                  ks, ks],
        out_specs=[out, out],
        out_shape=[jax.ShapeDtypeStruct((bsz, rdx, q, width), BF16)] * 2,
        compiler_params=_params("parallel", "parallel", "parallel"),
        name="dft_fwd",
    )(fwd, u, kr, ki)


def _dft_inv_kernel(m_ref, pr_ref, pi_ref, u_ref, b_ref, x_ref, o_ref, ra_scr, rb_scr):
    for k in range(DFT_RADIX):
        @pl.when((pl.program_id(3) == 0) & (pl.program_id(2) == k))
        def _(k=k):
            ra, rb = [], []
            for g in range(DFT_RADIX):
                ca, sa, b = _dft_coef(g, k)
                pr = functools.partial(lambda g: pr_ref[g].astype(F32), g)
                pi = functools.partial(lambda g: pi_ref[g].astype(F32), g)
                ra += [(ca, pr), (-sa, pi)]
                rb += [(b * sa, pr), (b * ca, pi)]
            ra_scr[...] = _signed_sum(ra).astype(BF16)
            rb_scr[...] = _signed_sum(rb).astype(BF16)

    y = _dot(m_ref[0], ra_scr[...]) - _dot(m_ref[1], rb_scr[...])
    u = u_ref[...].astype(F32)
    o_ref[...] = (x_ref[...] * (y + u * b_ref[...])).astype(o_ref.dtype)


def _dft_inv(inv, pr, pi, u, bias, x, order, tt, tn):
    bsz, rdx, q, width = u.shape
    tile = pl.BlockSpec((None, None, tt, tn), lambda bi, j, k, i: (bi, k, i, j))
    spec = pl.BlockSpec((None, rdx, q, tn), lambda bi, j, k, i: (bi, 0, 0, j))
    return pl.pallas_call(
        _dft_inv_kernel,
        grid=(bsz, width // tn, rdx, q // tt),
        in_specs=[pl.BlockSpec((2, None, tt, q), lambda bi, j, k, i: (0, k, i, 0)),
                  spec, spec, tile,
                  pl.BlockSpec((None, 1, tn), lambda bi, j, k, i: (order, 0, j)),
                  tile],
        out_specs=tile,
        out_shape=jax.ShapeDtypeStruct((bsz, rdx, q, width), BF16),
        scratch_shapes=[pltpu.VMEM((q, tn), BF16), pltpu.VMEM((q, tn), BF16)],
        compiler_params=_params("parallel", "parallel", "arbitrary", "arbitrary"),
        name="dft_inv",
    )(inv, pr, pi, u, bias, x)


def _filter_features(n):
    t = jnp.linspace(0.0, 1.0, n, dtype=F32)[:, None]
    w = 2.0 * math.pi * jnp.arange(n, dtype=F32) / n
    f = jnp.linspace(1e-4, FILTER_BANDS - 1, FILTER_BANDS, dtype=F32)
    z = jnp.concatenate([t, jnp.cos(w[:, None] * f), -jnp.sin(w[:, None] * f)], axis=-1)
    return jnp.pad(z, ((0, 0), (0, LANES - FILTER_EMB)))


def _pad_to(a, shape):
    return jnp.pad(a, [(0, t - s) for s, t in zip(a.shape, shape)])


def kernel(x, c, ctx, c_ctx, ada_w, ada_b, norm1_g, norm2_g, attn_w_in, attn_w_out, diff_q_g, diff_k_g, diff_lq1, diff_lk1, diff_lq2, diff_lk2, diff_sub_g, swa_q_g, swa_k_g, swa_sink, hy_w_in, hy_b_in, hy_conv_w, hy_conv_b, flt_w1, flt_b1, flt_f1, flt_w2, flt_b2, flt_f2, flt_w3, hy_bias, hy_w_out, hy_b_out, moe_wg1, moe_bg1, moe_wg2, moe_bg2, moe_w_gate, moe_w_up, moe_w_down):
    bsz, seq, d = x.shape
    depth = ada_w.shape[0]
    n_ctx = ctx.shape[1]
    cond = jnp.concatenate([c, c_ctx[None, :], jnp.zeros((8 - bsz - 1, d), F32)], axis=0)
    ada_b3 = ada_b[:, None, :]
    zero_bias = lambda n: jnp.zeros((1, n), F32)
    wg_all = moe_w_gate.astype(BF16).reshape(depth * MOE_EXPERTS, d, -1)
    wu_all = moe_w_up.astype(BF16).reshape(depth * MOE_EXPERTS, d, -1)
    wd_all = moe_w_down.astype(BF16).reshape(depth * MOE_EXPERTS, -1, d)
    tm = min(512, seq)

    for layer in range(depth):
        even = layer % 2 == 0
        i = layer // 2
        ctx_live = any(j % 2 == 0 for j in range(layer + 1, depth))
        assert not ctx_live, "context-stream update is only needed for deeper stacks"
        mod = _adaln_mod(cond, ada_w, ada_b3, layer)
        sh1, sc1, g1, sh2, sc2, g2 = [mod[:, k * d:(k + 1) * d] for k in range(6)]
        lat = lambda a: a[:bsz, None, :]
        cmod = lambda a: jnp.broadcast_to(a[bsz:bsz + 1, None, :], (bsz, 1, d))
        n1 = norm1_g[layer][None, :]
        if even:
            w_in = attn_w_in[i].astype(BF16)
            qkv = _norm_linear(x, n1, lat(sc1), lat(sh1), w_in, zero_bias(ATTN_IN), tm, 1536, "attn_in")
            qkv_c = _norm_linear(ctx, n1, cmod(sc1), cmod(sh1), w_in, zero_bias(ATTN_IN), min(tm, n_ctx), 1536,
                                 "attn_in_ctx")
            lam_init = 0.8 - 0.6 * math.exp(-0.3 * layer)
            lamp = _pad_to(jnp.stack([diff_lq1[i], diff_lk1[i], diff_lq2[i], diff_lk2[i]]), (8, 2 * DIFF_SUB))
            two = lambda a: jnp.tile(a, 2)[None, :]
            o_diff = _diff_attention(qkv, qkv_c, _rope_tables(seq, DIFF_SUB, 2 * DIFF_SUB), two(diff_q_g[i]),
                                     two(diff_k_g[i]), diff_sub_g[i][None, :], lamp, lam_init, min(2 * Q_BLOCK, seq))
            sink_rows = jnp.broadcast_to(swa_sink[i][:, None], (SWA_HEADS, HEAD_DIM))
            o_swa = _swa_attention(qkv, qkv_c, _rope_tables(seq, HEAD_DIM, HEAD_DIM), swa_q_g[i][None, :],
                                   swa_k_g[i][None, :], sink_rows)
            w_out = attn_w_out[i].astype(BF16)
            x = _mm_residual([o_diff, o_swa], [w_out[:DIFF_V], w_out[DIFF_V:]], zero_bias(d), x, lat(g1), tm,
                             "attn_out")
        else:
            width = hy_w_out.shape[1]
            q = seq // DFT_RADIX
            class_order = lambda a: a.reshape(a.shape[0], q, DFT_RADIX, -1).swapaxes(1, 2).reshape(a.shape)
            token_order = lambda a: a.reshape(a.shape[0], DFT_RADIX, q, -1).swapaxes(1, 2).reshape(a.shape)
            x = class_order(x)
            z = _norm_linear(x, n1, lat(sc1), lat(sh1), hy_w_in[i].astype(BF16), hy_b_in[i][None, :], tm, 1536,
                             "hyena_in")
            v, x1, x2 = _short_conv(z, _pad_to(hy_conv_w[i], (8, 3 * width)), hy_conv_b[i][None, :], 128)
            max_decay = math.log(DECAY_TARGET) / DECAY_FAST
            min_decay = math.log(DECAY_TARGET) / DECAY_SLOW
            deltas = jnp.abs(jnp.tile(jnp.linspace(min_decay, max_decay, width, dtype=F32), 2 * HYENA_ORDER))[None, :]
            is_bwd = jnp.tile(jnp.concatenate([jnp.zeros((width,), F32), jnp.ones((width,), F32)]),
                              HYENA_ORDER)[None, :]
            sq = (LANES, LANES)
            feats = _filter_features(seq)
            feats = jnp.concatenate([feats[k::DFT_RADIX] for k in range(DFT_RADIX)], axis=0)
            filt = _hyena_filters(
                feats, _pad_to(flt_w1[i], sq), _pad_to(flt_b1[i][None, :], (1, LANES)),
                _pad_to(flt_f1[i][None, :], (1, LANES)), _pad_to(flt_w2[i], sq),
                _pad_to(flt_b2[i][None, :], (1, LANES)), _pad_to(flt_f2[i][None, :], (1, LANES)),
                _pad_to(flt_w3[i], (LANES, 2 * HYENA_ORDER * width)), deltas, is_bwd, 256)
            fwd, inv = _dft_tables(seq)
            tf = min(256, q)
            tn = min(512, width)
            kr, ki = _dft_filter(fwd, filt, width, tf, min(256, width))
            bias = hy_bias[i][:, None, :]
            pr, pi = _dft_fwd(fwd, v, kr, ki, 0, tf, tn)
            tt = min(512, q)
            y1 = _dft_inv(inv, pr, pi, v, bias, x1, 0, tt, tn)
            pr, pi = _dft_fwd(fwd, y1, kr, ki, 1, tf, tn)
            y2 = _dft_inv(inv, pr, pi, y1, bias, x2, 1, tt, tn)
            x = _mm_residual([y2], [hy_w_out[i].astype(BF16)], hy_b_out[i][None, :], x, lat(g1), min(tm, q),
                             "hyena_out")
            x = token_order(x)
        w_route = _pad_to(jnp.concatenate([moe_wg1[layer], moe_wg2[layer]], axis=1), (d, LANES))
        b_route = _pad_to(jnp.concatenate([moe_bg1[layer], moe_bg2[layer]])[None, :], (1, LANES))
        x = _hier_moe(x, norm2_g[layer][None, :], lat(sc2), lat(sh2), lat(g2), w_route, b_route,
                      wg_all, wu_all, wd_all, layer)
    return x
```

```python
import functools
import math

import jax
import jax.numpy as jnp
from jax import lax
from jax.experimental import pallas as pl
from jax.experimental.pallas import tpu as pltpu

F32 = jnp.float32
BF16 = jnp.bfloat16
HIGHEST = lax.Precision.HIGHEST

LANES = 128
V7X_VMEM_BYTES = 64 * 1024 * 1024
VMEM_LIMIT = V7X_VMEM_BYTES * 7 // 8

GRID_W = 64
HEAD_DIM = 128
ROPE_BASE = 10000.0
EPS = 1e-6
Q_BLOCK = 128
DIFF_HEADS = 8
DIFF_SUB = HEAD_DIM // 2
SWA_HEADS = 8
SWA_KV_HEADS = 2
SWA_GROUP = SWA_HEADS // SWA_KV_HEADS
WINDOW = 128
DIFF_Q = DIFF_HEADS * 2 * DIFF_SUB
DIFF_V = DIFF_HEADS * HEAD_DIM
SWA_Q = SWA_HEADS * HEAD_DIM
SWA_KV = SWA_KV_HEADS * HEAD_DIM
ATTN_IN = 2 * DIFF_Q + DIFF_V + SWA_Q + 2 * SWA_KV
HYENA_ORDER = 2
SHORT_CONV = 3
FILTER_EMB = 33
FILTER_BANDS = (FILTER_EMB - 1) // 2
FILTER_HIDDEN = 64
DECAY_FAST = 0.3
DECAY_SLOW = 1.5
DECAY_TARGET = 1e-2
MOE_GROUPS = 4
MOE_PER_GROUP = 8
MOE_EXPERTS = MOE_GROUPS * MOE_PER_GROUP
MOE_TOPK = 2
MOE_BLOCK = 256
FF_CHUNK = 256
DIFF_KEY_CHUNK = 512
DFT_RADIX = 4
LOG2_E = 1.4426950408889634
NEG_INF = float("-inf")


def _tile(n, pref):
    t = pref
    while n % t:
        t //= 2
    return t


def _params(*sem):
    return pltpu.CompilerParams(dimension_semantics=sem, vmem_limit_bytes=VMEM_LIMIT)


def _dot(a, b):
    return jnp.dot(a, b, preferred_element_type=F32)


def _dot_nt(a, b):
    return lax.dot_general(a, b, (((1,), (1,)), ((), ())), preferred_element_type=F32)


def _mod_kernel(c_ref, w_ref, b_ref, o_ref):
    c = c_ref[...]
    s = c * jax.nn.sigmoid(c)
    o_ref[...] = jnp.dot(s, w_ref[...], precision=HIGHEST, preferred_element_type=F32) + b_ref[...]


def _adaln_mod(cond, ada_w, ada_b, layer):
    rows, d = cond.shape
    n = ada_w.shape[2]
    tn = _tile(n, 768)
    return pl.pallas_call(
        _mod_kernel,
        grid=(n // tn,),
        in_specs=[pl.BlockSpec((rows, d), lambda j: (0, 0)),
                  pl.BlockSpec((None, d, tn), lambda j: (layer, 0, j)),
                  pl.BlockSpec((None, 1, tn), lambda j: (layer, 0, j))],
        out_specs=pl.BlockSpec((rows, tn), lambda j: (0, j)),
        out_shape=jax.ShapeDtypeStruct((rows, n), F32),
        compiler_params=_params("parallel"),
        name="adaln_mod",
    )(cond, ada_w, ada_b)


def _norm_mod(x, g, sc, sh):
    inv = lax.rsqrt(jnp.mean(x * x, axis=-1, keepdims=True) + EPS)
    return ((x * inv) * g) * (1.0 + sc) + sh


def _class_perm(tm, classes):
    m = tm // classes
    out_row = jnp.arange(tm, dtype=jnp.int32)
    src = classes * (out_row % m) + out_row // m
    return (src[:, None] == jnp.arange(tm, dtype=jnp.int32)[None, :]).astype(BF16)


def _norm_linear_kernel(*refs, classes):
    if classes:
        p_ref, x_ref, g_ref, sc_ref, sh_ref, w_ref, b_ref, o_ref, h_ref = refs
    else:
        x_ref, g_ref, sc_ref, sh_ref, w_ref, b_ref, o_ref, h_ref = refs

    @pl.when(pl.program_id(2) == 0)
    def _():
        h = _norm_mod(x_ref[...], g_ref[...], sc_ref[...], sh_ref[...]).astype(BF16)
        if classes:
            h = _dot(p_ref[...], h).astype(BF16)
        h_ref[...] = h

    res = (_dot(h_ref[...], w_ref[...]) + b_ref[...]).astype(o_ref.dtype)
    if classes:
        m = res.shape[0] // classes
        for k in range(classes):
            o_ref[k] = res[k * m:(k + 1) * m]
    else:
        o_ref[...] = res


def _norm_linear(x, g, sc, sh, w, b, tm, tn, name, classes=0):
    bsz, l, d = x.shape
    n = w.shape[1]
    tn = _tile(n, tn)
    in_specs = [pl.BlockSpec((None, tm, d), lambda bi, i, j: (bi, i, 0)),
                pl.BlockSpec((1, d), lambda bi, i, j: (0, 0)),
                pl.BlockSpec((None, 1, d), lambda bi, i, j: (bi, 0, 0)),
                pl.BlockSpec((None, 1, d), lambda bi, i, j: (bi, 0, 0)),
                pl.BlockSpec((d, tn), lambda bi, i, j: (0, j)),
                pl.BlockSpec((1, tn), lambda bi, i, j: (0, j))]
    args = (x, g, sc, sh, w, b)
    if classes:
        in_specs = [pl.BlockSpec((tm, tm), lambda bi, i, j: (0, 0))] + in_specs
        args = (_class_perm(tm, classes),) + args
        out_specs = pl.BlockSpec((None, classes, tm // classes, tn), lambda bi, i, j: (bi, 0, i, j))
        out_shape = jax.ShapeDtypeStruct((bsz, classes, l // classes, n), BF16)
    else:
        out_specs = pl.BlockSpec((None, tm, tn), lambda bi, i, j: (bi, i, j))
        out_shape = jax.ShapeDtypeStruct((bsz, l, n), BF16)
    return pl.pallas_call(
        functools.partial(_norm_linear_kernel, classes=classes),
        grid=(bsz, l // tm, n // tn),
        in_specs=in_specs,
        out_specs=out_specs,
        out_shape=out_shape,
        scratch_shapes=[pltpu.VMEM((tm, d), BF16)],
        compiler_params=_params("parallel", "arbitrary", "arbitrary"),
        name=name,
    )(*args)


def _mm_res_kernel(*refs, n_pairs, classes):
    if classes:
        pt_ref, refs = refs[0], refs[1:]
    a_refs = refs[:n_pairs]
    w_refs = refs[n_pairs:2 * n_pairs]
    b_ref, x_ref, g_ref, o_ref = refs[2 * n_pairs:]
    acc = None
    for a_ref, w_ref in zip(a_refs, w_refs):
        if classes:
            a = jnp.concatenate([a_ref[k] for k in range(classes)], axis=0)
            a = _dot(pt_ref[...], a).astype(BF16)
        else:
            a = a_ref[...]
        prod = _dot(a, w_ref[...])
        acc = prod if acc is None else acc + prod
    o_ref[...] = x_ref[...] + g_ref[...] * (acc + b_ref[...])


def _mm_residual(a_list, w_list, b, x, g, tm, name, classes=0):
    bsz, l, n = x.shape
    n_pairs = len(a_list)
    if classes:
        in_specs = [pl.BlockSpec((tm, tm), lambda bi, i: (0, 0))]
        in_specs += [pl.BlockSpec((None, classes, tm // classes, a.shape[3]), lambda bi, i: (bi, 0, i, 0))
                     for a in a_list]
        args = (_class_perm(tm, classes).T,)
    else:
        in_specs = [pl.BlockSpec((None, tm, a.shape[2]), lambda bi, i: (bi, i, 0)) for a in a_list]
        args = ()
    in_specs += [pl.BlockSpec(w.shape, lambda bi, i: (0, 0)) for w in w_list]
    in_specs += [pl.BlockSpec((1, n), lambda bi, i: (0, 0)),
                 pl.BlockSpec((None, tm, n), lambda bi, i: (bi, i, 0)),
                 pl.BlockSpec((None, 1, n), lambda bi, i: (bi, 0, 0))]
    return pl.pallas_call(
        functools.partial(_mm_res_kernel, n_pairs=n_pairs, classes=classes),
        grid=(bsz, l // tm),
        in_specs=in_specs,
        out_specs=pl.BlockSpec((None, tm, n), lambda bi, i: (bi, i, 0)),
        out_shape=jax.ShapeDtypeStruct((bsz, l, n), F32),
        compiler_params=_params("parallel", "parallel"),
        name=name,
    )(*args, *a_list, *w_list, b, x, g)


def _rope_tables(n_tokens, dim, width):
    n_rows = n_tokens // GRID_W
    row = jnp.repeat(jnp.arange(n_rows, dtype=F32), GRID_W)
    col = jnp.tile(jnp.arange(GRID_W, dtype=F32), n_rows)
    n_freq = dim // 4
    inv = ROPE_BASE ** (-jnp.arange(n_freq, dtype=F32) / n_freq)
    ang = jnp.concatenate([row[:, None] * inv, col[:, None] * inv], axis=-1)
    ang = jnp.repeat(ang, 2, axis=-1)
    ang = jnp.tile(ang, (1, width // dim))
    sign = jnp.where(jnp.arange(width) % 2 == 0, -1.0, 1.0).astype(F32)
    return jnp.cos(ang), jnp.sin(ang) * sign


def _rope(x, cos_t, sin_t):
    width = x.shape[-1]
    lane = lax.broadcasted_iota(jnp.int32, (1, width), 1)
    partner = jnp.where(lane % 2 == 0, pltpu.roll(x, width - 1, 1), pltpu.roll(x, 1, 1))
    return x * cos_t + partner * sin_t


def _sub_rms(x, g):
    lane = lax.broadcasted_iota(jnp.int32, (1, 2 * DIFF_SUB), 1)
    lo = lane < DIFF_SUB
    sq = x * x
    ss_lo = jnp.sum(jnp.where(lo, sq, 0.0), axis=-1, keepdims=True)
    ss_hi = jnp.sum(jnp.where(lo, 0.0, sq), axis=-1, keepdims=True)
    inv = jnp.where(lo, lax.rsqrt(ss_lo / DIFF_SUB + EPS), lax.rsqrt(ss_hi / DIFF_SUB + EPS))
    return (x * inv) * g


def _rms(x, g):
    return (x * lax.rsqrt(jnp.mean(x * x, axis=-1, keepdims=True) + EPS)) * g


def _diff_attn_kernel(q_ref, k_ref, v_ref, kc_ref, vc_ref, cos_ref, sin_ref,
                      gq_ref, gk_ref, gs_ref, lamp_ref, o_ref, q_scr, k_scr, v_scr, s_scr, lam_scr,
                      *, lam_init, n_ctx, tq):
    n_keys = k_scr.shape[0]

    @pl.when(pl.program_id(2) == 0)
    def _():
        lane = lax.broadcasted_iota(jnp.int32, (1, 2 * DIFF_SUB), 1)
        lo = lane < DIFF_SUB
        gk = gk_ref[...]
        cos_t = cos_ref[...]
        sin_t = sin_ref[...]
        k_scr[0:n_ctx, :] = _sub_rms(kc_ref[...].astype(F32), gk).astype(BF16)
        k_scr[n_ctx:, :] = _rope(_sub_rms(k_ref[...].astype(F32), gk), cos_t, sin_t).astype(BF16)
        v_scr[0:n_ctx, 0:HEAD_DIM] = vc_ref[...].astype(BF16)
        v_scr[n_ctx:, 0:HEAD_DIM] = v_ref[...].astype(BF16)
        v_scr[:, HEAD_DIM:] = jnp.broadcast_to(jnp.where(lane == 0, 1.0, 0.0).astype(BF16), (n_keys, HEAD_DIM))
        q = _rope(_sub_rms(q_ref[...].astype(F32), gq_ref[...]), cos_t, sin_t) * (DIFF_SUB ** -0.5 * LOG2_E)
        q_scr[0] = jnp.where(lo, q, 0.0).astype(BF16)
        q_scr[1] = jnp.where(lo, 0.0, q).astype(BF16)
        lp = lamp_ref[...]
        t1 = jnp.sum(lp[0:1] * lp[1:2], axis=-1, keepdims=True)
        t2 = jnp.sum(lp[2:3] * lp[3:4], axis=-1, keepdims=True)
        lam_scr[...] = jnp.broadcast_to(jnp.exp(t1) - jnp.exp(t2) + lam_init, lam_scr.shape)

    chunks = [(c0, min(c0 + DIFF_KEY_CHUNK, n_keys)) for c0 in range(0, n_keys, DIFF_KEY_CHUNK)]
    lam = lam_scr[:, 0:1]
    for t in range(tq // Q_BLOCK):
        rows = pl.ds(pl.multiple_of(pl.program_id(2) * tq + t * Q_BLOCK, Q_BLOCK), Q_BLOCK)
        qs = [q_scr[0, rows, :], q_scr[1, rows, :]]

        mx = [jnp.full((Q_BLOCK, LANES), NEG_INF, F32) for _ in range(2)]
        for c0, c1 in chunks:
            kc = k_scr[c0:c1, :]
            for h in range(2):
                s = _dot_nt(qs[h], kc)
                s_scr[2 * t + h, :, c0:c1] = s
                for j in range((c1 - c0) // LANES):
                    mx[h] = jnp.maximum(mx[h], s[:, j * LANES:(j + 1) * LANES])
        mrow = [jnp.max(m, axis=-1, keepdims=True) for m in mx]

        out = [jnp.zeros((Q_BLOCK, 2 * HEAD_DIM), F32) for _ in range(2)]
        for c0, c1 in chunks:
            vc = v_scr[c0:c1, :]
            for h in range(2):
                e = jnp.exp2(s_scr[2 * t + h, :, c0:c1] - mrow[h])
                out[h] = out[h] + _dot(e.astype(BF16), vc)
        r1 = 1.0 / out[0][:, HEAD_DIM:HEAD_DIM + 1]
        r2 = lam / out[1][:, HEAD_DIM:HEAD_DIM + 1]
        o = out[0][:, 0:HEAD_DIM] * r1 - out[1][:, 0:HEAD_DIM] * r2
        o_ref[t * Q_BLOCK:(t + 1) * Q_BLOCK, :] = (_rms(o, gs_ref[...]) * (1.0 - lam_init)).astype(o_ref.dtype)


def _diff_attention(qkv, qkv_c, tabs, gq, gk, gs, lamp, lam_init, tq):
    bsz, l, _ = qkv.shape
    n_ctx = qkv_c.shape[1]
    cos_t, sin_t = tabs
    hb = 2 * DIFF_SUB
    kb = DIFF_Q // hb
    vb = 2 * DIFF_Q // hb
    kern = functools.partial(_diff_attn_kernel, lam_init=lam_init, n_ctx=n_ctx, tq=tq)
    vec = pl.BlockSpec((1, hb), lambda bi, h, i: (0, 0))
    return pl.pallas_call(
        kern,
        grid=(bsz, DIFF_HEADS, l // tq),
        in_specs=[pl.BlockSpec((None, l, hb), lambda bi, h, i: (bi, 0, h)),
                  pl.BlockSpec((None, l, hb), lambda bi, h, i: (bi, 0, kb + h)),
                  pl.BlockSpec((None, l, hb), lambda bi, h, i: (bi, 0, vb + h)),
                  pl.BlockSpec((None, n_ctx, hb), lambda bi, h, i: (bi, 0, kb + h)),
                  pl.BlockSpec((None, n_ctx, hb), lambda bi, h, i: (bi, 0, vb + h)),
                  pl.BlockSpec((l, hb), lambda bi, h, i: (0, 0)),
                  pl.BlockSpec((l, hb), lambda bi, h, i: (0, 0)),
                  vec, vec, vec,
                  pl.BlockSpec((8, hb), lambda bi, h, i: (0, 0))],
        out_specs=pl.BlockSpec((None, tq, hb), lambda bi, h, i: (bi, i, h)),
        out_shape=jax.ShapeDtypeStruct((bsz, l, DIFF_V), BF16),
        scratch_shapes=[pltpu.VMEM((2, l, hb), BF16), pltpu.VMEM((n_ctx + l, hb), BF16),
                        pltpu.VMEM((n_ctx + l, 2 * HEAD_DIM), BF16),
                        pltpu.VMEM((2 * (tq // Q_BLOCK), Q_BLOCK, n_ctx + l), F32),
                        pltpu.VMEM((1, hb), F32)],
        compiler_params=_params("parallel", "parallel", "arbitrary"),
        name="diff_attention",
    )(qkv, qkv, qkv, qkv_c, qkv_c, cos_t, sin_t, gq, gk, gs, lamp)


def _swa_kernel(q_ref, k_ref, v_ref, kc_ref, vc_ref, cos_ref, sin_ref, gq_ref, gk_ref, sink_ref,
                o_ref, q_scr, k_scr, v_scr, kc_scr, vc_scr, *, seq):
    kvh = pl.program_id(1)
    n = pl.program_id(2)
    span = Q_BLOCK + 2 * WINDOW

    @pl.when(n == 0)
    def _():
        gk = gk_ref[...]
        gq = gq_ref[...]
        cos_t = cos_ref[...]
        sin_t = sin_ref[...]
        zeros = jnp.zeros((WINDOW, HEAD_DIM), BF16)
        k_scr[0:WINDOW, :] = zeros
        v_scr[0:WINDOW, :] = zeros
        k_scr[WINDOW + seq:, :] = zeros
        v_scr[WINDOW + seq:, :] = zeros
        k_scr[WINDOW:WINDOW + seq, :] = _rope(_rms(k_ref[...].astype(F32), gk), cos_t, sin_t).astype(BF16)
        v_scr[WINDOW:WINDOW + seq, :] = v_ref[...].astype(BF16)
        kc_scr[...] = _rms(kc_ref[...].astype(F32), gk).astype(BF16)
        vc_scr[...] = vc_ref[...].astype(BF16)
        for j in range(SWA_GROUP):
            qj = q_ref[:, j * HEAD_DIM:(j + 1) * HEAD_DIM].astype(F32)
            qj = _rope(_rms(qj, gq), cos_t, sin_t) * (HEAD_DIM ** -0.5)
            q_scr[j] = qj.astype(BF16)

    start = pl.multiple_of(n * Q_BLOCK, Q_BLOCK)
    kw = k_scr[pl.ds(start, span), :]
    vw = v_scr[pl.ds(start, span), :]
    kc = kc_scr[...]
    vc = vc_scr[...]
    rows = SWA_GROUP * Q_BLOCK
    ridx = lax.broadcasted_iota(jnp.int32, (rows, span), 0)
    kidx = lax.broadcasted_iota(jnp.int32, (rows, span), 1)
    rel = kidx - (ridx & (Q_BLOCK - 1))
    key_pos = n * Q_BLOCK + kidx - WINDOW
    mask = (rel >= 0) & (rel <= 2 * WINDOW) & (key_pos >= 0) & (key_pos < seq)
    head = lax.broadcasted_iota(jnp.int32, (rows, 1), 0) // Q_BLOCK
    sink = jnp.zeros((rows, 1), F32)
    for j in range(SWA_GROUP):
        sink = jnp.where(head == j, sink_ref[pl.ds(kvh * SWA_GROUP + j, 1), 0:1], sink)
    q = jnp.concatenate([q_scr[j, pl.ds(start, Q_BLOCK), :] for j in range(SWA_GROUP)], axis=0)
    s_loc = jnp.where(mask, _dot_nt(q, kw), NEG_INF)
    s_ctx = _dot_nt(q, kc)
    m = jnp.maximum(jnp.maximum(jnp.max(s_loc, axis=-1, keepdims=True),
                                jnp.max(s_ctx, axis=-1, keepdims=True)), sink)
    e_loc = jnp.exp(s_loc - m)
    e_ctx = jnp.exp(s_ctx - m)
    denom = (jnp.sum(e_loc, axis=-1, keepdims=True) + jnp.sum(e_ctx, axis=-1, keepdims=True)
             + jnp.exp(sink - m))
    r = 1.0 / denom
    o = _dot((e_ctx * r).astype(BF16), vc) + _dot((e_loc * r).astype(BF16), vw)
    o_ref[...] = jnp.concatenate([o[j * Q_BLOCK:(j + 1) * Q_BLOCK] for j in range(SWA_GROUP)],
                                 axis=-1).astype(o_ref.dtype)


def _swa_attention(qkv, qkv_c, tabs, gq, gk, sink_rows):
    bsz, l, _ = qkv.shape
    n_ctx = qkv_c.shape[1]
    cos_t, sin_t = tabs
    gw = SWA_GROUP * HEAD_DIM
    q0 = (2 * DIFF_Q + DIFF_V) // gw
    k0 = (2 * DIFF_Q + DIFF_V + SWA_Q) // HEAD_DIM
    v0 = k0 + SWA_KV_HEADS
    vec = pl.BlockSpec((1, HEAD_DIM), lambda bi, h, i: (0, 0))
    return pl.pallas_call(
        functools.partial(_swa_kernel, seq=l),
        grid=(bsz, SWA_KV_HEADS, l // Q_BLOCK),
        in_specs=[pl.BlockSpec((None, l, gw), lambda bi, h, i: (bi, 0, q0 + h)),
                  pl.BlockSpec((None, l, HEAD_DIM), lambda bi, h, i: (bi, 0, k0 + h)),
                  pl.BlockSpec((None, l, HEAD_DIM), lambda bi, h, i: (bi, 0, v0 + h)),
                  pl.BlockSpec((None, n_ctx, HEAD_DIM), lambda bi, h, i: (bi, 0, k0 + h)),
                  pl.BlockSpec((None, n_ctx, HEAD_DIM), lambda bi, h, i: (bi, 0, v0 + h)),
                  pl.BlockSpec((l, HEAD_DIM), lambda bi, h, i: (0, 0)),
                  pl.BlockSpec((l, HEAD_DIM), lambda bi, h, i: (0, 0)),
                  vec, vec,
                  pl.BlockSpec((SWA_HEADS, HEAD_DIM), lambda bi, h, i: (0, 0))],
        out_specs=pl.BlockSpec((None, Q_BLOCK, gw), lambda bi, h, i: (bi, i, h)),
        out_shape=jax.ShapeDtypeStruct((bsz, l, SWA_Q), BF16),
        scratch_shapes=[pltpu.VMEM((SWA_GROUP, l, HEAD_DIM), BF16),
                        pltpu.VMEM((l + 2 * WINDOW, HEAD_DIM), BF16), pltpu.VMEM((l + 2 * WINDOW, HEAD_DIM), BF16),
                        pltpu.VMEM((n_ctx, HEAD_DIM), BF16), pltpu.VMEM((n_ctx, HEAD_DIM), BF16)],
        compiler_params=_params("parallel", "parallel", "arbitrary"),
        name="swa_attention",
    )(qkv, qkv, qkv, qkv_c, qkv_c, cos_t, sin_t, gq, gk, sink_rows)


def _router_kernel(x_ref, g_ref, sc_ref, sh_ref, w_ref, b_ref, tri_ref, h_ref, e_ref, p_ref, cnt_ref, run_scr):
    @pl.when((pl.program_id(0) == 0) & (pl.program_id(1) == 0))
    def _():
        run_scr[...] = jnp.zeros(run_scr.shape, F32)

    h = _norm_mod(x_ref[...], g_ref[...], sc_ref[...], sh_ref[...])
    h_ref[...] = h
    logits = jnp.dot(h, w_ref[...], precision=HIGHEST, preferred_element_type=F32) + b_ref[...]
    lane_i = lax.broadcasted_iota(jnp.int32, (1, LANES), 1)
    lane = lane_i.astype(F32)
    big = float(LANES)
    lg = jnp.where(lane_i < MOE_GROUPS, logits, NEG_INF)
    mg = jnp.max(lg, axis=-1, keepdims=True)
    p_top = 1.0 / jnp.sum(jnp.exp(lg - mg), axis=-1, keepdims=True)
    grp = jnp.min(jnp.where(lg == mg, lane, big), axis=-1, keepdims=True)
    e_lane = lane - MOE_GROUPS
    in_grp = (e_lane >= grp * MOE_PER_GROUP) & (e_lane < (grp + 1.0) * MOE_PER_GROUP)
    le = jnp.where(in_grp, logits, NEG_INF)
    t1 = jnp.max(le, axis=-1, keepdims=True)
    i1 = jnp.min(jnp.where(le == t1, lane, big), axis=-1, keepdims=True)
    le2 = jnp.where(lane == i1, NEG_INF, le)
    t2 = jnp.max(le2, axis=-1, keepdims=True)
    i2 = jnp.min(jnp.where(le2 == t2, lane, big), axis=-1, keepdims=True)
    w2 = jnp.exp(t2 - t1)
    inv = p_top / (1.0 + w2)
    hot1 = lane == i1
    hot2 = lane == i2
    onehot = jnp.where(hot1 | hot2, 1.0, 0.0)
    before = run_scr[...] + _dot(tri_ref[...], onehot.astype(BF16))
    r1 = jnp.sum(jnp.where(hot1, before, 0.0), axis=-1, keepdims=True)
    r2 = jnp.sum(jnp.where(hot2, before, 0.0), axis=-1, keepdims=True)
    run_scr[...] = run_scr[...] + jnp.sum(onehot, axis=0, keepdims=True)
    cnt_ref[...] = run_scr[...]
    e_ref[...] = jnp.where(lane_i == 0, i1 - MOE_GROUPS,
                           jnp.where(lane_i == 1, i2 - MOE_GROUPS,
                                     jnp.where(lane_i == 2, r1, jnp.where(lane_i == 3, r2, 0.0)))).astype(jnp.int32)
    p_ref[...] = jnp.where(lane_i == 0, inv, jnp.where(lane_i == 1, inv * w2, 0.0))


def _router(x, g, sc, sh, w, b, tm):
    bsz, l, d = x.shape
    row = lambda bi, i: (bi, i, 0)
    tri = (jnp.arange(tm)[:, None] > jnp.arange(tm)[None, :]).astype(BF16)
    return pl.pallas_call(
        _router_kernel,
        grid=(bsz, l // tm),
        in_specs=[pl.BlockSpec((None, tm, d), row),
                  pl.BlockSpec((1, d), lambda bi, i: (0, 0)),
                  pl.BlockSpec((None, 1, d), lambda bi, i: (bi, 0, 0)),
                  pl.BlockSpec((None, 1, d), lambda bi, i: (bi, 0, 0)),
                  pl.BlockSpec((d, LANES), lambda bi, i: (0, 0)),
                  pl.BlockSpec((1, LANES), lambda bi, i: (0, 0)),
                  pl.BlockSpec((tm, tm), lambda bi, i: (0, 0))],
        out_specs=[pl.BlockSpec((None, tm, d), row),
                   pl.BlockSpec((None, tm, LANES), row),
                   pl.BlockSpec((None, tm, LANES), row),
                   pl.BlockSpec((1, LANES), lambda bi, i: (0, 0))],
        out_shape=[jax.ShapeDtypeStruct((bsz, l, d), F32),
                   jax.ShapeDtypeStruct((bsz, l, LANES), jnp.int32),
                   jax.ShapeDtypeStruct((bsz, l, LANES), F32),
                   jax.ShapeDtypeStruct((1, LANES), F32)],
        scratch_shapes=[pltpu.VMEM((1, LANES), F32)],
        compiler_params=_params("arbitrary", "arbitrary"),
        name="moe_router",
    )(x, g, sc, sh, w, b, tri)


def _dispatch_kernel(slot_ref, h_ref, init_ref, xb_ref, sem):
    del init_ref
    tm = h_ref.shape[0]
    base = pl.program_id(0) * tm

    def copy(r, k):
        dst = slot_ref[(base + r) * MOE_TOPK + k]
        return pltpu.make_async_copy(h_ref.at[pl.ds(r, 1)], xb_ref.at[pl.ds(dst, 1)], sem)

    def start(r, carry):
        for k in range(MOE_TOPK):
            copy(r, k).start(priority=k)
        return carry

    def wait(r, carry):
        for k in range(MOE_TOPK):
            copy(r, k).wait()
        return carry

    lax.fori_loop(0, tm, start, 0, unroll=8)
    lax.fori_loop(0, tm, wait, 0, unroll=8)


def _dispatch(slot, h, n_rows, tm):
    n, d = h.shape
    grid_spec = pltpu.PrefetchScalarGridSpec(
        num_scalar_prefetch=1,
        grid=(n // tm,),
        in_specs=[pl.BlockSpec((tm, d), lambda i, s: (i, 0)),
                  pl.BlockSpec(memory_space=pl.ANY)],
        out_specs=pl.BlockSpec(memory_space=pl.ANY),
        scratch_shapes=[pltpu.SemaphoreType.DMA(())],
    )
    return pl.pallas_call(
        _dispatch_kernel,
        grid_spec=grid_spec,
        out_shape=jax.ShapeDtypeStruct((n_rows, d), F32),
        input_output_aliases={2: 0},
        compiler_params=_params("arbitrary"),
        name="moe_dispatch",
    )(slot, h, jnp.zeros((n_rows, d), F32))


def _combine_kernel(slot_ref, x_ref, g_ref, p_ref, yb_ref, o_ref, y_scr, sem):
    tm = x_ref.shape[0]
    base = pl.program_id(0) * tm

    def copy(r, k):
        src = slot_ref[(base + r) * MOE_TOPK + k]
        return pltpu.make_async_copy(yb_ref.at[pl.ds(src, 1)], y_scr.at[k, pl.ds(r, 1)], sem)

    def start(r, carry):
        for k in range(MOE_TOPK):
            copy(r, k).start(priority=k)
        return carry

    def wait(r, carry):
        for k in range(MOE_TOPK):
            copy(r, k).wait()
        return carry

    lax.fori_loop(0, tm, start, 0, unroll=8)
    lax.fori_loop(0, tm, wait, 0, unroll=8)
    p = p_ref[...]
    moe = y_scr[0] * p[:, 0:1]
    for k in range(1, MOE_TOPK):
        moe = moe + y_scr[k] * p[:, k:k + 1]
    o_ref[...] = x_ref[...] + g_ref[...] * moe


def _combine(slot, x, g, gates, yb, tm):
    bsz, l, d = x.shape
    nt = l // tm
    grid_spec = pltpu.PrefetchScalarGridSpec(
        num_scalar_prefetch=1,
        grid=(bsz * nt,),
        in_specs=[pl.BlockSpec((None, tm, d), lambda i, s: (i // nt, i % nt, 0)),
                  pl.BlockSpec((None, 1, d), lambda i, s: (i // nt, 0, 0)),
                  pl.BlockSpec((None, tm, LANES), lambda i, s: (i // nt, i % nt, 0)),
                  pl.BlockSpec(memory_space=pl.ANY)],
        out_specs=pl.BlockSpec((None, tm, d), lambda i, s: (i // nt, i % nt, 0)),
        scratch_shapes=[pltpu.VMEM((MOE_TOPK, tm, d), F32), pltpu.SemaphoreType.DMA(())],
    )
    return pl.pallas_call(
        _combine_kernel,
        grid_spec=grid_spec,
        out_shape=jax.ShapeDtypeStruct((bsz, l, d), F32),
        compiler_params=_params("arbitrary"),
        name="moe_combine",
    )(slot, x, g, gates, yb)


def _expert_kernel(be_ref, nu_ref, x_ref, wg_ref, wu_ref, wd_ref, o_ref):
    i = pl.program_id(0)

    @pl.when(i < nu_ref[0])
    def _():
        x = x_ref[...].astype(BF16)
        ff = wg_ref.shape[1]
        acc = jnp.zeros(o_ref.shape, F32)
        for c in range(ff // FF_CHUNK):
            cols = slice(c * FF_CHUNK, (c + 1) * FF_CHUNK)
            gate = _dot(x, wg_ref[:, cols])
            up = _dot(x, wu_ref[:, cols])
            hidden = (gate * jax.nn.sigmoid(gate) * up).astype(BF16)
            acc = acc + _dot(hidden, wd_ref[cols, :])
        o_ref[...] = acc

    @pl.when(i >= nu_ref[0])
    def _():
        o_ref[...] = jnp.zeros(o_ref.shape, F32)


def _expert_ffn(xb, block_e, n_used, w_gate, w_up, w_down, layer):
    n_blocks, _, d = xb.shape
    ff = w_gate.shape[2]
    base = layer * MOE_EXPERTS
    grid_spec = pltpu.PrefetchScalarGridSpec(
        num_scalar_prefetch=2,
        grid=(n_blocks,),
        in_specs=[pl.BlockSpec((None, MOE_BLOCK, d), lambda i, be, nu: (i, 0, 0)),
                  pl.BlockSpec((None, d, ff), lambda i, be, nu: (base + be[i], 0, 0)),
                  pl.BlockSpec((None, d, ff), lambda i, be, nu: (base + be[i], 0, 0)),
                  pl.BlockSpec((None, ff, d), lambda i, be, nu: (base + be[i], 0, 0))],
        out_specs=pl.BlockSpec((None, MOE_BLOCK, d), lambda i, be, nu: (i, 0, 0)),
    )
    return pl.pallas_call(
        _expert_kernel,
        grid_spec=grid_spec,
        out_shape=jax.ShapeDtypeStruct((n_blocks, MOE_BLOCK, d), F32),
        compiler_params=_params("arbitrary"),
        name="expert_ffn",
    )(block_e, n_used, xb, w_gate, w_up, w_down)


def _hier_moe(x, g, sc, sh, gate2, w_route, b_route, w_gate, w_up, w_down, layer):
    bsz, l, d = x.shape
    n = bsz * l
    tm = min(512, l)
    h, route, gates, cnt = _router(x, g, sc, sh, w_route, b_route, tm)
    route = route.reshape(n, LANES)
    expert = route[:, :MOE_TOPK]
    rank = route[:, MOE_TOPK:2 * MOE_TOPK]
    counts = cnt[0, MOE_GROUPS:MOE_GROUPS + MOE_EXPERTS].astype(jnp.int32)
    padded = (counts + MOE_BLOCK - 1) // MOE_BLOCK * MOE_BLOCK
    pad_end = jnp.cumsum(padded)
    pad_start = pad_end - padded
    hit = expert[:, :, None] == jnp.arange(MOE_EXPERTS, dtype=jnp.int32)
    slot = (jnp.sum(jnp.where(hit, pad_start, 0), axis=-1) + rank).reshape(-1)
    n_blocks = -(-(n * MOE_TOPK) // MOE_BLOCK) + MOE_EXPERTS
    block_row = jnp.arange(n_blocks, dtype=jnp.int32)[:, None] * MOE_BLOCK
    block_e = jnp.minimum(jnp.sum((pad_end[None, :] <= block_row).astype(jnp.int32), axis=-1), MOE_EXPERTS - 1)
    n_used = (pad_end[-1:] // MOE_BLOCK).astype(jnp.int32)
    xb = _dispatch(slot, h.reshape(n, d), n_blocks * MOE_BLOCK, tm).reshape(n_blocks, MOE_BLOCK, d)
    yb = _expert_ffn(xb, block_e, n_used, w_gate, w_up, w_down, layer).reshape(-1, d)
    return _combine(slot, x, gate2, gates, yb, tm)


def _short_conv_kernel(zv_ref, z1_ref, z2_ref, wv_ref, w1_ref, w2_ref, bv_ref, b1_ref, b2_ref,
                       v_ref, x1_ref, x2_ref):
    rdx, q = zv_ref.shape[0], zv_ref.shape[1]
    row = lax.broadcasted_iota(jnp.int32, (q, 1), 0)

    for z_ref, w_ref, b_ref, o_ref in ((zv_ref, wv_ref, bv_ref, v_ref), (z1_ref, w1_ref, b1_ref, x1_ref),
                                       (z2_ref, w2_ref, b2_ref, x2_ref)):
        z = [z_ref[k].astype(F32) for k in range(rdx)]
        w = w_ref[...]
        b = b_ref[...]
        for k in range(rdx):
            prev = z[k - 1] if k > 0 else jnp.where(row == 0, 0.0, pltpu.roll(z[rdx - 1], 1, 0))
            nxt = z[k + 1] if k < rdx - 1 else jnp.where(row == q - 1, 0.0, pltpu.roll(z[0], q - 1, 0))
            o_ref[k] = (b + prev * w[0:1] + z[k] * w[1:2] + nxt * w[2:3]).astype(o_ref.dtype)


def _short_conv(z, conv_w, conv_b, tc):
    bsz, rdx, q, w3 = z.shape
    w = w3 // 3
    nc = w // tc
    zs = lambda k: pl.BlockSpec((None, rdx, q, tc), lambda bi, j: (bi, 0, 0, k * nc + j))
    ws = lambda k: pl.BlockSpec((8, tc), lambda bi, j: (0, k * nc + j))
    bs = lambda k: pl.BlockSpec((1, tc), lambda bi, j: (0, k * nc + j))
    out = pl.BlockSpec((None, rdx, q, tc), lambda bi, j: (bi, 0, 0, j))
    return pl.pallas_call(
        _short_conv_kernel,
        grid=(bsz, nc),
        in_specs=[zs(0), zs(1), zs(2), ws(0), ws(1), ws(2), bs(0), bs(1), bs(2)],
        out_specs=[out, out, out],
        out_shape=[jax.ShapeDtypeStruct((bsz, rdx, q, w), BF16)] * 3,
        compiler_params=_params("parallel", "parallel"),
        name="short_conv",
    )(z, z, z, conv_w, conv_w, conv_w, conv_b, conv_b, conv_b)


def _filter_kernel(z_ref, w1_ref, b1_ref, f1_ref, w2_ref, b2_ref, f2_ref, w3_ref, dl_ref, bw_ref, o_ref, a_scr):
    n = z_ref.shape[0]

    @pl.when(pl.program_id(0) == 0)
    def _():
        a = jnp.sin(f1_ref[...] * (jnp.dot(z_ref[...], w1_ref[...], precision=HIGHEST,
                                           preferred_element_type=F32) + b1_ref[...]))
        a_scr[...] = jnp.sin(f2_ref[...] * (jnp.dot(a, w2_ref[...], precision=HIGHEST,
                                                    preferred_element_type=F32) + b2_ref[...]))

    hf = jnp.dot(a_scr[...], w3_ref[...], precision=HIGHEST, preferred_element_type=F32)
    q = n // DFT_RADIX
    row = lax.broadcasted_iota(jnp.int32, (n, 1), 0)
    assert q & (q - 1) == 0
    tap = DFT_RADIX * (row & (q - 1)) + lax.shift_right_logical(row, q.bit_length() - 1)
    t = tap.astype(F32) * (1.0 / (n - 1))
    hf = hf * jnp.exp(-t * dl_ref[...])
    hf = jnp.where((tap == 0) & (bw_ref[...] > 0.5), 0.0, hf)
    for k in range(DFT_RADIX):
        o_ref[k] = hf[k * q:(k + 1) * q]


def _hyena_filters(z, w1, b1, f1, w2, b2, f2, w3, deltas, is_bwd, tn):
    n = z.shape[0]
    cols = w3.shape[1]
    full = lambda a: pl.BlockSpec(a.shape, lambda j: (0, 0))
    tile = lambda r: pl.BlockSpec((r, tn), lambda j: (0, j))
    return pl.pallas_call(
        _filter_kernel,
        grid=(cols // tn,),
        in_specs=[full(z), full(w1), full(b1), full(f1), full(w2), full(b2), full(f2),
                  tile(w3.shape[0]), tile(1), tile(1)],
        out_specs=pl.BlockSpec((DFT_RADIX, n // DFT_RADIX, tn), lambda j: (0, 0, j)),
        out_shape=jax.ShapeDtypeStruct((DFT_RADIX, n // DFT_RADIX, cols), F32),
        scratch_shapes=[pltpu.VMEM((n, LANES), F32)],
        compiler_params=_params("arbitrary"),
        name="hyena_filters",
    )(z, w1, b1, f1, w2, b2, f2, w3, deltas, is_bwd)


def _dft_groups():
    groups = [(0, 1), (DFT_RADIX // 2, -1)]
    for m in range(1, DFT_RADIX // 2):
        groups += [(m, -1), (m, 1)]
    return groups


def _dft_coef(g, k):
    m, b = _dft_groups()[g]
    assert DFT_RADIX in (2, 4)
    quarter = (4 * m * k // DFT_RADIX) % 4
    return (1, 0, -1, 0)[quarter], (0, 1, 0, -1)[quarter], b


def _signed_sum(terms):
    acc = None
    for coef, thunk in terms:
        if coef == 0:
            continue
        val = thunk()
        if acc is None:
            acc = val if coef > 0 else -val
        else:
            acc = acc + val if coef > 0 else acc - val
    return acc


def _spectrum(cprod, sprod, want_c=True, want_s=True):
    cprod = functools.lru_cache(maxsize=None)(cprod)
    sprod = functools.lru_cache(maxsize=None)(sprod)
    out = []
    for g in range(DFT_RADIX):
        cu, su = [], []
        for k in range(DFT_RADIX):
            ca, sa, b = _dft_coef(g, k)
            cu += [(ca, functools.partial(cprod, k)), (-b * sa, functools.partial(sprod, k))]
            su += [(sa, functools.partial(cprod, k)), (b * ca, functools.partial(sprod, k))]
        out.append((_signed_sum(cu) if want_c else None, _signed_sum(su) if want_s else None))
    return out


def _dft_tables(n):
    q = n // DFT_RADIX
    r = 1 << (q.bit_length() // 2)
    f = jnp.arange(q, dtype=jnp.int32)[:, None]
    idx = jnp.arange(r, dtype=jnp.int32)[None, :]
    ang = lambda s: (((2 * f + 1) * s) % (4 * n)).astype(F32) * (math.pi / (2 * n))
    hi = ang(DFT_RADIX * r * idx)[:, :q // r]
    ch, sh = jnp.cos(hi), jnp.sin(hi)
    fwd, inv = [], []
    for trig in ("cos", "sin"):
        for k in range(DFT_RADIX):
            lo = ang(DFT_RADIX * idx + k)
            cl, sl = jnp.cos(lo), jnp.sin(lo)
            a, b, sign = (ch, sh, -1.0) if trig == "cos" else (sh, ch, 1.0)
            fwd.append((a[:, :, None] * cl[:, None, :] + sign * b[:, :, None] * sl[:, None, :]).reshape(q, q))
            inv.append((a.T[:, None, :] * cl.T[None, :, :] + sign * b.T[:, None, :] * sl.T[None, :, :]).reshape(q, q))
    shape = (2, DFT_RADIX, q, q)
    return jnp.stack(fwd).astype(BF16).reshape(shape), jnp.stack(inv).astype(BF16).reshape(shape)


def _dft_filter_kernel(m_ref, hf_ref, hb_ref, kr_ref, ki_ref, sum_scr, dif_scr, *, scale):
    @pl.when(pl.program_id(2) == 0)
    def _():
        hf = hf_ref[...]
        hb = hb_ref[...]
        sum_scr[...] = (hf + hb).astype(BF16)
        dif_scr[...] = (hb - hf).astype(BF16)

    of_sum = _spectrum(lambda k: _dot(m_ref[0, k], sum_scr[k]), lambda k: _dot(m_ref[1, k], sum_scr[k]),
                       want_s=False)
    of_dif = _spectrum(lambda k: _dot(m_ref[0, k], dif_scr[k]), lambda k: _dot(m_ref[1, k], dif_scr[k]),
                       want_c=False)
    for g in range(DFT_RADIX):
        kr_ref[g] = of_sum[g][0] * scale
        ki_ref[g] = of_dif[g][1] * scale


def _dft_filter(fwd, filt, width, tf, tn):
    rdx, q = fwd.shape[1], fwd.shape[2]
    nc = width // tn
    out = pl.BlockSpec((None, rdx, tf, tn), lambda o, j, i: (o, 0, i, j))
    return pl.pallas_call(
        functools.partial(_dft_filter_kernel, scale=1.0 / (rdx * q)),
        grid=(HYENA_ORDER, nc, q // tf),
        in_specs=[pl.BlockSpec((2, rdx, tf, q), lambda o, j, i: (0, 0, i, 0)),
                  pl.BlockSpec((rdx, q, tn), lambda o, j, i: (0, 0, (2 * o) * nc + j)),
                  pl.BlockSpec((rdx, q, tn), lambda o, j, i: (0, 0, (2 * o + 1) * nc + j))],
        out_specs=[out, out],
        out_shape=[jax.ShapeDtypeStruct((HYENA_ORDER, rdx, q, width), F32)] * 2,
        scratch_shapes=[pltpu.VMEM((rdx, q, tn), BF16), pltpu.VMEM((rdx, q, tn), BF16)],
        compiler_params=_params("parallel", "parallel", "arbitrary"),
        name="dft_filter",
    )(fwd, filt, filt)


def _dft_fwd_kernel(m_ref, u_ref, kr_ref, ki_ref, pr_ref, pi_ref):
    spec = _spectrum(lambda k: _dot(m_ref[0, k], u_ref[k]), lambda k: _dot(m_ref[1, k], u_ref[k]))
    for g, (ur, us) in enumerate(spec):
        kr = kr_ref[g]
        ki = ki_ref[g]
        pr_ref[g] = (ur * kr + us * ki).astype(pr_ref.dtype)
        pi_ref[g] = (ur * ki - us * kr).astype(pi_ref.dtype)


def _dft_fwd(fwd, u, kr, ki, order, tf, tn):
    bsz, rdx, q, width = u.shape
    out = pl.BlockSpec((None, rdx, tf, tn), lambda bi, j, i: (bi, 0, i, j))
    ks = pl.BlockSpec((None, rdx, tf, tn), lambda bi, j, i: (order, 0, i, j))
    return pl.pallas_call(
        _dft_fwd_kernel,
        grid=(bsz, width // tn, q // tf),
        in_specs=[pl.BlockSpec((2, rdx, tf, q), lambda bi, j, i: (0, 0, i, 0)),
                  pl.BlockSpec((None, rdx, q, tn), lambda bi, j, i: (bi, 0, 0, j)),
                  ks, ks],
        out_specs=[out, out],
        out_shape=[jax.ShapeDtypeStruct((bsz, rdx, q, width), BF16)] * 2,
        compiler_params=_params("parallel", "parallel", "parallel"),
        name="dft_fwd",
    )(fwd, u, kr, ki)


def _dft_inv_kernel(m_ref, pr_ref, pi_ref, u_ref, b_ref, x_ref, o_ref, ra_scr, rb_scr):
    for k in range(DFT_RADIX):
        @pl.when((pl.program_id(3) == 0) & (pl.program_id(2) == k))
        def _(k=k):
            ra, rb = [], []
            for g in range(DFT_RADIX):
                ca, sa, b = _dft_coef(g, k)
                pr = functools.partial(lambda g: pr_ref[g].astype(F32), g)
                pi = functools.partial(lambda g: pi_ref[g].astype(F32), g)
                ra += [(ca, pr), (-sa, pi)]
                rb += [(b * sa, pr), (b * ca, pi)]
            ra_scr[...] = _signed_sum(ra).astype(BF16)
            rb_scr[...] = _signed_sum(rb).astype(BF16)

    y = _dot(m_ref[0], ra_scr[...]) - _dot(m_ref[1], rb_scr[...])
    u = u_ref[...].astype(F32)
    o_ref[...] = (x_ref[...] * (y + u * b_ref[...])).astype(o_ref.dtype)


def _dft_inv(inv, pr, pi, u, bias, x, order, tt, tn):
    bsz, rdx, q, width = u.shape
    tile = pl.BlockSpec((None, None, tt, tn), lambda bi, j, k, i: (bi, k, i, j))
    spec = pl.BlockSpec((None, rdx, q, tn), lambda bi, j, k, i: (bi, 0, 0, j))
    return pl.pallas_call(
        _dft_inv_kernel,
        grid=(bsz, width // tn, rdx, q // tt),
        in_specs=[pl.BlockSpec((2, None, tt, q), lambda bi, j, k, i: (0, k, i, 0)),
                  spec, spec, tile,
                  pl.BlockSpec((None, 1, tn), lambda bi, j, k, i: (order, 0, j)),
                  tile],
        out_specs=tile,
        out_shape=jax.ShapeDtypeStruct((bsz, rdx, q, width), BF16),
        scratch_shapes=[pltpu.VMEM((q, tn), BF16), pltpu.VMEM((q, tn), BF16)],
        compiler_params=_params("parallel", "parallel", "arbitrary", "arbitrary"),
        name="dft_inv",
    )(inv, pr, pi, u, bias, x)


def _filter_features(n):
    t = jnp.linspace(0.0, 1.0, n, dtype=F32)[:, None]
    w = 2.0 * math.pi * jnp.arange(n, dtype=F32) / n
    f = jnp.linspace(1e-4, FILTER_BANDS - 1, FILTER_BANDS, dtype=F32)
    z = jnp.concatenate([t, jnp.cos(w[:, None] * f), -jnp.sin(w[:, None] * f)], axis=-1)
    return jnp.pad(z, ((0, 0), (0, LANES - FILTER_EMB)))


def _pad_to(a, shape):
    return jnp.pad(a, [(0, t - s) for s, t in zip(a.shape, shape)])


def kernel(x, c, ctx, c_ctx, ada_w, ada_b, norm1_g, norm2_g, attn_w_in, attn_w_out, diff_q_g, diff_k_g, diff_lq1, diff_lk1, diff_lq2, diff_lk2, diff_sub_g, swa_q_g, swa_k_g, swa_sink, hy_w_in, hy_b_in, hy_conv_w, hy_conv_b, flt_w1, flt_b1, flt_f1, flt_w2, flt_b2, flt_f2, flt_w3, hy_bias, hy_w_out, hy_b_out, moe_wg1, moe_bg1, moe_wg2, moe_bg2, moe_w_gate, moe_w_up, moe_w_down):
    bsz, seq, d = x.shape
    depth = ada_w.shape[0]
    n_ctx = ctx.shape[1]
    cond = jnp.concatenate([c, c_ctx[None, :], jnp.zeros((8 - bsz - 1, d), F32)], axis=0)
    ada_b3 = ada_b[:, None, :]
    zero_bias = lambda n: jnp.zeros((1, n), F32)
    wg_all = moe_w_gate.astype(BF16).reshape(depth * MOE_EXPERTS, d, -1)
    wu_all = moe_w_up.astype(BF16).reshape(depth * MOE_EXPERTS, d, -1)
    wd_all = moe_w_down.astype(BF16).reshape(depth * MOE_EXPERTS, -1, d)
    tm = min(512, seq)

    for layer in range(depth):
        even = layer % 2 == 0
        i = layer // 2
        ctx_live = any(j % 2 == 0 for j in range(layer + 1, depth))
        assert not ctx_live, "context-stream update is only needed for deeper stacks"
        mod = _adaln_mod(cond, ada_w, ada_b3, layer)
        sh1, sc1, g1, sh2, sc2, g2 = [mod[:, k * d:(k + 1) * d] for k in range(6)]
        lat = lambda a: a[:bsz, None, :]
        cmod = lambda a: jnp.broadcast_to(a[bsz:bsz + 1, None, :], (bsz, 1, d))
        n1 = norm1_g[layer][None, :]
        if even:
            w_in = attn_w_in[i].astype(BF16)
            qkv = _norm_linear(x, n1, lat(sc1), lat(sh1), w_in, zero_bias(ATTN_IN), tm, 1536, "attn_in")
            qkv_c = _norm_linear(ctx, n1, cmod(sc1), cmod(sh1), w_in, zero_bias(ATTN_IN), min(tm, n_ctx), 1536,
                                 "attn_in_ctx")
            lam_init = 0.8 - 0.6 * math.exp(-0.3 * layer)
            lamp = _pad_to(jnp.stack([diff_lq1[i], diff_lk1[i], diff_lq2[i], diff_lk2[i]]), (8, 2 * DIFF_SUB))
            two = lambda a: jnp.tile(a, 2)[None, :]
            o_diff = _diff_attention(qkv, qkv_c, _rope_tables(seq, DIFF_SUB, 2 * DIFF_SUB), two(diff_q_g[i]),
                                     two(diff_k_g[i]), diff_sub_g[i][None, :], lamp, lam_init, min(2 * Q_BLOCK, seq))
            sink_rows = jnp.broadcast_to(swa_sink[i][:, None], (SWA_HEADS, HEAD_DIM))
            o_swa = _swa_attention(qkv, qkv_c, _rope_tables(seq, HEAD_DIM, HEAD_DIM), swa_q_g[i][None, :],
                                   swa_k_g[i][None, :], sink_rows)
            w_out = attn_w_out[i].astype(BF16)
            x = _mm_residual([o_diff, o_swa], [w_out[:DIFF_V], w_out[DIFF_V:]], zero_bias(d), x, lat(g1), tm,
                             "attn_out")
        else:
            width = hy_w_out.shape[1]
            q = seq // DFT_RADIX
            z = _norm_linear(x, n1, lat(sc1), lat(sh1), hy_w_in[i].astype(BF16), hy_b_in[i][None, :], tm, 1536,
                             "hyena_in", classes=DFT_RADIX)
            v, x1, x2 = _short_conv(z, _pad_to(hy_conv_w[i], (8, 3 * width)), hy_conv_b[i][None, :], 128)
            max_decay = math.log(DECAY_TARGET) / DECAY_FAST
            min_decay = math.log(DECAY_TARGET) / DECAY_SLOW
            deltas = jnp.abs(jnp.tile(jnp.linspace(min_decay, max_decay, width, dtype=F32), 2 * HYENA_ORDER))[None, :]
            is_bwd = jnp.tile(jnp.concatenate([jnp.zeros((width,), F32), jnp.ones((width,), F32)]),
                              HYENA_ORDER)[None, :]
            sq = (LANES, LANES)
            feats = _filter_features(seq)
            feats = jnp.concatenate([feats[k::DFT_RADIX] for k in range(DFT_RADIX)], axis=0)
            filt = _hyena_filters(
                feats, _pad_to(flt_w1[i], sq), _pad_to(flt_b1[i][None, :], (1, LANES)),
                _pad_to(flt_f1[i][None, :], (1, LANES)), _pad_to(flt_w2[i], sq),
                _pad_to(flt_b2[i][None, :], (1, LANES)), _pad_to(flt_f2[i][None, :], (1, LANES)),
                _pad_to(flt_w3[i], (LANES, 2 * HYENA_ORDER * width)), deltas, is_bwd, 256)
            fwd, inv = _dft_tables(seq)
            tf = min(256, q)
            tn = min(512, width)
            kr, ki = _dft_filter(fwd, filt, width, tf, min(256, width))
            bias = hy_bias[i][:, None, :]
            pr, pi = _dft_fwd(fwd, v, kr, ki, 0, tf, tn)
            tt = min(1024, q)
            y1 = _dft_inv(inv, pr, pi, v, bias, x1, 0, tt, tn)
            pr, pi = _dft_fwd(fwd, y1, kr, ki, 1, tf, tn)
            y2 = _dft_inv(inv, pr, pi, y1, bias, x2, 1, tt, tn)
            x = _mm_residual([y2], [hy_w_out[i].astype(BF16)], hy_b_out[i][None, :], x, lat(g1), tm,
                             "hyena_out", classes=DFT_RADIX)
        w_route = _pad_to(jnp.concatenate([moe_wg1[layer], moe_wg2[layer]], axis=1), (d, LANES))
        b_route = _pad_to(jnp.concatenate([moe_bg1[layer], moe_bg2[layer]])[None, :], (1, LANES))
        x = _hier_moe(x, norm2_g[layer][None, :], lat(sc2), lat(sh2), lat(g2), w_route, b_route,
                      wg_all, wu_all, wd_all, layer)
    return x
```

```python
import functools
import math

import jax
import jax.numpy as jnp
from jax import lax
from jax.experimental import pallas as pl
from jax.experimental.pallas import tpu as pltpu

F32 = jnp.float32
BF16 = jnp.bfloat16
HIGHEST = lax.Precision.HIGHEST

LANES = 128
V7X_VMEM_BYTES = 64 * 1024 * 1024
VMEM_LIMIT = V7X_VMEM_BYTES * 7 // 8

GRID_W = 64
HEAD_DIM = 128
ROPE_BASE = 10000.0
EPS = 1e-6
Q_BLOCK = 128
DIFF_HEADS = 8
DIFF_SUB = HEAD_DIM // 2
SWA_HEADS = 8
SWA_KV_HEADS = 2
SWA_GROUP = SWA_HEADS // SWA_KV_HEADS
WINDOW = 128
DIFF_Q = DIFF_HEADS * 2 * DIFF_SUB
DIFF_V = DIFF_HEADS * HEAD_DIM
SWA_Q = SWA_HEADS * HEAD_DIM
SWA_KV = SWA_KV_HEADS * HEAD_DIM
ATTN_IN = 2 * DIFF_Q + DIFF_V + SWA_Q + 2 * SWA_KV
HYENA_ORDER = 2
SHORT_CONV = 3
FILTER_EMB = 33
FILTER_BANDS = (FILTER_EMB - 1) // 2
FILTER_HIDDEN = 64
DECAY_FAST = 0.3
DECAY_SLOW = 1.5
DECAY_TARGET = 1e-2
MOE_GROUPS = 4
MOE_PER_GROUP = 8
MOE_EXPERTS = MOE_GROUPS * MOE_PER_GROUP
MOE_TOPK = 2
MOE_BLOCK = 256
FF_CHUNK = 256
DIFF_KEY_CHUNK = 512
DFT_RADIX = 4
LOG2_E = 1.4426950408889634
NEG_INF = float("-inf")


def _tile(n, pref):
    t = pref
    while n % t:
        t //= 2
    return t


def _params(*sem):
    return pltpu.CompilerParams(dimension_semantics=sem, vmem_limit_bytes=VMEM_LIMIT)


def _dot(a, b):
    return jnp.dot(a, b, preferred_element_type=F32)


def _dot_nt(a, b):
    return lax.dot_general(a, b, (((1,), (1,)), ((), ())), preferred_element_type=F32)


def _mod_kernel(c_ref, w_ref, b_ref, o_ref):
    c = c_ref[...]
    s = c * jax.nn.sigmoid(c)
    o_ref[...] = jnp.dot(s, w_ref[...], precision=HIGHEST, preferred_element_type=F32) + b_ref[...]


def _adaln_mod(cond, ada_w, ada_b, layer):
    rows, d = cond.shape
    n = ada_w.shape[2]
    tn = _tile(n, 768)
    return pl.pallas_call(
        _mod_kernel,
        grid=(n // tn,),
        in_specs=[pl.BlockSpec((rows, d), lambda j: (0, 0)),
                  pl.BlockSpec((None, d, tn), lambda j: (layer, 0, j)),
                  pl.BlockSpec((None, 1, tn), lambda j: (layer, 0, j))],
        out_specs=pl.BlockSpec((rows, tn), lambda j: (0, j)),
        out_shape=jax.ShapeDtypeStruct((rows, n), F32),
        compiler_params=_params("parallel"),
        name="adaln_mod",
    )(cond, ada_w, ada_b)


def _norm_mod(x, g, sc, sh):
    inv = lax.rsqrt(jnp.mean(x * x, axis=-1, keepdims=True) + EPS)
    return ((x * inv) * g) * (1.0 + sc) + sh


def _class_perm(tm, classes):
    m = tm // classes
    out_row = jnp.arange(tm, dtype=jnp.int32)
    src = classes * (out_row % m) + out_row // m
    return (src[:, None] == jnp.arange(tm, dtype=jnp.int32)[None, :]).astype(BF16)


def _norm_linear_kernel(*refs, classes):
    if classes:
        p_ref, x_ref, g_ref, sc_ref, sh_ref, w_ref, b_ref, o_ref, h_ref = refs
    else:
        x_ref, g_ref, sc_ref, sh_ref, w_ref, b_ref, o_ref, h_ref = refs

    @pl.when(pl.program_id(2) == 0)
    def _():
        h = _norm_mod(x_ref[...], g_ref[...], sc_ref[...], sh_ref[...]).astype(BF16)
        if classes:
            h = _dot(p_ref[...], h).astype(BF16)
        h_ref[...] = h

    res = (_dot(h_ref[...], w_ref[...]) + b_ref[...]).astype(o_ref.dtype)
    if classes:
        m = res.shape[0] // classes
        for k in range(classes):
            o_ref[k] = res[k * m:(k + 1) * m]
    else:
        o_ref[...] = res


def _norm_linear(x, g, sc, sh, w, b, tm, tn, name, classes=0):
    bsz, l, d = x.shape
    n = w.shape[1]
    tn = _tile(n, tn)
    in_specs = [pl.BlockSpec((None, tm, d), lambda bi, i, j: (bi, i, 0)),
                pl.BlockSpec((1, d), lambda bi, i, j: (0, 0)),
                pl.BlockSpec((None, 1, d), lambda bi, i, j: (bi, 0, 0)),
                pl.BlockSpec((None, 1, d), lambda bi, i, j: (bi, 0, 0)),
                pl.BlockSpec((d, tn), lambda bi, i, j: (0, j)),
                pl.BlockSpec((1, tn), lambda bi, i, j: (0, j))]
    args = (x, g, sc, sh, w, b)
    if classes:
        in_specs = [pl.BlockSpec((tm, tm), lambda bi, i, j: (0, 0))] + in_specs
        args = (_class_perm(tm, classes),) + args
        out_specs = pl.BlockSpec((None, classes, tm // classes, tn), lambda bi, i, j: (bi, 0, i, j))
        out_shape = jax.ShapeDtypeStruct((bsz, classes, l // classes, n), BF16)
    else:
        out_specs = pl.BlockSpec((None, tm, tn), lambda bi, i, j: (bi, i, j))
        out_shape = jax.ShapeDtypeStruct((bsz, l, n), BF16)
    return pl.pallas_call(
        functools.partial(_norm_linear_kernel, classes=classes),
        grid=(bsz, l // tm, n // tn),
        in_specs=in_specs,
        out_specs=out_specs,
        out_shape=out_shape,
        scratch_shapes=[pltpu.VMEM((tm, d), BF16)],
        compiler_params=_params("parallel", "arbitrary", "arbitrary"),
        name=name,
    )(*args)


def _mm_res_kernel(*refs, n_pairs, classes):
    if classes:
        pt_ref, refs = refs[0], refs[1:]
    a_refs = refs[:n_pairs]
    w_refs = refs[n_pairs:2 * n_pairs]
    b_ref, x_ref, g_ref, o_ref = refs[2 * n_pairs:]
    acc = None
    for a_ref, w_ref in zip(a_refs, w_refs):
        if classes:
            a = jnp.concatenate([a_ref[k] for k in range(classes)], axis=0)
            a = _dot(pt_ref[...], a).astype(BF16)
        else:
            a = a_ref[...]
        prod = _dot(a, w_ref[...])
        acc = prod if acc is None else acc + prod
    o_ref[...] = x_ref[...] + g_ref[...] * (acc + b_ref[...])


def _mm_residual(a_list, w_list, b, x, g, tm, name, classes=0):
    bsz, l, n = x.shape
    n_pairs = len(a_list)
    if classes:
        in_specs = [pl.BlockSpec((tm, tm), lambda bi, i: (0, 0))]
        in_specs += [pl.BlockSpec((None, classes, tm // classes, a.shape[3]), lambda bi, i: (bi, 0, i, 0))
                     for a in a_list]
        args = (_class_perm(tm, classes).T,)
    else:
        in_specs = [pl.BlockSpec((None, tm, a.shape[2]), lambda bi, i: (bi, i, 0)) for a in a_list]
        args = ()
    in_specs += [pl.BlockSpec(w.shape, lambda bi, i: (0, 0)) for w in w_list]
    in_specs += [pl.BlockSpec((1, n), lambda bi, i: (0, 0)),
                 pl.BlockSpec((None, tm, n), lambda bi, i: (bi, i, 0)),
                 pl.BlockSpec((None, 1, n), lambda bi, i: (bi, 0, 0))]
    return pl.pallas_call(
        functools.partial(_mm_res_kernel, n_pairs=n_pairs, classes=classes),
        grid=(bsz, l // tm),
        in_specs=in_specs,
        out_specs=pl.BlockSpec((None, tm, n), lambda bi, i: (bi, i, 0)),
        out_shape=jax.ShapeDtypeStruct((bsz, l, n), F32),
        compiler_params=_params("parallel", "parallel"),
        name=name,
    )(*args, *a_list, *w_list, b, x, g)


def _rope_tables(n_tokens, dim, width):
    n_rows = n_tokens // GRID_W
    row = jnp.repeat(jnp.arange(n_rows, dtype=F32), GRID_W)
    col = jnp.tile(jnp.arange(GRID_W, dtype=F32), n_rows)
    n_freq = dim // 4
    inv = ROPE_BASE ** (-jnp.arange(n_freq, dtype=F32) / n_freq)
    ang = jnp.concatenate([row[:, None] * inv, col[:, None] * inv], axis=-1)
    ang = jnp.repeat(ang, 2, axis=-1)
    ang = jnp.tile(ang, (1, width // dim))
    sign = jnp.where(jnp.arange(width) % 2 == 0, -1.0, 1.0).astype(F32)
    return jnp.cos(ang), jnp.sin(ang) * sign


def _rope(x, cos_t, sin_t):
    width = x.shape[-1]
    lane = lax.broadcasted_iota(jnp.int32, (1, width), 1)
    partner = jnp.where(lane % 2 == 0, pltpu.roll(x, width - 1, 1), pltpu.roll(x, 1, 1))
    return x * cos_t + partner * sin_t


def _sub_rms(x, g):
    lane = lax.broadcasted_iota(jnp.int32, (1, 2 * DIFF_SUB), 1)
    lo = lane < DIFF_SUB
    sq = x * x
    ss_lo = jnp.sum(jnp.where(lo, sq, 0.0), axis=-1, keepdims=True)
    ss_hi = jnp.sum(jnp.where(lo, 0.0, sq), axis=-1, keepdims=True)
    inv = jnp.where(lo, lax.rsqrt(ss_lo / DIFF_SUB + EPS), lax.rsqrt(ss_hi / DIFF_SUB + EPS))
    return (x * inv) * g


def _rms(x, g):
    return (x * lax.rsqrt(jnp.mean(x * x, axis=-1, keepdims=True) + EPS)) * g


def _diff_attn_kernel(*refs, lam_init, n_ctx, tq, n_cast):
    (q_ref, k_ref, v_ref, kc_ref, vc_ref, cos_ref, sin_ref, gq_ref, gk_ref, gs_ref, lamp_ref) = refs[:11]
    cast_in = refs[11:11 + n_cast]
    o_ref = refs[11 + n_cast]
    cast_out = refs[12 + n_cast:12 + 2 * n_cast]
    q_scr, k_scr, v_scr, s_scr, lam_scr = refs[12 + 2 * n_cast:]
    for src, dst in zip(cast_in, cast_out):
        dst[...] = src[...].astype(dst.dtype)
    n_keys = k_scr.shape[0]

    @pl.when(pl.program_id(2) == 0)
    def _():
        lane = lax.broadcasted_iota(jnp.int32, (1, 2 * DIFF_SUB), 1)
        lo = lane < DIFF_SUB
        gk = gk_ref[...]
        cos_t = cos_ref[...]
        sin_t = sin_ref[...]
        k_scr[0:n_ctx, :] = _sub_rms(kc_ref[...].astype(F32), gk).astype(BF16)
        k_scr[n_ctx:, :] = _rope(_sub_rms(k_ref[...].astype(F32), gk), cos_t, sin_t).astype(BF16)
        v_scr[0:n_ctx, 0:HEAD_DIM] = vc_ref[...].astype(BF16)
        v_scr[n_ctx:, 0:HEAD_DIM] = v_ref[...].astype(BF16)
        v_scr[:, HEAD_DIM:] = jnp.broadcast_to(jnp.where(lane == 0, 1.0, 0.0).astype(BF16), (n_keys, HEAD_DIM))
        q = _rope(_sub_rms(q_ref[...].astype(F32), gq_ref[...]), cos_t, sin_t) * (DIFF_SUB ** -0.5 * LOG2_E)
        q_scr[0] = jnp.where(lo, q, 0.0).astype(BF16)
        q_scr[1] = jnp.where(lo, 0.0, q).astype(BF16)
        lp = lamp_ref[...]
        t1 = jnp.sum(lp[0:1] * lp[1:2], axis=-1, keepdims=True)
        t2 = jnp.sum(lp[2:3] * lp[3:4], axis=-1, keepdims=True)
        lam_scr[...] = jnp.broadcast_to(jnp.exp(t1) - jnp.exp(t2) + lam_init, lam_scr.shape)

    chunks = [(c0, min(c0 + DIFF_KEY_CHUNK, n_keys)) for c0 in range(0, n_keys, DIFF_KEY_CHUNK)]
    lam = lam_scr[:, 0:1]
    for t in range(tq // Q_BLOCK):
        rows = pl.ds(pl.multiple_of(pl.program_id(2) * tq + t * Q_BLOCK, Q_BLOCK), Q_BLOCK)
        qs = [q_scr[0, rows, :], q_scr[1, rows, :]]

        mx = [jnp.full((Q_BLOCK, LANES), NEG_INF, F32) for _ in range(2)]
        for c0, c1 in chunks:
            kc = k_scr[c0:c1, :]
            for h in range(2):
                s = _dot_nt(qs[h], kc)
                s_scr[2 * t + h, :, c0:c1] = s
                for j in range((c1 - c0) // LANES):
                    mx[h] = jnp.maximum(mx[h], s[:, j * LANES:(j + 1) * LANES])
        mrow = [jnp.max(m, axis=-1, keepdims=True) for m in mx]

        out = [jnp.zeros((Q_BLOCK, 2 * HEAD_DIM), F32) for _ in range(2)]
        for c0, c1 in chunks:
            vc = v_scr[c0:c1, :]
            for h in range(2):
                e = jnp.exp2(s_scr[2 * t + h, :, c0:c1] - mrow[h])
                out[h] = out[h] + _dot(e.astype(BF16), vc)
        r1 = 1.0 / out[0][:, HEAD_DIM:HEAD_DIM + 1]
        r2 = lam / out[1][:, HEAD_DIM:HEAD_DIM + 1]
        o = out[0][:, 0:HEAD_DIM] * r1 - out[1][:, 0:HEAD_DIM] * r2
        o_ref[t * Q_BLOCK:(t + 1) * Q_BLOCK, :] = (_rms(o, gs_ref[...]) * (1.0 - lam_init)).astype(o_ref.dtype)


def _diff_attention(qkv, qkv_c, tabs, gq, gk, gs, lamp, lam_init, tq, to_cast):
    bsz, l, _ = qkv.shape
    n_ctx = qkv_c.shape[1]
    cos_t, sin_t = tabs
    hb = 2 * DIFF_SUB
    kb = DIFF_Q // hb
    vb = 2 * DIFF_Q // hb
    nq = l // tq
    n_steps = bsz * DIFF_HEADS * nq
    slabs = [a.reshape(n_steps, -1, a.shape[-1]) for a in to_cast]
    slab_spec = lambda a: pl.BlockSpec((None,) + a.shape[1:], lambda bi, h, i: ((bi * DIFF_HEADS + h) * nq + i, 0, 0))
    kern = functools.partial(_diff_attn_kernel, lam_init=lam_init, n_ctx=n_ctx, tq=tq, n_cast=len(slabs))
    vec = pl.BlockSpec((1, hb), lambda bi, h, i: (0, 0))
    outs = pl.pallas_call(
        kern,
        grid=(bsz, DIFF_HEADS, nq),
        in_specs=[pl.BlockSpec((None, l, hb), lambda bi, h, i: (bi, 0, h)),
                  pl.BlockSpec((None, l, hb), lambda bi, h, i: (bi, 0, kb + h)),
                  pl.BlockSpec((None, l, hb), lambda bi, h, i: (bi, 0, vb + h)),
                  pl.BlockSpec((None, n_ctx, hb), lambda bi, h, i: (bi, 0, kb + h)),
                  pl.BlockSpec((None, n_ctx, hb), lambda bi, h, i: (bi, 0, vb + h)),
                  pl.BlockSpec((l, hb), lambda bi, h, i: (0, 0)),
                  pl.BlockSpec((l, hb), lambda bi, h, i: (0, 0)),
                  vec, vec, vec,
                  pl.BlockSpec((8, hb), lambda bi, h, i: (0, 0))] + [slab_spec(a) for a in slabs],
        out_specs=[pl.BlockSpec((None, tq, hb), lambda bi, h, i: (bi, i, h))] + [slab_spec(a) for a in slabs],
        out_shape=[jax.ShapeDtypeStruct((bsz, l, DIFF_V), BF16)]
        + [jax.ShapeDtypeStruct(a.shape, BF16) for a in slabs],
        scratch_shapes=[pltpu.VMEM((2, l, hb), BF16), pltpu.VMEM((n_ctx + l, hb), BF16),
                        pltpu.VMEM((n_ctx + l, 2 * HEAD_DIM), BF16),
                        pltpu.VMEM((2 * (tq // Q_BLOCK), Q_BLOCK, n_ctx + l), F32),
                        pltpu.VMEM((1, hb), F32)],
        compiler_params=_params("arbitrary", "arbitrary", "arbitrary"),
        name="diff_attention",
    )(qkv, qkv, qkv, qkv_c, qkv_c, cos_t, sin_t, gq, gk, gs, lamp, *slabs)
    return outs[0], [o.reshape(a.shape) for o, a in zip(outs[1:], to_cast)]


def _swa_kernel(q_ref, k_ref, v_ref, kc_ref, vc_ref, cos_ref, sin_ref, gq_ref, gk_ref, sink_ref,
                o_ref, q_scr, k_scr, v_scr, kc_scr, vc_scr, *, seq):
    kvh = pl.program_id(1)
    n = pl.program_id(2)
    span = Q_BLOCK + 2 * WINDOW

    @pl.when(n == 0)
    def _():
        gk = gk_ref[...]
        gq = gq_ref[...]
        cos_t = cos_ref[...]
        sin_t = sin_ref[...]
        zeros = jnp.zeros((WINDOW, HEAD_DIM), BF16)
        k_scr[0:WINDOW, :] = zeros
        v_scr[0:WINDOW, :] = zeros
        k_scr[WINDOW + seq:, :] = zeros
        v_scr[WINDOW + seq:, :] = zeros
        k_scr[WINDOW:WINDOW + seq, :] = _rope(_rms(k_ref[...].astype(F32), gk), cos_t, sin_t).astype(BF16)
        v_scr[WINDOW:WINDOW + seq, :] = v_ref[...].astype(BF16)
        kc_scr[...] = _rms(kc_ref[...].astype(F32), gk).astype(BF16)
        vc_scr[...] = vc_ref[...].astype(BF16)
        for j in range(SWA_GROUP):
            qj = q_ref[:, j * HEAD_DIM:(j + 1) * HEAD_DIM].astype(F32)
            qj = _rope(_rms(qj, gq), cos_t, sin_t) * (HEAD_DIM ** -0.5)
            q_scr[j] = qj.astype(BF16)

    kc = kc_scr[...]
    vc = vc_scr[...]
    rows = SWA_GROUP * Q_BLOCK
    ridx = lax.broadcasted_iota(jnp.int32, (rows, span), 0)
    kidx = lax.broadcasted_iota(jnp.int32, (rows, span), 1)
    rel = kidx - (ridx & (Q_BLOCK - 1))
    band = (rel >= 0) & (rel <= 2 * WINDOW)
    head = lax.broadcasted_iota(jnp.int32, (rows, 1), 0) // Q_BLOCK
    sink = jnp.zeros((rows, 1), F32)
    for j in range(SWA_GROUP):
        sink = jnp.where(head == j, sink_ref[pl.ds(kvh * SWA_GROUP + j, 1), 0:1], sink)
    for t in range(o_ref.shape[0] // Q_BLOCK):
        blk = n * (o_ref.shape[0] // Q_BLOCK) + t
        start = pl.multiple_of(blk * Q_BLOCK, Q_BLOCK)
        kw = k_scr[pl.ds(start, span), :]
        vw = v_scr[pl.ds(start, span), :]
        key_pos = blk * Q_BLOCK + kidx - WINDOW
        mask = band & (key_pos >= 0) & (key_pos < seq)
        q = jnp.concatenate([q_scr[j, pl.ds(start, Q_BLOCK), :] for j in range(SWA_GROUP)], axis=0)
        s = jnp.concatenate([_dot_nt(q, kc), jnp.where(mask, _dot_nt(q, kw), NEG_INF)], axis=-1)
        m = jnp.maximum(jnp.max(s, axis=-1, keepdims=True), sink)
        e = jnp.exp(s - m)
        r = 1.0 / (jnp.sum(e, axis=-1, keepdims=True) + jnp.exp(sink - m))
        o = _dot((e * r).astype(BF16), jnp.concatenate([vc, vw], axis=0))
        o_ref[t * Q_BLOCK:(t + 1) * Q_BLOCK, :] = jnp.concatenate(
            [o[j * Q_BLOCK:(j + 1) * Q_BLOCK] for j in range(SWA_GROUP)], axis=-1).astype(o_ref.dtype)


def _swa_attention(qkv, qkv_c, tabs, gq, gk, sink_rows):
    bsz, l, _ = qkv.shape
    n_ctx = qkv_c.shape[1]
    cos_t, sin_t = tabs
    gw = SWA_GROUP * HEAD_DIM
    q0 = (2 * DIFF_Q + DIFF_V) // gw
    k0 = (2 * DIFF_Q + DIFF_V + SWA_Q) // HEAD_DIM
    v0 = k0 + SWA_KV_HEADS
    vec = pl.BlockSpec((1, HEAD_DIM), lambda bi, h, i: (0, 0))
    return pl.pallas_call(
        functools.partial(_swa_kernel, seq=l),
        grid=(bsz, SWA_KV_HEADS, l // (2 * Q_BLOCK)),
        in_specs=[pl.BlockSpec((None, l, gw), lambda bi, h, i: (bi, 0, q0 + h)),
                  pl.BlockSpec((None, l, HEAD_DIM), lambda bi, h, i: (bi, 0, k0 + h)),
                  pl.BlockSpec((None, l, HEAD_DIM), lambda bi, h, i: (bi, 0, v0 + h)),
                  pl.BlockSpec((None, n_ctx, HEAD_DIM), lambda bi, h, i: (bi, 0, k0 + h)),
                  pl.BlockSpec((None, n_ctx, HEAD_DIM), lambda bi, h, i: (bi, 0, v0 + h)),
                  pl.BlockSpec((l, HEAD_DIM), lambda bi, h, i: (0, 0)),
                  pl.BlockSpec((l, HEAD_DIM), lambda bi, h, i: (0, 0)),
                  vec, vec,
                  pl.BlockSpec((SWA_HEADS, HEAD_DIM), lambda bi, h, i: (0, 0))],
        out_specs=pl.BlockSpec((None, 2 * Q_BLOCK, gw), lambda bi, h, i: (bi, i, h)),
        out_shape=jax.ShapeDtypeStruct((bsz, l, SWA_Q), BF16),
        scratch_shapes=[pltpu.VMEM((SWA_GROUP, l, HEAD_DIM), BF16),
                        pltpu.VMEM((l + 2 * WINDOW, HEAD_DIM), BF16), pltpu.VMEM((l + 2 * WINDOW, HEAD_DIM), BF16),
                        pltpu.VMEM((n_ctx, HEAD_DIM), BF16), pltpu.VMEM((n_ctx, HEAD_DIM), BF16)],
        compiler_params=_params("parallel", "parallel", "arbitrary"),
        name="swa_attention",
    )(qkv, qkv, qkv, qkv_c, qkv_c, cos_t, sin_t, gq, gk, sink_rows)


def _router_kernel(x_ref, g_ref, sc_ref, sh_ref, w_ref, b_ref, tri_ref, h_ref, e_ref, p_ref, cnt_ref, run_scr):
    @pl.when((pl.program_id(0) == 0) & (pl.program_id(1) == 0))
    def _():
        run_scr[...] = jnp.zeros(run_scr.shape, F32)

    h = _norm_mod(x_ref[...], g_ref[...], sc_ref[...], sh_ref[...])
    h_ref[...] = h
    logits = jnp.dot(h, w_ref[...], precision=HIGHEST, preferred_element_type=F32) + b_ref[...]
    lane_i = lax.broadcasted_iota(jnp.int32, (1, LANES), 1)
    lane = lane_i.astype(F32)
    big = float(LANES)
    lg = jnp.where(lane_i < MOE_GROUPS, logits, NEG_INF)
    mg = jnp.max(lg, axis=-1, keepdims=True)
    p_top = 1.0 / jnp.sum(jnp.exp(lg - mg), axis=-1, keepdims=True)
    grp = jnp.min(jnp.where(lg == mg, lane, big), axis=-1, keepdims=True)
    e_lane = lane - MOE_GROUPS
    in_grp = (e_lane >= grp * MOE_PER_GROUP) & (e_lane < (grp + 1.0) * MOE_PER_GROUP)
    le = jnp.where(in_grp, logits, NEG_INF)
    t1 = jnp.max(le, axis=-1, keepdims=True)
    i1 = jnp.min(jnp.where(le == t1, lane, big), axis=-1, keepdims=True)
    le2 = jnp.where(lane == i1, NEG_INF, le)
    t2 = jnp.max(le2, axis=-1, keepdims=True)
    i2 = jnp.min(jnp.where(le2 == t2, lane, big), axis=-1, keepdims=True)
    w2 = jnp.exp(t2 - t1)
    inv = p_top / (1.0 + w2)
    hot1 = lane == i1
    hot2 = lane == i2
    onehot = jnp.where(hot1 | hot2, 1.0, 0.0)
    before = run_scr[...] + _dot(tri_ref[...], onehot.astype(BF16))
    r1 = jnp.sum(jnp.where(hot1, before, 0.0), axis=-1, keepdims=True)
    r2 = jnp.sum(jnp.where(hot2, before, 0.0), axis=-1, keepdims=True)
    run_scr[...] = run_scr[...] + jnp.sum(onehot, axis=0, keepdims=True)
    cnt_ref[...] = run_scr[...]
    e_ref[...] = jnp.where(lane_i == 0, i1 - MOE_GROUPS,
                           jnp.where(lane_i == 1, i2 - MOE_GROUPS,
                                     jnp.where(lane_i == 2, r1, jnp.where(lane_i == 3, r2, 0.0)))).astype(jnp.int32)
    p_ref[...] = jnp.where(lane_i == 0, inv, jnp.where(lane_i == 1, inv * w2, 0.0))


def _router(x, g, sc, sh, w, b, tm):
    bsz, l, d = x.shape
    row = lambda bi, i: (bi, i, 0)
    tri = (jnp.arange(tm)[:, None] > jnp.arange(tm)[None, :]).astype(BF16)
    return pl.pallas_call(
        _router_kernel,
        grid=(bsz, l // tm),
        in_specs=[pl.BlockSpec((None, tm, d), row),
                  pl.BlockSpec((1, d), lambda bi, i: (0, 0)),
                  pl.BlockSpec((None, 1, d), lambda bi, i: (bi, 0, 0)),
                  pl.BlockSpec((None, 1, d), lambda bi, i: (bi, 0, 0)),
                  pl.BlockSpec((d, LANES), lambda bi, i: (0, 0)),
                  pl.BlockSpec((1, LANES), lambda bi, i: (0, 0)),
                  pl.BlockSpec((tm, tm), lambda bi, i: (0, 0))],
        out_specs=[pl.BlockSpec((None, tm, d), row),
                   pl.BlockSpec((None, tm, LANES), row),
                   pl.BlockSpec((None, tm, LANES), row),
                   pl.BlockSpec((1, LANES), lambda bi, i: (0, 0))],
        out_shape=[jax.ShapeDtypeStruct((bsz, l, d), F32),
                   jax.ShapeDtypeStruct((bsz, l, LANES), jnp.int32),
                   jax.ShapeDtypeStruct((bsz, l, LANES), F32),
                   jax.ShapeDtypeStruct((1, LANES), F32)],
        scratch_shapes=[pltpu.VMEM((1, LANES), F32)],
        compiler_params=_params("arbitrary", "arbitrary"),
        name="moe_router",
    )(x, g, sc, sh, w, b, tri)


def _dispatch_kernel(slot_ref, h_ref, init_ref, xb_ref, sem):
    del init_ref
    tm = h_ref.shape[0]
    base = pl.program_id(0) * tm

    def copy(r, k):
        dst = slot_ref[(base + r) * MOE_TOPK + k]
        return pltpu.make_async_copy(h_ref.at[pl.ds(r, 1)], xb_ref.at[pl.ds(dst, 1)], sem)

    def start(r, carry):
        for k in range(MOE_TOPK):
            copy(r, k).start(priority=k)
        return carry

    def wait(r, carry):
        for k in range(MOE_TOPK):
            copy(r, k).wait()
        return carry

    lax.fori_loop(0, tm, start, 0, unroll=8)
    lax.fori_loop(0, tm, wait, 0, unroll=8)


def _dispatch(slot, h, n_rows, tm):
    n, d = h.shape
    grid_spec = pltpu.PrefetchScalarGridSpec(
        num_scalar_prefetch=1,
        grid=(n // tm,),
        in_specs=[pl.BlockSpec((tm, d), lambda i, s: (i, 0)),
                  pl.BlockSpec(memory_space=pl.ANY)],
        out_specs=pl.BlockSpec(memory_space=pl.ANY),
        scratch_shapes=[pltpu.SemaphoreType.DMA(())],
    )
    return pl.pallas_call(
        _dispatch_kernel,
        grid_spec=grid_spec,
        out_shape=jax.ShapeDtypeStruct((n_rows, d), F32),
        input_output_aliases={2: 0},
        compiler_params=_params("arbitrary"),
        name="moe_dispatch",
    )(slot, h, jnp.zeros((n_rows, d), F32))


def _combine_kernel(slot_ref, x_ref, g_ref, p_ref, yb_ref, o_ref, y_scr, sem):
    tm = x_ref.shape[0]
    base = pl.program_id(0) * tm

    def copy(r, k):
        src = slot_ref[(base + r) * MOE_TOPK + k]
        return pltpu.make_async_copy(yb_ref.at[pl.ds(src, 1)], y_scr.at[k, pl.ds(r, 1)], sem)

    def start(r, carry):
        for k in range(MOE_TOPK):
            copy(r, k).start(priority=k)
        return carry

    def wait(r, carry):
        for k in range(MOE_TOPK):
            copy(r, k).wait()
        return carry

    lax.fori_loop(0, tm, start, 0, unroll=8)
    lax.fori_loop(0, tm, wait, 0, unroll=8)
    p = p_ref[...]
    moe = y_scr[0] * p[:, 0:1]
    for k in range(1, MOE_TOPK):
        moe = moe + y_scr[k] * p[:, k:k + 1]
    o_ref[...] = x_ref[...] + g_ref[...] * moe


def _combine(slot, x, g, gates, yb, tm):
    bsz, l, d = x.shape
    nt = l // tm
    grid_spec = pltpu.PrefetchScalarGridSpec(
        num_scalar_prefetch=1,
        grid=(bsz * nt,),
        in_specs=[pl.BlockSpec((None, tm, d), lambda i, s: (i // nt, i % nt, 0)),
                  pl.BlockSpec((None, 1, d), lambda i, s: (i // nt, 0, 0)),
                  pl.BlockSpec((None, tm, LANES), lambda i, s: (i // nt, i % nt, 0)),
                  pl.BlockSpec(memory_space=pl.ANY)],
        out_specs=pl.BlockSpec((None, tm, d), lambda i, s: (i // nt, i % nt, 0)),
        scratch_shapes=[pltpu.VMEM((MOE_TOPK, tm, d), F32), pltpu.SemaphoreType.DMA(())],
    )
    return pl.pallas_call(
        _combine_kernel,
        grid_spec=grid_spec,
        out_shape=jax.ShapeDtypeStruct((bsz, l, d), F32),
        compiler_params=_params("arbitrary"),
        name="moe_combine",
    )(slot, x, g, gates, yb)


def _expert_kernel(be_ref, nu_ref, x_ref, wg_ref, wu_ref, wd_ref, o_ref):
    i = pl.program_id(0)

    @pl.when(i < nu_ref[0])
    def _():
        x = x_ref[...].astype(BF16)
        ff = wg_ref.shape[1]
        acc = jnp.zeros(o_ref.shape, F32)
        for c in range(ff // FF_CHUNK):
            cols = slice(c * FF_CHUNK, (c + 1) * FF_CHUNK)
            gate = _dot(x, wg_ref[:, cols])
            up = _dot(x, wu_ref[:, cols])
            hidden = (gate * jax.nn.sigmoid(gate) * up).astype(BF16)
            acc = acc + _dot(hidden, wd_ref[cols, :])
        o_ref[...] = acc

    @pl.when(i >= nu_ref[0])
    def _():
        o_ref[...] = jnp.zeros(o_ref.shape, F32)


def _expert_ffn(xb, block_e, n_used, w_gate, w_up, w_down, layer):
    n_blocks, _, d = xb.shape
    ff = w_gate.shape[2]
    base = layer * MOE_EXPERTS
    grid_spec = pltpu.PrefetchScalarGridSpec(
        num_scalar_prefetch=2,
        grid=(n_blocks,),
        in_specs=[pl.BlockSpec((None, MOE_BLOCK, d), lambda i, be, nu: (i, 0, 0)),
                  pl.BlockSpec((None, d, ff), lambda i, be, nu: (base + be[i], 0, 0)),
                  pl.BlockSpec((None, d, ff), lambda i, be, nu: (base + be[i], 0, 0)),
                  pl.BlockSpec((None, ff, d), lambda i, be, nu: (base + be[i], 0, 0))],
        out_specs=pl.BlockSpec((None, MOE_BLOCK, d), lambda i, be, nu: (i, 0, 0)),
    )
    return pl.pallas_call(
        _expert_kernel,
        grid_spec=grid_spec,
        out_shape=jax.ShapeDtypeStruct((n_blocks, MOE_BLOCK, d), F32),
        compiler_params=_params("arbitrary"),
        name="expert_ffn",
    )(block_e, n_used, xb, w_gate, w_up, w_down)


def _hier_moe(x, g, sc, sh, gate2, w_route, b_route, w_gate, w_up, w_down, layer):
    bsz, l, d = x.shape
    n = bsz * l
    tm = min(512, l)
    h, route, gates, cnt = _router(x, g, sc, sh, w_route, b_route, tm)
    route = route.reshape(n, LANES)
    expert = route[:, :MOE_TOPK]
    rank = route[:, MOE_TOPK:2 * MOE_TOPK]
    counts = cnt[0, MOE_GROUPS:MOE_GROUPS + MOE_EXPERTS].astype(jnp.int32)
    padded = (counts + MOE_BLOCK - 1) // MOE_BLOCK * MOE_BLOCK
    pad_end = jnp.cumsum(padded)
    pad_start = pad_end - padded
    hit = expert[:, :, None] == jnp.arange(MOE_EXPERTS, dtype=jnp.int32)
    slot = (jnp.sum(jnp.where(hit, pad_start, 0), axis=-1) + rank).reshape(-1)
    n_blocks = -(-(n * MOE_TOPK) // MOE_BLOCK) + MOE_EXPERTS
    block_row = jnp.arange(n_blocks, dtype=jnp.int32)[:, None] * MOE_BLOCK
    block_e = jnp.minimum(jnp.sum((pad_end[None, :] <= block_row).astype(jnp.int32), axis=-1), MOE_EXPERTS - 1)
    n_used = (pad_end[-1:] // MOE_BLOCK).astype(jnp.int32)
    xb = _dispatch(slot, h.reshape(n, d), n_blocks * MOE_BLOCK, tm).reshape(n_blocks, MOE_BLOCK, d)
    yb = _expert_ffn(xb, block_e, n_used, w_gate, w_up, w_down, layer).reshape(-1, d)
    return _combine(slot, x, gate2, gates, yb, tm)


def _short_conv_kernel(zv_ref, z1_ref, z2_ref, wv_ref, w1_ref, w2_ref, bv_ref, b1_ref, b2_ref,
                       v_ref, x1_ref, x2_ref):
    rdx, q = zv_ref.shape[0], zv_ref.shape[1]
    row = lax.broadcasted_iota(jnp.int32, (q, 1), 0)

    for z_ref, w_ref, b_ref, o_ref in ((zv_ref, wv_ref, bv_ref, v_ref), (z1_ref, w1_ref, b1_ref, x1_ref),
                                       (z2_ref, w2_ref, b2_ref, x2_ref)):
        z = [z_ref[k].astype(F32) for k in range(rdx)]
        w = w_ref[...]
        b = b_ref[...]
        for k in range(rdx):
            prev = z[k - 1] if k > 0 else jnp.where(row == 0, 0.0, pltpu.roll(z[rdx - 1], 1, 0))
            nxt = z[k + 1] if k < rdx - 1 else jnp.where(row == q - 1, 0.0, pltpu.roll(z[0], q - 1, 0))
            o_ref[k] = (b + prev * w[0:1] + z[k] * w[1:2] + nxt * w[2:3]).astype(o_ref.dtype)


def _short_conv(z, conv_w, conv_b, tc):
    bsz, rdx, q, w3 = z.shape
    w = w3 // 3
    nc = w // tc
    zs = lambda k: pl.BlockSpec((None, rdx, q, tc), lambda bi, j: (bi, 0, 0, k * nc + j))
    ws = lambda k: pl.BlockSpec((8, tc), lambda bi, j: (0, k * nc + j))
    bs = lambda k: pl.BlockSpec((1, tc), lambda bi, j: (0, k * nc + j))
    out = pl.BlockSpec((None, rdx, q, tc), lambda bi, j: (bi, 0, 0, j))
    return pl.pallas_call(
        _short_conv_kernel,
        grid=(bsz, nc),
        in_specs=[zs(0), zs(1), zs(2), ws(0), ws(1), ws(2), bs(0), bs(1), bs(2)],
        out_specs=[out, out, out],
        out_shape=[jax.ShapeDtypeStruct((bsz, rdx, q, w), BF16)] * 3,
        compiler_params=_params("parallel", "parallel"),
        name="short_conv",
    )(z, z, z, conv_w, conv_w, conv_w, conv_b, conv_b, conv_b)


def _filter_kernel(z_ref, w1_ref, b1_ref, f1_ref, w2_ref, b2_ref, f2_ref, w3_ref, dl_ref, bw_ref, o_ref, a_scr):
    n = z_ref.shape[0]

    @pl.when(pl.program_id(0) == 0)
    def _():
        a = jnp.sin(f1_ref[...] * (jnp.dot(z_ref[...], w1_ref[...], precision=HIGHEST,
                                           preferred_element_type=F32) + b1_ref[...]))
        a_scr[...] = jnp.sin(f2_ref[...] * (jnp.dot(a, w2_ref[...], precision=HIGHEST,
                                                    preferred_element_type=F32) + b2_ref[...]))

    hf = jnp.dot(a_scr[...], w3_ref[...], precision=HIGHEST, preferred_element_type=F32)
    q = n // DFT_RADIX
    row = lax.broadcasted_iota(jnp.int32, (n, 1), 0)
    assert q & (q - 1) == 0
    tap = DFT_RADIX * (row & (q - 1)) + lax.shift_right_logical(row, q.bit_length() - 1)
    t = tap.astype(F32) * (1.0 / (n - 1))
    hf = hf * jnp.exp(-t * dl_ref[...])
    hf = jnp.where((tap == 0) & (bw_ref[...] > 0.5), 0.0, hf)
    for k in range(DFT_RADIX):
        o_ref[k] = hf[k * q:(k + 1) * q]


def _hyena_filters(z, w1, b1, f1, w2, b2, f2, w3, deltas, is_bwd, tn):
    n = z.shape[0]
    cols = w3.shape[1]
    full = lambda a: pl.BlockSpec(a.shape, lambda j: (0, 0))
    tile = lambda r: pl.BlockSpec((r, tn), lambda j: (0, j))
    return pl.pallas_call(
        _filter_kernel,
        grid=(cols // tn,),
        in_specs=[full(z), full(w1), full(b1), full(f1), full(w2), full(b2), full(f2),
                  tile(w3.shape[0]), tile(1), tile(1)],
        out_specs=pl.BlockSpec((DFT_RADIX, n // DFT_RADIX, tn), lambda j: (0, 0, j)),
        out_shape=jax.ShapeDtypeStruct((DFT_RADIX, n // DFT_RADIX, cols), F32),
        scratch_shapes=[pltpu.VMEM((n, LANES), F32)],
        compiler_params=_params("arbitrary"),
        name="hyena_filters",
    )(z, w1, b1, f1, w2, b2, f2, w3, deltas, is_bwd)


def _dft_groups():
    groups = [(0, 1), (DFT_RADIX // 2, -1)]
    for m in range(1, DFT_RADIX // 2):
        groups += [(m, -1), (m, 1)]
    return groups


def _dft_coef(g, k):
    m, b = _dft_groups()[g]
    assert DFT_RADIX in (2, 4)
    quarter = (4 * m * k // DFT_RADIX) % 4
    return (1, 0, -1, 0)[quarter], (0, 1, 0, -1)[quarter], b


def _signed_sum(terms):
    acc = None
    for coef, thunk in terms:
        if coef == 0:
            continue
        val = thunk()
        if acc is None:
            acc = val if coef > 0 else -val
        else:
            acc = acc + val if coef > 0 else acc - val
    return acc


def _spectrum(cprod, sprod, want_c=True, want_s=True):
    cprod = functools.lru_cache(maxsize=None)(cprod)
    sprod = functools.lru_cache(maxsize=None)(sprod)
    out = []
    for g in range(DFT_RADIX):
        cu, su = [], []
        for k in range(DFT_RADIX):
            ca, sa, b = _dft_coef(g, k)
            cu += [(ca, functools.partial(cprod, k)), (-b * sa, functools.partial(sprod, k))]
            su += [(sa, functools.partial(cprod, k)), (b * ca, functools.partial(sprod, k))]
        out.append((_signed_sum(cu) if want_c else None, _signed_sum(su) if want_s else None))
    return out


def _dft_tables(n):
    q = n // DFT_RADIX
    r = 1 << (q.bit_length() // 2)
    f = jnp.arange(q, dtype=jnp.int32)[:, None]
    idx = jnp.arange(r, dtype=jnp.int32)[None, :]
    ang = lambda s: (((2 * f + 1) * s) % (4 * n)).astype(F32) * (math.pi / (2 * n))
    hi = ang(DFT_RADIX * r * idx)[:, :q // r]
    ch, sh = jnp.cos(hi), jnp.sin(hi)
    fwd, inv = [], []
    for trig in ("cos", "sin"):
        for k in range(DFT_RADIX):
            lo = ang(DFT_RADIX * idx + k)
            cl, sl = jnp.cos(lo), jnp.sin(lo)
            a, b, sign = (ch, sh, -1.0) if trig == "cos" else (sh, ch, 1.0)
            fwd.append((a[:, :, None] * cl[:, None, :] + sign * b[:, :, None] * sl[:, None, :]).reshape(q, q))
            inv.append((a.T[:, None, :] * cl.T[None, :, :] + sign * b.T[:, None, :] * sl.T[None, :, :]).reshape(q, q))
    shape = (2, DFT_RADIX, q, q)
    return jnp.stack(fwd).astype(BF16).reshape(shape), jnp.stack(inv).astype(BF16).reshape(shape)


def _dft_filter_kernel(m_ref, hf_ref, hb_ref, kr_ref, ki_ref, sum_scr, dif_scr, *, scale):
    @pl.when(pl.program_id(2) == 0)
    def _():
        hf = hf_ref[...]
        hb = hb_ref[...]
        sum_scr[...] = (hf + hb).astype(BF16)
        dif_scr[...] = (hb - hf).astype(BF16)

    of_sum = _spectrum(lambda k: _dot(m_ref[0, k], sum_scr[k]), lambda k: _dot(m_ref[1, k], sum_scr[k]),
                       want_s=False)
    of_dif = _spectrum(lambda k: _dot(m_ref[0, k], dif_scr[k]), lambda k: _dot(m_ref[1, k], dif_scr[k]),
                       want_c=False)
    for g in range(DFT_RADIX):
        kr_ref[g] = of_sum[g][0] * scale
        ki_ref[g] = of_dif[g][1] * scale


def _dft_filter(fwd, filt, width, tf, tn):
    rdx, q = fwd.shape[1], fwd.shape[2]
    nc = width // tn
    out = pl.BlockSpec((None, rdx, tf, tn), lambda o, j, i: (o, 0, i, j))
    return pl.pallas_call(
        functools.partial(_dft_filter_kernel, scale=1.0 / (rdx * q)),
        grid=(HYENA_ORDER, nc, q // tf),
        in_specs=[pl.BlockSpec((2, rdx, tf, q), lambda o, j, i: (0, 0, i, 0)),
                  pl.BlockSpec((rdx, q, tn), lambda o, j, i: (0, 0, (2 * o) * nc + j)),
                  pl.BlockSpec((rdx, q, tn), lambda o, j, i: (0, 0, (2 * o + 1) * nc + j))],
        out_specs=[out, out],
        out_shape=[jax.ShapeDtypeStruct((HYENA_ORDER, rdx, q, width), F32)] * 2,
        scratch_shapes=[pltpu.VMEM((rdx, q, tn), BF16), pltpu.VMEM((rdx, q, tn), BF16)],
        compiler_params=_params("parallel", "parallel", "arbitrary"),
        name="dft_filter",
    )(fwd, filt, filt)


def _dft_fwd_kernel(m_ref, u_ref, kr_ref, ki_ref, pr_ref, pi_ref):
    spec = _spectrum(lambda k: _dot(m_ref[0, k], u_ref[k]), lambda k: _dot(m_ref[1, k], u_ref[k]))
    for g, (ur, us) in enumerate(spec):
        kr = kr_ref[g]
        ki = ki_ref[g]
        pr_ref[g] = (ur * kr + us * ki).astype(pr_ref.dtype)
        pi_ref[g] = (ur * ki - us * kr).astype(pi_ref.dtype)


def _dft_fwd(fwd, u, kr, ki, order, tf, tn):
    bsz, rdx, q, width = u.shape
    out = pl.BlockSpec((None, rdx, tf, tn), lambda bi, j, i: (bi, 0, i, j))
    ks = pl.BlockSpec((None, rdx, tf, tn), lambda bi, j, i: (order, 0, i, j))
    return pl.pallas_call(
        _dft_fwd_kernel,
        grid=(bsz, width // tn, q // tf),
        in_specs=[pl.BlockSpec((2, rdx, tf, q), lambda bi, j, i: (0, 0, i, 0)),
                  pl.BlockSpec((None, rdx, q, tn), lambda bi, j, i: (bi, 0, 0, j)),
                  ks, ks],
        out_specs=[out, out],
        out_shape=[jax.ShapeDtypeStruct((bsz, rdx, q, width), BF16)] * 2,
        compiler_params=_params("parallel", "parallel", "parallel"),
        name="dft_fwd",
    )(fwd, u, kr, ki)


def _dft_inv_kernel(m_ref, pr_ref, pi_ref, u_ref, b_ref, x_ref, o_ref, ra_scr, rb_scr):
    for k in range(DFT_RADIX):
        @pl.when((pl.program_id(3) == 0) & (pl.program_id(2) == k))
        def _(k=k):
            ra, rb = [], []
            for g in range(DFT_RADIX):
                ca, sa, b = _dft_coef(g, k)
                pr = functools.partial(lambda g: pr_ref[g].astype(F32), g)
                pi = functools.partial(lambda g: pi_ref[g].astype(F32), g)
                ra += [(ca, pr), (-sa, pi)]
                rb += [(b * sa, pr), (b * ca, pi)]
            ra_scr[...] = _signed_sum(ra).astype(BF16)
            rb_scr[...] = _signed_sum(rb).astype(BF16)

    y = _dot(m_ref[0], ra_scr[...]) - _dot(m_ref[1], rb_scr[...])
    u = u_ref[...].astype(F32)
    o_ref[...] = (x_ref[...] * (y + u * b_ref[...])).astype(o_ref.dtype)


def _dft_inv(inv, pr, pi, u, bias, x, order, tt, tn):
    bsz, rdx, q, width = u.shape
    tile = pl.BlockSpec((None, None, tt, tn), lambda bi, j, k, i: (bi, k, i, j))
    spec = pl.BlockSpec((None, rdx, q, tn), lambda bi, j, k, i: (bi, 0, 0, j))
    return pl.pallas_call(
        _dft_inv_kernel,
        grid=(bsz, width // tn, rdx, q // tt),
        in_specs=[pl.BlockSpec((2, None, tt, q), lambda bi, j, k, i: (0, k, i, 0)),
                  spec, spec, tile,
                  pl.BlockSpec((None, 1, tn), lambda bi, j, k, i: (order, 0, j)),
                  tile],
        out_specs=tile,
        out_shape=jax.ShapeDtypeStruct((bsz, rdx, q, width), BF16),
        scratch_shapes=[pltpu.VMEM((q, tn), BF16), pltpu.VMEM((q, tn), BF16)],
        compiler_params=_params("parallel", "parallel", "arbitrary", "arbitrary"),
        name="dft_inv",
    )(inv, pr, pi, u, bias, x)


def _filter_features(n):
    t = jnp.linspace(0.0, 1.0, n, dtype=F32)[:, None]
    w = 2.0 * math.pi * jnp.arange(n, dtype=F32) / n
    f = jnp.linspace(1e-4, FILTER_BANDS - 1, FILTER_BANDS, dtype=F32)
    z = jnp.concatenate([t, jnp.cos(w[:, None] * f), -jnp.sin(w[:, None] * f)], axis=-1)
    return jnp.pad(z, ((0, 0), (0, LANES - FILTER_EMB)))


def _pad_to(a, shape):
    return jnp.pad(a, [(0, t - s) for s, t in zip(a.shape, shape)])


def kernel(x, c, ctx, c_ctx, ada_w, ada_b, norm1_g, norm2_g, attn_w_in, attn_w_out, diff_q_g, diff_k_g, diff_lq1, diff_lk1, diff_lq2, diff_lk2, diff_sub_g, swa_q_g, swa_k_g, swa_sink, hy_w_in, hy_b_in, hy_conv_w, hy_conv_b, flt_w1, flt_b1, flt_f1, flt_w2, flt_b2, flt_f2, flt_w3, hy_bias, hy_w_out, hy_b_out, moe_wg1, moe_bg1, moe_wg2, moe_bg2, moe_w_gate, moe_w_up, moe_w_down):
    bsz, seq, d = x.shape
    depth = ada_w.shape[0]
    n_ctx = ctx.shape[1]
    cond = jnp.concatenate([c, c_ctx[None, :], jnp.zeros((8 - bsz - 1, d), F32)], axis=0)
    ada_b3 = ada_b[:, None, :]
    zero_bias = lambda n: jnp.zeros((1, n), F32)
    expert_w = None
    tm = min(512, seq)

    for layer in range(depth):
        even = layer % 2 == 0
        i = layer // 2
        ctx_live = any(j % 2 == 0 for j in range(layer + 1, depth))
        assert not ctx_live, "context-stream update is only needed for deeper stacks"
        mod = _adaln_mod(cond, ada_w, ada_b3, layer)
        sh1, sc1, g1, sh2, sc2, g2 = [mod[:, k * d:(k + 1) * d] for k in range(6)]
        lat = lambda a: a[:bsz, None, :]
        cmod = lambda a: jnp.broadcast_to(a[bsz:bsz + 1, None, :], (bsz, 1, d))
        n1 = norm1_g[layer][None, :]
        if even:
            w_in = attn_w_in[i].astype(BF16)
            qkv = _norm_linear(x, n1, lat(sc1), lat(sh1), w_in, zero_bias(ATTN_IN), tm, 1536, "attn_in")
            qkv_c = _norm_linear(ctx, n1, cmod(sc1), cmod(sh1), w_in, zero_bias(ATTN_IN), min(tm, n_ctx), 1536,
                                 "attn_in_ctx")
            lam_init = 0.8 - 0.6 * math.exp(-0.3 * layer)
            lamp = _pad_to(jnp.stack([diff_lq1[i], diff_lk1[i], diff_lq2[i], diff_lk2[i]]), (8, 2 * DIFF_SUB))
            two = lambda a: jnp.tile(a, 2)[None, :]
            to_cast = [] if expert_w is not None else [moe_w_gate, moe_w_up, moe_w_down]
            o_diff, cast = _diff_attention(qkv, qkv_c, _rope_tables(seq, DIFF_SUB, 2 * DIFF_SUB), two(diff_q_g[i]),
                                           two(diff_k_g[i]), diff_sub_g[i][None, :], lamp, lam_init,
                                           min(2 * Q_BLOCK, seq), to_cast)
            if cast:
                expert_w = (cast[0].reshape(depth * MOE_EXPERTS, d, -1), cast[1].reshape(depth * MOE_EXPERTS, d, -1),
                            cast[2].reshape(depth * MOE_EXPERTS, -1, d))
            sink_rows = jnp.broadcast_to(swa_sink[i][:, None], (SWA_HEADS, HEAD_DIM))
            o_swa = _swa_attention(qkv, qkv_c, _rope_tables(seq, HEAD_DIM, HEAD_DIM), swa_q_g[i][None, :],
                                   swa_k_g[i][None, :], sink_rows)
            w_out = attn_w_out[i].astype(BF16)
            x = _mm_residual([o_diff, o_swa], [w_out[:DIFF_V], w_out[DIFF_V:]], zero_bias(d), x, lat(g1), tm,
                             "attn_out")
        else:
            width = hy_w_out.shape[1]
            q = seq // DFT_RADIX
            z = _norm_linear(x, n1, lat(sc1), lat(sh1), hy_w_in[i].astype(BF16), hy_b_in[i][None, :], tm, 1536,
                             "hyena_in", classes=DFT_RADIX)
            v, x1, x2 = _short_conv(z, _pad_to(hy_conv_w[i], (8, 3 * width)), hy_conv_b[i][None, :], 128)
            max_decay = math.log(DECAY_TARGET) / DECAY_FAST
            min_decay = math.log(DECAY_TARGET) / DECAY_SLOW
            deltas = jnp.abs(jnp.tile(jnp.linspace(min_decay, max_decay, width, dtype=F32), 2 * HYENA_ORDER))[None, :]
            is_bwd = jnp.tile(jnp.concatenate([jnp.zeros((width,), F32), jnp.ones((width,), F32)]),
                              HYENA_ORDER)[None, :]
            sq = (LANES, LANES)
            feats = _filter_features(seq)
            feats = jnp.concatenate([feats[k::DFT_RADIX] for k in range(DFT_RADIX)], axis=0)
            filt = _hyena_filters(
                feats, _pad_to(flt_w1[i], sq), _pad_to(flt_b1[i][None, :], (1, LANES)),
                _pad_to(flt_f1[i][None, :], (1, LANES)), _pad_to(flt_w2[i], sq),
                _pad_to(flt_b2[i][None, :], (1, LANES)), _pad_to(flt_f2[i][None, :], (1, LANES)),
                _pad_to(flt_w3[i], (LANES, 2 * HYENA_ORDER * width)), deltas, is_bwd, 256)
            fwd, inv = _dft_tables(seq)
            tf = min(256, q)
            tn = min(512, width)
            kr, ki = _dft_filter(fwd, filt, width, tf, min(256, width))
            bias = hy_bias[i][:, None, :]
            pr, pi = _dft_fwd(fwd, v, kr, ki, 0, tf, tn)
            tt = min(1024, q)
            y1 = _dft_inv(inv, pr, pi, v, bias, x1, 0, tt, tn)
            pr, pi = _dft_fwd(fwd, y1, kr, ki, 1, tf, tn)
            y2 = _dft_inv(inv, pr, pi, y1, bias, x2, 1, tt, tn)
            x = _mm_residual([y2], [hy_w_out[i].astype(BF16)], hy_b_out[i][None, :], x, lat(g1), tm,
                             "hyena_out", classes=DFT_RADIX)
        w_route = _pad_to(jnp.concatenate([moe_wg1[layer], moe_wg2[layer]], axis=1), (d, LANES))
        b_route = _pad_to(jnp.concatenate([moe_bg1[layer], moe_bg2[layer]])[None, :], (1, LANES))
        x = _hier_moe(x, norm2_g[layer][None, :], lat(sc2), lat(sh2), lat(g2), w_route, b_route,
                      *expert_w, layer)
    return x
```

```python
import functools
import math

import jax
import jax.numpy as jnp
from jax import lax
from jax.experimental import pallas as pl
from jax.experimental.pallas import tpu as pltpu

F32 = jnp.float32
BF16 = jnp.bfloat16
HIGHEST = lax.Precision.HIGHEST

LANES = 128
V7X_VMEM_BYTES = 64 * 1024 * 1024
VMEM_LIMIT = V7X_VMEM_BYTES * 7 // 8

GRID_W = 64
HEAD_DIM = 128
ROPE_BASE = 10000.0
EPS = 1e-6
Q_BLOCK = 128
DIFF_HEADS = 8
DIFF_SUB = HEAD_DIM // 2
SWA_HEADS = 8
SWA_KV_HEADS = 2
SWA_GROUP = SWA_HEADS // SWA_KV_HEADS
WINDOW = 128
DIFF_Q = DIFF_HEADS * 2 * DIFF_SUB
DIFF_V = DIFF_HEADS * HEAD_DIM
SWA_Q = SWA_HEADS * HEAD_DIM
SWA_KV = SWA_KV_HEADS * HEAD_DIM
ATTN_IN = 2 * DIFF_Q + DIFF_V + SWA_Q + 2 * SWA_KV
HYENA_ORDER = 2
SHORT_CONV = 3
FILTER_EMB = 33
FILTER_BANDS = (FILTER_EMB - 1) // 2
FILTER_HIDDEN = 64
DECAY_FAST = 0.3
DECAY_SLOW = 1.5
DECAY_TARGET = 1e-2
MOE_GROUPS = 4
MOE_PER_GROUP = 8
MOE_EXPERTS = MOE_GROUPS * MOE_PER_GROUP
MOE_TOPK = 2
MOE_BLOCK = 256
FF_CHUNK = 256
DIFF_KEY_CHUNK = 512
DFT_RADIX = 4
LOG2_E = 1.4426950408889634
NEG_INF = float("-inf")


def _tile(n, pref):
    t = pref
    while n % t:
        t //= 2
    return t


def _params(*sem):
    return pltpu.CompilerParams(dimension_semantics=sem, vmem_limit_bytes=VMEM_LIMIT)


def _dot(a, b):
    return jnp.dot(a, b, preferred_element_type=F32)


def _dot_nt(a, b):
    return lax.dot_general(a, b, (((1,), (1,)), ((), ())), preferred_element_type=F32)


def _mod_kernel(c_ref, w_ref, b_ref, o_ref):
    c = c_ref[...]
    s = c * jax.nn.sigmoid(c)
    o_ref[...] = jnp.dot(s, w_ref[...], precision=HIGHEST, preferred_element_type=F32) + b_ref[...]


def _adaln_mod(cond, ada_w, ada_b, layer):
    rows, d = cond.shape
    n = ada_w.shape[2]
    tn = _tile(n, 768)
    return pl.pallas_call(
        _mod_kernel,
        grid=(n // tn,),
        in_specs=[pl.BlockSpec((rows, d), lambda j: (0, 0)),
                  pl.BlockSpec((None, d, tn), lambda j: (layer, 0, j)),
                  pl.BlockSpec((None, 1, tn), lambda j: (layer, 0, j))],
        out_specs=pl.BlockSpec((rows, tn), lambda j: (0, j)),
        out_shape=jax.ShapeDtypeStruct((rows, n), F32),
        compiler_params=_params("parallel"),
        name="adaln_mod",
    )(cond, ada_w, ada_b)


def _norm_mod(x, g, sc, sh):
    inv = lax.rsqrt(jnp.mean(x * x, axis=-1, keepdims=True) + EPS)
    return ((x * inv) * g) * (1.0 + sc) + sh


def _class_perm(tm, classes):
    m = tm // classes
    out_row = jnp.arange(tm, dtype=jnp.int32)
    src = classes * (out_row % m) + out_row // m
    return (src[:, None] == jnp.arange(tm, dtype=jnp.int32)[None, :]).astype(BF16)


def _norm_linear_kernel(*refs, classes):
    if classes:
        p_ref, x_ref, g_ref, sc_ref, sh_ref, w_ref, b_ref, o_ref, h_ref = refs
    else:
        x_ref, g_ref, sc_ref, sh_ref, w_ref, b_ref, o_ref, h_ref = refs

    @pl.when(pl.program_id(2) == 0)
    def _():
        h = _norm_mod(x_ref[...], g_ref[...], sc_ref[...], sh_ref[...]).astype(BF16)
        if classes:
            h = _dot(p_ref[...], h).astype(BF16)
        h_ref[...] = h

    res = (_dot(h_ref[...], w_ref[...]) + b_ref[...]).astype(o_ref.dtype)
    if classes:
        m = res.shape[0] // classes
        for k in range(classes):
            o_ref[k] = res[k * m:(k + 1) * m]
    else:
        o_ref[...] = res


def _norm_linear(x, g, sc, sh, w, b, tm, tn, name, classes=0):
    bsz, l, d = x.shape
    n = w.shape[1]
    tn = _tile(n, tn)
    in_specs = [pl.BlockSpec((None, tm, d), lambda bi, i, j: (bi, i, 0)),
                pl.BlockSpec((1, d), lambda bi, i, j: (0, 0)),
                pl.BlockSpec((None, 1, d), lambda bi, i, j: (bi, 0, 0)),
                pl.BlockSpec((None, 1, d), lambda bi, i, j: (bi, 0, 0)),
                pl.BlockSpec((d, tn), lambda bi, i, j: (0, j)),
                pl.BlockSpec((1, tn), lambda bi, i, j: (0, j))]
    args = (x, g, sc, sh, w, b)
    if classes:
        in_specs = [pl.BlockSpec((tm, tm), lambda bi, i, j: (0, 0))] + in_specs
        args = (_class_perm(tm, classes),) + args
        out_specs = pl.BlockSpec((None, classes, tm // classes, tn), lambda bi, i, j: (bi, 0, i, j))
        out_shape = jax.ShapeDtypeStruct((bsz, classes, l // classes, n), BF16)
    else:
        out_specs = pl.BlockSpec((None, tm, tn), lambda bi, i, j: (bi, i, j))
        out_shape = jax.ShapeDtypeStruct((bsz, l, n), BF16)
    return pl.pallas_call(
        functools.partial(_norm_linear_kernel, classes=classes),
        grid=(bsz, l // tm, n // tn),
        in_specs=in_specs,
        out_specs=out_specs,
        out_shape=out_shape,
        scratch_shapes=[pltpu.VMEM((tm, d), BF16)],
        compiler_params=_params("parallel", "arbitrary", "arbitrary"),
        name=name,
    )(*args)


def _mm_res_kernel(*refs, n_pairs, classes):
    if classes:
        pt_ref, refs = refs[0], refs[1:]
    a_refs = refs[:n_pairs]
    w_refs = refs[n_pairs:2 * n_pairs]
    b_ref, x_ref, g_ref, o_ref = refs[2 * n_pairs:]
    acc = None
    for a_ref, w_ref in zip(a_refs, w_refs):
        if classes:
            a = jnp.concatenate([a_ref[k] for k in range(classes)], axis=0)
            a = _dot(pt_ref[...], a).astype(BF16)
        else:
            a = a_ref[...]
        prod = _dot(a, w_ref[...])
        acc = prod if acc is None else acc + prod
    o_ref[...] = x_ref[...] + g_ref[...] * (acc + b_ref[...])


def _mm_residual(a_list, w_list, b, x, g, tm, name, classes=0):
    bsz, l, n = x.shape
    n_pairs = len(a_list)
    if classes:
        in_specs = [pl.BlockSpec((tm, tm), lambda bi, i: (0, 0))]
        in_specs += [pl.BlockSpec((None, classes, tm // classes, a.shape[3]), lambda bi, i: (bi, 0, i, 0))
                     for a in a_list]
        args = (_class_perm(tm, classes).T,)
    else:
        in_specs = [pl.BlockSpec((None, tm, a.shape[2]), lambda bi, i: (bi, i, 0)) for a in a_list]
        args = ()
    in_specs += [pl.BlockSpec(w.shape, lambda bi, i: (0, 0)) for w in w_list]
    in_specs += [pl.BlockSpec((1, n), lambda bi, i: (0, 0)),
                 pl.BlockSpec((None, tm, n), lambda bi, i: (bi, i, 0)),
                 pl.BlockSpec((None, 1, n), lambda bi, i: (bi, 0, 0))]
    return pl.pallas_call(
        functools.partial(_mm_res_kernel, n_pairs=n_pairs, classes=classes),
        grid=(bsz, l // tm),
        in_specs=in_specs,
        out_specs=pl.BlockSpec((None, tm, n), lambda bi, i: (bi, i, 0)),
        out_shape=jax.ShapeDtypeStruct((bsz, l, n), F32),
        compiler_params=_params("parallel", "parallel"),
        name=name,
    )(*args, *a_list, *w_list, b, x, g)


def _rope_tables(n_tokens, dim, width):
    n_rows = n_tokens // GRID_W
    row = jnp.repeat(jnp.arange(n_rows, dtype=F32), GRID_W)
    col = jnp.tile(jnp.arange(GRID_W, dtype=F32), n_rows)
    n_freq = dim // 4
    inv = ROPE_BASE ** (-jnp.arange(n_freq, dtype=F32) / n_freq)
    ang = jnp.concatenate([row[:, None] * inv, col[:, None] * inv], axis=-1)
    ang = jnp.repeat(ang, 2, axis=-1)
    ang = jnp.tile(ang, (1, width // dim))
    sign = jnp.where(jnp.arange(width) % 2 == 0, -1.0, 1.0).astype(F32)
    return jnp.cos(ang), jnp.sin(ang) * sign


def _rope(x, cos_t, sin_t):
    width = x.shape[-1]
    lane = lax.broadcasted_iota(jnp.int32, (1, width), 1)
    partner = jnp.where(lane % 2 == 0, pltpu.roll(x, width - 1, 1), pltpu.roll(x, 1, 1))
    return x * cos_t + partner * sin_t


def _sub_rms(x, g):
    lane = lax.broadcasted_iota(jnp.int32, (1, 2 * DIFF_SUB), 1)
    lo = lane < DIFF_SUB
    sq = x * x
    ss_lo = jnp.sum(jnp.where(lo, sq, 0.0), axis=-1, keepdims=True)
    ss_hi = jnp.sum(jnp.where(lo, 0.0, sq), axis=-1, keepdims=True)
    inv = jnp.where(lo, lax.rsqrt(ss_lo / DIFF_SUB + EPS), lax.rsqrt(ss_hi / DIFF_SUB + EPS))
    return (x * inv) * g


def _rms(x, g):
    return (x * lax.rsqrt(jnp.mean(x * x, axis=-1, keepdims=True) + EPS)) * g


def _diff_attn_kernel(*refs, lam_init, n_ctx, tq, n_cast, n_zero):
    (q_ref, k_ref, v_ref, kc_ref, vc_ref, cos_ref, sin_ref, gq_ref, gk_ref, gs_ref, lamp_ref) = refs[:11]
    cast_in = refs[11:11 + n_cast]
    o_ref = refs[11 + n_cast]
    cast_out = refs[12 + n_cast:12 + 2 * n_cast]
    zero_out = refs[12 + 2 * n_cast:12 + 2 * n_cast + n_zero]
    q_scr, k_scr, v_scr, s_scr, lam_scr = refs[12 + 2 * n_cast + n_zero:]
    for src, dst in zip(cast_in, cast_out):
        dst[...] = src[...].astype(dst.dtype)
    for dst in zero_out:
        dst[...] = jnp.zeros(dst.shape, dst.dtype)
    n_keys = k_scr.shape[0]

    @pl.when(pl.program_id(2) == 0)
    def _():
        lane = lax.broadcasted_iota(jnp.int32, (1, 2 * DIFF_SUB), 1)
        lo = lane < DIFF_SUB
        gk = gk_ref[...]
        cos_t = cos_ref[...]
        sin_t = sin_ref[...]
        k_scr[0:n_ctx, :] = _sub_rms(kc_ref[...].astype(F32), gk).astype(BF16)
        k_scr[n_ctx:, :] = _rope(_sub_rms(k_ref[...].astype(F32), gk), cos_t, sin_t).astype(BF16)
        v_scr[0:n_ctx, 0:HEAD_DIM] = vc_ref[...].astype(BF16)
        v_scr[n_ctx:, 0:HEAD_DIM] = v_ref[...].astype(BF16)
        v_scr[:, HEAD_DIM:] = jnp.broadcast_to(jnp.where(lane == 0, 1.0, 0.0).astype(BF16), (n_keys, HEAD_DIM))
        q = _rope(_sub_rms(q_ref[...].astype(F32), gq_ref[...]), cos_t, sin_t) * (DIFF_SUB ** -0.5 * LOG2_E)
        q_scr[0] = jnp.where(lo, q, 0.0).astype(BF16)
        q_scr[1] = jnp.where(lo, 0.0, q).astype(BF16)
        lp = lamp_ref[...]
        t1 = jnp.sum(lp[0:1] * lp[1:2], axis=-1, keepdims=True)
        t2 = jnp.sum(lp[2:3] * lp[3:4], axis=-1, keepdims=True)
        lam_scr[...] = jnp.broadcast_to(jnp.exp(t1) - jnp.exp(t2) + lam_init, lam_scr.shape)

    chunks = [(c0, min(c0 + DIFF_KEY_CHUNK, n_keys)) for c0 in range(0, n_keys, DIFF_KEY_CHUNK)]
    lam = lam_scr[:, 0:1]
    for t in range(tq // Q_BLOCK):
        rows = pl.ds(pl.multiple_of(pl.program_id(2) * tq + t * Q_BLOCK, Q_BLOCK), Q_BLOCK)
        qs = [q_scr[0, rows, :], q_scr[1, rows, :]]

        mx = [jnp.full((Q_BLOCK, LANES), NEG_INF, F32) for _ in range(2)]
        for c0, c1 in chunks:
            kc = k_scr[c0:c1, :]
            for h in range(2):
                s = _dot_nt(qs[h], kc)
                s_scr[2 * t + h, :, c0:c1] = s
                for j in range((c1 - c0) // LANES):
                    mx[h] = jnp.maximum(mx[h], s[:, j * LANES:(j + 1) * LANES])
        mrow = [jnp.max(m, axis=-1, keepdims=True) for m in mx]

        out = [jnp.zeros((Q_BLOCK, 2 * HEAD_DIM), F32) for _ in range(2)]
        for c0, c1 in chunks:
            vc = v_scr[c0:c1, :]
            for h in range(2):
                e = jnp.exp2(s_scr[2 * t + h, :, c0:c1] - mrow[h])
                out[h] = out[h] + _dot(e.astype(BF16), vc)
        r1 = 1.0 / out[0][:, HEAD_DIM:HEAD_DIM + 1]
        r2 = lam / out[1][:, HEAD_DIM:HEAD_DIM + 1]
        o = out[0][:, 0:HEAD_DIM] * r1 - out[1][:, 0:HEAD_DIM] * r2
        o_ref[t * Q_BLOCK:(t + 1) * Q_BLOCK, :] = (_rms(o, gs_ref[...]) * (1.0 - lam_init)).astype(o_ref.dtype)


def _diff_attention(qkv, qkv_c, tabs, gq, gk, gs, lamp, lam_init, tq, to_cast, zero_shapes):
    bsz, l, _ = qkv.shape
    n_ctx = qkv_c.shape[1]
    cos_t, sin_t = tabs
    hb = 2 * DIFF_SUB
    kb = DIFF_Q // hb
    vb = 2 * DIFF_Q // hb
    nq = l // tq
    n_steps = bsz * DIFF_HEADS * nq
    slabs = [a.reshape(n_steps, -1, a.shape[-1]) for a in to_cast]
    slab_spec = lambda a: pl.BlockSpec((None,) + a.shape[1:], lambda bi, h, i: ((bi * DIFF_HEADS + h) * nq + i, 0, 0))
    zero_slabs = [jax.ShapeDtypeStruct((n_steps, rows // n_steps, cols), F32) for rows, cols in zero_shapes]
    kern = functools.partial(_diff_attn_kernel, lam_init=lam_init, n_ctx=n_ctx, tq=tq, n_cast=len(slabs),
                             n_zero=len(zero_slabs))
    vec = pl.BlockSpec((1, hb), lambda bi, h, i: (0, 0))
    outs = pl.pallas_call(
        kern,
        grid=(bsz, DIFF_HEADS, nq),
        in_specs=[pl.BlockSpec((None, l, hb), lambda bi, h, i: (bi, 0, h)),
                  pl.BlockSpec((None, l, hb), lambda bi, h, i: (bi, 0, kb + h)),
                  pl.BlockSpec((None, l, hb), lambda bi, h, i: (bi, 0, vb + h)),
                  pl.BlockSpec((None, n_ctx, hb), lambda bi, h, i: (bi, 0, kb + h)),
                  pl.BlockSpec((None, n_ctx, hb), lambda bi, h, i: (bi, 0, vb + h)),
                  pl.BlockSpec((l, hb), lambda bi, h, i: (0, 0)),
                  pl.BlockSpec((l, hb), lambda bi, h, i: (0, 0)),
                  vec, vec, vec,
                  pl.BlockSpec((8, hb), lambda bi, h, i: (0, 0))] + [slab_spec(a) for a in slabs],
        out_specs=[pl.BlockSpec((None, tq, hb), lambda bi, h, i: (bi, i, h))]
        + [slab_spec(a) for a in slabs] + [slab_spec(a) for a in zero_slabs],
        out_shape=[jax.ShapeDtypeStruct((bsz, l, DIFF_V), BF16)]
        + [jax.ShapeDtypeStruct(a.shape, BF16) for a in slabs] + zero_slabs,
        scratch_shapes=[pltpu.VMEM((2, l, hb), BF16), pltpu.VMEM((n_ctx + l, hb), BF16),
                        pltpu.VMEM((n_ctx + l, 2 * HEAD_DIM), BF16),
                        pltpu.VMEM((2 * (tq // Q_BLOCK), Q_BLOCK, n_ctx + l), F32),
                        pltpu.VMEM((1, hb), F32)],
        compiler_params=_params("arbitrary", "arbitrary", "arbitrary"),
        name="diff_attention",
    )(qkv, qkv, qkv, qkv_c, qkv_c, cos_t, sin_t, gq, gk, gs, lamp, *slabs)
    casts = [o.reshape(a.shape) for o, a in zip(outs[1:1 + len(slabs)], to_cast)]
    zeros = [o.reshape(shape) for o, shape in zip(outs[1 + len(slabs):], zero_shapes)]
    return outs[0], casts, zeros


def _swa_kernel(q_ref, k_ref, v_ref, kc_ref, vc_ref, cos_ref, sin_ref, gq_ref, gk_ref, sink_ref,
                o_ref, q_scr, k_scr, v_scr, kc_scr, vc_scr, *, seq):
    kvh = pl.program_id(1)
    n = pl.program_id(2)
    span = Q_BLOCK + 2 * WINDOW

    @pl.when(n == 0)
    def _():
        gk = gk_ref[...]
        gq = gq_ref[...]
        cos_t = cos_ref[...]
        sin_t = sin_ref[...]
        zeros = jnp.zeros((WINDOW, HEAD_DIM), BF16)
        k_scr[0:WINDOW, :] = zeros
        v_scr[0:WINDOW, :] = zeros
        k_scr[WINDOW + seq:, :] = zeros
        v_scr[WINDOW + seq:, :] = zeros
        k_scr[WINDOW:WINDOW + seq, :] = _rope(_rms(k_ref[...].astype(F32), gk), cos_t, sin_t).astype(BF16)
        v_scr[WINDOW:WINDOW + seq, :] = v_ref[...].astype(BF16)
        kc_scr[...] = _rms(kc_ref[...].astype(F32), gk).astype(BF16)
        vc_scr[...] = vc_ref[...].astype(BF16)
        for j in range(SWA_GROUP):
            qj = q_ref[:, j * HEAD_DIM:(j + 1) * HEAD_DIM].astype(F32)
            qj = _rope(_rms(qj, gq), cos_t, sin_t) * (HEAD_DIM ** -0.5)
            q_scr[j] = qj.astype(BF16)

    kc = kc_scr[...]
    vc = vc_scr[...]
    rows = SWA_GROUP * Q_BLOCK
    ridx = lax.broadcasted_iota(jnp.int32, (rows, span), 0)
    kidx = lax.broadcasted_iota(jnp.int32, (rows, span), 1)
    rel = kidx - (ridx & (Q_BLOCK - 1))
    band = (rel >= 0) & (rel <= 2 * WINDOW)
    head = lax.broadcasted_iota(jnp.int32, (rows, 1), 0) // Q_BLOCK
    sink = jnp.zeros((rows, 1), F32)
    for j in range(SWA_GROUP):
        sink = jnp.where(head == j, sink_ref[pl.ds(kvh * SWA_GROUP + j, 1), 0:1], sink)
    for t in range(o_ref.shape[0] // Q_BLOCK):
        blk = n * (o_ref.shape[0] // Q_BLOCK) + t
        start = pl.multiple_of(blk * Q_BLOCK, Q_BLOCK)
        kw = k_scr[pl.ds(start, span), :]
        vw = v_scr[pl.ds(start, span), :]
        key_pos = blk * Q_BLOCK + kidx - WINDOW
        mask = band & (key_pos >= 0) & (key_pos < seq)
        q = jnp.concatenate([q_scr[j, pl.ds(start, Q_BLOCK), :] for j in range(SWA_GROUP)], axis=0)
        s = jnp.concatenate([_dot_nt(q, kc), jnp.where(mask, _dot_nt(q, kw), NEG_INF)], axis=-1)
        m = jnp.maximum(jnp.max(s, axis=-1, keepdims=True), sink)
        e = jnp.exp(s - m)
        r = 1.0 / (jnp.sum(e, axis=-1, keepdims=True) + jnp.exp(sink - m))
        o = _dot((e * r).astype(BF16), jnp.concatenate([vc, vw], axis=0))
        o_ref[t * Q_BLOCK:(t + 1) * Q_BLOCK, :] = jnp.concatenate(
            [o[j * Q_BLOCK:(j + 1) * Q_BLOCK] for j in range(SWA_GROUP)], axis=-1).astype(o_ref.dtype)


def _swa_attention(qkv, qkv_c, tabs, gq, gk, sink_rows):
    bsz, l, _ = qkv.shape
    n_ctx = qkv_c.shape[1]
    cos_t, sin_t = tabs
    gw = SWA_GROUP * HEAD_DIM
    q0 = (2 * DIFF_Q + DIFF_V) // gw
    k0 = (2 * DIFF_Q + DIFF_V + SWA_Q) // HEAD_DIM
    v0 = k0 + SWA_KV_HEADS
    vec = pl.BlockSpec((1, HEAD_DIM), lambda bi, h, i: (0, 0))
    return pl.pallas_call(
        functools.partial(_swa_kernel, seq=l),
        grid=(bsz, SWA_KV_HEADS, l // (2 * Q_BLOCK)),
        in_specs=[pl.BlockSpec((None, l, gw), lambda bi, h, i: (bi, 0, q0 + h)),
                  pl.BlockSpec((None, l, HEAD_DIM), lambda bi, h, i: (bi, 0, k0 + h)),
                  pl.BlockSpec((None, l, HEAD_DIM), lambda bi, h, i: (bi, 0, v0 + h)),
                  pl.BlockSpec((None, n_ctx, HEAD_DIM), lambda bi, h, i: (bi, 0, k0 + h)),
                  pl.BlockSpec((None, n_ctx, HEAD_DIM), lambda bi, h, i: (bi, 0, v0 + h)),
                  pl.BlockSpec((l, HEAD_DIM), lambda bi, h, i: (0, 0)),
                  pl.BlockSpec((l, HEAD_DIM), lambda bi, h, i: (0, 0)),
                  vec, vec,
                  pl.BlockSpec((SWA_HEADS, HEAD_DIM), lambda bi, h, i: (0, 0))],
        out_specs=pl.BlockSpec((None, 2 * Q_BLOCK, gw), lambda bi, h, i: (bi, i, h)),
        out_shape=jax.ShapeDtypeStruct((bsz, l, SWA_Q), BF16),
        scratch_shapes=[pltpu.VMEM((SWA_GROUP, l, HEAD_DIM), BF16),
                        pltpu.VMEM((l + 2 * WINDOW, HEAD_DIM), BF16), pltpu.VMEM((l + 2 * WINDOW, HEAD_DIM), BF16),
                        pltpu.VMEM((n_ctx, HEAD_DIM), BF16), pltpu.VMEM((n_ctx, HEAD_DIM), BF16)],
        compiler_params=_params("parallel", "parallel", "arbitrary"),
        name="swa_attention",
    )(qkv, qkv, qkv, qkv_c, qkv_c, cos_t, sin_t, gq, gk, sink_rows)


def _router_kernel(x_ref, g_ref, sc_ref, sh_ref, w_ref, b_ref, tri_ref, h_ref, e_ref, p_ref, cnt_ref, run_scr):
    @pl.when((pl.program_id(0) == 0) & (pl.program_id(1) == 0))
    def _():
        run_scr[...] = jnp.zeros(run_scr.shape, F32)

    h = _norm_mod(x_ref[...], g_ref[...], sc_ref[...], sh_ref[...])
    h_ref[...] = h
    logits = jnp.dot(h, w_ref[...], precision=HIGHEST, preferred_element_type=F32) + b_ref[...]
    lane_i = lax.broadcasted_iota(jnp.int32, (1, LANES), 1)
    lane = lane_i.astype(F32)
    big = float(LANES)
    lg = jnp.where(lane_i < MOE_GROUPS, logits, NEG_INF)
    mg = jnp.max(lg, axis=-1, keepdims=True)
    p_top = 1.0 / jnp.sum(jnp.exp(lg - mg), axis=-1, keepdims=True)
    grp = jnp.min(jnp.where(lg == mg, lane, big), axis=-1, keepdims=True)
    e_lane = lane - MOE_GROUPS
    in_grp = (e_lane >= grp * MOE_PER_GROUP) & (e_lane < (grp + 1.0) * MOE_PER_GROUP)
    le = jnp.where(in_grp, logits, NEG_INF)
    t1 = jnp.max(le, axis=-1, keepdims=True)
    i1 = jnp.min(jnp.where(le == t1, lane, big), axis=-1, keepdims=True)
    le2 = jnp.where(lane == i1, NEG_INF, le)
    t2 = jnp.max(le2, axis=-1, keepdims=True)
    i2 = jnp.min(jnp.where(le2 == t2, lane, big), axis=-1, keepdims=True)
    w2 = jnp.exp(t2 - t1)
    inv = p_top / (1.0 + w2)
    hot1 = lane == i1
    hot2 = lane == i2
    onehot = jnp.where(hot1 | hot2, 1.0, 0.0)
    before = run_scr[...] + _dot(tri_ref[...], onehot.astype(BF16))
    r1 = jnp.sum(jnp.where(hot1, before, 0.0), axis=-1, keepdims=True)
    r2 = jnp.sum(jnp.where(hot2, before, 0.0), axis=-1, keepdims=True)
    run_scr[...] = run_scr[...] + jnp.sum(onehot, axis=0, keepdims=True)
    cnt_ref[...] = run_scr[...]
    e_ref[...] = jnp.where(lane_i == 0, i1 - MOE_GROUPS,
                           jnp.where(lane_i == 1, i2 - MOE_GROUPS,
                                     jnp.where(lane_i == 2, r1, jnp.where(lane_i == 3, r2, 0.0)))).astype(jnp.int32)
    p_ref[...] = jnp.where(lane_i == 0, inv, jnp.where(lane_i == 1, inv * w2, 0.0))


def _router(x, g, sc, sh, w, b, tm):
    bsz, l, d = x.shape
    row = lambda bi, i: (bi, i, 0)
    tri = (jnp.arange(tm)[:, None] > jnp.arange(tm)[None, :]).astype(BF16)
    return pl.pallas_call(
        _router_kernel,
        grid=(bsz, l // tm),
        in_specs=[pl.BlockSpec((None, tm, d), row),
                  pl.BlockSpec((1, d), lambda bi, i: (0, 0)),
                  pl.BlockSpec((None, 1, d), lambda bi, i: (bi, 0, 0)),
                  pl.BlockSpec((None, 1, d), lambda bi, i: (bi, 0, 0)),
                  pl.BlockSpec((d, LANES), lambda bi, i: (0, 0)),
                  pl.BlockSpec((1, LANES), lambda bi, i: (0, 0)),
                  pl.BlockSpec((tm, tm), lambda bi, i: (0, 0))],
        out_specs=[pl.BlockSpec((None, tm, d), row),
                   pl.BlockSpec((None, tm, LANES), row),
                   pl.BlockSpec((None, tm, LANES), row),
                   pl.BlockSpec((1, LANES), lambda bi, i: (0, 0))],
        out_shape=[jax.ShapeDtypeStruct((bsz, l, d), F32),
                   jax.ShapeDtypeStruct((bsz, l, LANES), jnp.int32),
                   jax.ShapeDtypeStruct((bsz, l, LANES), F32),
                   jax.ShapeDtypeStruct((1, LANES), F32)],
        scratch_shapes=[pltpu.VMEM((1, LANES), F32)],
        compiler_params=_params("arbitrary", "arbitrary"),
        name="moe_router",
    )(x, g, sc, sh, w, b, tri)


def _dispatch_kernel(slot_ref, h_ref, init_ref, xb_ref, sem):
    del init_ref
    tm = h_ref.shape[0]
    base = pl.program_id(0) * tm

    def copy(r, k):
        dst = slot_ref[(base + r) * MOE_TOPK + k]
        return pltpu.make_async_copy(h_ref.at[pl.ds(r, 1)], xb_ref.at[pl.ds(dst, 1)], sem)

    def start(r, carry):
        for k in range(MOE_TOPK):
            copy(r, k).start(priority=k)
        return carry

    def wait(r, carry):
        for k in range(MOE_TOPK):
            copy(r, k).wait()
        return carry

    lax.fori_loop(0, tm, start, 0, unroll=8)
    lax.fori_loop(0, tm, wait, 0, unroll=8)


def _dispatch(slot, h, zeros, tm):
    n, d = h.shape
    n_rows = zeros.shape[0]
    grid_spec = pltpu.PrefetchScalarGridSpec(
        num_scalar_prefetch=1,
        grid=(n // tm,),
        in_specs=[pl.BlockSpec((tm, d), lambda i, s: (i, 0)),
                  pl.BlockSpec(memory_space=pl.ANY)],
        out_specs=pl.BlockSpec(memory_space=pl.ANY),
        scratch_shapes=[pltpu.SemaphoreType.DMA(())],
    )
    return pl.pallas_call(
        _dispatch_kernel,
        grid_spec=grid_spec,
        out_shape=jax.ShapeDtypeStruct((n_rows, d), F32),
        input_output_aliases={2: 0},
        compiler_params=_params("arbitrary"),
        name="moe_dispatch",
    )(slot, h, zeros)


def _combine_kernel(slot_ref, x_ref, g_ref, p_ref, yb_ref, o_ref, y_scr, sem):
    tm = x_ref.shape[0]
    base = pl.program_id(0) * tm

    def copy(r, k):
        src = slot_ref[(base + r) * MOE_TOPK + k]
        return pltpu.make_async_copy(yb_ref.at[pl.ds(src, 1)], y_scr.at[k, pl.ds(r, 1)], sem)

    def start(r, carry):
        for k in range(MOE_TOPK):
            copy(r, k).start(priority=k)
        return carry

    def wait(r, carry):
        for k in range(MOE_TOPK):
            copy(r, k).wait()
        return carry

    lax.fori_loop(0, tm, start, 0, unroll=8)
    lax.fori_loop(0, tm, wait, 0, unroll=8)
    p = p_ref[...]
    moe = y_scr[0] * p[:, 0:1]
    for k in range(1, MOE_TOPK):
        moe = moe + y_scr[k] * p[:, k:k + 1]
    o_ref[...] = x_ref[...] + g_ref[...] * moe


def _combine(slot, x, g, gates, yb, tm):
    bsz, l, d = x.shape
    nt = l // tm
    grid_spec = pltpu.PrefetchScalarGridSpec(
        num_scalar_prefetch=1,
        grid=(bsz * nt,),
        in_specs=[pl.BlockSpec((None, tm, d), lambda i, s: (i // nt, i % nt, 0)),
                  pl.BlockSpec((None, 1, d), lambda i, s: (i // nt, 0, 0)),
                  pl.BlockSpec((None, tm, LANES), lambda i, s: (i // nt, i % nt, 0)),
                  pl.BlockSpec(memory_space=pl.ANY)],
        out_specs=pl.BlockSpec((None, tm, d), lambda i, s: (i // nt, i % nt, 0)),
        scratch_shapes=[pltpu.VMEM((MOE_TOPK, tm, d), F32), pltpu.SemaphoreType.DMA(())],
    )
    return pl.pallas_call(
        _combine_kernel,
        grid_spec=grid_spec,
        out_shape=jax.ShapeDtypeStruct((bsz, l, d), F32),
        compiler_params=_params("arbitrary"),
        name="moe_combine",
    )(slot, x, g, gates, yb)


def _expert_kernel(be_ref, nu_ref, x_ref, wg_ref, wu_ref, wd_ref, o_ref):
    i = pl.program_id(0)

    @pl.when(i < nu_ref[0])
    def _():
        x = x_ref[...].astype(BF16)
        ff = wg_ref.shape[1]
        acc = jnp.zeros(o_ref.shape, F32)
        for c in range(ff // FF_CHUNK):
            cols = slice(c * FF_CHUNK, (c + 1) * FF_CHUNK)
            gate = _dot(x, wg_ref[:, cols])
            up = _dot(x, wu_ref[:, cols])
            hidden = (gate * jax.nn.sigmoid(gate) * up).astype(BF16)
            acc = acc + _dot(hidden, wd_ref[cols, :])
        o_ref[...] = acc

    @pl.when(i >= nu_ref[0])
    def _():
        o_ref[...] = jnp.zeros(o_ref.shape, F32)


def _expert_ffn(xb, block_e, n_used, w_gate, w_up, w_down, layer):
    n_blocks, _, d = xb.shape
    ff = w_gate.shape[2]
    base = layer * MOE_EXPERTS
    grid_spec = pltpu.PrefetchScalarGridSpec(
        num_scalar_prefetch=2,
        grid=(n_blocks,),
        in_specs=[pl.BlockSpec((None, MOE_BLOCK, d), lambda i, be, nu: (i, 0, 0)),
                  pl.BlockSpec((None, d, ff), lambda i, be, nu: (base + be[i], 0, 0)),
                  pl.BlockSpec((None, d, ff), lambda i, be, nu: (base + be[i], 0, 0)),
                  pl.BlockSpec((None, ff, d), lambda i, be, nu: (base + be[i], 0, 0))],
        out_specs=pl.BlockSpec((None, MOE_BLOCK, d), lambda i, be, nu: (i, 0, 0)),
    )
    return pl.pallas_call(
        _expert_kernel,
        grid_spec=grid_spec,
        out_shape=jax.ShapeDtypeStruct((n_blocks, MOE_BLOCK, d), F32),
        compiler_params=_params("arbitrary"),
        name="expert_ffn",
    )(block_e, n_used, xb, w_gate, w_up, w_down)


def _moe_rows(n_tokens):
    return (-(-(n_tokens * MOE_TOPK) // MOE_BLOCK) + MOE_EXPERTS) * MOE_BLOCK


def _hier_moe(x, g, sc, sh, gate2, w_route, b_route, w_gate, w_up, w_down, layer, zeros):
    bsz, l, d = x.shape
    n = bsz * l
    tm = min(512, l)
    h, route, gates, cnt = _router(x, g, sc, sh, w_route, b_route, tm)
    route = route.reshape(n, LANES)
    expert = route[:, :MOE_TOPK]
    rank = route[:, MOE_TOPK:2 * MOE_TOPK]
    counts = cnt[0, MOE_GROUPS:MOE_GROUPS + MOE_EXPERTS].astype(jnp.int32)
    padded = (counts + MOE_BLOCK - 1) // MOE_BLOCK * MOE_BLOCK
    pad_end = jnp.cumsum(padded)
    pad_start = pad_end - padded
    hit = expert[:, :, None] == jnp.arange(MOE_EXPERTS, dtype=jnp.int32)
    slot = (jnp.sum(jnp.where(hit, pad_start, 0), axis=-1) + rank).reshape(-1)
    n_blocks = _moe_rows(n) // MOE_BLOCK
    block_row = jnp.arange(n_blocks, dtype=jnp.int32)[:, None] * MOE_BLOCK
    block_e = jnp.minimum(jnp.sum((pad_end[None, :] <= block_row).astype(jnp.int32), axis=-1), MOE_EXPERTS - 1)
    n_used = (pad_end[-1:] // MOE_BLOCK).astype(jnp.int32)
    xb = _dispatch(slot, h.reshape(n, d), zeros, tm).reshape(n_blocks, MOE_BLOCK, d)
    yb = _expert_ffn(xb, block_e, n_used, w_gate, w_up, w_down, layer).reshape(-1, d)
    return _combine(slot, x, gate2, gates, yb, tm)


def _short_conv_kernel(zv_ref, z1_ref, z2_ref, wv_ref, w1_ref, w2_ref, bv_ref, b1_ref, b2_ref,
                       v_ref, x1_ref, x2_ref):
    rdx, q = zv_ref.shape[0], zv_ref.shape[1]
    row = lax.broadcasted_iota(jnp.int32, (q, 1), 0)

    for z_ref, w_ref, b_ref, o_ref in ((zv_ref, wv_ref, bv_ref, v_ref), (z1_ref, w1_ref, b1_ref, x1_ref),
                                       (z2_ref, w2_ref, b2_ref, x2_ref)):
        z = [z_ref[k].astype(F32) for k in range(rdx)]
        w = w_ref[...]
        b = b_ref[...]
        for k in range(rdx):
            prev = z[k - 1] if k > 0 else jnp.where(row == 0, 0.0, pltpu.roll(z[rdx - 1], 1, 0))
            nxt = z[k + 1] if k < rdx - 1 else jnp.where(row == q - 1, 0.0, pltpu.roll(z[0], q - 1, 0))
            o_ref[k] = (b + prev * w[0:1] + z[k] * w[1:2] + nxt * w[2:3]).astype(o_ref.dtype)


def _short_conv(z, conv_w, conv_b, tc):
    bsz, rdx, q, w3 = z.shape
    w = w3 // 3
    nc = w // tc
    zs = lambda k: pl.BlockSpec((None, rdx, q, tc), lambda bi, j: (bi, 0, 0, k * nc + j))
    ws = lambda k: pl.BlockSpec((8, tc), lambda bi, j: (0, k * nc + j))
    bs = lambda k: pl.BlockSpec((1, tc), lambda bi, j: (0, k * nc + j))
    out = pl.BlockSpec((None, rdx, q, tc), lambda bi, j: (bi, 0, 0, j))
    return pl.pallas_call(
        _short_conv_kernel,
        grid=(bsz, nc),
        in_specs=[zs(0), zs(1), zs(2), ws(0), ws(1), ws(2), bs(0), bs(1), bs(2)],
        out_specs=[out, out, out],
        out_shape=[jax.ShapeDtypeStruct((bsz, rdx, q, w), BF16)] * 3,
        compiler_params=_params("parallel", "parallel"),
        name="short_conv",
    )(z, z, z, conv_w, conv_w, conv_w, conv_b, conv_b, conv_b)


def _filter_kernel(z_ref, w1_ref, b1_ref, f1_ref, w2_ref, b2_ref, f2_ref, w3_ref, dl_ref, bw_ref, o_ref, a_scr):
    n = z_ref.shape[0]

    @pl.when(pl.program_id(0) == 0)
    def _():
        a = jnp.sin(f1_ref[...] * (jnp.dot(z_ref[...], w1_ref[...], precision=HIGHEST,
                                           preferred_element_type=F32) + b1_ref[...]))
        a_scr[...] = jnp.sin(f2_ref[...] * (jnp.dot(a, w2_ref[...], precision=HIGHEST,
                                                    preferred_element_type=F32) + b2_ref[...]))

    hf = jnp.dot(a_scr[...], w3_ref[...], precision=HIGHEST, preferred_element_type=F32)
    q = n // DFT_RADIX
    row = lax.broadcasted_iota(jnp.int32, (n, 1), 0)
    assert q & (q - 1) == 0
    tap = DFT_RADIX * (row & (q - 1)) + lax.shift_right_logical(row, q.bit_length() - 1)
    t = tap.astype(F32) * (1.0 / (n - 1))
    hf = hf * jnp.exp(-t * dl_ref[...])
    hf = jnp.where((tap == 0) & (bw_ref[...] > 0.5), 0.0, hf)
    for k in range(DFT_RADIX):
        o_ref[k] = hf[k * q:(k + 1) * q]


def _hyena_filters(z, w1, b1, f1, w2, b2, f2, w3, deltas, is_bwd, tn):
    n = z.shape[0]
    cols = w3.shape[1]
    full = lambda a: pl.BlockSpec(a.shape, lambda j: (0, 0))
    tile = lambda r: pl.BlockSpec((r, tn), lambda j: (0, j))
    return pl.pallas_call(
        _filter_kernel,
        grid=(cols // tn,),
        in_specs=[full(z), full(w1), full(b1), full(f1), full(w2), full(b2), full(f2),
                  tile(w3.shape[0]), tile(1), tile(1)],
        out_specs=pl.BlockSpec((DFT_RADIX, n // DFT_RADIX, tn), lambda j: (0, 0, j)),
        out_shape=jax.ShapeDtypeStruct((DFT_RADIX, n // DFT_RADIX, cols), F32),
        scratch_shapes=[pltpu.VMEM((n, LANES), F32)],
        compiler_params=_params("arbitrary"),
        name="hyena_filters",
    )(z, w1, b1, f1, w2, b2, f2, w3, deltas, is_bwd)


def _dft_groups():
    groups = [(0, 1), (DFT_RADIX // 2, -1)]
    for m in range(1, DFT_RADIX // 2):
        groups += [(m, -1), (m, 1)]
    return groups


def _dft_coef(g, k):
    m, b = _dft_groups()[g]
    assert DFT_RADIX in (2, 4)
    quarter = (4 * m * k // DFT_RADIX) % 4
    return (1, 0, -1, 0)[quarter], (0, 1, 0, -1)[quarter], b


def _signed_sum(terms):
    acc = None
    for coef, thunk in terms:
        if coef == 0:
            continue
        val = thunk()
        if acc is None:
            acc = val if coef > 0 else -val
        else:
            acc = acc + val if coef > 0 else acc - val
    return acc


def _spectrum(cprod, sprod, want_c=True, want_s=True):
    cprod = functools.lru_cache(maxsize=None)(cprod)
    sprod = functools.lru_cache(maxsize=None)(sprod)
    out = []
    for g in range(DFT_RADIX):
        cu, su = [], []
        for k in range(DFT_RADIX):
            ca, sa, b = _dft_coef(g, k)
            cu += [(ca, functools.partial(cprod, k)), (-b * sa, functools.partial(sprod, k))]
            su += [(sa, functools.partial(cprod, k)), (b * ca, functools.partial(sprod, k))]
        out.append((_signed_sum(cu) if want_c else None, _signed_sum(su) if want_s else None))
    return out


def _dft_tables(n):
    q = n // DFT_RADIX
    r = 1 << (q.bit_length() // 2)
    f = jnp.arange(q, dtype=jnp.int32)[:, None]
    idx = jnp.arange(r, dtype=jnp.int32)[None, :]
    ang = lambda s: (((2 * f + 1) * s) % (4 * n)).astype(F32) * (math.pi / (2 * n))
    hi = ang(DFT_RADIX * r * idx)[:, :q // r]
    ch, sh = jnp.cos(hi), jnp.sin(hi)
    fwd, inv = [], []
    for trig in ("cos", "sin"):
        for k in range(DFT_RADIX):
            lo = ang(DFT_RADIX * idx + k)
            cl, sl = jnp.cos(lo), jnp.sin(lo)
            a, b, sign = (ch, sh, -1.0) if trig == "cos" else (sh, ch, 1.0)
            fwd.append((a[:, :, None] * cl[:, None, :] + sign * b[:, :, None] * sl[:, None, :]).reshape(q, q))
            inv.append((a.T[:, None, :] * cl.T[None, :, :] + sign * b.T[:, None, :] * sl.T[None, :, :]).reshape(q, q))
    shape = (2, DFT_RADIX, q, q)
    return jnp.stack(fwd).astype(BF16).reshape(shape), jnp.stack(inv).astype(BF16).reshape(shape)


def _dft_filter_kernel(m_ref, hf_ref, hb_ref, kr_ref, ki_ref, sum_scr, dif_scr, *, scale):
    @pl.when(pl.program_id(2) == 0)
    def _():
        hf = hf_ref[...]
        hb = hb_ref[...]
        sum_scr[...] = (hf + hb).astype(BF16)
        dif_scr[...] = (hb - hf).astype(BF16)

    of_sum = _spectrum(lambda k: _dot(m_ref[0, k], sum_scr[k]), lambda k: _dot(m_ref[1, k], sum_scr[k]),
                       want_s=False)
    of_dif = _spectrum(lambda k: _dot(m_ref[0, k], dif_scr[k]), lambda k: _dot(m_ref[1, k], dif_scr[k]),
                       want_c=False)
    for g in range(DFT_RADIX):
        kr_ref[g] = of_sum[g][0] * scale
        ki_ref[g] = of_dif[g][1] * scale


def _dft_filter(fwd, filt, width, tf, tn):
    rdx, q = fwd.shape[1], fwd.shape[2]
    nc = width // tn
    out = pl.BlockSpec((None, rdx, tf, tn), lambda o, j, i: (o, 0, i, j))
    return pl.pallas_call(
        functools.partial(_dft_filter_kernel, scale=1.0 / (rdx * q)),
        grid=(HYENA_ORDER, nc, q // tf),
        in_specs=[pl.BlockSpec((2, rdx, tf, q), lambda o, j, i: (0, 0, i, 0)),
                  pl.BlockSpec((rdx, q, tn), lambda o, j, i: (0, 0, (2 * o) * nc + j)),
                  pl.BlockSpec((rdx, q, tn), lambda o, j, i: (0, 0, (2 * o + 1) * nc + j))],
        out_specs=[out, out],
        out_shape=[jax.ShapeDtypeStruct((HYENA_ORDER, rdx, q, width), F32)] * 2,
        scratch_shapes=[pltpu.VMEM((rdx, q, tn), BF16), pltpu.VMEM((rdx, q, tn), BF16)],
        compiler_params=_params("parallel", "parallel", "arbitrary"),
        name="dft_filter",
    )(fwd, filt, filt)


def _dft_fwd_kernel(m_ref, u_ref, kr_ref, ki_ref, pr_ref, pi_ref):
    spec = _spectrum(lambda k: _dot(m_ref[0, k], u_ref[k]), lambda k: _dot(m_ref[1, k], u_ref[k]))
    for g, (ur, us) in enumerate(spec):
        kr = kr_ref[g]
        ki = ki_ref[g]
        pr_ref[g] = (ur * kr + us * ki).astype(pr_ref.dtype)
        pi_ref[g] = (ur * ki - us * kr).astype(pi_ref.dtype)


def _dft_fwd(fwd, u, kr, ki, order, tf, tn):
    bsz, rdx, q, width = u.shape
    out = pl.BlockSpec((None, rdx, tf, tn), lambda bi, j, i: (bi, 0, i, j))
    ks = pl.BlockSpec((None, rdx, tf, tn), lambda bi, j, i: (order, 0, i, j))
    return pl.pallas_call(
        _dft_fwd_kernel,
        grid=(bsz, width // tn, q // tf),
        in_specs=[pl.BlockSpec((2, rdx, tf, q), lambda bi, j, i: (0, 0, i, 0)),
                  pl.BlockSpec((None, rdx, q, tn), lambda bi, j, i: (bi, 0, 0, j)),
                  ks, ks],
        out_specs=[out, out],
        out_shape=[jax.ShapeDtypeStruct((bsz, rdx, q, width), BF16)] * 2,
        compiler_params=_params("parallel", "parallel", "parallel"),
        name="dft_fwd",
    )(fwd, u, kr, ki)


def _dft_inv_kernel(m_ref, pr_ref, pi_ref, u_ref, b_ref, x_ref, o_ref, ra_scr, rb_scr):
    for k in range(DFT_RADIX):
        @pl.when((pl.program_id(3) == 0) & (pl.program_id(2) == k))
        def _(k=k):
            ra, rb = [], []
            for g in range(DFT_RADIX):
                ca, sa, b = _dft_coef(g, k)
                pr = functools.partial(lambda g: pr_ref[g].astype(F32), g)
                pi = functools.partial(lambda g: pi_ref[g].astype(F32), g)
                ra += [(ca, pr), (-sa, pi)]
                rb += [(b * sa, pr), (b * ca, pi)]
            ra_scr[...] = _signed_sum(ra).astype(BF16)
            rb_scr[...] = _signed_sum(rb).astype(BF16)

    y = _dot(m_ref[0], ra_scr[...]) - _dot(m_ref[1], rb_scr[...])
    u = u_ref[...].astype(F32)
    o_ref[...] = (x_ref[...] * (y + u * b_ref[...])).astype(o_ref.dtype)


def _dft_inv(inv, pr, pi, u, bias, x, order, tt, tn):
    bsz, rdx, q, width = u.shape
    tile = pl.BlockSpec((None, None, tt, tn), lambda bi, j, k, i: (bi, k, i, j))
    spec = pl.BlockSpec((None, rdx, q, tn), lambda bi, j, k, i: (bi, 0, 0, j))
    return pl.pallas_call(
        _dft_inv_kernel,
        grid=(bsz, width // tn, rdx, q // tt),
        in_specs=[pl.BlockSpec((2, None, tt, q), lambda bi, j, k, i: (0, k, i, 0)),
                  spec, spec, tile,
                  pl.BlockSpec((None, 1, tn), lambda bi, j, k, i: (order, 0, j)),
                  tile],
        out_specs=tile,
        out_shape=jax.ShapeDtypeStruct((bsz, rdx, q, width), BF16),
        scratch_shapes=[pltpu.VMEM((q, tn), BF16), pltpu.VMEM((q, tn), BF16)],
        compiler_params=_params("parallel", "parallel", "arbitrary", "arbitrary"),
        name="dft_inv",
    )(inv, pr, pi, u, bias, x)


def _filter_features(n):
    t = jnp.linspace(0.0, 1.0, n, dtype=F32)[:, None]
    w = 2.0 * math.pi * jnp.arange(n, dtype=F32) / n
    f = jnp.linspace(1e-4, FILTER_BANDS - 1, FILTER_BANDS, dtype=F32)
    z = jnp.concatenate([t, jnp.cos(w[:, None] * f), -jnp.sin(w[:, None] * f)], axis=-1)
    return jnp.pad(z, ((0, 0), (0, LANES - FILTER_EMB)))


def _pad_to(a, shape):
    return jnp.pad(a, [(0, t - s) for s, t in zip(a.shape, shape)])


def kernel(x, c, ctx, c_ctx, ada_w, ada_b, norm1_g, norm2_g, attn_w_in, attn_w_out, diff_q_g, diff_k_g, diff_lq1, diff_lk1, diff_lq2, diff_lk2, diff_sub_g, swa_q_g, swa_k_g, swa_sink, hy_w_in, hy_b_in, hy_conv_w, hy_conv_b, flt_w1, flt_b1, flt_f1, flt_w2, flt_b2, flt_f2, flt_w3, hy_bias, hy_w_out, hy_b_out, moe_wg1, moe_bg1, moe_wg2, moe_bg2, moe_w_gate, moe_w_up, moe_w_down):
    bsz, seq, d = x.shape
    depth = ada_w.shape[0]
    n_ctx = ctx.shape[1]
    cond = jnp.concatenate([c, c_ctx[None, :], jnp.zeros((8 - bsz - 1, d), F32)], axis=0)
    ada_b3 = ada_b[:, None, :]
    zero_bias = lambda n: jnp.zeros((1, n), F32)
    expert_w = None
    moe_zeros = []
    tm = min(512, seq)

    for layer in range(depth):
        even = layer % 2 == 0
        i = layer // 2
        ctx_live = any(j % 2 == 0 for j in range(layer + 1, depth))
        assert not ctx_live, "context-stream update is only needed for deeper stacks"
        mod = _adaln_mod(cond, ada_w, ada_b3, layer)
        sh1, sc1, g1, sh2, sc2, g2 = [mod[:, k * d:(k + 1) * d] for k in range(6)]
        lat = lambda a: a[:bsz, None, :]
        cmod = lambda a: jnp.broadcast_to(a[bsz:bsz + 1, None, :], (bsz, 1, d))
        n1 = norm1_g[layer][None, :]
        if even:
            w_in = attn_w_in[i].astype(BF16)
            qkv = _norm_linear(x, n1, lat(sc1), lat(sh1), w_in, zero_bias(ATTN_IN), tm, 1536, "attn_in")
            qkv_c = _norm_linear(ctx, n1, cmod(sc1), cmod(sh1), w_in, zero_bias(ATTN_IN), min(tm, n_ctx), 1536,
                                 "attn_in_ctx")
            lam_init = 0.8 - 0.6 * math.exp(-0.3 * layer)
            lamp = _pad_to(jnp.stack([diff_lq1[i], diff_lk1[i], diff_lq2[i], diff_lk2[i]]), (8, 2 * DIFF_SUB))
            two = lambda a: jnp.tile(a, 2)[None, :]
            first = expert_w is None
            to_cast = [moe_w_gate, moe_w_up, moe_w_down] if first else []
            zero_shapes = [(_moe_rows(bsz * seq), d)] * depth if first else []
            o_diff, cast, zeros = _diff_attention(
                qkv, qkv_c, _rope_tables(seq, DIFF_SUB, 2 * DIFF_SUB), two(diff_q_g[i]), two(diff_k_g[i]),
                diff_sub_g[i][None, :], lamp, lam_init, min(2 * Q_BLOCK, seq), to_cast, zero_shapes)
            moe_zeros += zeros
            if cast:
                expert_w = (cast[0].reshape(depth * MOE_EXPERTS, d, -1), cast[1].reshape(depth * MOE_EXPERTS, d, -1),
                            cast[2].reshape(depth * MOE_EXPERTS, -1, d))
            sink_rows = jnp.broadcast_to(swa_sink[i][:, None], (SWA_HEADS, HEAD_DIM))
            o_swa = _swa_attention(qkv, qkv_c, _rope_tables(seq, HEAD_DIM, HEAD_DIM), swa_q_g[i][None, :],
                                   swa_k_g[i][None, :], sink_rows)
            w_out = attn_w_out[i].astype(BF16)
            x = _mm_residual([o_diff, o_swa], [w_out[:DIFF_V], w_out[DIFF_V:]], zero_bias(d), x, lat(g1), tm,
                             "attn_out")
        else:
            width = hy_w_out.shape[1]
            q = seq // DFT_RADIX
            z = _norm_linear(x, n1, lat(sc1), lat(sh1), hy_w_in[i].astype(BF16), hy_b_in[i][None, :], tm, 1536,
                             "hyena_in", classes=DFT_RADIX)
            v, x1, x2 = _short_conv(z, _pad_to(hy_conv_w[i], (8, 3 * width)), hy_conv_b[i][None, :], 128)
            max_decay = math.log(DECAY_TARGET) / DECAY_FAST
            min_decay = math.log(DECAY_TARGET) / DECAY_SLOW
            deltas = jnp.abs(jnp.tile(jnp.linspace(min_decay, max_decay, width, dtype=F32), 2 * HYENA_ORDER))[None, :]
            is_bwd = jnp.tile(jnp.concatenate([jnp.zeros((width,), F32), jnp.ones((width,), F32)]),
                              HYENA_ORDER)[None, :]
            sq = (LANES, LANES)
            feats = _filter_features(seq)
            feats = jnp.concatenate([feats[k::DFT_RADIX] for k in range(DFT_RADIX)], axis=0)
            filt = _hyena_filters(
                feats, _pad_to(flt_w1[i], sq), _pad_to(flt_b1[i][None, :], (1, LANES)),
                _pad_to(flt_f1[i][None, :], (1, LANES)), _pad_to(flt_w2[i], sq),
                _pad_to(flt_b2[i][None, :], (1, LANES)), _pad_to(flt_f2[i][None, :], (1, LANES)),
                _pad_to(flt_w3[i], (LANES, 2 * HYENA_ORDER * width)), deltas, is_bwd, 256)
            fwd, inv = _dft_tables(seq)
            tf = min(256, q)
            tn = min(512, width)
            kr, ki = _dft_filter(fwd, filt, width, tf, min(256, width))
            bias = hy_bias[i][:, None, :]
            pr, pi = _dft_fwd(fwd, v, kr, ki, 0, tf, tn)
            tt = min(1024, q)
            y1 = _dft_inv(inv, pr, pi, v, bias, x1, 0, tt, tn)
            pr, pi = _dft_fwd(fwd, y1, kr, ki, 1, tf, tn)
            y2 = _dft_inv(inv, pr, pi, y1, bias, x2, 1, tt, tn)
            x = _mm_residual([y2], [hy_w_out[i].astype(BF16)], hy_b_out[i][None, :], x, lat(g1), tm,
                             "hyena_out", classes=DFT_RADIX)
        w_route = _pad_to(jnp.concatenate([moe_wg1[layer], moe_wg2[layer]], axis=1), (d, LANES))
        b_route = _pad_to(jnp.concatenate([moe_bg1[layer], moe_bg2[layer]])[None, :], (1, LANES))
        x = _hier_moe(x, norm2_g[layer][None, :], lat(sc2), lat(sh2), lat(g2), w_route, b_route,
                      *expert_w, layer, moe_zeros[layer])
    return x
```

```python
import functools
import math

import jax
import jax.numpy as jnp
from jax import lax
from jax.experimental import pallas as pl
from jax.experimental.pallas import tpu as pltpu

F32 = jnp.float32
BF16 = jnp.bfloat16
HIGHEST = lax.Precision.HIGHEST

LANES = 128
V7X_VMEM_BYTES = 64 * 1024 * 1024
VMEM_LIMIT = V7X_VMEM_BYTES * 7 // 8

GRID_W = 64
HEAD_DIM = 128
ROPE_BASE = 10000.0
EPS = 1e-6
Q_BLOCK = 128
DIFF_HEADS = 8
DIFF_SUB = HEAD_DIM // 2
SWA_HEADS = 8
SWA_KV_HEADS = 2
SWA_GROUP = SWA_HEADS // SWA_KV_HEADS
WINDOW = 128
DIFF_Q = DIFF_HEADS * 2 * DIFF_SUB
DIFF_V = DIFF_HEADS * HEAD_DIM
SWA_Q = SWA_HEADS * HEAD_DIM
SWA_KV = SWA_KV_HEADS * HEAD_DIM
ATTN_IN = 2 * DIFF_Q + DIFF_V + SWA_Q + 2 * SWA_KV
HYENA_ORDER = 2
SHORT_CONV = 3
FILTER_EMB = 33
FILTER_BANDS = (FILTER_EMB - 1) // 2
FILTER_HIDDEN = 64
DECAY_FAST = 0.3
DECAY_SLOW = 1.5
DECAY_TARGET = 1e-2
MOE_GROUPS = 4
MOE_PER_GROUP = 8
MOE_EXPERTS = MOE_GROUPS * MOE_PER_GROUP
MOE_TOPK = 2
MOE_BLOCK = 256
FF_CHUNK = 1024
DIFF_KEY_CHUNK = 512
DFT_RADIX = 4
LOG2_E = 1.4426950408889634
NEG_INF = float("-inf")


def _tile(n, pref):
    t = pref
    while n % t:
        t //= 2
    return t


def _params(*sem):
    return pltpu.CompilerParams(dimension_semantics=sem, vmem_limit_bytes=VMEM_LIMIT)


def _dot(a, b):
    return jnp.dot(a, b, preferred_element_type=F32)


def _dot_nt(a, b):
    return lax.dot_general(a, b, (((1,), (1,)), ((), ())), preferred_element_type=F32)


def _mod_kernel(c_ref, w_ref, b_ref, o_ref):
    c = c_ref[...]
    s = c * jax.nn.sigmoid(c)
    o_ref[...] = jnp.dot(s, w_ref[...], precision=HIGHEST, preferred_element_type=F32) + b_ref[...]


def _adaln_mod(cond, ada_w, ada_b, layer):
    rows, d = cond.shape
    n = ada_w.shape[2]
    tn = _tile(n, 768)
    return pl.pallas_call(
        _mod_kernel,
        grid=(n // tn,),
        in_specs=[pl.BlockSpec((rows, d), lambda j: (0, 0)),
                  pl.BlockSpec((None, d, tn), lambda j: (layer, 0, j)),
                  pl.BlockSpec((None, 1, tn), lambda j: (layer, 0, j))],
        out_specs=pl.BlockSpec((rows, tn), lambda j: (0, j)),
        out_shape=jax.ShapeDtypeStruct((rows, n), F32),
        compiler_params=_params("parallel"),
        name="adaln_mod",
    )(cond, ada_w, ada_b)


def _norm_mod(x, g, sc, sh):
    inv = lax.rsqrt(jnp.mean(x * x, axis=-1, keepdims=True) + EPS)
    return ((x * inv) * g) * (1.0 + sc) + sh


def _class_perm(tm, classes):
    m = tm // classes
    out_row = jnp.arange(tm, dtype=jnp.int32)
    src = classes * (out_row % m) + out_row // m
    return (src[:, None] == jnp.arange(tm, dtype=jnp.int32)[None, :]).astype(BF16)


def _norm_linear_kernel(*refs, classes):
    if classes:
        p_ref, x_ref, g_ref, sc_ref, sh_ref, w_ref, b_ref, o_ref, h_ref = refs
    else:
        x_ref, g_ref, sc_ref, sh_ref, w_ref, b_ref, o_ref, h_ref = refs

    @pl.when(pl.program_id(2) == 0)
    def _():
        h = _norm_mod(x_ref[...], g_ref[...], sc_ref[...], sh_ref[...]).astype(BF16)
        if classes:
            h = _dot(p_ref[...], h).astype(BF16)
        h_ref[...] = h

    res = (_dot(h_ref[...], w_ref[...]) + b_ref[...]).astype(o_ref.dtype)
    if classes:
        m = res.shape[0] // classes
        for k in range(classes):
            o_ref[k] = res[k * m:(k + 1) * m]
    else:
        o_ref[...] = res


def _norm_linear(x, g, sc, sh, w, b, tm, tn, name, classes=0):
    bsz, l, d = x.shape
    n = w.shape[1]
    tn = _tile(n, tn)
    in_specs = [pl.BlockSpec((None, tm, d), lambda bi, i, j: (bi, i, 0)),
                pl.BlockSpec((1, d), lambda bi, i, j: (0, 0)),
                pl.BlockSpec((None, 1, d), lambda bi, i, j: (bi, 0, 0)),
                pl.BlockSpec((None, 1, d), lambda bi, i, j: (bi, 0, 0)),
                pl.BlockSpec((d, tn), lambda bi, i, j: (0, j)),
                pl.BlockSpec((1, tn), lambda bi, i, j: (0, j))]
    args = (x, g, sc, sh, w, b)
    if classes:
        in_specs = [pl.BlockSpec((tm, tm), lambda bi, i, j: (0, 0))] + in_specs
        args = (_class_perm(tm, classes),) + args
        out_specs = pl.BlockSpec((None, classes, tm // classes, tn), lambda bi, i, j: (bi, 0, i, j))
        out_shape = jax.ShapeDtypeStruct((bsz, classes, l // classes, n), BF16)
    else:
        out_specs = pl.BlockSpec((None, tm, tn), lambda bi, i, j: (bi, i, j))
        out_shape = jax.ShapeDtypeStruct((bsz, l, n), BF16)
    return pl.pallas_call(
        functools.partial(_norm_linear_kernel, classes=classes),
        grid=(bsz, l // tm, n // tn),
        in_specs=in_specs,
        out_specs=out_specs,
        out_shape=out_shape,
        scratch_shapes=[pltpu.VMEM((tm, d), BF16)],
        compiler_params=_params("parallel", "arbitrary", "arbitrary"),
        name=name,
    )(*args)


def _mm_res_kernel(*refs, n_pairs, classes):
    if classes:
        pt_ref, refs = refs[0], refs[1:]
    a_refs = refs[:n_pairs]
    w_refs = refs[n_pairs:2 * n_pairs]
    b_ref, x_ref, g_ref, o_ref = refs[2 * n_pairs:]
    acc = None
    for a_ref, w_ref in zip(a_refs, w_refs):
        if classes:
            a = jnp.concatenate([a_ref[k] for k in range(classes)], axis=0)
            a = _dot(pt_ref[...], a).astype(BF16)
        else:
            a = a_ref[...]
        prod = _dot(a, w_ref[...])
        acc = prod if acc is None else acc + prod
    o_ref[...] = x_ref[...] + g_ref[...] * (acc + b_ref[...])


def _mm_residual(a_list, w_list, b, x, g, tm, name, classes=0):
    bsz, l, n = x.shape
    n_pairs = len(a_list)
    if classes:
        in_specs = [pl.BlockSpec((tm, tm), lambda bi, i: (0, 0))]
        in_specs += [pl.BlockSpec((None, classes, tm // classes, a.shape[3]), lambda bi, i: (bi, 0, i, 0))
                     for a in a_list]
        args = (_class_perm(tm, classes).T,)
    else:
        in_specs = [pl.BlockSpec((None, tm, a.shape[2]), lambda bi, i: (bi, i, 0)) for a in a_list]
        args = ()
    in_specs += [pl.BlockSpec(w.shape, lambda bi, i: (0, 0)) for w in w_list]
    in_specs += [pl.BlockSpec((1, n), lambda bi, i: (0, 0)),
                 pl.BlockSpec((None, tm, n), lambda bi, i: (bi, i, 0)),
                 pl.BlockSpec((None, 1, n), lambda bi, i: (bi, 0, 0))]
    return pl.pallas_call(
        functools.partial(_mm_res_kernel, n_pairs=n_pairs, classes=classes),
        grid=(bsz, l // tm),
        in_specs=in_specs,
        out_specs=pl.BlockSpec((None, tm, n), lambda bi, i: (bi, i, 0)),
        out_shape=jax.ShapeDtypeStruct((bsz, l, n), F32),
        compiler_params=_params("parallel", "parallel"),
        name=name,
    )(*args, *a_list, *w_list, b, x, g)


def _rope_tables(n_tokens, dim, width):
    n_rows = n_tokens // GRID_W
    row = jnp.repeat(jnp.arange(n_rows, dtype=F32), GRID_W)
    col = jnp.tile(jnp.arange(GRID_W, dtype=F32), n_rows)
    n_freq = dim // 4
    inv = ROPE_BASE ** (-jnp.arange(n_freq, dtype=F32) / n_freq)
    ang = jnp.concatenate([row[:, None] * inv, col[:, None] * inv], axis=-1)
    ang = jnp.repeat(ang, 2, axis=-1)
    ang = jnp.tile(ang, (1, width // dim))
    sign = jnp.where(jnp.arange(width) % 2 == 0, -1.0, 1.0).astype(F32)
    return jnp.cos(ang), jnp.sin(ang) * sign


def _rope(x, cos_t, sin_t):
    width = x.shape[-1]
    lane = lax.broadcasted_iota(jnp.int32, (1, width), 1)
    partner = jnp.where(lane % 2 == 0, pltpu.roll(x, width - 1, 1), pltpu.roll(x, 1, 1))
    return x * cos_t + partner * sin_t


def _sub_rms(x, g):
    lane = lax.broadcasted_iota(jnp.int32, (1, 2 * DIFF_SUB), 1)
    lo = lane < DIFF_SUB
    sq = x * x
    ss_lo = jnp.sum(jnp.where(lo, sq, 0.0), axis=-1, keepdims=True)
    ss_hi = jnp.sum(jnp.where(lo, 0.0, sq), axis=-1, keepdims=True)
    inv = jnp.where(lo, lax.rsqrt(ss_lo / DIFF_SUB + EPS), lax.rsqrt(ss_hi / DIFF_SUB + EPS))
    return (x * inv) * g


def _rms(x, g):
    return (x * lax.rsqrt(jnp.mean(x * x, axis=-1, keepdims=True) + EPS)) * g


def _diff_attn_kernel(*refs, lam_init, n_ctx, tq, n_cast, n_zero):
    (q_ref, k_ref, v_ref, kc_ref, vc_ref, cos_ref, sin_ref, gq_ref, gk_ref, gs_ref, lamp_ref) = refs[:11]
    cast_in = refs[11:11 + n_cast]
    o_ref = refs[11 + n_cast]
    cast_out = refs[12 + n_cast:12 + 2 * n_cast]
    zero_out = refs[12 + 2 * n_cast:12 + 2 * n_cast + n_zero]
    q_scr, k_scr, v_scr, s_scr, lam_scr = refs[12 + 2 * n_cast + n_zero:]
    for src, dst in zip(cast_in, cast_out):
        dst[...] = src[...].astype(dst.dtype)
    for dst in zero_out:
        dst[...] = jnp.zeros(dst.shape, dst.dtype)
    n_keys = k_scr.shape[0]

    @pl.when(pl.program_id(2) == 0)
    def _():
        lane = lax.broadcasted_iota(jnp.int32, (1, 2 * DIFF_SUB), 1)
        lo = lane < DIFF_SUB
        gk = gk_ref[...]
        cos_t = cos_ref[...]
        sin_t = sin_ref[...]
        k_scr[0:n_ctx, :] = _sub_rms(kc_ref[...].astype(F32), gk).astype(BF16)
        k_scr[n_ctx:, :] = _rope(_sub_rms(k_ref[...].astype(F32), gk), cos_t, sin_t).astype(BF16)
        v_scr[0:n_ctx, 0:HEAD_DIM] = vc_ref[...].astype(BF16)
        v_scr[n_ctx:, 0:HEAD_DIM] = v_ref[...].astype(BF16)
        v_scr[:, HEAD_DIM:] = jnp.broadcast_to(jnp.where(lane == 0, 1.0, 0.0).astype(BF16), (n_keys, HEAD_DIM))
        q = _rope(_sub_rms(q_ref[...].astype(F32), gq_ref[...]), cos_t, sin_t) * (DIFF_SUB ** -0.5 * LOG2_E)
        q_scr[0] = jnp.where(lo, q, 0.0).astype(BF16)
        q_scr[1] = jnp.where(lo, 0.0, q).astype(BF16)
        lp = lamp_ref[...]
        t1 = jnp.sum(lp[0:1] * lp[1:2], axis=-1, keepdims=True)
        t2 = jnp.sum(lp[2:3] * lp[3:4], axis=-1, keepdims=True)
        lam_scr[...] = jnp.broadcast_to(jnp.exp(t1) - jnp.exp(t2) + lam_init, lam_scr.shape)

    chunks = [(c0, min(c0 + DIFF_KEY_CHUNK, n_keys)) for c0 in range(0, n_keys, DIFF_KEY_CHUNK)]
    lam = lam_scr[:, 0:1]
    for t in range(tq // Q_BLOCK):
        rows = pl.ds(pl.multiple_of(pl.program_id(2) * tq + t * Q_BLOCK, Q_BLOCK), Q_BLOCK)
        qs = [q_scr[0, rows, :], q_scr[1, rows, :]]

        mx = [jnp.full((Q_BLOCK, LANES), NEG_INF, F32) for _ in range(2)]
        for c0, c1 in chunks:
            kc = k_scr[c0:c1, :]
            for h in range(2):
                s = _dot_nt(qs[h], kc)
                s_scr[2 * t + h, :, c0:c1] = s
                for j in range((c1 - c0) // LANES):
                    mx[h] = jnp.maximum(mx[h], s[:, j * LANES:(j + 1) * LANES])
        mrow = [jnp.max(m, axis=-1, keepdims=True) for m in mx]

        out = [jnp.zeros((Q_BLOCK, 2 * HEAD_DIM), F32) for _ in range(2)]
        for c0, c1 in chunks:
            vc = v_scr[c0:c1, :]
            for h in range(2):
                e = jnp.exp2(s_scr[2 * t + h, :, c0:c1] - mrow[h])
                out[h] = out[h] + _dot(e.astype(BF16), vc)
        r1 = 1.0 / out[0][:, HEAD_DIM:HEAD_DIM + 1]
        r2 = lam / out[1][:, HEAD_DIM:HEAD_DIM + 1]
        o = out[0][:, 0:HEAD_DIM] * r1 - out[1][:, 0:HEAD_DIM] * r2
        o_ref[t * Q_BLOCK:(t + 1) * Q_BLOCK, :] = (_rms(o, gs_ref[...]) * (1.0 - lam_init)).astype(o_ref.dtype)


def _diff_attention(qkv, qkv_c, tabs, gq, gk, gs, lamp, lam_init, tq, to_cast, zero_shapes):
    bsz, l, _ = qkv.shape
    n_ctx = qkv_c.shape[1]
    cos_t, sin_t = tabs
    hb = 2 * DIFF_SUB
    kb = DIFF_Q // hb
    vb = 2 * DIFF_Q // hb
    nq = l // tq
    n_steps = bsz * DIFF_HEADS * nq
    slabs = [a.reshape(n_steps, -1, a.shape[-1]) for a in to_cast]
    slab_spec = lambda a: pl.BlockSpec((None,) + a.shape[1:], lambda bi, h, i: ((bi * DIFF_HEADS + h) * nq + i, 0, 0))
    zero_slabs = [jax.ShapeDtypeStruct((n_steps, rows // n_steps, cols), F32) for rows, cols in zero_shapes]
    kern = functools.partial(_diff_attn_kernel, lam_init=lam_init, n_ctx=n_ctx, tq=tq, n_cast=len(slabs),
                             n_zero=len(zero_slabs))
    vec = pl.BlockSpec((1, hb), lambda bi, h, i: (0, 0))
    outs = pl.pallas_call(
        kern,
        grid=(bsz, DIFF_HEADS, nq),
        in_specs=[pl.BlockSpec((None, l, hb), lambda bi, h, i: (bi, 0, h)),
                  pl.BlockSpec((None, l, hb), lambda bi, h, i: (bi, 0, kb + h)),
                  pl.BlockSpec((None, l, hb), lambda bi, h, i: (bi, 0, vb + h)),
                  pl.BlockSpec((None, n_ctx, hb), lambda bi, h, i: (bi, 0, kb + h)),
                  pl.BlockSpec((None, n_ctx, hb), lambda bi, h, i: (bi, 0, vb + h)),
                  pl.BlockSpec((l, hb), lambda bi, h, i: (0, 0)),
                  pl.BlockSpec((l, hb), lambda bi, h, i: (0, 0)),
                  vec, vec, vec,
                  pl.BlockSpec((8, hb), lambda bi, h, i: (0, 0))] + [slab_spec(a) for a in slabs],
        out_specs=[pl.BlockSpec((None, tq, hb), lambda bi, h, i: (bi, i, h))]
        + [slab_spec(a) for a in slabs] + [slab_spec(a) for a in zero_slabs],
        out_shape=[jax.ShapeDtypeStruct((bsz, l, DIFF_V), BF16)]
        + [jax.ShapeDtypeStruct(a.shape, BF16) for a in slabs] + zero_slabs,
        scratch_shapes=[pltpu.VMEM((2, l, hb), BF16), pltpu.VMEM((n_ctx + l, hb), BF16),
                        pltpu.VMEM((n_ctx + l, 2 * HEAD_DIM), BF16),
                        pltpu.VMEM((2 * (tq // Q_BLOCK), Q_BLOCK, n_ctx + l), F32),
                        pltpu.VMEM((1, hb), F32)],
        compiler_params=_params("arbitrary", "arbitrary", "arbitrary"),
        name="diff_attention",
    )(qkv, qkv, qkv, qkv_c, qkv_c, cos_t, sin_t, gq, gk, gs, lamp, *slabs)
    casts = [o.reshape(a.shape) for o, a in zip(outs[1:1 + len(slabs)], to_cast)]
    zeros = [o.reshape(shape) for o, shape in zip(outs[1 + len(slabs):], zero_shapes)]
    return outs[0], casts, zeros


def _swa_kernel(q_ref, k_ref, v_ref, kc_ref, vc_ref, cos_ref, sin_ref, gq_ref, gk_ref, sink_ref,
                o_ref, q_scr, k_scr, v_scr, kc_scr, vc_scr, *, seq):
    kvh = pl.program_id(1)
    n = pl.program_id(2)
    span = Q_BLOCK + 2 * WINDOW

    @pl.when(n == 0)
    def _():
        gk = gk_ref[...]
        gq = gq_ref[...]
        cos_t = cos_ref[...]
        sin_t = sin_ref[...]
        zeros = jnp.zeros((WINDOW, HEAD_DIM), BF16)
        k_scr[0:WINDOW, :] = zeros
        v_scr[0:WINDOW, :] = zeros
        k_scr[WINDOW + seq:, :] = zeros
        v_scr[WINDOW + seq:, :] = zeros
        k_scr[WINDOW:WINDOW + seq, :] = _rope(_rms(k_ref[...].astype(F32), gk), cos_t, sin_t).astype(BF16)
        v_scr[WINDOW:WINDOW + seq, :] = v_ref[...].astype(BF16)
        kc_scr[...] = _rms(kc_ref[...].astype(F32), gk).astype(BF16)
        vc_scr[...] = vc_ref[...].astype(BF16)
        for j in range(SWA_GROUP):
            qj = q_ref[:, j * HEAD_DIM:(j + 1) * HEAD_DIM].astype(F32)
            qj = _rope(_rms(qj, gq), cos_t, sin_t) * (HEAD_DIM ** -0.5)
            q_scr[j] = qj.astype(BF16)

    kc = kc_scr[...]
    vc = vc_scr[...]
    rows = SWA_GROUP * Q_BLOCK
    ridx = lax.broadcasted_iota(jnp.int32, (rows, span), 0)
    kidx = lax.broadcasted_iota(jnp.int32, (rows, span), 1)
    rel = kidx - (ridx & (Q_BLOCK - 1))
    band = (rel >= 0) & (rel <= 2 * WINDOW)
    head = lax.broadcasted_iota(jnp.int32, (rows, 1), 0) // Q_BLOCK
    sink = jnp.zeros((rows, 1), F32)
    for j in range(SWA_GROUP):
        sink = jnp.where(head == j, sink_ref[pl.ds(kvh * SWA_GROUP + j, 1), 0:1], sink)
    for t in range(o_ref.shape[0] // Q_BLOCK):
        blk = n * (o_ref.shape[0] // Q_BLOCK) + t
        start = pl.multiple_of(blk * Q_BLOCK, Q_BLOCK)
        kw = k_scr[pl.ds(start, span), :]
        vw = v_scr[pl.ds(start, span), :]
        key_pos = blk * Q_BLOCK + kidx - WINDOW
        mask = band & (key_pos >= 0) & (key_pos < seq)
        q = jnp.concatenate([q_scr[j, pl.ds(start, Q_BLOCK), :] for j in range(SWA_GROUP)], axis=0)
        s = jnp.concatenate([_dot_nt(q, kc), jnp.where(mask, _dot_nt(q, kw), NEG_INF)], axis=-1)
        m = jnp.maximum(jnp.max(s, axis=-1, keepdims=True), sink)
        e = jnp.exp(s - m)
        r = 1.0 / (jnp.sum(e, axis=-1, keepdims=True) + jnp.exp(sink - m))
        o = _dot((e * r).astype(BF16), jnp.concatenate([vc, vw], axis=0))
        o_ref[t * Q_BLOCK:(t + 1) * Q_BLOCK, :] = jnp.concatenate(
            [o[j * Q_BLOCK:(j + 1) * Q_BLOCK] for j in range(SWA_GROUP)], axis=-1).astype(o_ref.dtype)


def _swa_attention(qkv, qkv_c, tabs, gq, gk, sink_rows):
    bsz, l, _ = qkv.shape
    n_ctx = qkv_c.shape[1]
    cos_t, sin_t = tabs
    gw = SWA_GROUP * HEAD_DIM
    q0 = (2 * DIFF_Q + DIFF_V) // gw
    k0 = (2 * DIFF_Q + DIFF_V + SWA_Q) // HEAD_DIM
    v0 = k0 + SWA_KV_HEADS
    vec = pl.BlockSpec((1, HEAD_DIM), lambda bi, h, i: (0, 0))
    return pl.pallas_call(
        functools.partial(_swa_kernel, seq=l),
        grid=(bsz, SWA_KV_HEADS, l // (2 * Q_BLOCK)),
        in_specs=[pl.BlockSpec((None, l, gw), lambda bi, h, i: (bi, 0, q0 + h)),
                  pl.BlockSpec((None, l, HEAD_DIM), lambda bi, h, i: (bi, 0, k0 + h)),
                  pl.BlockSpec((None, l, HEAD_DIM), lambda bi, h, i: (bi, 0, v0 + h)),
                  pl.BlockSpec((None, n_ctx, HEAD_DIM), lambda bi, h, i: (bi, 0, k0 + h)),
                  pl.BlockSpec((None, n_ctx, HEAD_DIM), lambda bi, h, i: (bi, 0, v0 + h)),
                  pl.BlockSpec((l, HEAD_DIM), lambda bi, h, i: (0, 0)),
                  pl.BlockSpec((l, HEAD_DIM), lambda bi, h, i: (0, 0)),
                  vec, vec,
                  pl.BlockSpec((SWA_HEADS, HEAD_DIM), lambda bi, h, i: (0, 0))],
        out_specs=pl.BlockSpec((None, 2 * Q_BLOCK, gw), lambda bi, h, i: (bi, i, h)),
        out_shape=jax.ShapeDtypeStruct((bsz, l, SWA_Q), BF16),
        scratch_shapes=[pltpu.VMEM((SWA_GROUP, l, HEAD_DIM), BF16),
                        pltpu.VMEM((l + 2 * WINDOW, HEAD_DIM), BF16), pltpu.VMEM((l + 2 * WINDOW, HEAD_DIM), BF16),
                        pltpu.VMEM((n_ctx, HEAD_DIM), BF16), pltpu.VMEM((n_ctx, HEAD_DIM), BF16)],
        compiler_params=_params("parallel", "parallel", "arbitrary"),
        name="swa_attention",
    )(qkv, qkv, qkv, qkv_c, qkv_c, cos_t, sin_t, gq, gk, sink_rows)


def _router_kernel(x_ref, g_ref, sc_ref, sh_ref, w_ref, b_ref, tri_ref, h_ref, e_ref, p_ref, cnt_ref, run_scr):
    @pl.when((pl.program_id(0) == 0) & (pl.program_id(1) == 0))
    def _():
        run_scr[...] = jnp.zeros(run_scr.shape, F32)

    h = _norm_mod(x_ref[...], g_ref[...], sc_ref[...], sh_ref[...])
    h_ref[...] = h
    logits = jnp.dot(h, w_ref[...], precision=HIGHEST, preferred_element_type=F32) + b_ref[...]
    lane_i = lax.broadcasted_iota(jnp.int32, (1, LANES), 1)
    lane = lane_i.astype(F32)
    big = float(LANES)
    lg = jnp.where(lane_i < MOE_GROUPS, logits, NEG_INF)
    mg = jnp.max(lg, axis=-1, keepdims=True)
    p_top = 1.0 / jnp.sum(jnp.exp(lg - mg), axis=-1, keepdims=True)
    grp = jnp.min(jnp.where(lg == mg, lane, big), axis=-1, keepdims=True)
    e_lane = lane - MOE_GROUPS
    in_grp = (e_lane >= grp * MOE_PER_GROUP) & (e_lane < (grp + 1.0) * MOE_PER_GROUP)
    le = jnp.where(in_grp, logits, NEG_INF)
    t1 = jnp.max(le, axis=-1, keepdims=True)
    i1 = jnp.min(jnp.where(le == t1, lane, big), axis=-1, keepdims=True)
    le2 = jnp.where(lane == i1, NEG_INF, le)
    t2 = jnp.max(le2, axis=-1, keepdims=True)
    i2 = jnp.min(jnp.where(le2 == t2, lane, big), axis=-1, keepdims=True)
    w2 = jnp.exp(t2 - t1)
    inv = p_top / (1.0 + w2)
    hot1 = lane == i1
    hot2 = lane == i2
    onehot = jnp.where(hot1 | hot2, 1.0, 0.0)
    before = run_scr[...] + _dot(tri_ref[...], onehot.astype(BF16))
    r1 = jnp.sum(jnp.where(hot1, before, 0.0), axis=-1, keepdims=True)
    r2 = jnp.sum(jnp.where(hot2, before, 0.0), axis=-1, keepdims=True)
    run_scr[...] = run_scr[...] + jnp.sum(onehot, axis=0, keepdims=True)
    cnt_ref[...] = run_scr[...]
    e_ref[...] = jnp.where(lane_i == 0, i1 - MOE_GROUPS,
                           jnp.where(lane_i == 1, i2 - MOE_GROUPS,
                                     jnp.where(lane_i == 2, r1, jnp.where(lane_i == 3, r2, 0.0)))).astype(jnp.int32)
    p_ref[...] = jnp.where(lane_i == 0, inv, jnp.where(lane_i == 1, inv * w2, 0.0))


def _router(x, g, sc, sh, w, b, tm):
    bsz, l, d = x.shape
    row = lambda bi, i: (bi, i, 0)
    tri = (jnp.arange(tm)[:, None] > jnp.arange(tm)[None, :]).astype(BF16)
    return pl.pallas_call(
        _router_kernel,
        grid=(bsz, l // tm),
        in_specs=[pl.BlockSpec((None, tm, d), row),
                  pl.BlockSpec((1, d), lambda bi, i: (0, 0)),
                  pl.BlockSpec((None, 1, d), lambda bi, i: (bi, 0, 0)),
                  pl.BlockSpec((None, 1, d), lambda bi, i: (bi, 0, 0)),
                  pl.BlockSpec((d, LANES), lambda bi, i: (0, 0)),
                  pl.BlockSpec((1, LANES), lambda bi, i: (0, 0)),
                  pl.BlockSpec((tm, tm), lambda bi, i: (0, 0))],
        out_specs=[pl.BlockSpec((None, tm, d), row),
                   pl.BlockSpec((None, tm, LANES), row),
                   pl.BlockSpec((None, tm, LANES), row),
                   pl.BlockSpec((1, LANES), lambda bi, i: (0, 0))],
        out_shape=[jax.ShapeDtypeStruct((bsz, l, d), F32),
                   jax.ShapeDtypeStruct((bsz, l, LANES), jnp.int32),
                   jax.ShapeDtypeStruct((bsz, l, LANES), F32),
                   jax.ShapeDtypeStruct((1, LANES), F32)],
        scratch_shapes=[pltpu.VMEM((1, LANES), F32)],
        compiler_params=_params("arbitrary", "arbitrary"),
        name="moe_router",
    )(x, g, sc, sh, w, b, tri)


def _dispatch_kernel(slot_ref, h_ref, init_ref, xb_ref, sem):
    del init_ref
    tm = h_ref.shape[0]
    base = pl.program_id(0) * tm

    def copy(r, k):
        dst = slot_ref[(base + r) * MOE_TOPK + k]
        return pltpu.make_async_copy(h_ref.at[pl.ds(r, 1)], xb_ref.at[pl.ds(dst, 1)], sem)

    def start(r, carry):
        for k in range(MOE_TOPK):
            copy(r, k).start(priority=k)
        return carry

    def wait(r, carry):
        for k in range(MOE_TOPK):
            copy(r, k).wait()
        return carry

    lax.fori_loop(0, tm, start, 0, unroll=8)
    lax.fori_loop(0, tm, wait, 0, unroll=8)


def _dispatch(slot, h, zeros, tm):
    n, d = h.shape
    n_rows = zeros.shape[0]
    grid_spec = pltpu.PrefetchScalarGridSpec(
        num_scalar_prefetch=1,
        grid=(n // tm,),
        in_specs=[pl.BlockSpec((tm, d), lambda i, s: (i, 0)),
                  pl.BlockSpec(memory_space=pl.ANY)],
        out_specs=pl.BlockSpec(memory_space=pl.ANY),
        scratch_shapes=[pltpu.SemaphoreType.DMA(())],
    )
    return pl.pallas_call(
        _dispatch_kernel,
        grid_spec=grid_spec,
        out_shape=jax.ShapeDtypeStruct((n_rows, d), F32),
        input_output_aliases={2: 0},
        compiler_params=_params("arbitrary"),
        name="moe_dispatch",
    )(slot, h, zeros)


def _combine_kernel(slot_ref, x_ref, g_ref, p_ref, yb_ref, o_ref, y_scr, sem):
    tm = x_ref.shape[0]
    base = pl.program_id(0) * tm

    def copy(r, k):
        src = slot_ref[(base + r) * MOE_TOPK + k]
        return pltpu.make_async_copy(yb_ref.at[pl.ds(src, 1)], y_scr.at[k, pl.ds(r, 1)], sem)

    def start(r, carry):
        for k in range(MOE_TOPK):
            copy(r, k).start(priority=k)
        return carry

    def wait(r, carry):
        for k in range(MOE_TOPK):
            copy(r, k).wait()
        return carry

    lax.fori_loop(0, tm, start, 0, unroll=8)
    lax.fori_loop(0, tm, wait, 0, unroll=8)
    p = p_ref[...]
    moe = y_scr[0] * p[:, 0:1]
    for k in range(1, MOE_TOPK):
        moe = moe + y_scr[k] * p[:, k:k + 1]
    o_ref[...] = x_ref[...] + g_ref[...] * moe


def _combine(slot, x, g, gates, yb, tm):
    bsz, l, d = x.shape
    nt = l // tm
    grid_spec = pltpu.PrefetchScalarGridSpec(
        num_scalar_prefetch=1,
        grid=(bsz * nt,),
        in_specs=[pl.BlockSpec((None, tm, d), lambda i, s: (i // nt, i % nt, 0)),
                  pl.BlockSpec((None, 1, d), lambda i, s: (i // nt, 0, 0)),
                  pl.BlockSpec((None, tm, LANES), lambda i, s: (i // nt, i % nt, 0)),
                  pl.BlockSpec(memory_space=pl.ANY)],
        out_specs=pl.BlockSpec((None, tm, d), lambda i, s: (i // nt, i % nt, 0)),
        scratch_shapes=[pltpu.VMEM((MOE_TOPK, tm, d), F32), pltpu.SemaphoreType.DMA(())],
    )
    return pl.pallas_call(
        _combine_kernel,
        grid_spec=grid_spec,
        out_shape=jax.ShapeDtypeStruct((bsz, l, d), F32),
        compiler_params=_params("arbitrary"),
        name="moe_combine",
    )(slot, x, g, gates, yb)


def _expert_kernel(be_ref, nu_ref, x_ref, wg_ref, wu_ref, wd_ref, o_ref):
    i = pl.program_id(0)

    @pl.when(i < nu_ref[0])
    def _():
        x = x_ref[...].astype(BF16)
        ff = wg_ref.shape[1]
        acc = jnp.zeros(o_ref.shape, F32)
        for c in range(ff // FF_CHUNK):
            cols = slice(c * FF_CHUNK, (c + 1) * FF_CHUNK)
            gate = _dot(x, wg_ref[:, cols])
            up = _dot(x, wu_ref[:, cols])
            hidden = (gate * jax.nn.sigmoid(gate) * up).astype(BF16)
            acc = acc + _dot(hidden, wd_ref[cols, :])
        o_ref[...] = acc

    @pl.when(i >= nu_ref[0])
    def _():
        o_ref[...] = jnp.zeros(o_ref.shape, F32)


def _expert_ffn(xb, block_e, n_used, w_gate, w_up, w_down, layer):
    n_blocks, _, d = xb.shape
    ff = w_gate.shape[2]
    base = layer * MOE_EXPERTS
    grid_spec = pltpu.PrefetchScalarGridSpec(
        num_scalar_prefetch=2,
        grid=(n_blocks,),
        in_specs=[pl.BlockSpec((None, MOE_BLOCK, d), lambda i, be, nu: (i, 0, 0)),
                  pl.BlockSpec((None, d, ff), lambda i, be, nu: (base + be[i], 0, 0)),
                  pl.BlockSpec((None, d, ff), lambda i, be, nu: (base + be[i], 0, 0)),
                  pl.BlockSpec((None, ff, d), lambda i, be, nu: (base + be[i], 0, 0))],
        out_specs=pl.BlockSpec((None, MOE_BLOCK, d), lambda i, be, nu: (i, 0, 0)),
    )
    return pl.pallas_call(
        _expert_kernel,
        grid_spec=grid_spec,
        out_shape=jax.ShapeDtypeStruct((n_blocks, MOE_BLOCK, d), F32),
        compiler_params=_params("arbitrary"),
        name="expert_ffn",
    )(block_e, n_used, xb, w_gate, w_up, w_down)


def _moe_rows(n_tokens):
    return (-(-(n_tokens * MOE_TOPK) // MOE_BLOCK) + MOE_EXPERTS) * MOE_BLOCK


def _hier_moe(x, g, sc, sh, gate2, w_route, b_route, w_gate, w_up, w_down, layer, zeros):
    bsz, l, d = x.shape
    n = bsz * l
    tm = min(512, l)
    h, route, gates, cnt = _router(x, g, sc, sh, w_route, b_route, tm)
    route = route.reshape(n, LANES)
    expert = route[:, :MOE_TOPK]
    rank = route[:, MOE_TOPK:2 * MOE_TOPK]
    counts = cnt[0, MOE_GROUPS:MOE_GROUPS + MOE_EXPERTS].astype(jnp.int32)
    padded = (counts + MOE_BLOCK - 1) // MOE_BLOCK * MOE_BLOCK
    pad_end = jnp.cumsum(padded)
    pad_start = pad_end - padded
    hit = expert[:, :, None] == jnp.arange(MOE_EXPERTS, dtype=jnp.int32)
    slot = (jnp.sum(jnp.where(hit, pad_start, 0), axis=-1) + rank).reshape(-1)
    n_blocks = _moe_rows(n) // MOE_BLOCK
    block_row = jnp.arange(n_blocks, dtype=jnp.int32)[:, None] * MOE_BLOCK
    block_e = jnp.minimum(jnp.sum((pad_end[None, :] <= block_row).astype(jnp.int32), axis=-1), MOE_EXPERTS - 1)
    n_used = (pad_end[-1:] // MOE_BLOCK).astype(jnp.int32)
    xb = _dispatch(slot, h.reshape(n, d), zeros, tm).reshape(n_blocks, MOE_BLOCK, d)
    yb = _expert_ffn(xb, block_e, n_used, w_gate, w_up, w_down, layer).reshape(-1, d)
    return _combine(slot, x, gate2, gates, yb, tm)


def _short_conv_kernel(zv_ref, z1_ref, z2_ref, wv_ref, w1_ref, w2_ref, bv_ref, b1_ref, b2_ref,
                       v_ref, x1_ref, x2_ref):
    rdx, q = zv_ref.shape[0], zv_ref.shape[1]
    row = lax.broadcasted_iota(jnp.int32, (q, 1), 0)

    for z_ref, w_ref, b_ref, o_ref in ((zv_ref, wv_ref, bv_ref, v_ref), (z1_ref, w1_ref, b1_ref, x1_ref),
                                       (z2_ref, w2_ref, b2_ref, x2_ref)):
        z = [z_ref[k].astype(F32) for k in range(rdx)]
        w = w_ref[...]
        b = b_ref[...]
        for k in range(rdx):
            prev = z[k - 1] if k > 0 else jnp.where(row == 0, 0.0, pltpu.roll(z[rdx - 1], 1, 0))
            nxt = z[k + 1] if k < rdx - 1 else jnp.where(row == q - 1, 0.0, pltpu.roll(z[0], q - 1, 0))
            o_ref[k] = (b + prev * w[0:1] + z[k] * w[1:2] + nxt * w[2:3]).astype(o_ref.dtype)


def _short_conv(z, conv_w, conv_b, tc):
    bsz, rdx, q, w3 = z.shape
    w = w3 // 3
    nc = w // tc
    zs = lambda k: pl.BlockSpec((None, rdx, q, tc), lambda bi, j: (bi, 0, 0, k * nc + j))
    ws = lambda k: pl.BlockSpec((8, tc), lambda bi, j: (0, k * nc + j))
    bs = lambda k: pl.BlockSpec((1, tc), lambda bi, j: (0, k * nc + j))
    out = pl.BlockSpec((None, rdx, q, tc), lambda bi, j: (bi, 0, 0, j))
    return pl.pallas_call(
        _short_conv_kernel,
        grid=(bsz, nc),
        in_specs=[zs(0), zs(1), zs(2), ws(0), ws(1), ws(2), bs(0), bs(1), bs(2)],
        out_specs=[out, out, out],
        out_shape=[jax.ShapeDtypeStruct((bsz, rdx, q, w), BF16)] * 3,
        compiler_params=_params("parallel", "parallel"),
        name="short_conv",
    )(z, z, z, conv_w, conv_w, conv_w, conv_b, conv_b, conv_b)


def _filter_kernel(z_ref, w1_ref, b1_ref, f1_ref, w2_ref, b2_ref, f2_ref, w3_ref, dl_ref, bw_ref, o_ref, a_scr):
    n = z_ref.shape[0]

    @pl.when(pl.program_id(0) == 0)
    def _():
        a = jnp.sin(f1_ref[...] * (jnp.dot(z_ref[...], w1_ref[...], precision=HIGHEST,
                                           preferred_element_type=F32) + b1_ref[...]))
        a_scr[...] = jnp.sin(f2_ref[...] * (jnp.dot(a, w2_ref[...], precision=HIGHEST,
                                                    preferred_element_type=F32) + b2_ref[...]))

    hf = jnp.dot(a_scr[...], w3_ref[...], precision=HIGHEST, preferred_element_type=F32)
    q = n // DFT_RADIX
    row = lax.broadcasted_iota(jnp.int32, (n, 1), 0)
    assert q & (q - 1) == 0
    tap = DFT_RADIX * (row & (q - 1)) + lax.shift_right_logical(row, q.bit_length() - 1)
    t = tap.astype(F32) * (1.0 / (n - 1))
    hf = hf * jnp.exp(-t * dl_ref[...])
    hf = jnp.where((tap == 0) & (bw_ref[...] > 0.5), 0.0, hf)
    for k in range(DFT_RADIX):
        o_ref[k] = hf[k * q:(k + 1) * q]


def _hyena_filters(z, w1, b1, f1, w2, b2, f2, w3, deltas, is_bwd, tn):
    n = z.shape[0]
    cols = w3.shape[1]
    full = lambda a: pl.BlockSpec(a.shape, lambda j: (0, 0))
    tile = lambda r: pl.BlockSpec((r, tn), lambda j: (0, j))
    return pl.pallas_call(
        _filter_kernel,
        grid=(cols // tn,),
        in_specs=[full(z), full(w1), full(b1), full(f1), full(w2), full(b2), full(f2),
                  tile(w3.shape[0]), tile(1), tile(1)],
        out_specs=pl.BlockSpec((DFT_RADIX, n // DFT_RADIX, tn), lambda j: (0, 0, j)),
        out_shape=jax.ShapeDtypeStruct((DFT_RADIX, n // DFT_RADIX, cols), F32),
        scratch_shapes=[pltpu.VMEM((n, LANES), F32)],
        compiler_params=_params("arbitrary"),
        name="hyena_filters",
    )(z, w1, b1, f1, w2, b2, f2, w3, deltas, is_bwd)


def _dft_groups():
    groups = [(0, 1), (DFT_RADIX // 2, -1)]
    for m in range(1, DFT_RADIX // 2):
        groups += [(m, -1), (m, 1)]
    return groups


def _dft_coef(g, k):
    m, b = _dft_groups()[g]
    assert DFT_RADIX in (2, 4)
    quarter = (4 * m * k // DFT_RADIX) % 4
    return (1, 0, -1, 0)[quarter], (0, 1, 0, -1)[quarter], b


def _signed_sum(terms):
    acc = None
    for coef, thunk in terms:
        if coef == 0:
            continue
        val = thunk()
        if acc is None:
            acc = val if coef > 0 else -val
        else:
            acc = acc + val if coef > 0 else acc - val
    return acc


def _spectrum(cprod, sprod, want_c=True, want_s=True):
    cprod = functools.lru_cache(maxsize=None)(cprod)
    sprod = functools.lru_cache(maxsize=None)(sprod)
    out = []
    for g in range(DFT_RADIX):
        cu, su = [], []
        for k in range(DFT_RADIX):
            ca, sa, b = _dft_coef(g, k)
            cu += [(ca, functools.partial(cprod, k)), (-b * sa, functools.partial(sprod, k))]
            su += [(sa, functools.partial(cprod, k)), (b * ca, functools.partial(sprod, k))]
        out.append((_signed_sum(cu) if want_c else None, _signed_sum(su) if want_s else None))
    return out


def _dft_tables(n):
    q = n // DFT_RADIX
    r = 1 << (q.bit_length() // 2)
    f = jnp.arange(q, dtype=jnp.int32)[:, None]
    idx = jnp.arange(r, dtype=jnp.int32)[None, :]
    ang = lambda s: (((2 * f + 1) * s) % (4 * n)).astype(F32) * (math.pi / (2 * n))
    hi = ang(DFT_RADIX * r * idx)[:, :q // r]
    ch, sh = jnp.cos(hi), jnp.sin(hi)
    fwd, inv = [], []
    for trig in ("cos", "sin"):
        for k in range(DFT_RADIX):
            lo = ang(DFT_RADIX * idx + k)
            cl, sl = jnp.cos(lo), jnp.sin(lo)
            a, b, sign = (ch, sh, -1.0) if trig == "cos" else (sh, ch, 1.0)
            fwd.append((a[:, :, None] * cl[:, None, :] + sign * b[:, :, None] * sl[:, None, :]).reshape(q, q))
            inv.append((a.T[:, None, :] * cl.T[None, :, :] + sign * b.T[:, None, :] * sl.T[None, :, :]).reshape(q, q))
    shape = (2, DFT_RADIX, q, q)
    return jnp.stack(fwd).astype(BF16).reshape(shape), jnp.stack(inv).astype(BF16).reshape(shape)


def _dft_filter_kernel(m_ref, hf_ref, hb_ref, kr_ref, ki_ref, sum_scr, dif_scr, *, scale):
    @pl.when(pl.program_id(2) == 0)
    def _():
        hf = hf_ref[...]
        hb = hb_ref[...]
        sum_scr[...] = (hf + hb).astype(BF16)
        dif_scr[...] = (hb - hf).astype(BF16)

    of_sum = _spectrum(lambda k: _dot(m_ref[0, k], sum_scr[k]), lambda k: _dot(m_ref[1, k], sum_scr[k]),
                       want_s=False)
    of_dif = _spectrum(lambda k: _dot(m_ref[0, k], dif_scr[k]), lambda k: _dot(m_ref[1, k], dif_scr[k]),
                       want_c=False)
    for g in range(DFT_RADIX):
        kr_ref[g] = of_sum[g][0] * scale
        ki_ref[g] = of_dif[g][1] * scale


def _dft_filter(fwd, filt, width, tf, tn):
    rdx, q = fwd.shape[1], fwd.shape[2]
    nc = width // tn
    out = pl.BlockSpec((None, rdx, tf, tn), lambda o, j, i: (o, 0, i, j))
    return pl.pallas_call(
        functools.partial(_dft_filter_kernel, scale=1.0 / (rdx * q)),
        grid=(HYENA_ORDER, nc, q // tf),
        in_specs=[pl.BlockSpec((2, rdx, tf, q), lambda o, j, i: (0, 0, i, 0)),
                  pl.BlockSpec((rdx, q, tn), lambda o, j, i: (0, 0, (2 * o) * nc + j)),
                  pl.BlockSpec((rdx, q, tn), lambda o, j, i: (0, 0, (2 * o + 1) * nc + j))],
        out_specs=[out, out],
        out_shape=[jax.ShapeDtypeStruct((HYENA_ORDER, rdx, q, width), F32)] * 2,
        scratch_shapes=[pltpu.VMEM((rdx, q, tn), BF16), pltpu.VMEM((rdx, q, tn), BF16)],
        compiler_params=_params("parallel", "parallel", "arbitrary"),
        name="dft_filter",
    )(fwd, filt, filt)


def _dft_fwd_kernel(m_ref, u_ref, kr_ref, ki_ref, pr_ref, pi_ref):
    spec = _spectrum(lambda k: _dot(m_ref[0, k], u_ref[k]), lambda k: _dot(m_ref[1, k], u_ref[k]))
    for g, (ur, us) in enumerate(spec):
        kr = kr_ref[g]
        ki = ki_ref[g]
        pr_ref[g] = (ur * kr + us * ki).astype(pr_ref.dtype)
        pi_ref[g] = (ur * ki - us * kr).astype(pi_ref.dtype)


def _dft_fwd(fwd, u, kr, ki, order, tf, tn):
    bsz, rdx, q, width = u.shape
    out = pl.BlockSpec((None, rdx, tf, tn), lambda bi, j, i: (bi, 0, i, j))
    ks = pl.BlockSpec((None, rdx, tf, tn), lambda bi, j, i: (order, 0, i, j))
    return pl.pallas_call(
        _dft_fwd_kernel,
        grid=(bsz, width // tn, q // tf),
        in_specs=[pl.BlockSpec((2, rdx, tf, q), lambda bi, j, i: (0, 0, i, 0)),
                  pl.BlockSpec((None, rdx, q, tn), lambda bi, j, i: (bi, 0, 0, j)),
                  ks, ks],
        out_specs=[out, out],
        out_shape=[jax.ShapeDtypeStruct((bsz, rdx, q, width), BF16)] * 2,
        compiler_params=_params("parallel", "parallel", "parallel"),
        name="dft_fwd",
    )(fwd, u, kr, ki)


def _dft_inv_kernel(m_ref, pr_ref, pi_ref, u_ref, b_ref, x_ref, o_ref, ra_scr, rb_scr):
    for k in range(DFT_RADIX):
        @pl.when((pl.program_id(3) == 0) & (pl.program_id(2) == k))
        def _(k=k):
            ra, rb = [], []
            for g in range(DFT_RADIX):
                ca, sa, b = _dft_coef(g, k)
                pr = functools.partial(lambda g: pr_ref[g].astype(F32), g)
                pi = functools.partial(lambda g: pi_ref[g].astype(F32), g)
                ra += [(ca, pr), (-sa, pi)]
                rb += [(b * sa, pr), (b * ca, pi)]
            ra_scr[...] = _signed_sum(ra).astype(BF16)
            rb_scr[...] = _signed_sum(rb).astype(BF16)

    y = _dot(m_ref[0], ra_scr[...]) - _dot(m_ref[1], rb_scr[...])
    u = u_ref[...].astype(F32)
    o_ref[...] = (x_ref[...] * (y + u * b_ref[...])).astype(o_ref.dtype)


def _dft_inv(inv, pr, pi, u, bias, x, order, tt, tn):
    bsz, rdx, q, width = u.shape
    tile = pl.BlockSpec((None, None, tt, tn), lambda bi, j, k, i: (bi, k, i, j))
    spec = pl.BlockSpec((None, rdx, q, tn), lambda bi, j, k, i: (bi, 0, 0, j))
    return pl.pallas_call(
        _dft_inv_kernel,
        grid=(bsz, width // tn, rdx, q // tt),
        in_specs=[pl.BlockSpec((2, None, tt, q), lambda bi, j, k, i: (0, k, i, 0)),
                  spec, spec, tile,
                  pl.BlockSpec((None, 1, tn), lambda bi, j, k, i: (order, 0, j)),
                  tile],
        out_specs=tile,
        out_shape=jax.ShapeDtypeStruct((bsz, rdx, q, width), BF16),
        scratch_shapes=[pltpu.VMEM((q, tn), BF16), pltpu.VMEM((q, tn), BF16)],
        compiler_params=_params("parallel", "parallel", "arbitrary", "arbitrary"),
        name="dft_inv",
    )(inv, pr, pi, u, bias, x)


def _filter_features(n):
    t = jnp.linspace(0.0, 1.0, n, dtype=F32)[:, None]
    w = 2.0 * math.pi * jnp.arange(n, dtype=F32) / n
    f = jnp.linspace(1e-4, FILTER_BANDS - 1, FILTER_BANDS, dtype=F32)
    z = jnp.concatenate([t, jnp.cos(w[:, None] * f), -jnp.sin(w[:, None] * f)], axis=-1)
    return jnp.pad(z, ((0, 0), (0, LANES - FILTER_EMB)))


def _pad_to(a, shape):
    return jnp.pad(a, [(0, t - s) for s, t in zip(a.shape, shape)])


def kernel(x, c, ctx, c_ctx, ada_w, ada_b, norm1_g, norm2_g, attn_w_in, attn_w_out, diff_q_g, diff_k_g, diff_lq1, diff_lk1, diff_lq2, diff_lk2, diff_sub_g, swa_q_g, swa_k_g, swa_sink, hy_w_in, hy_b_in, hy_conv_w, hy_conv_b, flt_w1, flt_b1, flt_f1, flt_w2, flt_b2, flt_f2, flt_w3, hy_bias, hy_w_out, hy_b_out, moe_wg1, moe_bg1, moe_wg2, moe_bg2, moe_w_gate, moe_w_up, moe_w_down):
    bsz, seq, d = x.shape
    depth = ada_w.shape[0]
    n_ctx = ctx.shape[1]
    cond = jnp.concatenate([c, c_ctx[None, :], jnp.zeros((8 - bsz - 1, d), F32)], axis=0)
    ada_b3 = ada_b[:, None, :]
    zero_bias = lambda n: jnp.zeros((1, n), F32)
    expert_w = None
    moe_zeros = []
    tm = min(512, seq)

    for layer in range(depth):
        even = layer % 2 == 0
        i = layer // 2
        ctx_live = any(j % 2 == 0 for j in range(layer + 1, depth))
        assert not ctx_live, "context-stream update is only needed for deeper stacks"
        mod = _adaln_mod(cond, ada_w, ada_b3, layer)
        sh1, sc1, g1, sh2, sc2, g2 = [mod[:, k * d:(k + 1) * d] for k in range(6)]
        lat = lambda a: a[:bsz, None, :]
        cmod = lambda a: jnp.broadcast_to(a[bsz:bsz + 1, None, :], (bsz, 1, d))
        n1 = norm1_g[layer][None, :]
        if even:
            w_in = attn_w_in[i].astype(BF16)
            qkv = _norm_linear(x, n1, lat(sc1), lat(sh1), w_in, zero_bias(ATTN_IN), tm, 1536, "attn_in")
            qkv_c = _norm_linear(ctx, n1, cmod(sc1), cmod(sh1), w_in, zero_bias(ATTN_IN), min(tm, n_ctx), 1536,
                                 "attn_in_ctx")
            lam_init = 0.8 - 0.6 * math.exp(-0.3 * layer)
            lamp = _pad_to(jnp.stack([diff_lq1[i], diff_lk1[i], diff_lq2[i], diff_lk2[i]]), (8, 2 * DIFF_SUB))
            two = lambda a: jnp.tile(a, 2)[None, :]
            first = expert_w is None
            to_cast = [moe_w_gate, moe_w_up, moe_w_down] if first else []
            zero_shapes = [(_moe_rows(bsz * seq), d)] * depth if first else []
            o_diff, cast, zeros = _diff_attention(
                qkv, qkv_c, _rope_tables(seq, DIFF_SUB, 2 * DIFF_SUB), two(diff_q_g[i]), two(diff_k_g[i]),
                diff_sub_g[i][None, :], lamp, lam_init, min(2 * Q_BLOCK, seq), to_cast, zero_shapes)
            moe_zeros += zeros
            if cast:
                expert_w = (cast[0].reshape(depth * MOE_EXPERTS, d, -1), cast[1].reshape(depth * MOE_EXPERTS, d, -1),
                            cast[2].reshape(depth * MOE_EXPERTS, -1, d))
            sink_rows = jnp.broadcast_to(swa_sink[i][:, None], (SWA_HEADS, HEAD_DIM))
            o_swa = _swa_attention(qkv, qkv_c, _rope_tables(seq, HEAD_DIM, HEAD_DIM), swa_q_g[i][None, :],
                                   swa_k_g[i][None, :], sink_rows)
            w_out = attn_w_out[i].astype(BF16)
            x = _mm_residual([o_diff, o_swa], [w_out[:DIFF_V], w_out[DIFF_V:]], zero_bias(d), x, lat(g1), tm,
                             "attn_out")
        else:
            width = hy_w_out.shape[1]
            q = seq // DFT_RADIX
            z = _norm_linear(x, n1, lat(sc1), lat(sh1), hy_w_in[i].astype(BF16), hy_b_in[i][None, :], tm, 1536,
                             "hyena_in", classes=DFT_RADIX)
            v, x1, x2 = _short_conv(z, _pad_to(hy_conv_w[i], (8, 3 * width)), hy_conv_b[i][None, :], 128)
            max_decay = math.log(DECAY_TARGET) / DECAY_FAST
            min_decay = math.log(DECAY_TARGET) / DECAY_SLOW
            deltas = jnp.abs(jnp.tile(jnp.linspace(min_decay, max_decay, width, dtype=F32), 2 * HYENA_ORDER))[None, :]
            is_bwd = jnp.tile(jnp.concatenate([jnp.zeros((width,), F32), jnp.ones((width,), F32)]),
                              HYENA_ORDER)[None, :]
            sq = (LANES, LANES)
            feats = _filter_features(seq)
            feats = jnp.concatenate([feats[k::DFT_RADIX] for k in range(DFT_RADIX)], axis=0)
            filt = _hyena_filters(
                feats, _pad_to(flt_w1[i], sq), _pad_to(flt_b1[i][None, :], (1, LANES)),
                _pad_to(flt_f1[i][None, :], (1, LANES)), _pad_to(flt_w2[i], sq),
                _pad_to(flt_b2[i][None, :], (1, LANES)), _pad_to(flt_f2[i][None, :], (1, LANES)),
                _pad_to(flt_w3[i], (LANES, 2 * HYENA_ORDER * width)), deltas, is_bwd, 256)
            fwd, inv = _dft_tables(seq)
            tf = min(256, q)
            tn = min(512, width)
            kr, ki = _dft_filter(fwd, filt, width, tf, min(256, width))
            bias = hy_bias[i][:, None, :]
            pr, pi = _dft_fwd(fwd, v, kr, ki, 0, tf, tn)
            tt = min(1024, q)
            y1 = _dft_inv(inv, pr, pi, v, bias, x1, 0, tt, tn)
            pr, pi = _dft_fwd(fwd, y1, kr, ki, 1, tf, tn)
            y2 = _dft_inv(inv, pr, pi, y1, bias, x2, 1, tt, tn)
            x = _mm_residual([y2], [hy_w_out[i].astype(BF16)], hy_b_out[i][None, :], x, lat(g1), tm,
                             "hyena_out", classes=DFT_RADIX)
        w_route = _pad_to(jnp.concatenate([moe_wg1[layer], moe_wg2[layer]], axis=1), (d, LANES))
        b_route = _pad_to(jnp.concatenate([moe_bg1[layer], moe_bg2[layer]])[None, :], (1, LANES))
        x = _hier_moe(x, norm2_g[layer][None, :], lat(sc2), lat(sh2), lat(g2), w_route, b_route,
                      *expert_w, layer, moe_zeros[layer])
    return x
```

```python
import functools
import math

import jax
import jax.numpy as jnp
from jax import lax
from jax.experimental import pallas as pl
from jax.experimental.pallas import tpu as pltpu

F32 = jnp.float32
BF16 = jnp.bfloat16
HIGHEST = lax.Precision.HIGHEST

LANES = 128
V7X_VMEM_BYTES = 64 * 1024 * 1024
VMEM_LIMIT = V7X_VMEM_BYTES * 7 // 8

GRID_W = 64
HEAD_DIM = 128
ROPE_BASE = 10000.0
EPS = 1e-6
Q_BLOCK = 128
DIFF_HEADS = 8
DIFF_SUB = HEAD_DIM // 2
SWA_HEADS = 8
SWA_KV_HEADS = 2
SWA_GROUP = SWA_HEADS // SWA_KV_HEADS
WINDOW = 128
DIFF_Q = DIFF_HEADS * 2 * DIFF_SUB
DIFF_V = DIFF_HEADS * HEAD_DIM
SWA_Q = SWA_HEADS * HEAD_DIM
SWA_KV = SWA_KV_HEADS * HEAD_DIM
ATTN_IN = 2 * DIFF_Q + DIFF_V + SWA_Q + 2 * SWA_KV
HYENA_ORDER = 2
SHORT_CONV = 3
FILTER_EMB = 33
FILTER_BANDS = (FILTER_EMB - 1) // 2
FILTER_HIDDEN = 64
DECAY_FAST = 0.3
DECAY_SLOW = 1.5
DECAY_TARGET = 1e-2
MOE_GROUPS = 4
MOE_PER_GROUP = 8
MOE_EXPERTS = MOE_GROUPS * MOE_PER_GROUP
MOE_TOPK = 2
MOE_BLOCK = 256
FF_CHUNK = 1024
DIFF_KEY_CHUNK = 512
DFT_RADIX = 4
LOG2_E = 1.4426950408889634
NEG_INF = float("-inf")


def _tile(n, pref):
    t = pref
    while n % t:
        t //= 2
    return t


def _params(*sem):
    return pltpu.CompilerParams(dimension_semantics=sem, vmem_limit_bytes=VMEM_LIMIT)


def _dot(a, b):
    return jnp.dot(a, b, preferred_element_type=F32)


def _dot_nt(a, b):
    return lax.dot_general(a, b, (((1,), (1,)), ((), ())), preferred_element_type=F32)


def _mod_kernel(c_ref, w_ref, b_ref, o_ref):
    c = c_ref[...]
    s = c * jax.nn.sigmoid(c)
    o_ref[...] = jnp.dot(s, w_ref[...], precision=HIGHEST, preferred_element_type=F32) + b_ref[...]


def _adaln_mod(cond, ada_w, ada_b, layer):
    rows, d = cond.shape
    n = ada_w.shape[2]
    tn = _tile(n, 768)
    return pl.pallas_call(
        _mod_kernel,
        grid=(n // tn,),
        in_specs=[pl.BlockSpec((rows, d), lambda j: (0, 0)),
                  pl.BlockSpec((None, d, tn), lambda j: (layer, 0, j)),
                  pl.BlockSpec((None, 1, tn), lambda j: (layer, 0, j))],
        out_specs=pl.BlockSpec((rows, tn), lambda j: (0, j)),
        out_shape=jax.ShapeDtypeStruct((rows, n), F32),
        compiler_params=_params("parallel"),
        name="adaln_mod",
    )(cond, ada_w, ada_b)


def _norm_mod(x, g, sc, sh):
    inv = lax.rsqrt(jnp.mean(x * x, axis=-1, keepdims=True) + EPS)
    return ((x * inv) * g) * (1.0 + sc) + sh


def _class_perm(tm, classes):
    m = tm // classes
    out_row = jnp.arange(tm, dtype=jnp.int32)
    src = classes * (out_row % m) + out_row // m
    return (src[:, None] == jnp.arange(tm, dtype=jnp.int32)[None, :]).astype(BF16)


def _norm_linear_kernel(*refs, classes):
    if classes:
        p_ref, x_ref, g_ref, sc_ref, sh_ref, w_ref, b_ref, o_ref, h_ref = refs
    else:
        x_ref, g_ref, sc_ref, sh_ref, w_ref, b_ref, o_ref, h_ref = refs

    @pl.when(pl.program_id(2) == 0)
    def _():
        h = _norm_mod(x_ref[...], g_ref[...], sc_ref[...], sh_ref[...]).astype(BF16)
        if classes:
            h = _dot(p_ref[...], h).astype(BF16)
        h_ref[...] = h

    res = (_dot(h_ref[...], w_ref[...]) + b_ref[...]).astype(o_ref.dtype)
    if classes:
        m = res.shape[0] // classes
        for k in range(classes):
            o_ref[k] = res[k * m:(k + 1) * m]
    else:
        o_ref[...] = res


def _norm_linear(x, g, sc, sh, w, b, tm, tn, name, classes=0):
    bsz, l, d = x.shape
    n = w.shape[1]
    tn = _tile(n, tn)
    in_specs = [pl.BlockSpec((None, tm, d), lambda bi, i, j: (bi, i, 0)),
                pl.BlockSpec((1, d), lambda bi, i, j: (0, 0)),
                pl.BlockSpec((None, 1, d), lambda bi, i, j: (bi, 0, 0)),
                pl.BlockSpec((None, 1, d), lambda bi, i, j: (bi, 0, 0)),
                pl.BlockSpec((d, tn), lambda bi, i, j: (0, j)),
                pl.BlockSpec((1, tn), lambda bi, i, j: (0, j))]
    args = (x, g, sc, sh, w, b)
    if classes:
        in_specs = [pl.BlockSpec((tm, tm), lambda bi, i, j: (0, 0))] + in_specs
        args = (_class_perm(tm, classes),) + args
        out_specs = pl.BlockSpec((None, classes, tm // classes, tn), lambda bi, i, j: (bi, 0, i, j))
        out_shape = jax.ShapeDtypeStruct((bsz, classes, l // classes, n), BF16)
    else:
        out_specs = pl.BlockSpec((None, tm, tn), lambda bi, i, j: (bi, i, j))
        out_shape = jax.ShapeDtypeStruct((bsz, l, n), BF16)
    return pl.pallas_call(
        functools.partial(_norm_linear_kernel, classes=classes),
        grid=(bsz, l // tm, n // tn),
        in_specs=in_specs,
        out_specs=out_specs,
        out_shape=out_shape,
        scratch_shapes=[pltpu.VMEM((tm, d), BF16)],
        compiler_params=_params("parallel", "arbitrary", "arbitrary"),
        name=name,
    )(*args)


def _mm_res_kernel(*refs, n_pairs, classes):
    if classes:
        pt_ref, refs = refs[0], refs[1:]
    a_refs = refs[:n_pairs]
    w_refs = refs[n_pairs:2 * n_pairs]
    b_ref, x_ref, g_ref, o_ref = refs[2 * n_pairs:]
    acc = None
    for a_ref, w_ref in zip(a_refs, w_refs):
        if classes:
            a = jnp.concatenate([a_ref[k] for k in range(classes)], axis=0)
            a = _dot(pt_ref[...], a).astype(BF16)
        else:
            a = a_ref[...]
        prod = _dot(a, w_ref[...])
        acc = prod if acc is None else acc + prod
    o_ref[...] = x_ref[...] + g_ref[...] * (acc + b_ref[...])


def _mm_residual(a_list, w_list, b, x, g, tm, name, classes=0):
    bsz, l, n = x.shape
    n_pairs = len(a_list)
    if classes:
        in_specs = [pl.BlockSpec((tm, tm), lambda bi, i: (0, 0))]
        in_specs += [pl.BlockSpec((None, classes, tm // classes, a.shape[3]), lambda bi, i: (bi, 0, i, 0))
                     for a in a_list]
        args = (_class_perm(tm, classes).T,)
    else:
        in_specs = [pl.BlockSpec((None, tm, a.shape[2]), lambda bi, i: (bi, i, 0)) for a in a_list]
        args = ()
    in_specs += [pl.BlockSpec(w.shape, lambda bi, i: (0, 0)) for w in w_list]
    in_specs += [pl.BlockSpec((1, n), lambda bi, i: (0, 0)),
                 pl.BlockSpec((None, tm, n), lambda bi, i: (bi, i, 0)),
                 pl.BlockSpec((None, 1, n), lambda bi, i: (bi, 0, 0))]
    return pl.pallas_call(
        functools.partial(_mm_res_kernel, n_pairs=n_pairs, classes=classes),
        grid=(bsz, l // tm),
        in_specs=in_specs,
        out_specs=pl.BlockSpec((None, tm, n), lambda bi, i: (bi, i, 0)),
        out_shape=jax.ShapeDtypeStruct((bsz, l, n), F32),
        compiler_params=_params("parallel", "parallel"),
        name=name,
    )(*args, *a_list, *w_list, b, x, g)


def _rope_tables(n_tokens, dim, width):
    n_rows = n_tokens // GRID_W
    row = jnp.repeat(jnp.arange(n_rows, dtype=F32), GRID_W)
    col = jnp.tile(jnp.arange(GRID_W, dtype=F32), n_rows)
    n_freq = dim // 4
    inv = ROPE_BASE ** (-jnp.arange(n_freq, dtype=F32) / n_freq)
    ang = jnp.concatenate([row[:, None] * inv, col[:, None] * inv], axis=-1)
    ang = jnp.repeat(ang, 2, axis=-1)
    ang = jnp.tile(ang, (1, width // dim))
    sign = jnp.where(jnp.arange(width) % 2 == 0, -1.0, 1.0).astype(F32)
    return jnp.cos(ang), jnp.sin(ang) * sign


def _rope(x, cos_t, sin_t):
    width = x.shape[-1]
    lane = lax.broadcasted_iota(jnp.int32, (1, width), 1)
    partner = jnp.where(lane % 2 == 0, pltpu.roll(x, width - 1, 1), pltpu.roll(x, 1, 1))
    return x * cos_t + partner * sin_t


def _sub_rms(x, g):
    lane = lax.broadcasted_iota(jnp.int32, (1, 2 * DIFF_SUB), 1)
    lo = lane < DIFF_SUB
    sq = x * x
    ss_lo = jnp.sum(jnp.where(lo, sq, 0.0), axis=-1, keepdims=True)
    ss_hi = jnp.sum(jnp.where(lo, 0.0, sq), axis=-1, keepdims=True)
    inv = jnp.where(lo, lax.rsqrt(ss_lo / DIFF_SUB + EPS), lax.rsqrt(ss_hi / DIFF_SUB + EPS))
    return (x * inv) * g


def _rms(x, g):
    return (x * lax.rsqrt(jnp.mean(x * x, axis=-1, keepdims=True) + EPS)) * g


def _diff_attn_kernel(*refs, lam_init, n_ctx, tq, n_cast, n_zero):
    (q_ref, k_ref, v_ref, kc_ref, vc_ref, cos_ref, sin_ref, gq_ref, gk_ref, gs_ref, lamp_ref) = refs[:11]
    cast_in = refs[11:11 + n_cast]
    o_ref = refs[11 + n_cast]
    cast_out = refs[12 + n_cast:12 + 2 * n_cast]
    zero_out = refs[12 + 2 * n_cast:12 + 2 * n_cast + n_zero]
    q_scr, k_scr, v_scr, s_scr, lam_scr = refs[12 + 2 * n_cast + n_zero:]
    for src, dst in zip(cast_in, cast_out):
        dst[...] = src[...].astype(dst.dtype)
    for dst in zero_out:
        dst[...] = jnp.zeros(dst.shape, dst.dtype)
    n_keys = k_scr.shape[0]

    @pl.when(pl.program_id(2) == 0)
    def _():
        lane = lax.broadcasted_iota(jnp.int32, (1, 2 * DIFF_SUB), 1)
        lo = lane < DIFF_SUB
        gk = gk_ref[...]
        cos_t = cos_ref[...]
        sin_t = sin_ref[...]
        k_scr[0:n_ctx, :] = _sub_rms(kc_ref[...].astype(F32), gk).astype(BF16)
        k_scr[n_ctx:, :] = _rope(_sub_rms(k_ref[...].astype(F32), gk), cos_t, sin_t).astype(BF16)
        v_scr[0:n_ctx, 0:HEAD_DIM] = vc_ref[...].astype(BF16)
        v_scr[n_ctx:, 0:HEAD_DIM] = v_ref[...].astype(BF16)
        v_scr[:, HEAD_DIM:] = jnp.broadcast_to(jnp.where(lane == 0, 1.0, 0.0).astype(BF16), (n_keys, HEAD_DIM))
        q = _rope(_sub_rms(q_ref[...].astype(F32), gq_ref[...]), cos_t, sin_t) * (DIFF_SUB ** -0.5 * LOG2_E)
        q_scr[0] = jnp.where(lo, q, 0.0).astype(BF16)
        q_scr[1] = jnp.where(lo, 0.0, q).astype(BF16)
        lp = lamp_ref[...]
        t1 = jnp.sum(lp[0:1] * lp[1:2], axis=-1, keepdims=True)
        t2 = jnp.sum(lp[2:3] * lp[3:4], axis=-1, keepdims=True)
        lam_scr[...] = jnp.broadcast_to(jnp.exp(t1) - jnp.exp(t2) + lam_init, lam_scr.shape)

    chunks = [(c0, min(c0 + DIFF_KEY_CHUNK, n_keys)) for c0 in range(0, n_keys, DIFF_KEY_CHUNK)]
    lam = lam_scr[:, 0:1]
    for t in range(tq // Q_BLOCK):
        rows = pl.ds(pl.multiple_of(pl.program_id(2) * tq + t * Q_BLOCK, Q_BLOCK), Q_BLOCK)
        qs = [q_scr[0, rows, :], q_scr[1, rows, :]]

        mx = [jnp.full((Q_BLOCK, LANES), NEG_INF, F32) for _ in range(2)]
        for c0, c1 in chunks:
            kc = k_scr[c0:c1, :]
            for h in range(2):
                s = _dot_nt(qs[h], kc)
                s_scr[2 * t + h, :, c0:c1] = s
                for j in range((c1 - c0) // LANES):
                    mx[h] = jnp.maximum(mx[h], s[:, j * LANES:(j + 1) * LANES])
        mrow = [jnp.max(m, axis=-1, keepdims=True) for m in mx]

        out = [jnp.zeros((Q_BLOCK, 2 * HEAD_DIM), F32) for _ in range(2)]
        for c0, c1 in chunks:
            vc = v_scr[c0:c1, :]
            for h in range(2):
                e = jnp.exp2(s_scr[2 * t + h, :, c0:c1] - mrow[h])
                out[h] = out[h] + _dot(e.astype(BF16), vc)
        r1 = 1.0 / out[0][:, HEAD_DIM:HEAD_DIM + 1]
        r2 = lam / out[1][:, HEAD_DIM:HEAD_DIM + 1]
        o = out[0][:, 0:HEAD_DIM] * r1 - out[1][:, 0:HEAD_DIM] * r2
        o_ref[t * Q_BLOCK:(t + 1) * Q_BLOCK, :] = (_rms(o, gs_ref[...]) * (1.0 - lam_init)).astype(o_ref.dtype)


def _diff_attention(qkv, qkv_c, tabs, gq, gk, gs, lamp, lam_init, tq, to_cast, zero_shapes):
    bsz, l, _ = qkv.shape
    n_ctx = qkv_c.shape[1]
    cos_t, sin_t = tabs
    hb = 2 * DIFF_SUB
    kb = DIFF_Q // hb
    vb = 2 * DIFF_Q // hb
    nq = l // tq
    n_steps = bsz * DIFF_HEADS * nq
    slabs = [a.reshape(n_steps, -1, a.shape[-1]) for a in to_cast]
    slab_spec = lambda a: pl.BlockSpec((None,) + a.shape[1:], lambda bi, h, i: ((bi * DIFF_HEADS + h) * nq + i, 0, 0))
    zero_slabs = [jax.ShapeDtypeStruct((n_steps, rows // n_steps, cols), F32) for rows, cols in zero_shapes]
    kern = functools.partial(_diff_attn_kernel, lam_init=lam_init, n_ctx=n_ctx, tq=tq, n_cast=len(slabs),
                             n_zero=len(zero_slabs))
    vec = pl.BlockSpec((1, hb), lambda bi, h, i: (0, 0))
    outs = pl.pallas_call(
        kern,
        grid=(bsz, DIFF_HEADS, nq),
        in_specs=[pl.BlockSpec((None, l, hb), lambda bi, h, i: (bi, 0, h)),
                  pl.BlockSpec((None, l, hb), lambda bi, h, i: (bi, 0, kb + h)),
                  pl.BlockSpec((None, l, hb), lambda bi, h, i: (bi, 0, vb + h)),
                  pl.BlockSpec((None, n_ctx, hb), lambda bi, h, i: (bi, 0, kb + h)),
                  pl.BlockSpec((None, n_ctx, hb), lambda bi, h, i: (bi, 0, vb + h)),
                  pl.BlockSpec((l, hb), lambda bi, h, i: (0, 0)),
                  pl.BlockSpec((l, hb), lambda bi, h, i: (0, 0)),
                  vec, vec, vec,
                  pl.BlockSpec((8, hb), lambda bi, h, i: (0, 0))] + [slab_spec(a) for a in slabs],
        out_specs=[pl.BlockSpec((None, tq, hb), lambda bi, h, i: (bi, i, h))]
        + [slab_spec(a) for a in slabs] + [slab_spec(a) for a in zero_slabs],
        out_shape=[jax.ShapeDtypeStruct((bsz, l, DIFF_V), BF16)]
        + [jax.ShapeDtypeStruct(a.shape, BF16) for a in slabs] + zero_slabs,
        scratch_shapes=[pltpu.VMEM((2, l, hb), BF16), pltpu.VMEM((n_ctx + l, hb), BF16),
                        pltpu.VMEM((n_ctx + l, 2 * HEAD_DIM), BF16),
                        pltpu.VMEM((2 * (tq // Q_BLOCK), Q_BLOCK, n_ctx + l), F32),
                        pltpu.VMEM((1, hb), F32)],
        compiler_params=_params("arbitrary", "arbitrary", "arbitrary"),
        name="diff_attention",
    )(qkv, qkv, qkv, qkv_c, qkv_c, cos_t, sin_t, gq, gk, gs, lamp, *slabs)
    casts = [o.reshape(a.shape) for o, a in zip(outs[1:1 + len(slabs)], to_cast)]
    zeros = [o.reshape(shape) for o, shape in zip(outs[1 + len(slabs):], zero_shapes)]
    return outs[0], casts, zeros


def _swa_kernel(q_ref, k_ref, v_ref, kc_ref, vc_ref, cos_ref, sin_ref, gq_ref, gk_ref, sink_ref,
                o_ref, q_scr, k_scr, v_scr, kc_scr, vc_scr, *, seq):
    kvh = pl.program_id(1)
    n = pl.program_id(2)
    span = Q_BLOCK + 2 * WINDOW

    @pl.when(n == 0)
    def _():
        gk = gk_ref[...]
        gq = gq_ref[...]
        cos_t = cos_ref[...]
        sin_t = sin_ref[...]
        zeros = jnp.zeros((WINDOW, HEAD_DIM), BF16)
        k_scr[0:WINDOW, :] = zeros
        v_scr[0:WINDOW, :] = zeros
        k_scr[WINDOW + seq:, :] = zeros
        v_scr[WINDOW + seq:, :] = zeros
        k_scr[WINDOW:WINDOW + seq, :] = _rope(_rms(k_ref[...].astype(F32), gk), cos_t, sin_t).astype(BF16)
        v_scr[WINDOW:WINDOW + seq, :] = v_ref[...].astype(BF16)
        kc_scr[...] = _rms(kc_ref[...].astype(F32), gk).astype(BF16)
        vc_scr[...] = vc_ref[...].astype(BF16)
        for j in range(SWA_GROUP):
            qj = q_ref[:, j * HEAD_DIM:(j + 1) * HEAD_DIM].astype(F32)
            qj = _rope(_rms(qj, gq), cos_t, sin_t) * (HEAD_DIM ** -0.5)
            q_scr[j] = qj.astype(BF16)

    kc = kc_scr[...]
    vc = vc_scr[...]
    rows = SWA_GROUP * Q_BLOCK
    ridx = lax.broadcasted_iota(jnp.int32, (rows, span), 0)
    kidx = lax.broadcasted_iota(jnp.int32, (rows, span), 1)
    rel = kidx - (ridx & (Q_BLOCK - 1))
    band = (rel >= 0) & (rel <= 2 * WINDOW)
    head = lax.broadcasted_iota(jnp.int32, (rows, 1), 0) // Q_BLOCK
    sink = jnp.zeros((rows, 1), F32)
    for j in range(SWA_GROUP):
        sink = jnp.where(head == j, sink_ref[pl.ds(kvh * SWA_GROUP + j, 1), 0:1], sink)
    for t in range(o_ref.shape[0] // Q_BLOCK):
        blk = n * (o_ref.shape[0] // Q_BLOCK) + t
        start = pl.multiple_of(blk * Q_BLOCK, Q_BLOCK)
        kw = k_scr[pl.ds(start, span), :]
        vw = v_scr[pl.ds(start, span), :]
        key_pos = blk * Q_BLOCK + kidx - WINDOW
        mask = band & (key_pos >= 0) & (key_pos < seq)
        q = jnp.concatenate([q_scr[j, pl.ds(start, Q_BLOCK), :] for j in range(SWA_GROUP)], axis=0)
        s = jnp.concatenate([_dot_nt(q, kc), jnp.where(mask, _dot_nt(q, kw), NEG_INF)], axis=-1)
        m = jnp.maximum(jnp.max(s, axis=-1, keepdims=True), sink)
        e = jnp.exp(s - m)
        r = 1.0 / (jnp.sum(e, axis=-1, keepdims=True) + jnp.exp(sink - m))
        o = _dot((e * r).astype(BF16), jnp.concatenate([vc, vw], axis=0))
        o_ref[t * Q_BLOCK:(t + 1) * Q_BLOCK, :] = jnp.concatenate(
            [o[j * Q_BLOCK:(j + 1) * Q_BLOCK] for j in range(SWA_GROUP)], axis=-1).astype(o_ref.dtype)


def _swa_attention(qkv, qkv_c, tabs, gq, gk, sink_rows):
    bsz, l, _ = qkv.shape
    n_ctx = qkv_c.shape[1]
    cos_t, sin_t = tabs
    gw = SWA_GROUP * HEAD_DIM
    q0 = (2 * DIFF_Q + DIFF_V) // gw
    k0 = (2 * DIFF_Q + DIFF_V + SWA_Q) // HEAD_DIM
    v0 = k0 + SWA_KV_HEADS
    vec = pl.BlockSpec((1, HEAD_DIM), lambda bi, h, i: (0, 0))
    return pl.pallas_call(
        functools.partial(_swa_kernel, seq=l),
        grid=(bsz, SWA_KV_HEADS, l // (2 * Q_BLOCK)),
        in_specs=[pl.BlockSpec((None, l, gw), lambda bi, h, i: (bi, 0, q0 + h)),
                  pl.BlockSpec((None, l, HEAD_DIM), lambda bi, h, i: (bi, 0, k0 + h)),
                  pl.BlockSpec((None, l, HEAD_DIM), lambda bi, h, i: (bi, 0, v0 + h)),
                  pl.BlockSpec((None, n_ctx, HEAD_DIM), lambda bi, h, i: (bi, 0, k0 + h)),
                  pl.BlockSpec((None, n_ctx, HEAD_DIM), lambda bi, h, i: (bi, 0, v0 + h)),
                  pl.BlockSpec((l, HEAD_DIM), lambda bi, h, i: (0, 0)),
                  pl.BlockSpec((l, HEAD_DIM), lambda bi, h, i: (0, 0)),
                  vec, vec,
                  pl.BlockSpec((SWA_HEADS, HEAD_DIM), lambda bi, h, i: (0, 0))],
        out_specs=pl.BlockSpec((None, 2 * Q_BLOCK, gw), lambda bi, h, i: (bi, i, h)),
        out_shape=jax.ShapeDtypeStruct((bsz, l, SWA_Q), BF16),
        scratch_shapes=[pltpu.VMEM((SWA_GROUP, l, HEAD_DIM), BF16),
                        pltpu.VMEM((l + 2 * WINDOW, HEAD_DIM), BF16), pltpu.VMEM((l + 2 * WINDOW, HEAD_DIM), BF16),
                        pltpu.VMEM((n_ctx, HEAD_DIM), BF16), pltpu.VMEM((n_ctx, HEAD_DIM), BF16)],
        compiler_params=_params("parallel", "parallel", "arbitrary"),
        name="swa_attention",
    )(qkv, qkv, qkv, qkv_c, qkv_c, cos_t, sin_t, gq, gk, sink_rows)


def _router_kernel(x_ref, g_ref, sc_ref, sh_ref, w_ref, b_ref, tri_ref, h_ref, e_ref, p_ref, cnt_ref, run_scr):
    @pl.when((pl.program_id(0) == 0) & (pl.program_id(1) == 0))
    def _():
        run_scr[...] = jnp.zeros(run_scr.shape, F32)

    h = _norm_mod(x_ref[...], g_ref[...], sc_ref[...], sh_ref[...])
    h_ref[...] = h
    logits = jnp.dot(h, w_ref[...], precision=HIGHEST, preferred_element_type=F32) + b_ref[...]
    lane_i = lax.broadcasted_iota(jnp.int32, (1, LANES), 1)
    lane = lane_i.astype(F32)
    big = float(LANES)
    lg = jnp.where(lane_i < MOE_GROUPS, logits, NEG_INF)
    mg = jnp.max(lg, axis=-1, keepdims=True)
    p_top = 1.0 / jnp.sum(jnp.exp(lg - mg), axis=-1, keepdims=True)
    grp = jnp.min(jnp.where(lg == mg, lane, big), axis=-1, keepdims=True)
    e_lane = lane - MOE_GROUPS
    in_grp = (e_lane >= grp * MOE_PER_GROUP) & (e_lane < (grp + 1.0) * MOE_PER_GROUP)
    le = jnp.where(in_grp, logits, NEG_INF)
    t1 = jnp.max(le, axis=-1, keepdims=True)
    i1 = jnp.min(jnp.where(le == t1, lane, big), axis=-1, keepdims=True)
    le2 = jnp.where(lane == i1, NEG_INF, le)
    t2 = jnp.max(le2, axis=-1, keepdims=True)
    i2 = jnp.min(jnp.where(le2 == t2, lane, big), axis=-1, keepdims=True)
    w2 = jnp.exp(t2 - t1)
    inv = p_top / (1.0 + w2)
    hot1 = lane == i1
    hot2 = lane == i2
    onehot = jnp.where(hot1 | hot2, 1.0, 0.0)
    before = run_scr[...] + _dot(tri_ref[...], onehot.astype(BF16))
    r1 = jnp.sum(jnp.where(hot1, before, 0.0), axis=-1, keepdims=True)
    r2 = jnp.sum(jnp.where(hot2, before, 0.0), axis=-1, keepdims=True)
    run_scr[...] = run_scr[...] + jnp.sum(onehot, axis=0, keepdims=True)
    cnt_ref[...] = run_scr[...]
    e_ref[...] = jnp.where(lane_i == 0, i1 - MOE_GROUPS,
                           jnp.where(lane_i == 1, i2 - MOE_GROUPS,
                                     jnp.where(lane_i == 2, r1, jnp.where(lane_i == 3, r2, 0.0)))).astype(jnp.int32)
    p_ref[...] = jnp.where(lane_i == 0, inv, jnp.where(lane_i == 1, inv * w2, 0.0))


def _router(x, g, sc, sh, w, b, tm):
    bsz, l, d = x.shape
    row = lambda bi, i: (bi, i, 0)
    tri = (jnp.arange(tm)[:, None] > jnp.arange(tm)[None, :]).astype(BF16)
    return pl.pallas_call(
        _router_kernel,
        grid=(bsz, l // tm),
        in_specs=[pl.BlockSpec((None, tm, d), row),
                  pl.BlockSpec((1, d), lambda bi, i: (0, 0)),
                  pl.BlockSpec((None, 1, d), lambda bi, i: (bi, 0, 0)),
                  pl.BlockSpec((None, 1, d), lambda bi, i: (bi, 0, 0)),
                  pl.BlockSpec((d, LANES), lambda bi, i: (0, 0)),
                  pl.BlockSpec((1, LANES), lambda bi, i: (0, 0)),
                  pl.BlockSpec((tm, tm), lambda bi, i: (0, 0))],
        out_specs=[pl.BlockSpec((None, tm, d), row),
                   pl.BlockSpec((None, tm, LANES), row),
                   pl.BlockSpec((None, tm, LANES), row),
                   pl.BlockSpec((1, LANES), lambda bi, i: (0, 0))],
        out_shape=[jax.ShapeDtypeStruct((bsz, l, d), F32),
                   jax.ShapeDtypeStruct((bsz, l, LANES), jnp.int32),
                   jax.ShapeDtypeStruct((bsz, l, LANES), F32),
                   jax.ShapeDtypeStruct((1, LANES), F32)],
        scratch_shapes=[pltpu.VMEM((1, LANES), F32)],
        compiler_params=_params("arbitrary", "arbitrary"),
        name="moe_router",
    )(x, g, sc, sh, w, b, tri)


def _dispatch_kernel(slot_ref, h_ref, init_ref, xb_ref, sem):
    del init_ref
    tm = h_ref.shape[0]
    base = pl.program_id(0) * tm

    def copy(r, k):
        dst = slot_ref[(base + r) * MOE_TOPK + k]
        return pltpu.make_async_copy(h_ref.at[pl.ds(r, 1)], xb_ref.at[pl.ds(dst, 1)], sem)

    def start(r, carry):
        for k in range(MOE_TOPK):
            copy(r, k).start(priority=k)
        return carry

    def wait(r, carry):
        for k in range(MOE_TOPK):
            copy(r, k).wait()
        return carry

    lax.fori_loop(0, tm, start, 0, unroll=8)
    lax.fori_loop(0, tm, wait, 0, unroll=8)


def _dispatch(slot, h, zeros, tm):
    n, d = h.shape
    n_rows = zeros.shape[0]
    grid_spec = pltpu.PrefetchScalarGridSpec(
        num_scalar_prefetch=1,
        grid=(n // tm,),
        in_specs=[pl.BlockSpec((tm, d), lambda i, s: (i, 0)),
                  pl.BlockSpec(memory_space=pl.ANY)],
        out_specs=pl.BlockSpec(memory_space=pl.ANY),
        scratch_shapes=[pltpu.SemaphoreType.DMA(())],
    )
    return pl.pallas_call(
        _dispatch_kernel,
        grid_spec=grid_spec,
        out_shape=jax.ShapeDtypeStruct((n_rows, d), F32),
        input_output_aliases={2: 0},
        compiler_params=_params("arbitrary"),
        name="moe_dispatch",
    )(slot, h, zeros)


def _combine_kernel(slot_ref, x_ref, g_ref, p_ref, yb_ref, o_ref, y_scr, sem):
    tm = x_ref.shape[0]
    base = pl.program_id(0) * tm

    def copy(r, k):
        src = slot_ref[(base + r) * MOE_TOPK + k]
        return pltpu.make_async_copy(yb_ref.at[pl.ds(src, 1)], y_scr.at[k, pl.ds(r, 1)], sem)

    def start(r, carry):
        for k in range(MOE_TOPK):
            copy(r, k).start(priority=k)
        return carry

    def wait(r, carry):
        for k in range(MOE_TOPK):
            copy(r, k).wait()
        return carry

    lax.fori_loop(0, tm, start, 0, unroll=8)
    lax.fori_loop(0, tm, wait, 0, unroll=8)
    p = p_ref[...]
    moe = y_scr[0] * p[:, 0:1]
    for k in range(1, MOE_TOPK):
        moe = moe + y_scr[k] * p[:, k:k + 1]
    o_ref[...] = x_ref[...] + g_ref[...] * moe


def _combine(slot, x, g, gates, yb, tm):
    bsz, l, d = x.shape
    nt = l // tm
    grid_spec = pltpu.PrefetchScalarGridSpec(
        num_scalar_prefetch=1,
        grid=(bsz * nt,),
        in_specs=[pl.BlockSpec((None, tm, d), lambda i, s: (i // nt, i % nt, 0)),
                  pl.BlockSpec((None, 1, d), lambda i, s: (i // nt, 0, 0)),
                  pl.BlockSpec((None, tm, LANES), lambda i, s: (i // nt, i % nt, 0)),
                  pl.BlockSpec(memory_space=pl.ANY)],
        out_specs=pl.BlockSpec((None, tm, d), lambda i, s: (i // nt, i % nt, 0)),
        scratch_shapes=[pltpu.VMEM((MOE_TOPK, tm, d), F32), pltpu.SemaphoreType.DMA(())],
    )
    return pl.pallas_call(
        _combine_kernel,
        grid_spec=grid_spec,
        out_shape=jax.ShapeDtypeStruct((bsz, l, d), F32),
        compiler_params=_params("arbitrary"),
        name="moe_combine",
    )(slot, x, g, gates, yb)


def _expert_kernel(be_ref, nu_ref, x_ref, wg_ref, wu_ref, wd_ref, o_ref):
    i = pl.program_id(0)

    @pl.when(i < nu_ref[0])
    def _():
        x = x_ref[...].astype(BF16)
        ff = wg_ref.shape[1]
        acc = jnp.zeros(o_ref.shape, F32)
        chunk = min(FF_CHUNK, ff)
        assert ff % chunk == 0
        for c in range(ff // chunk):
            cols = slice(c * chunk, (c + 1) * chunk)
            gate = _dot(x, wg_ref[:, cols])
            up = _dot(x, wu_ref[:, cols])
            hidden = (gate * jax.nn.sigmoid(gate) * up).astype(BF16)
            acc = acc + _dot(hidden, wd_ref[cols, :])
        o_ref[...] = acc

    @pl.when(i >= nu_ref[0])
    def _():
        o_ref[...] = jnp.zeros(o_ref.shape, F32)


def _expert_ffn(xb, block_e, n_used, w_gate, w_up, w_down, layer):
    n_blocks, _, d = xb.shape
    ff = w_gate.shape[2]
    base = layer * MOE_EXPERTS
    grid_spec = pltpu.PrefetchScalarGridSpec(
        num_scalar_prefetch=2,
        grid=(n_blocks,),
        in_specs=[pl.BlockSpec((None, MOE_BLOCK, d), lambda i, be, nu: (i, 0, 0)),
                  pl.BlockSpec((None, d, ff), lambda i, be, nu: (base + be[i], 0, 0)),
                  pl.BlockSpec((None, d, ff), lambda i, be, nu: (base + be[i], 0, 0)),
                  pl.BlockSpec((None, ff, d), lambda i, be, nu: (base + be[i], 0, 0))],
        out_specs=pl.BlockSpec((None, MOE_BLOCK, d), lambda i, be, nu: (i, 0, 0)),
    )
    return pl.pallas_call(
        _expert_kernel,
        grid_spec=grid_spec,
        out_shape=jax.ShapeDtypeStruct((n_blocks, MOE_BLOCK, d), F32),
        compiler_params=_params("arbitrary"),
        name="expert_ffn",
    )(block_e, n_used, xb, w_gate, w_up, w_down)


def _moe_rows(n_tokens):
    return (-(-(n_tokens * MOE_TOPK) // MOE_BLOCK) + MOE_EXPERTS) * MOE_BLOCK


def _hier_moe(x, g, sc, sh, gate2, w_route, b_route, w_gate, w_up, w_down, layer, zeros):
    bsz, l, d = x.shape
    n = bsz * l
    tm = min(512, l)
    h, route, gates, cnt = _router(x, g, sc, sh, w_route, b_route, tm)
    route = route.reshape(n, LANES)
    expert = route[:, :MOE_TOPK]
    rank = route[:, MOE_TOPK:2 * MOE_TOPK]
    counts = cnt[0, MOE_GROUPS:MOE_GROUPS + MOE_EXPERTS].astype(jnp.int32)
    padded = (counts + MOE_BLOCK - 1) // MOE_BLOCK * MOE_BLOCK
    pad_end = jnp.cumsum(padded)
    pad_start = pad_end - padded
    hit = expert[:, :, None] == jnp.arange(MOE_EXPERTS, dtype=jnp.int32)
    slot = (jnp.sum(jnp.where(hit, pad_start, 0), axis=-1) + rank).reshape(-1)
    n_blocks = _moe_rows(n) // MOE_BLOCK
    block_row = jnp.arange(n_blocks, dtype=jnp.int32)[:, None] * MOE_BLOCK
    block_e = jnp.minimum(jnp.sum((pad_end[None, :] <= block_row).astype(jnp.int32), axis=-1), MOE_EXPERTS - 1)
    n_used = (pad_end[-1:] // MOE_BLOCK).astype(jnp.int32)
    xb = _dispatch(slot, h.reshape(n, d), zeros, tm).reshape(n_blocks, MOE_BLOCK, d)
    yb = _expert_ffn(xb, block_e, n_used, w_gate, w_up, w_down, layer).reshape(-1, d)
    return _combine(slot, x, gate2, gates, yb, tm)


def _short_conv_kernel(zv_ref, z1_ref, z2_ref, wv_ref, w1_ref, w2_ref, bv_ref, b1_ref, b2_ref,
                       v_ref, x1_ref, x2_ref):
    rdx, q = zv_ref.shape[0], zv_ref.shape[1]
    row = lax.broadcasted_iota(jnp.int32, (q, 1), 0)

    for z_ref, w_ref, b_ref, o_ref in ((zv_ref, wv_ref, bv_ref, v_ref), (z1_ref, w1_ref, b1_ref, x1_ref),
                                       (z2_ref, w2_ref, b2_ref, x2_ref)):
        z = [z_ref[k].astype(F32) for k in range(rdx)]
        w = w_ref[...]
        b = b_ref[...]
        for k in range(rdx):
            prev = z[k - 1] if k > 0 else jnp.where(row == 0, 0.0, pltpu.roll(z[rdx - 1], 1, 0))
            nxt = z[k + 1] if k < rdx - 1 else jnp.where(row == q - 1, 0.0, pltpu.roll(z[0], q - 1, 0))
            o_ref[k] = (b + prev * w[0:1] + z[k] * w[1:2] + nxt * w[2:3]).astype(o_ref.dtype)


def _short_conv(z, conv_w, conv_b, tc):
    bsz, rdx, q, w3 = z.shape
    w = w3 // 3
    nc = w // tc
    zs = lambda k: pl.BlockSpec((None, rdx, q, tc), lambda bi, j: (bi, 0, 0, k * nc + j))
    ws = lambda k: pl.BlockSpec((8, tc), lambda bi, j: (0, k * nc + j))
    bs = lambda k: pl.BlockSpec((1, tc), lambda bi, j: (0, k * nc + j))
    out = pl.BlockSpec((None, rdx, q, tc), lambda bi, j: (bi, 0, 0, j))
    return pl.pallas_call(
        _short_conv_kernel,
        grid=(bsz, nc),
        in_specs=[zs(0), zs(1), zs(2), ws(0), ws(1), ws(2), bs(0), bs(1), bs(2)],
        out_specs=[out, out, out],
        out_shape=[jax.ShapeDtypeStruct((bsz, rdx, q, w), BF16)] * 3,
        compiler_params=_params("parallel", "parallel"),
        name="short_conv",
    )(z, z, z, conv_w, conv_w, conv_w, conv_b, conv_b, conv_b)


def _filter_kernel(z_ref, w1_ref, b1_ref, f1_ref, w2_ref, b2_ref, f2_ref, w3_ref, dl_ref, bw_ref, o_ref, a_scr):
    n = z_ref.shape[0]

    @pl.when(pl.program_id(0) == 0)
    def _():
        a = jnp.sin(f1_ref[...] * (jnp.dot(z_ref[...], w1_ref[...], precision=HIGHEST,
                                           preferred_element_type=F32) + b1_ref[...]))
        a_scr[...] = jnp.sin(f2_ref[...] * (jnp.dot(a, w2_ref[...], precision=HIGHEST,
                                                    preferred_element_type=F32) + b2_ref[...]))

    hf = jnp.dot(a_scr[...], w3_ref[...], precision=HIGHEST, preferred_element_type=F32)
    q = n // DFT_RADIX
    row = lax.broadcasted_iota(jnp.int32, (n, 1), 0)
    assert q & (q - 1) == 0
    tap = DFT_RADIX * (row & (q - 1)) + lax.shift_right_logical(row, q.bit_length() - 1)
    t = tap.astype(F32) * (1.0 / (n - 1))
    hf = hf * jnp.exp(-t * dl_ref[...])
    hf = jnp.where((tap == 0) & (bw_ref[...] > 0.5), 0.0, hf)
    for k in range(DFT_RADIX):
        o_ref[k] = hf[k * q:(k + 1) * q]


def _hyena_filters(z, w1, b1, f1, w2, b2, f2, w3, deltas, is_bwd, tn):
    n = z.shape[0]
    cols = w3.shape[1]
    full = lambda a: pl.BlockSpec(a.shape, lambda j: (0, 0))
    tile = lambda r: pl.BlockSpec((r, tn), lambda j: (0, j))
    return pl.pallas_call(
        _filter_kernel,
        grid=(cols // tn,),
        in_specs=[full(z), full(w1), full(b1), full(f1), full(w2), full(b2), full(f2),
                  tile(w3.shape[0]), tile(1), tile(1)],
        out_specs=pl.BlockSpec((DFT_RADIX, n // DFT_RADIX, tn), lambda j: (0, 0, j)),
        out_shape=jax.ShapeDtypeStruct((DFT_RADIX, n // DFT_RADIX, cols), F32),
        scratch_shapes=[pltpu.VMEM((n, LANES), F32)],
        compiler_params=_params("arbitrary"),
        name="hyena_filters",
    )(z, w1, b1, f1, w2, b2, f2, w3, deltas, is_bwd)


def _dft_groups():
    groups = [(0, 1), (DFT_RADIX // 2, -1)]
    for m in range(1, DFT_RADIX // 2):
        groups += [(m, -1), (m, 1)]
    return groups


def _dft_coef(g, k):
    m, b = _dft_groups()[g]
    assert DFT_RADIX in (2, 4)
    quarter = (4 * m * k // DFT_RADIX) % 4
    return (1, 0, -1, 0)[quarter], (0, 1, 0, -1)[quarter], b


def _signed_sum(terms):
    acc = None
    for coef, thunk in terms:
        if coef == 0:
            continue
        val = thunk()
        if acc is None:
            acc = val if coef > 0 else -val
        else:
            acc = acc + val if coef > 0 else acc - val
    return acc


def _spectrum(cprod, sprod, want_c=True, want_s=True):
    cprod = functools.lru_cache(maxsize=None)(cprod)
    sprod = functools.lru_cache(maxsize=None)(sprod)
    out = []
    for g in range(DFT_RADIX):
        cu, su = [], []
        for k in range(DFT_RADIX):
            ca, sa, b = _dft_coef(g, k)
            cu += [(ca, functools.partial(cprod, k)), (-b * sa, functools.partial(sprod, k))]
            su += [(sa, functools.partial(cprod, k)), (b * ca, functools.partial(sprod, k))]
        out.append((_signed_sum(cu) if want_c else None, _signed_sum(su) if want_s else None))
    return out


def _dft_tables(n):
    q = n // DFT_RADIX
    r = 1 << (q.bit_length() // 2)
    f = jnp.arange(q, dtype=jnp.int32)[:, None]
    idx = jnp.arange(r, dtype=jnp.int32)[None, :]
    ang = lambda s: (((2 * f + 1) * s) % (4 * n)).astype(F32) * (math.pi / (2 * n))
    hi = ang(DFT_RADIX * r * idx)[:, :q // r]
    ch, sh = jnp.cos(hi), jnp.sin(hi)
    fwd, inv = [], []
    for trig in ("cos", "sin"):
        for k in range(DFT_RADIX):
            lo = ang(DFT_RADIX * idx + k)
            cl, sl = jnp.cos(lo), jnp.sin(lo)
            a, b, sign = (ch, sh, -1.0) if trig == "cos" else (sh, ch, 1.0)
            fwd.append((a[:, :, None] * cl[:, None, :] + sign * b[:, :, None] * sl[:, None, :]).reshape(q, q))
            inv.append((a.T[:, None, :] * cl.T[None, :, :] + sign * b.T[:, None, :] * sl.T[None, :, :]).reshape(q, q))
    shape = (2, DFT_RADIX, q, q)
    return jnp.stack(fwd).astype(BF16).reshape(shape), jnp.stack(inv).astype(BF16).reshape(shape)


def _dft_filter_kernel(m_ref, hf_ref, hb_ref, kr_ref, ki_ref, sum_scr, dif_scr, *, scale):
    @pl.when(pl.program_id(2) == 0)
    def _():
        hf = hf_ref[...]
        hb = hb_ref[...]
        sum_scr[...] = (hf + hb).astype(BF16)
        dif_scr[...] = (hb - hf).astype(BF16)

    of_sum = _spectrum(lambda k: _dot(m_ref[0, k], sum_scr[k]), lambda k: _dot(m_ref[1, k], sum_scr[k]),
                       want_s=False)
    of_dif = _spectrum(lambda k: _dot(m_ref[0, k], dif_scr[k]), lambda k: _dot(m_ref[1, k], dif_scr[k]),
                       want_c=False)
    for g in range(DFT_RADIX):
        kr_ref[g] = of_sum[g][0] * scale
        ki_ref[g] = of_dif[g][1] * scale


def _dft_filter(fwd, filt, width, tf, tn):
    rdx, q = fwd.shape[1], fwd.shape[2]
    nc = width // tn
    out = pl.BlockSpec((None, rdx, tf, tn), lambda o, j, i: (o, 0, i, j))
    return pl.pallas_call(
        functools.partial(_dft_filter_kernel, scale=1.0 / (rdx * q)),
        grid=(HYENA_ORDER, nc, q // tf),
        in_specs=[pl.BlockSpec((2, rdx, tf, q), lambda o, j, i: (0, 0, i, 0)),
                  pl.BlockSpec((rdx, q, tn), lambda o, j, i: (0, 0, (2 * o) * nc + j)),
                  pl.BlockSpec((rdx, q, tn), lambda o, j, i: (0, 0, (2 * o + 1) * nc + j))],
        out_specs=[out, out],
        out_shape=[jax.ShapeDtypeStruct((HYENA_ORDER, rdx, q, width), F32)] * 2,
        scratch_shapes=[pltpu.VMEM((rdx, q, tn), BF16), pltpu.VMEM((rdx, q, tn), BF16)],
        compiler_params=_params("parallel", "parallel", "arbitrary"),
        name="dft_filter",
    )(fwd, filt, filt)


def _dft_fwd_kernel(m_ref, u_ref, kr_ref, ki_ref, pr_ref, pi_ref):
    spec = _spectrum(lambda k: _dot(m_ref[0, k], u_ref[k]), lambda k: _dot(m_ref[1, k], u_ref[k]))
    for g, (ur, us) in enumerate(spec):
        kr = kr_ref[g]
        ki = ki_ref[g]
        pr_ref[g] = (ur * kr + us * ki).astype(pr_ref.dtype)
        pi_ref[g] = (ur * ki - us * kr).astype(pi_ref.dtype)


def _dft_fwd(fwd, u, kr, ki, order, tf, tn):
    bsz, rdx, q, width = u.shape
    out = pl.BlockSpec((None, rdx, tf, tn), lambda bi, j, i: (bi, 0, i, j))
    ks = pl.BlockSpec((None, rdx, tf, tn), lambda bi, j, i: (order, 0, i, j))
    return pl.pallas_call(
        _dft_fwd_kernel,
        grid=(bsz, width // tn, q // tf),
        in_specs=[pl.BlockSpec((2, rdx, tf, q), lambda bi, j, i: (0, 0, i, 0)),
                  pl.BlockSpec((None, rdx, q, tn), lambda bi, j, i: (bi, 0, 0, j)),
                  ks, ks],
        out_specs=[out, out],
        out_shape=[jax.ShapeDtypeStruct((bsz, rdx, q, width), BF16)] * 2,
        compiler_params=_params("parallel", "parallel", "parallel"),
        name="dft_fwd",
    )(fwd, u, kr, ki)


def _dft_inv_kernel(m_ref, pr_ref, pi_ref, u_ref, b_ref, x_ref, o_ref, ra_scr, rb_scr):
    for k in range(DFT_RADIX):
        @pl.when((pl.program_id(3) == 0) & (pl.program_id(2) == k))
        def _(k=k):
            ra, rb = [], []
            for g in range(DFT_RADIX):
                ca, sa, b = _dft_coef(g, k)
                pr = functools.partial(lambda g: pr_ref[g].astype(F32), g)
                pi = functools.partial(lambda g: pi_ref[g].astype(F32), g)
                ra += [(ca, pr), (-sa, pi)]
                rb += [(b * sa, pr), (b * ca, pi)]
            ra_scr[...] = _signed_sum(ra).astype(BF16)
            rb_scr[...] = _signed_sum(rb).astype(BF16)

    y = _dot(m_ref[0], ra_scr[...]) - _dot(m_ref[1], rb_scr[...])
    u = u_ref[...].astype(F32)
    o_ref[...] = (x_ref[...] * (y + u * b_ref[...])).astype(o_ref.dtype)


def _dft_inv(inv, pr, pi, u, bias, x, order, tt, tn):
    bsz, rdx, q, width = u.shape
    tile = pl.BlockSpec((None, None, tt, tn), lambda bi, j, k, i: (bi, k, i, j))
    spec = pl.BlockSpec((None, rdx, q, tn), lambda bi, j, k, i: (bi, 0, 0, j))
    return pl.pallas_call(
        _dft_inv_kernel,
        grid=(bsz, width // tn, rdx, q // tt),
        in_specs=[pl.BlockSpec((2, None, tt, q), lambda bi, j, k, i: (0, k, i, 0)),
                  spec, spec, tile,
                  pl.BlockSpec((None, 1, tn), lambda bi, j, k, i: (order, 0, j)),
                  tile],
        out_specs=tile,
        out_shape=jax.ShapeDtypeStruct((bsz, rdx, q, width), BF16),
        scratch_shapes=[pltpu.VMEM((q, tn), BF16), pltpu.VMEM((q, tn), BF16)],
        compiler_params=_params("parallel", "parallel", "arbitrary", "arbitrary"),
        name="dft_inv",
    )(inv, pr, pi, u, bias, x)


def _filter_features(n):
    t = jnp.linspace(0.0, 1.0, n, dtype=F32)[:, None]
    w = 2.0 * math.pi * jnp.arange(n, dtype=F32) / n
    f = jnp.linspace(1e-4, FILTER_BANDS - 1, FILTER_BANDS, dtype=F32)
    z = jnp.concatenate([t, jnp.cos(w[:, None] * f), -jnp.sin(w[:, None] * f)], axis=-1)
    return jnp.pad(z, ((0, 0), (0, LANES - FILTER_EMB)))


def _pad_to(a, shape):
    return jnp.pad(a, [(0, t - s) for s, t in zip(a.shape, shape)])


def kernel(x, c, ctx, c_ctx, ada_w, ada_b, norm1_g, norm2_g, attn_w_in, attn_w_out, diff_q_g, diff_k_g, diff_lq1, diff_lk1, diff_lq2, diff_lk2, diff_sub_g, swa_q_g, swa_k_g, swa_sink, hy_w_in, hy_b_in, hy_conv_w, hy_conv_b, flt_w1, flt_b1, flt_f1, flt_w2, flt_b2, flt_f2, flt_w3, hy_bias, hy_w_out, hy_b_out, moe_wg1, moe_bg1, moe_wg2, moe_bg2, moe_w_gate, moe_w_up, moe_w_down):
    bsz, seq, d = x.shape
    depth = ada_w.shape[0]
    n_ctx = ctx.shape[1]
    cond = jnp.concatenate([c, c_ctx[None, :], jnp.zeros((8 - bsz - 1, d), F32)], axis=0)
    ada_b3 = ada_b[:, None, :]
    zero_bias = lambda n: jnp.zeros((1, n), F32)
    expert_w = None
    moe_zeros = []
    tm = min(512, seq)

    for layer in range(depth):
        even = layer % 2 == 0
        i = layer // 2
        ctx_live = any(j % 2 == 0 for j in range(layer + 1, depth))
        assert not ctx_live, "context-stream update is only needed for deeper stacks"
        mod = _adaln_mod(cond, ada_w, ada_b3, layer)
        sh1, sc1, g1, sh2, sc2, g2 = [mod[:, k * d:(k + 1) * d] for k in range(6)]
        lat = lambda a: a[:bsz, None, :]
        cmod = lambda a: jnp.broadcast_to(a[bsz:bsz + 1, None, :], (bsz, 1, d))
        n1 = norm1_g[layer][None, :]
        if even:
            w_in = attn_w_in[i].astype(BF16)
            qkv = _norm_linear(x, n1, lat(sc1), lat(sh1), w_in, zero_bias(ATTN_IN), tm, 1536, "attn_in")
            qkv_c = _norm_linear(ctx, n1, cmod(sc1), cmod(sh1), w_in, zero_bias(ATTN_IN), min(tm, n_ctx), 1536,
                                 "attn_in_ctx")
            lam_init = 0.8 - 0.6 * math.exp(-0.3 * layer)
            lamp = _pad_to(jnp.stack([diff_lq1[i], diff_lk1[i], diff_lq2[i], diff_lk2[i]]), (8, 2 * DIFF_SUB))
            two = lambda a: jnp.tile(a, 2)[None, :]
            first = expert_w is None
            to_cast = [moe_w_gate, moe_w_up, moe_w_down] if first else []
            zero_shapes = [(_moe_rows(bsz * seq), d)] * depth if first else []
            o_diff, cast, zeros = _diff_attention(
                qkv, qkv_c, _rope_tables(seq, DIFF_SUB, 2 * DIFF_SUB), two(diff_q_g[i]), two(diff_k_g[i]),
                diff_sub_g[i][None, :], lamp, lam_init, min(2 * Q_BLOCK, seq), to_cast, zero_shapes)
            moe_zeros += zeros
            if cast:
                expert_w = (cast[0].reshape(depth * MOE_EXPERTS, d, -1), cast[1].reshape(depth * MOE_EXPERTS, d, -1),
                            cast[2].reshape(depth * MOE_EXPERTS, -1, d))
            sink_rows = jnp.broadcast_to(swa_sink[i][:, None], (SWA_HEADS, HEAD_DIM))
            o_swa = _swa_attention(qkv, qkv_c, _rope_tables(seq, HEAD_DIM, HEAD_DIM), swa_q_g[i][None, :],
                                   swa_k_g[i][None, :], sink_rows)
            w_out = attn_w_out[i].astype(BF16)
            x = _mm_residual([o_diff, o_swa], [w_out[:DIFF_V], w_out[DIFF_V:]], zero_bias(d), x, lat(g1), tm,
                             "attn_out")
        else:
            width = hy_w_out.shape[1]
            q = seq // DFT_RADIX
            z = _norm_linear(x, n1, lat(sc1), lat(sh1), hy_w_in[i].astype(BF16), hy_b_in[i][None, :], tm, 1536,
                             "hyena_in", classes=DFT_RADIX)
            v, x1, x2 = _short_conv(z, _pad_to(hy_conv_w[i], (8, 3 * width)), hy_conv_b[i][None, :], 128)
            max_decay = math.log(DECAY_TARGET) / DECAY_FAST
            min_decay = math.log(DECAY_TARGET) / DECAY_SLOW
            deltas = jnp.abs(jnp.tile(jnp.linspace(min_decay, max_decay, width, dtype=F32), 2 * HYENA_ORDER))[None, :]
            is_bwd = jnp.tile(jnp.concatenate([jnp.zeros((width,), F32), jnp.ones((width,), F32)]),
                              HYENA_ORDER)[None, :]
            sq = (LANES, LANES)
            feats = _filter_features(seq)
            feats = jnp.concatenate([feats[k::DFT_RADIX] for k in range(DFT_RADIX)], axis=0)
            filt = _hyena_filters(
                feats, _pad_to(flt_w1[i], sq), _pad_to(flt_b1[i][None, :], (1, LANES)),
                _pad_to(flt_f1[i][None, :], (1, LANES)), _pad_to(flt_w2[i], sq),
                _pad_to(flt_b2[i][None, :], (1, LANES)), _pad_to(flt_f2[i][None, :], (1, LANES)),
                _pad_to(flt_w3[i], (LANES, 2 * HYENA_ORDER * width)), deltas, is_bwd, 256)
            fwd, inv = _dft_tables(seq)
            tf = min(256, q)
            tn = min(512, width)
            kr, ki = _dft_filter(fwd, filt, width, tf, min(256, width))
            bias = hy_bias[i][:, None, :]
            pr, pi = _dft_fwd(fwd, v, kr, ki, 0, tf, tn)
            tt = min(1024, q)
            y1 = _dft_inv(inv, pr, pi, v, bias, x1, 0, tt, tn)
            pr, pi = _dft_fwd(fwd, y1, kr, ki, 1, tf, tn)
            y2 = _dft_inv(inv, pr, pi, y1, bias, x2, 1, tt, tn)
            x = _mm_residual([y2], [hy_w_out[i].astype(BF16)], hy_b_out[i][None, :], x, lat(g1), tm,
                             "hyena_out", classes=DFT_RADIX)
        w_route = _pad_to(jnp.concatenate([moe_wg1[layer], moe_wg2[layer]], axis=1), (d, LANES))
        b_route = _pad_to(jnp.concatenate([moe_bg1[layer], moe_bg2[layer]])[None, :], (1, LANES))
        x = _hier_moe(x, norm2_g[layer][None, :], lat(sc2), lat(sh2), lat(g2), w_route, b_route,
                      *expert_w, layer, moe_zeros[layer])
    return x
```
